```python
import jax, jax.numpy as jnp
from jax import lax
import numpy as np

D_MODEL = 1024
BATCH = 2
SEQ = 8192
DEPTH = 2

CTX_LEN = 256
GRID_W = 64
EPS = 1e-6
GLA_HEADS = 4
GLA_DK = 128
GLA_DV = 256
GLA_LR = 16
GLA_TAU = 16.0
GLA_CHUNK = 64
ROPE_BASE = 10000.0
NA_HEADS = 8
NA_HD = 64
WIN_R = 8
WIN_C = 16
N_EXPERTS = 16
N_GROUPS = 4
EXPERTS_PER_GROUP = N_EXPERTS // N_GROUPS
TOP_K = 2
D_FF_EXPERT = 1024
MOE_BLOCK = 128

GLA_QK = GLA_HEADS * GLA_DK
GLA_V = GLA_HEADS * GLA_DV
NA_W = NA_HEADS * NA_HD
IN_SIZES = (GLA_QK, GLA_QK, GLA_V, GLA_V, 2 * GLA_LR, NA_W, NA_W, NA_W, D_MODEL, D_MODEL)
IN_COLS = sum(IN_SIZES)

kernel_name = 'hybrid_gla_natten_moe_prefix_dit'


def rmsnorm(x, w):
    xf = x.astype(jnp.float32)
    y = xf * lax.rsqrt(jnp.mean(xf * xf, axis=-1, keepdims=True) + EPS)
    return (y * w.astype(jnp.float32)).astype(x.dtype)


def heads(t, n, d):
    B, L, _ = t.shape
    return t.reshape(B, L, n, d).transpose(0, 2, 1, 3)


def axial_rope(t, rows, cols):
    half = t.shape[-1] // 2
    inv = ROPE_BASE ** (-jnp.arange(0, half, 2, dtype=jnp.float32) / half)

    def rot(u, pos):
        ang = pos.astype(jnp.float32)[:, None] * inv[None, :]
        cos, sin = jnp.cos(ang), jnp.sin(ang)
        u1, u2 = jnp.split(u, 2, axis=-1)
        return jnp.concatenate([u1 * cos - u2 * sin, u1 * sin + u2 * cos], axis=-1)

    return jnp.concatenate([rot(t[..., :half], rows), rot(t[..., half:], cols)], axis=-1)


def gla_chunk_scan(q, k, v, logg, s0, need_out):
    B, H, L, _ = q.shape
    dv = v.shape[-1]
    n = L // GLA_CHUNK

    def chunks(t):
        return jnp.moveaxis(t.reshape(B, H, n, GLA_CHUNK, t.shape[-1]), 2, 0)

    lower = jnp.tril(jnp.ones((GLA_CHUNK, GLA_CHUNK), dtype=bool))

    def step(S, inp):
        qc, kc, vc, gc = inp
        b = jnp.cumsum(gc, axis=2)
        b_end = b[:, :, -1:, :]
        k_dec = kc * jnp.exp(b_end - b)
        S_new = S * jnp.exp(b_end[:, :, 0, :, None]) + jnp.einsum('bhsd,bhse->bhde', k_dec, vc)
        if not need_out:
            return S_new, None
        inter = jnp.einsum('bhtd,bhde->bhte', qc * jnp.exp(b), S)
        att = jnp.einsum('bhtd,bhsd->bhts', qc * jnp.exp(b - b_end), k_dec)
        att = jnp.where(lower, att, 0.0)
        o = inter + jnp.einsum('bhts,bhse->bhte', att, vc)
        return S_new, o

    S, o = lax.scan(step, s0, (chunks(q), chunks(k), chunks(v), chunks(logg)))
    if need_out:
        o = jnp.moveaxis(o, 0, 2).reshape(B, H, L, dv)
    return S, o


def gla_branch(xp, cp, w_gk_up, b_gk, norm_w, ctx_out):
    f32 = jnp.float32

    def prep(parts, grid):
        q, k, v, _, lr = parts
        q = heads(q, GLA_HEADS, GLA_DK).astype(f32)
        k = heads(k, GLA_HEADS, GLA_DK).astype(f32)
        if grid:
            t = jnp.arange(q.shape[2])
            q = axial_rope(q, t // GRID_W, t % GRID_W)
            k = axial_rope(k, t // GRID_W, t % GRID_W)
        q = q * (GLA_DK ** -0.5)
        v = heads(v, GLA_HEADS, GLA_DV).astype(f32)
        gates = []
        for d in range(2):
            z = lr[..., d * GLA_LR:(d + 1) * GLA_LR].astype(f32) @ w_gk_up[d].astype(f32) + b_gk[d].astype(f32)
            gates.append(heads(jax.nn.log_sigmoid(z) / GLA_TAU, GLA_HEADS, GLA_DK))
        return q, k, v, gates

    qx, kx, vx, (gxf, gxb) = prep(xp, True)
    qc, kc, vc, (gcf, gcb) = prep(cp, False)
    B = qx.shape[0]
    s0 = jnp.zeros((B, GLA_HEADS, GLA_DK, GLA_DV), f32)

    def flip(t):
        return jnp.flip(t, axis=2)

    s_cf, oc_f = gla_chunk_scan(qc, kc, vc, gcf, s0, ctx_out)
    s_cb, oc_b = gla_chunk_scan(flip(qc), flip(kc), flip(vc), flip(gcb), s0, ctx_out)
    _, ox_f = gla_chunk_scan(qx, kx, vx, gxf, s_cf, True)
    _, ox_b = gla_chunk_scan(flip(qx), flip(kx), flip(vx), flip(gxb), s_cb, True)

    def finish(o, g):
        o = o * lax.rsqrt(jnp.mean(o * o, axis=-1, keepdims=True) + EPS) * norm_w.astype(f32)
        Bo, H, L, dv = o.shape
        o = o.transpose(0, 2, 1, 3).reshape(Bo, L, H * dv)
        return (o * jax.nn.silu(g.astype(f32))).astype(g.dtype)

    yx = finish(ox_f + flip(ox_b), xp[3])
    yc = finish(oc_f + flip(oc_b), cp[3]) if ctx_out else None
    return yx, yc


def na_branch(xp, cp, rpb, ctx_out):
    f32 = jnp.float32
    q, k, v = xp
    B, L, _ = q.shape
    rows = L // GRID_W
    kr = min(WIN_R, rows)
    scale = NA_HD ** -0.5
    qg = (q * scale).reshape(B, rows, GRID_W, NA_HEADS, NA_HD)
    kg = k.reshape(B, rows, GRID_W, NA_HEADS, NA_HD)
    vg = v.reshape(B, rows, GRID_W, NA_HEADS, NA_HD)
    qc, kc, vc = [t.reshape(B, -1, NA_HEADS, NA_HD) for t in cp]

    col = jnp.arange(GRID_W)
    c0 = jnp.clip(col - WIN_C // 2, 0, GRID_W - WIN_C)
    col_mask = (col[None, :] >= c0[:, None]) & (col[None, :] < c0[:, None] + WIN_C)
    col_idx = jnp.clip(col[None, :] - col[:, None], -(WIN_C - 1), WIN_C - 1) + (WIN_C - 1)
    rpb32 = rpb.astype(f32)
    n_loc = kr * GRID_W

    def row_block(r):
        r0 = jnp.clip(r - kr // 2, 0, rows - kr)
        q_r = lax.dynamic_index_in_dim(qg, r, axis=1, keepdims=False)
        k_b = lax.dynamic_slice_in_dim(kg, r0, kr, axis=1)
        v_b = lax.dynamic_slice_in_dim(vg, r0, kr, axis=1)
        row_idx = r0 + jnp.arange(kr) - r + (WIN_R - 1)
        bias = rpb32[:, row_idx][:, :, col_idx].transpose(0, 2, 1, 3)
        s_loc = jnp.einsum('bqhd,bikhd->bhqik', q_r, k_b).astype(f32) + bias[None]
        s_loc = jnp.where(col_mask[:, None, :], s_loc, -jnp.inf).reshape(B, NA_HEADS, GRID_W, n_loc)
        s_ctx = jnp.einsum('bqhd,bchd->bhqc', q_r, kc).astype(f32)
        p = jax.nn.softmax(jnp.concatenate([s_loc, s_ctx], axis=-1), axis=-1).astype(v.dtype)
        o = jnp.einsum('bhqk,bkhd->bqhd', p[..., :n_loc], v_b.reshape(B, n_loc, NA_HEADS, NA_HD))
        o = o + jnp.einsum('bhqc,bchd->bqhd', p[..., n_loc:], vc)
        return o

    o = lax.map(row_block, jnp.arange(rows))
    yx = o.transpose(1, 0, 2, 3, 4).reshape(B, L, NA_W)
    yc = None
    if ctx_out:
        s = jnp.einsum('bqhd,bkhd->bhqk', qc * scale, kc).astype(f32)
        p = jax.nn.softmax(s, axis=-1).astype(vc.dtype)
        yc = jnp.einsum('bhqk,bkhd->bqhd', p, vc).reshape(B, -1, NA_W)
    return yx, yc


def hybrid_mixer(hx, hc, w_in, w_gk_up, b_gk, gla_norm_w, rpb, w_bo_gla, w_bo_na, w_out, ctx_out):
    idx = np.cumsum(IN_SIZES)[:-1].tolist()
    xp = jnp.split(hx @ w_in, idx, axis=-1)
    cp = jnp.split(hc @ w_in, idx, axis=-1)
    gla_x, gla_c = gla_branch(xp[:5], cp[:5], w_gk_up, b_gk, gla_norm_w, ctx_out)
    na_x, na_c = na_branch(xp[5:8], cp[5:8], rpb, ctx_out)

    def merge(gla_o, na_o, gates):
        m = jax.nn.sigmoid(gates[0]) * (gla_o @ w_bo_gla) + jax.nn.sigmoid(gates[1]) * (na_o @ w_bo_na)
        return m @ w_out

    yx = merge(gla_x, na_x, xp[8:])
    yc = merge(gla_c, na_c, cp[8:]) if ctx_out else None
    return yx, yc


def moe_ffn(h, w_router, b_router, w1, w3, w2):
    N, D = h.shape
    aff = jax.nn.sigmoid(h.astype(jnp.float32) @ w_router.astype(jnp.float32))
    sel = aff + b_router.astype(jnp.float32)
    gscore = lax.top_k(sel.reshape(N, N_GROUPS, EXPERTS_PER_GROUP), 2)[0].sum(-1)
    gbest = jnp.argmax(gscore, axis=-1)
    in_grp = (jnp.arange(N_EXPERTS) // EXPERTS_PER_GROUP)[None, :] == gbest[:, None]
    _, top_idx = lax.top_k(jnp.where(in_grp, sel, -jnp.inf), TOP_K)
    top_w = jnp.take_along_axis(aff, top_idx, axis=1)
    top_w = top_w / jnp.sum(top_w, axis=-1, keepdims=True)

    flat_e = top_idx.reshape(-1)
    flat_tok = jnp.repeat(jnp.arange(N, dtype=jnp.int32), TOP_K)
    flat_w = top_w.reshape(-1)
    order = jnp.argsort(flat_e)
    se, stok, sw = flat_e[order], flat_tok[order], flat_w[order]
    counts = jnp.bincount(flat_e, length=N_EXPERTS)
    starts = jnp.cumsum(counts) - counts
    padded = (counts + MOE_BLOCK - 1) // MOE_BLOCK * MOE_BLOCK
    pends = jnp.cumsum(padded)
    pstarts = pends - padded
    dest = pstarts[se] + (jnp.arange(N * TOP_K) - starts[se])
    n_blocks = (N * TOP_K + N_EXPERTS * (MOE_BLOCK - 1) + MOE_BLOCK - 1) // MOE_BLOCK
    P = n_blocks * MOE_BLOCK
    buf_tok = jnp.full((P,), N, dtype=jnp.int32).at[dest].set(stok)
    buf_w = jnp.zeros((P,), jnp.float32).at[dest].set(sw)
    blk_expert = jnp.minimum(jnp.searchsorted(pends, jnp.arange(n_blocks) * MOE_BLOCK, side='right'), N_EXPERTS - 1)
    h_pad = jnp.concatenate([h, jnp.zeros((1, D), h.dtype)], axis=0)

    def run_block(args):
        tok, e = args
        xb = h_pad[tok]
        return (jax.nn.silu(xb @ w1[e]) * (xb @ w3[e])) @ w2[e]

    y = lax.map(run_block, (buf_tok.reshape(n_blocks, MOE_BLOCK), blk_expert)).reshape(P, D)
    y = y * buf_w[:, None].astype(y.dtype)
    return jnp.zeros((N + 1, D), y.dtype).at[buf_tok].add(y)[:N]


def setup_inputs(seed: int = 0) -> dict:
    key = jax.random.key(seed)
    ks = jax.random.split(key, 24)
    D = D_MODEL

    def nrm(k, shape, s):
        return jax.random.normal(k, shape, jnp.float32) * s

    return {
        'x': nrm(ks[0], (BATCH, SEQ, D), 1.0),
        'c': nrm(ks[1], (BATCH, D), 1.0),
        'ctx': nrm(ks[2], (BATCH, CTX_LEN, D), 1.0),
        'c_ctx': nrm(ks[3], (D,), 1.0),
        'w_ada': nrm(ks[4], (DEPTH, D, 6 * D), 0.5 * D ** -0.5),
        'b_ada': nrm(ks[5], (DEPTH, 6 * D), 0.01),
        'norm1_w': 1.0 + nrm(ks[6], (DEPTH, D), 0.05),
        'norm2_w': 1.0 + nrm(ks[7], (DEPTH, D), 0.05),
        'w_in': nrm(ks[8], (DEPTH, D, IN_COLS), D ** -0.5),
        'w_gk_up': nrm(ks[9], (DEPTH, 2, GLA_LR, GLA_QK), GLA_LR ** -0.5),
        'b_gk': nrm(ks[10], (DEPTH, 2, GLA_QK), 0.1),
        'gla_norm_w': 1.0 + nrm(ks[11], (DEPTH, GLA_DV), 0.05),
        'rpb': nrm(ks[12], (DEPTH, NA_HEADS, 2 * WIN_R - 1, 2 * WIN_C - 1), 0.1),
        'w_bo_gla': nrm(ks[13], (DEPTH, GLA_V, D), GLA_V ** -0.5),
        'w_bo_na': nrm(ks[14], (DEPTH, NA_W, D), NA_W ** -0.5),
        'w_out': nrm(ks[15], (DEPTH, D, D), D ** -0.5),
        'w_router': nrm(ks[16], (D, N_EXPERTS), D ** -0.5),
        'b_router': nrm(ks[17], (N_EXPERTS,), 0.01),
        'w1': nrm(ks[18], (DEPTH, N_EXPERTS, D, D_FF_EXPERT), D ** -0.5),
        'w3': nrm(ks[19], (DEPTH, N_EXPERTS, D, D_FF_EXPERT), D ** -0.5),
        'w2': nrm(ks[20], (DEPTH, N_EXPERTS, D_FF_EXPERT, D), D_FF_EXPERT ** -0.5),
        'final_norm_w': 1.0 + nrm(ks[21], (D,), 0.05),
    }


def reference(x, c, ctx, c_ctx, w_ada, b_ada, norm1_w, norm2_w, w_in, w_gk_up, b_gk, gla_norm_w, rpb,
              w_bo_gla, w_bo_na, w_out, w_router, b_router, w1, w3, w2, final_norm_w):
    B, L, D = x.shape
    Lc = ctx.shape[1]
    for l in range(DEPTH):
        ctx_out = l < DEPTH - 1
        mx = (jax.nn.silu(c) @ w_ada[l] + b_ada[l]).reshape(B, 6, 1, D)
        mc = (jax.nn.silu(c_ctx) @ w_ada[l] + b_ada[l]).reshape(6, D)
        hx = rmsnorm(x, norm1_w[l]) * (1.0 + mx[:, 1]) + mx[:, 0]
        hc = rmsnorm(ctx, norm1_w[l]) * (1.0 + mc[1]) + mc[0]
        yx, yc = hybrid_mixer(hx, hc, w_in[l], w_gk_up[l], b_gk[l], gla_norm_w[l], rpb[l],
                              w_bo_gla[l], w_bo_na[l], w_out[l], ctx_out)
        x = x + mx[:, 2] * yx
        hx = rmsnorm(x, norm2_w[l]) * (1.0 + mx[:, 4]) + mx[:, 3]
        if ctx_out:
            ctx = ctx + mc[2] * yc
            hc = rmsnorm(ctx, norm2_w[l]) * (1.0 + mc[4]) + mc[3]
            tokens = jnp.concatenate([hx.reshape(-1, D), hc.reshape(-1, D)], axis=0)
            y = moe_ffn(tokens, w_router, b_router, w1[l], w3[l], w2[l])
            x = x + mx[:, 5] * y[:B * L].reshape(B, L, D)
            ctx = ctx + mc[5] * y[B * L:].reshape(B, Lc, D)
        else:
            y = moe_ffn(hx.reshape(-1, D), w_router, b_router, w1[l], w3[l], w2[l])
            x = x + mx[:, 5] * y.reshape(B, L, D)
    return rmsnorm(x, final_norm_w)
```

```python
import functools

import jax
import jax.numpy as jnp
import numpy as np
from jax import lax
from jax.experimental import pallas as pl
from jax.experimental.pallas import tpu as pltpu

F32 = jnp.float32
BF16 = jnp.bfloat16

EPS = 1e-6
GRID_W = 64
GLA_HEADS = 4
GLA_DK = 128
GLA_DV = 256
GLA_LR = 16
GLA_TAU = 16.0
GLA_CHUNK = 64
ROPE_BASE = 10000.0
NA_HEADS = 8
NA_HD = 64
WIN_R = 8
WIN_C = 16
N_EXPERTS = 16
N_GROUPS = 4
EXPERTS_PER_GROUP = N_EXPERTS // N_GROUPS
TOP_K = 2

GLA_QK = GLA_HEADS * GLA_DK
GLA_V = GLA_HEADS * GLA_DV
NA_W = NA_HEADS * NA_HD

TM = 256
LANES = 128
LR_PAD = LANES
NEG = -1e30
VMEM_LIMIT = 56 * 1024 * 1024

_NT = (((1,), (1,)), ((), ()))
_TN = (((0,), (0,)), ((), ()))


def _cparams(sem):
    return pltpu.CompilerParams(dimension_semantics=sem, vmem_limit_bytes=VMEM_LIMIT)


def _dot(a, b):
    return jnp.dot(a, b, preferred_element_type=F32)


def _dg(a, b, dims):
    return lax.dot_general(a, b, dims, preferred_element_type=F32)


def _split(a):
    hi = a.astype(BF16)
    lo = (a - hi.astype(F32)).astype(BF16)
    return hi, lo


def _sigmoid(x):
    return 1.0 / (1.0 + jnp.exp(-x))


def _norm_mod(x, w, shift, scale):
    y = x * lax.rsqrt(jnp.mean(x * x, axis=-1, keepdims=True) + EPS)
    return (y * w) * (1.0 + scale) + shift


def _pack(h):
    c = h.shape[1] // 2
    a = lax.bitcast_convert_type(h[:, :c].astype(BF16).astype(F32), jnp.uint32)
    b = lax.bitcast_convert_type(h[:, c:].astype(BF16).astype(F32), jnp.uint32)
    return (a >> 16) | (b & jnp.uint32(0xFFFF0000))


def _unpack(u):
    a = lax.bitcast_convert_type(u << 16, F32)
    b = lax.bitcast_convert_type(u & jnp.uint32(0xFFFF0000), F32)
    return a, b


def _mod_kernel(c_ref, w_ref, b_ref, o_ref):
    c = c_ref[...]
    s = c * _sigmoid(c)
    sh, sl = _split(s)
    wh, wl = _split(w_ref[0])
    o_ref[0] = _dot(sh, wh) + _dot(sl, wh) + _dot(sh, wl) + b_ref[0]


def _modulation(cvec, w_ada, b_ada):
    depth, d, n = w_ada.shape
    tn = 1024
    return pl.pallas_call(
        _mod_kernel,
        grid=(depth, n // tn),
        in_specs=[pl.BlockSpec((8, d), lambda l, j: (0, 0)),
                  pl.BlockSpec((1, d, tn), lambda l, j: (l, 0, j)),
                  pl.BlockSpec((1, 1, tn), lambda l, j: (l, 0, j))],
        out_specs=pl.BlockSpec((1, 8, tn), lambda l, j: (l, 0, j)),
        out_shape=jax.ShapeDtypeStruct((depth, 8, n), F32),
        compiler_params=_cparams(("arbitrary", "arbitrary")),
        name="modulation",
    )(cvec, w_ada, b_ada.reshape(depth, 1, n))


_IN_OUT = (("gq", GLA_QK, BF16), ("gk", GLA_QK, BF16), ("gv", GLA_V, BF16), ("gg", GLA_V, BF16),
           ("nq", NA_W, BF16), ("nk", NA_W, BF16), ("nv", NA_W, BF16),
           ("m1", None, BF16), ("m2", None, BF16), ("lr", LR_PAD, F32))


def _inproj_kernel(x_ref, mod_ref, nw_ref, w_ref, cos_ref, sa_ref, sb_ref, *o_refs):
    d = x_ref.shape[2]
    mod = mod_ref[0, 0]
    h = _norm_mod(x_ref[0], nw_ref[...], mod[0:1], mod[1:2]).astype(BF16)
    cos, sa, sb = cos_ref[...], sa_ref[...], sb_ref[...]

    def rope(r, scale):
        parts = []
        for hh in range(GLA_HEADS):
            xs = r[:, hh * GLA_DK:(hh + 1) * GLA_DK]
            y = xs * cos + pltpu.roll(xs, GLA_DK - 32, 1) * sa + pltpu.roll(xs, 32, 1) * sb
            parts.append(y * scale if scale != 1.0 else y)
        return jnp.concatenate(parts, axis=1)

    off = 0
    for (name, width, dt), o_ref in zip(_IN_OUT, o_refs):
        width = d if width is None else width
        r = _dot(h, w_ref[:, off:off + width])
        if name == "gq":
            r = rope(r, GLA_DK ** -0.5)
        elif name == "gk":
            r = rope(r, 1.0)
        elif name == "nq":
            r = r * (NA_HD ** -0.5)
        o_ref[0] = r.astype(dt)
        off += width


def _inproj(x, modl, nw, wp, cos, sa, sb):
    b, t, d = x.shape
    nt = t // TM
    widths = [d if w is None else w for _, w, _ in _IN_OUT]
    ncols = sum(widths)
    return pl.pallas_call(
        _inproj_kernel,
        grid=(b, nt),
        in_specs=[pl.BlockSpec((1, TM, d), lambda i, j: (i, j, 0)),
                  pl.BlockSpec((1, 1, 8, d), lambda i, j: (i, jnp.minimum(j, 1), 0, 0)),
                  pl.BlockSpec((1, d), lambda i, j: (0, 0)),
                  pl.BlockSpec((d, ncols), lambda i, j: (0, 0)),
                  pl.BlockSpec((TM, GLA_DK), lambda i, j: (j, 0)),
                  pl.BlockSpec((TM, GLA_DK), lambda i, j: (j, 0)),
                  pl.BlockSpec((TM, GLA_DK), lambda i, j: (j, 0))],
        out_specs=[pl.BlockSpec((1, TM, w), lambda i, j: (i, j, 0)) for w in widths],
        out_shape=[jax.ShapeDtypeStruct((b, t, w), dt) for w, (_, _, dt) in zip(widths, _IN_OUT)],
        compiler_params=_cparams(("arbitrary", "arbitrary")),
        name="inproj",
    )(x, modl, nw, wp, cos, sa, sb)


def _gla_kernel(reverse, finish, q_ref, k_ref, v_ref, lr_ref, wup_ref, bg_ref, *rest):
    if finish:
        ob_ref, g_ref, nw_ref, o_ref, st_ref = rest
    else:
        o_ref, st_ref = rest
    nchunk = TM // GLA_CHUNK

    @pl.when(pl.program_id(1) == 0)
    def _():
        st_ref[...] = jnp.zeros_like(st_ref)

    z = _dot(lr_ref[0].astype(BF16), wup_ref[...]) + bg_ref[...]
    logg = (jnp.minimum(z, 0.0) - jnp.log(1.0 + jnp.exp(-jnp.abs(z)))) * (1.0 / GLA_TAU)
    r = lax.broadcasted_iota(jnp.int32, (TM, TM), 0)
    s = lax.broadcasted_iota(jnp.int32, (TM, TM), 1)
    order = (s >= r) if reverse else (s <= r)
    tri = jnp.where(((r // GLA_CHUNK) == (s // GLA_CHUNK)) & order, 1.0, 0.0).astype(BF16)
    hi, lo = _split(logg)
    bc = _dot(tri, hi) + _dot(tri, lo)
    cmask = order[:GLA_CHUNK, :GLA_CHUNK]

    for c in (range(nchunk - 1, -1, -1) if reverse else range(nchunk)):
        rs = slice(c * GLA_CHUNK, (c + 1) * GLA_CHUNK)
        for hh in range(GLA_HEADS):
            ls = slice(hh * GLA_DK, (hh + 1) * GLA_DK)
            vs = slice(hh * GLA_DV, (hh + 1) * GLA_DV)
            b = bc[rs, ls]
            bend = b[0:1] if reverse else b[GLA_CHUNK - 1:GLA_CHUNK]
            qc = q_ref[0, rs, ls].astype(F32)
            kc = k_ref[0, rs, ls].astype(F32)
            vc = v_ref[0, rs, vs]
            kd = (kc * jnp.exp(bend - b)).astype(BF16)
            qi = (qc * jnp.exp(b)).astype(BF16)
            qa = (qc * jnp.exp(b - bend)).astype(BF16)
            att = jnp.where(cmask, _dg(qa, kd, _NT), 0.0).astype(BF16)
            st = st_ref[hh]
            o = _dg(qi, st.astype(BF16), _NT) + _dot(att, vc)
            st_ref[hh] = st * jnp.exp(bend) + _dg(vc, kd, _TN)
            if finish:
                o = o + ob_ref[0, rs, vs].astype(F32)
                o = o * lax.rsqrt(jnp.mean(o * o, axis=-1, keepdims=True) + EPS) * nw_ref[...]
                g = g_ref[0, rs, vs].astype(F32)
                o = o * (g * _sigmoid(g))
            o_ref[0, rs, vs] = o.astype(o_ref.dtype)


def _gla_pass(reverse, q, k, v, lr, wup, bg, extra=None):
    b, t, _ = q.shape
    nt = t // TM
    if reverse:
        tile = lambda i, j: (i, jnp.where(j == 0, 0, nt - j), 0)
    else:
        tile = lambda i, j: (i, j, 0)
    const = lambda i, j: (0, 0)
    in_specs = [pl.BlockSpec((1, TM, GLA_QK), tile), pl.BlockSpec((1, TM, GLA_QK), tile),
                pl.BlockSpec((1, TM, GLA_V), tile), pl.BlockSpec((1, TM, LR_PAD), tile),
                pl.BlockSpec((LR_PAD, GLA_QK), const), pl.BlockSpec((1, GLA_QK), const)]
    args = [q, k, v, lr, wup, bg]
    if extra is not None:
        ob, g, nw = extra
        in_specs += [pl.BlockSpec((1, TM, GLA_V), tile), pl.BlockSpec((1, TM, GLA_V), tile),
                     pl.BlockSpec((1, GLA_DV), const)]
        args += [ob, g, nw]
    return pl.pallas_call(
        functools.partial(_gla_kernel, reverse, extra is not None),
        grid=(b, nt),
        in_specs=in_specs,
        out_specs=pl.BlockSpec((1, TM, GLA_V), tile),
        out_shape=jax.ShapeDtypeStruct((b, t, GLA_V), BF16),
        scratch_shapes=[pltpu.VMEM((GLA_HEADS, GLA_DV, GLA_DK), F32)],
        compiler_params=_cparams(("arbitrary", "arbitrary")),
        name="gla_bwd" if reverse else "gla_fwd",
    )(*args)


def _na_bias_tables(rpb, rows):
    rpt = TM // GRID_W
    qi = np.arange(TM)
    ki = np.arange(3 * TM)
    q_r, q_c = qi // GRID_W, qi % GRID_W
    k_r, k_c = ki // GRID_W, ki % GRID_W
    kr = min(WIN_R, rows)
    tabs = [jnp.full((NA_HEADS, TM, 3 * TM), NEG, F32)]
    for r_base, u_base in ((0, 0), (rpt, 0), (rows - rpt, rows - 3 * rpt)):
        rq = r_base + q_r[:, None]
        rk = u_base + k_r[None, :]
        r0 = np.clip(rq - kr // 2, 0, rows - kr)
        c0 = np.clip(q_c[:, None] - WIN_C // 2, 0, GRID_W - WIN_C)
        valid = (rk >= r0) & (rk < r0 + kr) & (k_c[None, :] >= c0) & (k_c[None, :] < c0 + WIN_C)
        ridx = np.clip(rk - rq + (WIN_R - 1), 0, 2 * WIN_R - 2)
        cidx = np.clip(k_c[None, :] - q_c[:, None], -(WIN_C - 1), WIN_C - 1) + (WIN_C - 1)
        bias = rpb.astype(F32)[:, ridx, cidx]
        tabs.append(jnp.where(valid[None], bias, NEG))
    return jnp.stack(tabs)


def _na_kernel(q_ref, kp_ref, kc_ref, kn_ref, kx_ref, vp_ref, vc_ref, vn_ref, vx_ref, bias_ref, o_ref):
    lane = lax.broadcasted_iota(jnp.int32, (TM, 2 * NA_HD), 1)
    first = lane < NA_HD
    k_refs = (kp_ref, kc_ref, kn_ref, kx_ref)
    v_refs = (vp_ref, vc_ref, vn_ref, vx_ref)
    for hp in range(NA_HEADS // 2):
        ls = slice(hp * 2 * NA_HD, (hp + 1) * 2 * NA_HD)
        q2 = q_ref[0, :, ls]
        ks = [kr[0, :, ls] for kr in k_refs]
        vs = [vr[0, :, ls] for vr in v_refs]
        outs = []
        for sub in range(2):
            hd = 2 * hp + sub
            qm = jnp.where(first if sub == 0 else jnp.logical_not(first), q2, jnp.zeros_like(q2))
            sc = []
            for i in range(4):
                s = _dg(qm, ks[i], _NT)
                if i < 3:
                    s = s + bias_ref[0, hd, :, i * TM:(i + 1) * TM]
                sc.append(s)
            m = jnp.max(sc[0], axis=-1, keepdims=True)
            for i in range(1, 4):
                m = jnp.maximum(m, jnp.max(sc[i], axis=-1, keepdims=True))
            l = jnp.zeros_like(m)
            acc = jnp.zeros((TM, 2 * NA_HD), F32)
            for i in range(4):
                p = jnp.exp(sc[i] - m)
                l = l + jnp.sum(p, axis=-1, keepdims=True)
                acc = acc + _dot(p.astype(BF16), vs[i])
            outs.append(acc / l)
        o_ref[0, :, ls] = jnp.where(first, outs[0], outs[1]).astype(o_ref.dtype)


def _na(q, k, v, bias):
    b, t, _ = q.shape
    nt = t // TM
    qt = lambda i, j: (i, j, 0)
    ctx = lambda i, j: (i, 0, 0)

    def near(o):
        return lambda i, j: (i, jnp.clip(j, 2, nt - 2) + o, 0)

    def kind(i, j):
        return (jnp.where(j == 0, 0, jnp.where(j == 1, 1, jnp.where(j == nt - 1, 3, 2))), 0, 0, 0)

    blk = lambda f: pl.BlockSpec((1, TM, NA_W), f)
    return pl.pallas_call(
        _na_kernel,
        grid=(b, nt),
        in_specs=[blk(qt), blk(near(-1)), blk(near(0)), blk(near(1)), blk(ctx),
                  blk(near(-1)), blk(near(0)), blk(near(1)), blk(ctx),
                  pl.BlockSpec((1, NA_HEADS, TM, 3 * TM), kind)],
        out_specs=blk(qt),
        out_shape=jax.ShapeDtypeStruct((b, t, NA_W), BF16),
        compiler_params=_cparams(("arbitrary", "arbitrary")),
        name="natten",
    )(q, k, k, k, k, v, v, v, v, bias)


def _route(sel, aff):
    rows = lambda a, e: a[e:e + 1, :]
    gscore = []
    for g in range(N_GROUPS):
        a, b, c, d = (rows(sel, EXPERTS_PER_GROUP * g + i) for i in range(EXPERTS_PER_GROUP))
        hi1, lo1 = jnp.maximum(a, b), jnp.minimum(a, b)
        hi2, lo2 = jnp.maximum(c, d), jnp.minimum(c, d)
        gscore.append(jnp.maximum(hi1, hi2) + jnp.maximum(jnp.minimum(hi1, hi2), jnp.maximum(lo1, lo2)))
    gbest = jnp.zeros_like(gscore[0], dtype=jnp.int32)
    gval = gscore[0]
    for g in range(1, N_GROUPS):
        better = gscore[g] > gval
        gbest = jnp.where(better, g, gbest)
        gval = jnp.where(better, gscore[g], gval)
    cs, ca = [], []
    for i in range(EXPERTS_PER_GROUP):
        s_i, a_i = rows(sel, i), rows(aff, i)
        for g in range(1, N_GROUPS):
            pick = gbest == g
            s_i = jnp.where(pick, rows(sel, EXPERTS_PER_GROUP * g + i), s_i)
            a_i = jnp.where(pick, rows(aff, EXPERTS_PER_GROUP * g + i), a_i)
        cs.append(s_i)
        ca.append(a_i)
    i1 = jnp.zeros_like(gbest)
    v1, w1 = cs[0], ca[0]
    for i in range(1, EXPERTS_PER_GROUP):
        better = cs[i] > v1
        i1 = jnp.where(better, i, i1)
        v1 = jnp.where(better, cs[i], v1)
        w1 = jnp.where(better, ca[i], w1)
    i2 = jnp.full_like(gbest, -1)
    v2 = jnp.full_like(v1, -jnp.inf)
    w2 = jnp.zeros_like(w1)
    for i in range(EXPERTS_PER_GROUP):
        better = (i1 != i) & ((cs[i] > v2) | (i2 < 0))
        i2 = jnp.where(better, i, i2)
        v2 = jnp.where(better, cs[i], v2)
        w2 = jnp.where(better, ca[i], w2)
    tot = w1 + w2
    base = gbest * EXPERTS_PER_GROUP
    return base + i1, base + i2, w1 / tot, w2 / tot


def _merge_kernel(gy_ref, ny_ref, m1_ref, m2_ref, x_ref, mod_ref, nw_ref, wbg_ref, wbn_ref, wo_ref,
                  wr_ref, br_ref, xo_ref, hp_ref, ri_ref, rw_ref):
    a = _dot(gy_ref[0], wbg_ref[...])
    b = _dot(ny_ref[0], wbn_ref[...])
    m = _sigmoid(m1_ref[0].astype(F32)) * a + _sigmoid(m2_ref[0].astype(F32)) * b
    y = _dot(m.astype(BF16), wo_ref[...])
    mod = mod_ref[0, 0]
    xn = x_ref[0] + mod[2:3] * y
    xo_ref[0] = xn
    h2 = _norm_mod(xn, nw_ref[...], mod[3:4], mod[4:5])
    hp_ref[0] = _pack(h2)
    hh, hl = _split(h2)
    wh, wl = _split(wr_ref[...])
    logit = _dg(wh, hh, _NT) + _dg(wh, hl, _NT) + _dg(wl, hh, _NT)
    aff = _sigmoid(logit)
    i1, i2, w1, w2 = _route(aff + br_ref[...], aff)
    zi = jnp.zeros((6, TM), jnp.int32)
    ri_ref[0] = jnp.concatenate([i1, i2, zi], axis=0)
    rw_ref[0] = jnp.concatenate([w1, w2, zi.astype(F32)], axis=0)


def _merge(gy, ny, m1, m2, x, modl, nw2, wbg, wbn, wo, wrt, brt):
    b, t, d = x.shape
    nt = t // TM
    tile = lambda i, j: (i, j, 0)
    const = lambda i, j: (0, 0)
    return pl.pallas_call(
        _merge_kernel,
        grid=(b, nt),
        in_specs=[pl.BlockSpec((1, TM, GLA_V), tile), pl.BlockSpec((1, TM, NA_W), tile),
                  pl.BlockSpec((1, TM, d), tile), pl.BlockSpec((1, TM, d), tile),
                  pl.BlockSpec((1, TM, d), tile),
                  pl.BlockSpec((1, 1, 8, d), lambda i, j: (i, jnp.minimum(j, 1), 0, 0)),
                  pl.BlockSpec((1, d), const),
                  pl.BlockSpec((GLA_V, d), const), pl.BlockSpec((NA_W, d), const),
                  pl.BlockSpec((d, d), const),
                  pl.BlockSpec((N_EXPERTS, d), const), pl.BlockSpec((N_EXPERTS, TM), const)],
        out_specs=[pl.BlockSpec((1, TM, d), tile), pl.BlockSpec((1, TM, d // 2), tile),
                   pl.BlockSpec((1, 8, TM), lambda i, j: (i, 0, j)),
                   pl.BlockSpec((1, 8, TM), lambda i, j: (i, 0, j))],
        out_shape=[jax.ShapeDtypeStruct((b, t, d), F32), jax.ShapeDtypeStruct((b, t, d // 2), jnp.uint32),
                   jax.ShapeDtypeStruct((b, 8, t), jnp.int32), jax.ShapeDtypeStruct((b, 8, t), F32)],
        compiler_params=_cparams(("arbitrary", "arbitrary")),
        name="merge_router",
    )(gy, ny, m1, m2, x, modl, nw2, wbg, wbn, wo, wrt, brt)


def _row_copy(src, s, dst, d, sem):
    return pltpu.make_async_copy(src.at[pl.ds(s, 1)], dst.at[pl.ds(d, 1)], sem)


def _dispatch_kernel(dest_ref, hp_ref, xs_in_ref, xs_ref, sem):
    del xs_in_ref
    base = pl.program_id(0) * TM

    def issue(r, carry):
        for kk in range(TOP_K):
            _row_copy(hp_ref, base + r, xs_ref, dest_ref[0, kk, r], sem).start()
        return carry

    def drain(r, carry):
        for kk in range(TOP_K):
            _row_copy(hp_ref, base + r, xs_ref, dest_ref[0, kk, r], sem).wait()
        return carry

    lax.fori_loop(0, TM, issue, 0)
    lax.fori_loop(0, TM, drain, 0)


def _dispatch(dest3, hp, n_slots):
    n, c = hp.shape
    return pl.pallas_call(
        _dispatch_kernel,
        grid=(n // TM,),
        in_specs=[pl.BlockSpec((1, TOP_K, TM), lambda i: (i, 0, 0), memory_space=pltpu.SMEM),
                  pl.BlockSpec(memory_space=pl.ANY), pl.BlockSpec(memory_space=pl.ANY)],
        out_specs=pl.BlockSpec(memory_space=pl.ANY),
        out_shape=jax.ShapeDtypeStruct((n_slots, c), jnp.uint32),
        scratch_shapes=[pltpu.SemaphoreType.DMA(())],
        input_output_aliases={2: 0},
        compiler_params=_cparams(("arbitrary",)),
        name="moe_dispatch",
    )(dest3, hp, jnp.zeros((n_slots, c), jnp.uint32))


def _expert_kernel(be_ref, nu_ref, xs_ref, w1_ref, w3_ref, w2_ref, y_ref):
    del be_ref

    @pl.when(pl.program_id(0) < nu_ref[0])
    def _():
        xa, xb = _unpack(xs_ref[...])
        x = jnp.concatenate([xa.astype(BF16), xb.astype(BF16)], axis=1)
        a = _dot(x, w1_ref[0])
        b = _dot(x, w3_ref[0])
        hmid = (a * _sigmoid(a)) * b
        y_ref[...] = _pack(_dot(hmid.astype(BF16), w2_ref[0]))

    @pl.when(pl.program_id(0) >= nu_ref[0])
    def _():
        y_ref[...] = jnp.zeros_like(y_ref)


def _experts(blk_expert, n_used, xs, w1, w3, w2):
    p, c = xs.shape
    _, d, f = w1.shape
    slot = lambda i, be, nu: (jnp.minimum(i, nu[0] - 1), 0)
    wsel = lambda i, be, nu: (be[i], 0, 0)
    return pl.pallas_call(
        _expert_kernel,
        grid_spec=pltpu.PrefetchScalarGridSpec(
            num_scalar_prefetch=2,
            grid=(p // TM,),
            in_specs=[pl.BlockSpec((TM, c), slot), pl.BlockSpec((1, d, f), wsel),
                      pl.BlockSpec((1, d, f), wsel), pl.BlockSpec((1, f, d), wsel)],
            out_specs=pl.BlockSpec((TM, c), lambda i, be, nu: (i, 0))),
        out_shape=jax.ShapeDtypeStruct((p, c), jnp.uint32),
        compiler_params=_cparams(("arbitrary",)),
        name="moe_experts",
    )(blk_expert, n_used, xs, w1, w3, w2)


def _combine_kernel(final, dest_ref, y_ref, x_ref, w_ref, mod_ref, *rest):
    if final:
        fw_ref, o_ref, buf, sem = rest
    else:
        o_ref, buf, sem = rest

    def issue(r, carry):
        for kk in range(TOP_K):
            _row_copy(y_ref, dest_ref[0, kk, r], buf.at[kk], r, sem).start()
        return carry

    def drain(r, carry):
        for kk in range(TOP_K):
            _row_copy(y_ref, dest_ref[0, kk, r], buf.at[kk], r, sem).wait()
        return carry

    lax.fori_loop(0, TM, issue, 0)
    lax.fori_loop(0, TM, drain, 0)
    w = w_ref[0]
    y = None
    for kk in range(TOP_K):
        ya, yb = _unpack(buf[kk])
        yk = jnp.concatenate([ya, yb], axis=1) * w[:, kk:kk + 1]
        y = yk if y is None else y + yk
    xn = x_ref[0] + mod_ref[0, 0][5:6] * y
    if final:
        xn = xn * lax.rsqrt(jnp.mean(xn * xn, axis=-1, keepdims=True) + EPS) * fw_ref[...]
    o_ref[0] = xn


def _combine(dest3, y, x, wcol, modl, final_w=None):
    b, t, d = x.shape
    nt = t // TM
    final = final_w is not None
    skip = 1 if final else 0
    tile = lambda i, j: (i, j + skip, 0)
    in_specs = [pl.BlockSpec((1, TOP_K, TM), lambda i, j: (i * nt + j + skip, 0, 0), memory_space=pltpu.SMEM),
                pl.BlockSpec(memory_space=pl.ANY),
                pl.BlockSpec((1, TM, d), tile), pl.BlockSpec((1, TM, TOP_K), tile),
                pl.BlockSpec((1, 1, 8, d), lambda i, j: (i, jnp.minimum(j + skip, 1), 0, 0))]
    args = [dest3, y, x, wcol, modl]
    if final:
        in_specs.append(pl.BlockSpec((1, d), lambda i, j: (0, 0)))
        args.append(final_w)
    return pl.pallas_call(
        functools.partial(_combine_kernel, final),
        grid=(b, nt - skip),
        in_specs=in_specs,
        out_specs=pl.BlockSpec((1, TM, d), lambda i, j: (i, j, 0)),
        out_shape=jax.ShapeDtypeStruct((b, t - skip * TM, d), F32),
        scratch_shapes=[pltpu.VMEM((TOP_K, TM, d // 2), jnp.uint32), pltpu.SemaphoreType.DMA(())],
        compiler_params=_cparams(("arbitrary", "arbitrary")),
        name="moe_combine_final" if final else "moe_combine",
    )(*args)


def _slot_plan(ridx, n_tok):
    e = ridx.reshape(-1)
    onehot = (e[:, None] == jnp.arange(N_EXPERTS, dtype=jnp.int32)[None, :]).astype(jnp.int32)
    incl = jnp.cumsum(onehot, axis=0)
    rank = jnp.sum((incl - onehot) * onehot, axis=1)
    counts = incl[-1]
    padded = (counts + TM - 1) // TM * TM
    pends = jnp.cumsum(padded)
    dest = (pends - padded)[e] + rank
    n_blocks = (n_tok * TOP_K + N_EXPERTS * (TM - 1) + TM - 1) // TM
    blk_expert = jnp.minimum(
        jnp.searchsorted(pends, jnp.arange(n_blocks, dtype=jnp.int32) * TM, side="right"), N_EXPERTS - 1)
    n_used = (pends[-1] // TM).astype(jnp.int32).reshape(1)
    return dest.astype(jnp.int32), blk_expert.astype(jnp.int32), n_used, n_blocks


def _rope_tables(t_len, ctx_len):
    half = GLA_DK // 2
    inv = ROPE_BASE ** (-np.arange(0, half, 2, dtype=np.float32) / half)
    lane = np.arange(GLA_DK)
    p = np.arange(t_len - ctx_len)
    pos = np.where(lane[None, :] < half, (p // GRID_W)[:, None], (p % GRID_W)[:, None]).astype(np.float32)
    ang = jnp.asarray(pos) * jnp.asarray(inv[(lane % half) % (half // 2)])[None, :]
    cos, sin = jnp.cos(ang), jnp.sin(ang)
    lower = jnp.asarray((lane % half) < half // 2)[None, :]
    pad = lambda a, v: jnp.concatenate([jnp.full((ctx_len, GLA_DK), v, F32), a], axis=0)
    return pad(cos, 1.0), pad(jnp.where(lower, -sin, 0.0), 0.0), pad(jnp.where(lower, 0.0, sin), 0.0)


def kernel(x, c, ctx, c_ctx, w_ada, b_ada, norm1_w, norm2_w, w_in, w_gk_up, b_gk, gla_norm_w, rpb,
           w_bo_gla, w_bo_na, w_out, w_router, b_router, w1, w3, w2, final_norm_w):
    bsz, seq, d = x.shape
    ctx_len = ctx.shape[1]
    depth = w_ada.shape[0]
    assert ctx_len == TM and seq % TM == 0 and TM % GRID_W == 0 and seq // GRID_W >= 3 * (TM // GRID_W)
    t = ctx_len + seq
    nt = t // TM
    n_tok = bsz * t

    xs = jnp.concatenate([ctx, x], axis=1)
    cvec = jnp.zeros((8, d), F32).at[:bsz].set(c).at[bsz].set(c_ctx)
    mods = _modulation(cvec, w_ada, b_ada).reshape(depth, 8, 6, d)
    cos, sa, sb = _rope_tables(t, ctx_len)
    wrt = w_router.T.astype(F32)
    brt = jnp.broadcast_to(b_router.astype(F32)[:, None], (N_EXPERTS, TM))
    gq_end = 2 * GLA_QK + 2 * GLA_V

    out = None
    for l in range(depth):
        mx = jnp.pad(mods[l, :bsz], ((0, 0), (0, 2), (0, 0)))
        mc = jnp.broadcast_to(jnp.pad(mods[l, bsz], ((0, 2), (0, 0)))[None], mx.shape)
        modl = jnp.stack([mc, mx], axis=1)
        wl = w_in[l]
        wp = jnp.concatenate([wl[:, :gq_end], wl[:, gq_end + 2 * GLA_LR:], wl[:, gq_end:gq_end + 2 * GLA_LR],
                              jnp.zeros((d, LR_PAD - 2 * GLA_LR), wl.dtype)], axis=1).astype(BF16)
        gq, gk, gv, gg, nq, nk, nv, m1, m2, lr = _inproj(xs, modl, norm1_w[l][None], wp, cos, sa, sb)

        def wup(dirn):
            w = jnp.zeros((LR_PAD, GLA_QK), F32)
            return w.at[dirn * GLA_LR:(dirn + 1) * GLA_LR].set(w_gk_up[l, dirn]).astype(BF16)

        ob = _gla_pass(True, gq, gk, gv, lr, wup(1), b_gk[l, 1][None])
        gy = _gla_pass(False, gq, gk, gv, lr, wup(0), b_gk[l, 0][None],
                       extra=(ob, gg, gla_norm_w[l][None]))
        ny = _na(nq, nk, nv, _na_bias_tables(rpb[l], seq // GRID_W))
        xs, hp, ridx, rw = _merge(gy, ny, m1, m2, xs, modl, norm2_w[l][None],
                                  w_bo_gla[l].astype(BF16), w_bo_na[l].astype(BF16), w_out[l].astype(BF16),
                                  wrt, brt)
        ridx2 = jnp.transpose(ridx[:, :TOP_K, :], (0, 2, 1))
        wcol = jnp.transpose(rw[:, :TOP_K, :], (0, 2, 1))
        dest, blk_expert, n_used, n_blocks = _slot_plan(ridx2, n_tok)
        dest3 = jnp.transpose(dest.reshape(bsz * nt, TM, TOP_K), (0, 2, 1))
        xsl = _dispatch(dest3, hp.reshape(n_tok, d // 2), n_blocks * TM)
        y = _experts(blk_expert, n_used, xsl, w1[l].astype(BF16), w3[l].astype(BF16), w2[l].astype(BF16))
        if l < depth - 1:
            xs = _combine(dest3, y, xs, wcol, modl)
        else:
            out = _combine(dest3, y, xs, wcol, modl, final_norm_w[None])
    return out
```

```python
import functools

import jax
import jax.numpy as jnp
import numpy as np
from jax import lax
from jax.experimental import pallas as pl
from jax.experimental.pallas import tpu as pltpu

F32 = jnp.float32
BF16 = jnp.bfloat16

EPS = 1e-6
GRID_W = 64
GLA_HEADS = 4
GLA_DK = 128
GLA_DV = 256
GLA_LR = 16
GLA_TAU = 16.0
GLA_CHUNK = 64
ROPE_BASE = 10000.0
NA_HEADS = 8
NA_HD = 64
WIN_R = 8
WIN_C = 16
N_EXPERTS = 16
N_GROUPS = 4
EXPERTS_PER_GROUP = N_EXPERTS // N_GROUPS
TOP_K = 2

GLA_QK = GLA_HEADS * GLA_DK
GLA_V = GLA_HEADS * GLA_DV
NA_W = NA_HEADS * NA_HD

TM = 256
LANES = 128
LR_PAD = LANES
NEG = -1e30
VMEM_LIMIT = 56 * 1024 * 1024

_NT = (((1,), (1,)), ((), ()))
_TN = (((0,), (0,)), ((), ()))


def _cparams(sem):
    return pltpu.CompilerParams(dimension_semantics=sem, vmem_limit_bytes=VMEM_LIMIT)


def _dot(a, b):
    return jnp.dot(a, b, preferred_element_type=F32)


def _dg(a, b, dims):
    return lax.dot_general(a, b, dims, preferred_element_type=F32)


def _split(a):
    hi = a.astype(BF16)
    lo = (a - hi.astype(F32)).astype(BF16)
    return hi, lo


def _sigmoid(x):
    return 1.0 / (1.0 + jnp.exp(-x))


def _norm_mod(x, w, shift, scale):
    y = x * lax.rsqrt(jnp.mean(x * x, axis=-1, keepdims=True) + EPS)
    return (y * w) * (1.0 + scale) + shift


def _mod_kernel(c_ref, w_ref, b_ref, o_ref):
    c = c_ref[...]
    s = c * _sigmoid(c)
    sh, sl = _split(s)
    wh, wl = _split(w_ref[0])
    o_ref[0] = _dot(sh, wh) + _dot(sl, wh) + _dot(sh, wl) + b_ref[0]


def _modulation(cvec, w_ada, b_ada):
    depth, d, n = w_ada.shape
    tn = 1024
    return pl.pallas_call(
        _mod_kernel,
        grid=(depth, n // tn),
        in_specs=[pl.BlockSpec((8, d), lambda l, j: (0, 0)),
                  pl.BlockSpec((1, d, tn), lambda l, j: (l, 0, j)),
                  pl.BlockSpec((1, 1, tn), lambda l, j: (l, 0, j))],
        out_specs=pl.BlockSpec((1, 8, tn), lambda l, j: (l, 0, j)),
        out_shape=jax.ShapeDtypeStruct((depth, 8, n), F32),
        compiler_params=_cparams(("arbitrary", "arbitrary")),
        name="modulation",
    )(cvec, w_ada, b_ada.reshape(depth, 1, n))


_IN_OUT = (("gq", GLA_QK, BF16), ("gk", GLA_QK, BF16), ("gv", GLA_V, BF16), ("gg", GLA_V, BF16),
           ("nq", NA_W, BF16), ("nk", NA_W, BF16), ("nv", NA_W, BF16),
           ("m1", None, BF16), ("m2", None, BF16), ("lr", LR_PAD, F32))


def _inproj_kernel(x_ref, mod_ref, nw_ref, w_ref, cos_ref, sa_ref, sb_ref, *o_refs):
    d = x_ref.shape[2]
    mod = mod_ref[0, 0]
    h = _norm_mod(x_ref[0], nw_ref[...], mod[0:1], mod[1:2]).astype(BF16)
    cos, sa, sb = cos_ref[...], sa_ref[...], sb_ref[...]

    def rope(r, scale):
        parts = []
        for hh in range(GLA_HEADS):
            xs = r[:, hh * GLA_DK:(hh + 1) * GLA_DK]
            y = xs * cos + pltpu.roll(xs, GLA_DK - 32, 1) * sa + pltpu.roll(xs, 32, 1) * sb
            parts.append(y * scale if scale != 1.0 else y)
        return jnp.concatenate(parts, axis=1)

    off = 0
    for (name, width, dt), o_ref in zip(_IN_OUT, o_refs):
        width = d if width is None else width
        r = _dot(h, w_ref[:, off:off + width])
        if name == "gq":
            r = rope(r, GLA_DK ** -0.5)
        elif name == "gk":
            r = rope(r, 1.0)
        elif name == "nq":
            r = r * (NA_HD ** -0.5)
        o_ref[0] = r.astype(dt)
        off += width


def _inproj(x, modl, nw, wp, cos, sa, sb):
    b, t, d = x.shape
    nt = t // TM
    widths = [d if w is None else w for _, w, _ in _IN_OUT]
    ncols = sum(widths)
    return pl.pallas_call(
        _inproj_kernel,
        grid=(b, nt),
        in_specs=[pl.BlockSpec((1, TM, d), lambda i, j: (i, j, 0)),
                  pl.BlockSpec((1, 1, 8, d), lambda i, j: (i, jnp.minimum(j, 1), 0, 0)),
                  pl.BlockSpec((1, d), lambda i, j: (0, 0)),
                  pl.BlockSpec((d, ncols), lambda i, j: (0, 0)),
                  pl.BlockSpec((TM, GLA_DK), lambda i, j: (j, 0)),
                  pl.BlockSpec((TM, GLA_DK), lambda i, j: (j, 0)),
                  pl.BlockSpec((TM, GLA_DK), lambda i, j: (j, 0))],
        out_specs=[pl.BlockSpec((1, TM, w), lambda i, j: (i, j, 0)) for w in widths],
        out_shape=[jax.ShapeDtypeStruct((b, t, w), dt) for w, (_, _, dt) in zip(widths, _IN_OUT)],
        compiler_params=_cparams(("arbitrary", "arbitrary")),
        name="inproj",
    )(x, modl, nw, wp, cos, sa, sb)


def _gla_kernel(reverse, finish, q_ref, k_ref, v_ref, lr_ref, wup_ref, bg_ref, *rest):
    if finish:
        ob_ref, g_ref, nw_ref, o_ref, st_ref = rest
    else:
        o_ref, st_ref = rest
    nchunk = TM // GLA_CHUNK

    @pl.when(pl.program_id(1) == 0)
    def _():
        st_ref[...] = jnp.zeros_like(st_ref)

    z = _dot(lr_ref[0].astype(BF16), wup_ref[...]) + bg_ref[...]
    logg = (jnp.minimum(z, 0.0) - jnp.log(1.0 + jnp.exp(-jnp.abs(z)))) * (1.0 / GLA_TAU)
    r = lax.broadcasted_iota(jnp.int32, (TM, TM), 0)
    s = lax.broadcasted_iota(jnp.int32, (TM, TM), 1)
    order = (s >= r) if reverse else (s <= r)
    tri = jnp.where(((r // GLA_CHUNK) == (s // GLA_CHUNK)) & order, 1.0, 0.0).astype(BF16)
    hi, lo = _split(logg)
    bc = _dot(tri, hi) + _dot(tri, lo)
    cmask = order[:GLA_CHUNK, :GLA_CHUNK]

    for c in (range(nchunk - 1, -1, -1) if reverse else range(nchunk)):
        rs = slice(c * GLA_CHUNK, (c + 1) * GLA_CHUNK)
        for hh in range(GLA_HEADS):
            ls = slice(hh * GLA_DK, (hh + 1) * GLA_DK)
            vs = slice(hh * GLA_DV, (hh + 1) * GLA_DV)
            b = bc[rs, ls]
            bend = b[0:1] if reverse else b[GLA_CHUNK - 1:GLA_CHUNK]
            qc = q_ref[0, rs, ls].astype(F32)
            kc = k_ref[0, rs, ls].astype(F32)
            vc = v_ref[0, rs, vs]
            kd = (kc * jnp.exp(bend - b)).astype(BF16)
            qi = (qc * jnp.exp(b)).astype(BF16)
            qa = (qc * jnp.exp(b - bend)).astype(BF16)
            att = jnp.where(cmask, _dg(qa, kd, _NT), 0.0).astype(BF16)
            st = st_ref[hh]
            o = _dg(qi, st.astype(BF16), _NT) + _dot(att, vc)
            st_ref[hh] = st * jnp.exp(bend) + _dg(vc, kd, _TN)
            if finish:
                o = o + ob_ref[0, rs, vs].astype(F32)
                o = o * lax.rsqrt(jnp.mean(o * o, axis=-1, keepdims=True) + EPS) * nw_ref[...]
                g = g_ref[0, rs, vs].astype(F32)
                o = o * (g * _sigmoid(g))
            o_ref[0, rs, vs] = o.astype(o_ref.dtype)


def _gla_pass(reverse, q, k, v, lr, wup, bg, extra=None):
    b, t, _ = q.shape
    nt = t // TM
    if reverse:
        tile = lambda i, j: (i, jnp.where(j == 0, 0, nt - j), 0)
    else:
        tile = lambda i, j: (i, j, 0)
    const = lambda i, j: (0, 0)
    in_specs = [pl.BlockSpec((1, TM, GLA_QK), tile), pl.BlockSpec((1, TM, GLA_QK), tile),
                pl.BlockSpec((1, TM, GLA_V), tile), pl.BlockSpec((1, TM, LR_PAD), tile),
                pl.BlockSpec((LR_PAD, GLA_QK), const), pl.BlockSpec((1, GLA_QK), const)]
    args = [q, k, v, lr, wup, bg]
    if extra is not None:
        ob, g, nw = extra
        in_specs += [pl.BlockSpec((1, TM, GLA_V), tile), pl.BlockSpec((1, TM, GLA_V), tile),
                     pl.BlockSpec((1, GLA_DV), const)]
        args += [ob, g, nw]
    return pl.pallas_call(
        functools.partial(_gla_kernel, reverse, extra is not None),
        grid=(b, nt),
        in_specs=in_specs,
        out_specs=pl.BlockSpec((1, TM, GLA_V), tile),
        out_shape=jax.ShapeDtypeStruct((b, t, GLA_V), BF16),
        scratch_shapes=[pltpu.VMEM((GLA_HEADS, GLA_DV, GLA_DK), F32)],
        compiler_params=_cparams(("arbitrary", "arbitrary")),
        name="gla_bwd" if reverse else "gla_fwd",
    )(*args)


def _na_bias_tables(rpb, rows):
    rpt = TM // GRID_W
    kr = min(WIN_R, rows)
    col = np.arange(GRID_W)
    cidx = np.clip(col[None, :] - col[:, None], -(WIN_C - 1), WIN_C - 1) + (WIN_C - 1)
    c_sel = (cidx[None] == np.arange(2 * WIN_C - 1)[:, None, None]).astype(np.float32)
    c0 = np.clip(col - WIN_C // 2, 0, GRID_W - WIN_C)
    c_ok = (col[None, :] >= c0[:, None]) & (col[None, :] < c0[:, None] + WIN_C)
    toep = jnp.einsum("hrd,dqk->hrqk", rpb.astype(F32), c_sel, precision=lax.Precision.HIGHEST)
    tabs = [jnp.full((NA_HEADS, TM, 3 * TM), NEG, F32)]
    for r_base, u_base in ((0, 0), (rpt, 0), (rows - rpt, rows - 3 * rpt)):
        rq = r_base + np.arange(rpt)[:, None]
        rk = u_base + np.arange(3 * rpt)[None, :]
        r0 = np.clip(rq - kr // 2, 0, rows - kr)
        r_ok = (rk >= r0) & (rk < r0 + kr)
        ridx = np.clip(rk - rq + (WIN_R - 1), 0, 2 * WIN_R - 2)
        r_sel = (ridx[None] == np.arange(2 * WIN_R - 1)[:, None, None]).astype(np.float32)
        bias = jnp.einsum("hrqk,rim->hiqmk", toep, r_sel, precision=lax.Precision.HIGHEST)
        valid = r_ok[:, None, :, None] & c_ok[None, :, None, :]
        tabs.append(jnp.where(valid[None], bias, NEG).reshape(NA_HEADS, TM, 3 * TM))
    return jnp.stack(tabs)


def _na_kernel(q_ref, kp_ref, kc_ref, kn_ref, kx_ref, vp_ref, vc_ref, vn_ref, vx_ref, bias_ref, o_ref):
    lane = lax.broadcasted_iota(jnp.int32, (TM, 2 * NA_HD), 1)
    first = lane < NA_HD
    k_refs = (kp_ref, kc_ref, kn_ref, kx_ref)
    v_refs = (vp_ref, vc_ref, vn_ref, vx_ref)
    for hp in range(NA_HEADS // 2):
        ls = slice(hp * 2 * NA_HD, (hp + 1) * 2 * NA_HD)
        q2 = q_ref[0, :, ls]
        ks = [kr[0, :, ls] for kr in k_refs]
        vs = [vr[0, :, ls] for vr in v_refs]
        outs = []
        for sub in range(2):
            hd = 2 * hp + sub
            qm = jnp.where(first if sub == 0 else jnp.logical_not(first), q2, jnp.zeros_like(q2))
            sc = []
            for i in range(4):
                s = _dg(qm, ks[i], _NT)
                if i < 3:
                    s = s + bias_ref[0, hd, :, i * TM:(i + 1) * TM]
                sc.append(s)
            m = jnp.max(sc[0], axis=-1, keepdims=True)
            for i in range(1, 4):
                m = jnp.maximum(m, jnp.max(sc[i], axis=-1, keepdims=True))
            l = jnp.zeros_like(m)
            acc = jnp.zeros((TM, 2 * NA_HD), F32)
            for i in range(4):
                p = jnp.exp(sc[i] - m)
                l = l + jnp.sum(p, axis=-1, keepdims=True)
                acc = acc + _dot(p.astype(BF16), vs[i])
            outs.append(acc / l)
        o_ref[0, :, ls] = jnp.where(first, outs[0], outs[1]).astype(o_ref.dtype)


def _na(q, k, v, bias):
    b, t, _ = q.shape
    nt = t // TM
    qt = lambda i, j: (i, j, 0)
    ctx = lambda i, j: (i, 0, 0)

    def near(o):
        return lambda i, j: (i, jnp.clip(j, 2, nt - 2) + o, 0)

    def kind(i, j):
        return (jnp.where(j == 0, 0, jnp.where(j == 1, 1, jnp.where(j == nt - 1, 3, 2))), 0, 0, 0)

    blk = lambda f: pl.BlockSpec((1, TM, NA_W), f)
    return pl.pallas_call(
        _na_kernel,
        grid=(b, nt),
        in_specs=[blk(qt), blk(near(-1)), blk(near(0)), blk(near(1)), blk(ctx),
                  blk(near(-1)), blk(near(0)), blk(near(1)), blk(ctx),
                  pl.BlockSpec((1, NA_HEADS, TM, 3 * TM), kind)],
        out_specs=blk(qt),
        out_shape=jax.ShapeDtypeStruct((b, t, NA_W), BF16),
        compiler_params=_cparams(("arbitrary", "arbitrary")),
        name="natten",
    )(q, k, k, k, k, v, v, v, v, bias)


def _route(sel, aff):
    rows = lambda a, e: a[e:e + 1, :]
    gscore = []
    for g in range(N_GROUPS):
        a, b, c, d = (rows(sel, EXPERTS_PER_GROUP * g + i) for i in range(EXPERTS_PER_GROUP))
        hi1, lo1 = jnp.maximum(a, b), jnp.minimum(a, b)
        hi2, lo2 = jnp.maximum(c, d), jnp.minimum(c, d)
        gscore.append(jnp.maximum(hi1, hi2) + jnp.maximum(jnp.minimum(hi1, hi2), jnp.maximum(lo1, lo2)))
    gbest = jnp.zeros_like(gscore[0], dtype=jnp.int32)
    gval = gscore[0]
    for g in range(1, N_GROUPS):
        better = gscore[g] > gval
        gbest = jnp.where(better, g, gbest)
        gval = jnp.where(better, gscore[g], gval)
    cs, ca = [], []
    for i in range(EXPERTS_PER_GROUP):
        s_i, a_i = rows(sel, i), rows(aff, i)
        for g in range(1, N_GROUPS):
            pick = gbest == g
            s_i = jnp.where(pick, rows(sel, EXPERTS_PER_GROUP * g + i), s_i)
            a_i = jnp.where(pick, rows(aff, EXPERTS_PER_GROUP * g + i), a_i)
        cs.append(s_i)
        ca.append(a_i)
    i1 = jnp.zeros_like(gbest)
    v1, w1 = cs[0], ca[0]
    for i in range(1, EXPERTS_PER_GROUP):
        better = cs[i] > v1
        i1 = jnp.where(better, i, i1)
        v1 = jnp.where(better, cs[i], v1)
        w1 = jnp.where(better, ca[i], w1)
    i2 = jnp.full_like(gbest, -1)
    v2 = jnp.full_like(v1, -jnp.inf)
    w2 = jnp.zeros_like(w1)
    for i in range(EXPERTS_PER_GROUP):
        better = (i1 != i) & ((cs[i] > v2) | (i2 < 0))
        i2 = jnp.where(better, i, i2)
        v2 = jnp.where(better, cs[i], v2)
        w2 = jnp.where(better, ca[i], w2)
    tot = w1 + w2
    base = gbest * EXPERTS_PER_GROUP
    return base + i1, base + i2, w1 / tot, w2 / tot


def _merge_kernel(gy_ref, ny_ref, m1_ref, m2_ref, x_ref, mod_ref, nw_ref, wbg_ref, wbn_ref, wo_ref,
                  wr_ref, br_ref, xo_ref, hp_ref, ri_ref, rw_ref):
    a = _dot(gy_ref[0], wbg_ref[...])
    b = _dot(ny_ref[0], wbn_ref[...])
    m = _sigmoid(m1_ref[0].astype(F32)) * a + _sigmoid(m2_ref[0].astype(F32)) * b
    y = _dot(m.astype(BF16), wo_ref[...])
    mod = mod_ref[0, 0]
    xn = x_ref[0] + mod[2:3] * y
    xo_ref[0] = xn
    h2 = _norm_mod(xn, nw_ref[...], mod[3:4], mod[4:5])
    hp_ref[0] = h2
    hh, hl = _split(h2)
    wh, wl = _split(wr_ref[...])
    logit = _dg(wh, hh, _NT) + _dg(wh, hl, _NT) + _dg(wl, hh, _NT)
    aff = _sigmoid(logit)
    i1, i2, w1, w2 = _route(aff + br_ref[...], aff)
    zi = jnp.zeros((6, TM), jnp.int32)
    ri_ref[0] = jnp.concatenate([i1, i2, zi], axis=0)
    rw_ref[0] = jnp.concatenate([w1, w2, zi.astype(F32)], axis=0)


def _merge(gy, ny, m1, m2, x, modl, nw2, wbg, wbn, wo, wrt, brt):
    b, t, d = x.shape
    nt = t // TM
    tile = lambda i, j: (i, j, 0)
    const = lambda i, j: (0, 0)
    return pl.pallas_call(
        _merge_kernel,
        grid=(b, nt),
        in_specs=[pl.BlockSpec((1, TM, GLA_V), tile), pl.BlockSpec((1, TM, NA_W), tile),
                  pl.BlockSpec((1, TM, d), tile), pl.BlockSpec((1, TM, d), tile),
                  pl.BlockSpec((1, TM, d), tile),
                  pl.BlockSpec((1, 1, 8, d), lambda i, j: (i, jnp.minimum(j, 1), 0, 0)),
                  pl.BlockSpec((1, d), const),
                  pl.BlockSpec((GLA_V, d), const), pl.BlockSpec((NA_W, d), const),
                  pl.BlockSpec((d, d), const),
                  pl.BlockSpec((N_EXPERTS, d), const), pl.BlockSpec((N_EXPERTS, TM), const)],
        out_specs=[pl.BlockSpec((1, TM, d), tile), pl.BlockSpec((1, TM, d), tile),
                   pl.BlockSpec((1, 8, TM), lambda i, j: (i, 0, j)),
                   pl.BlockSpec((1, 8, TM), lambda i, j: (i, 0, j))],
        out_shape=[jax.ShapeDtypeStruct((b, t, d), F32), jax.ShapeDtypeStruct((b, t, d), F32),
                   jax.ShapeDtypeStruct((b, 8, t), jnp.int32), jax.ShapeDtypeStruct((b, 8, t), F32)],
        compiler_params=_cparams(("arbitrary", "arbitrary")),
        name="merge_router",
    )(gy, ny, m1, m2, x, modl, nw2, wbg, wbn, wo, wrt, brt)


def _row_copy(src, s, dst, d, sem):
    return pltpu.make_async_copy(src.at[pl.ds(s, 1)], dst.at[pl.ds(d, 1)], sem)


def _dispatch_kernel(dest_ref, hp_ref, xs_in_ref, xs_ref, sem):
    del xs_in_ref

    def issue(r, carry):
        for kk in range(TOP_K):
            _row_copy(hp_ref, r, xs_ref, dest_ref[0, kk, r], sem).start()
        return carry

    def drain(r, carry):
        for kk in range(TOP_K):
            _row_copy(hp_ref, r, xs_ref, dest_ref[0, kk, r], sem).wait()
        return carry

    lax.fori_loop(0, TM, issue, 0)
    lax.fori_loop(0, TM, drain, 0)


def _dispatch(dest3, hp, n_slots):
    n, c = hp.shape
    return pl.pallas_call(
        _dispatch_kernel,
        grid=(n // TM,),
        in_specs=[pl.BlockSpec((1, TOP_K, TM), lambda i: (i, 0, 0), memory_space=pltpu.SMEM),
                  pl.BlockSpec((TM, c), lambda i: (i, 0)), pl.BlockSpec(memory_space=pl.ANY)],
        out_specs=pl.BlockSpec(memory_space=pl.ANY),
        out_shape=jax.ShapeDtypeStruct((n_slots, c), F32),
        scratch_shapes=[pltpu.SemaphoreType.DMA(())],
        input_output_aliases={2: 0},
        compiler_params=_cparams(("arbitrary",)),
        name="moe_dispatch",
    )(dest3, hp, jnp.zeros((n_slots, c), F32))


def _expert_kernel(be_ref, nu_ref, xs_ref, w1_ref, w3_ref, w2_ref, y_ref, w1b, w3b, w2b):
    i = pl.program_id(0)
    used = i < nu_ref[0]
    fresh = jnp.logical_or(i == 0, be_ref[i] != be_ref[jnp.maximum(i - 1, 0)])

    @pl.when(jnp.logical_and(used, fresh))
    def _():
        for src, dst in ((w1_ref, w1b), (w3_ref, w3b), (w2_ref, w2b)):
            def body(r, carry, src=src, dst=dst):
                rows = pl.ds(pl.multiple_of(r * LANES, LANES), LANES)
                dst[rows, :] = src[0, 0, rows, :].astype(BF16)
                return carry
            lax.fori_loop(0, src.shape[2] // LANES, body, 0)

    @pl.when(used)
    def _():
        x = xs_ref[...].astype(BF16)
        a = _dot(x, w1b[...])
        b = _dot(x, w3b[...])
        hmid = (a * _sigmoid(a)) * b
        y_ref[...] = _dot(hmid.astype(BF16), w2b[...])

    @pl.when(jnp.logical_not(used))
    def _():
        y_ref[...] = jnp.zeros_like(y_ref)


def _experts(blk_expert, n_used, xs, w1, w3, w2, layer):
    p, c = xs.shape
    _, _, d, f = w1.shape
    slot = lambda i, be, nu: (jnp.minimum(i, nu[0] - 1), 0)
    wsel = lambda i, be, nu: (layer, be[i], 0, 0)
    return pl.pallas_call(
        _expert_kernel,
        grid_spec=pltpu.PrefetchScalarGridSpec(
            num_scalar_prefetch=2,
            grid=(p // TM,),
            in_specs=[pl.BlockSpec((TM, c), slot), pl.BlockSpec((1, 1, d, f), wsel),
                      pl.BlockSpec((1, 1, d, f), wsel), pl.BlockSpec((1, 1, f, d), wsel)],
            out_specs=pl.BlockSpec((TM, c), lambda i, be, nu: (i, 0)),
            scratch_shapes=[pltpu.VMEM((d, f), BF16), pltpu.VMEM((d, f), BF16), pltpu.VMEM((f, d), BF16)]),
        out_shape=jax.ShapeDtypeStruct((p, c), F32),
        compiler_params=_cparams(("arbitrary",)),
        name="moe_experts",
    )(blk_expert, n_used, xs, w1, w3, w2)


def _combine_kernel(final, dest_ref, y_ref, x_ref, w_ref, mod_ref, *rest):
    if final:
        fw_ref, o_ref, buf, sem = rest
    else:
        o_ref, buf, sem = rest

    def issue(r, carry):
        for kk in range(TOP_K):
            _row_copy(y_ref, dest_ref[0, kk, r], buf.at[kk], r, sem).start()
        return carry

    def drain(r, carry):
        for kk in range(TOP_K):
            _row_copy(y_ref, dest_ref[0, kk, r], buf.at[kk], r, sem).wait()
        return carry

    lax.fori_loop(0, TM, issue, 0)
    lax.fori_loop(0, TM, drain, 0)
    w = w_ref[0]
    y = None
    for kk in range(TOP_K):
        yk = buf[kk] * w[:, kk:kk + 1]
        y = yk if y is None else y + yk
    xn = x_ref[0] + mod_ref[0, 0][5:6] * y
    if final:
        xn = xn * lax.rsqrt(jnp.mean(xn * xn, axis=-1, keepdims=True) + EPS) * fw_ref[...]
    o_ref[0] = xn


def _combine(dest3, y, x, wcol, modl, final_w=None):
    b, t, d = x.shape
    nt = t // TM
    final = final_w is not None
    skip = 1 if final else 0
    tile = lambda i, j: (i, j + skip, 0)
    in_specs = [pl.BlockSpec((1, TOP_K, TM), lambda i, j: (i * nt + j + skip, 0, 0), memory_space=pltpu.SMEM),
                pl.BlockSpec(memory_space=pl.ANY),
                pl.BlockSpec((1, TM, d), tile), pl.BlockSpec((1, TM, TOP_K), tile),
                pl.BlockSpec((1, 1, 8, d), lambda i, j: (i, jnp.minimum(j + skip, 1), 0, 0))]
    args = [dest3, y, x, wcol, modl]
    if final:
        in_specs.append(pl.BlockSpec((1, d), lambda i, j: (0, 0)))
        args.append(final_w)
    return pl.pallas_call(
        functools.partial(_combine_kernel, final),
        grid=(b, nt - skip),
        in_specs=in_specs,
        out_specs=pl.BlockSpec((1, TM, d), lambda i, j: (i, j, 0)),
        out_shape=jax.ShapeDtypeStruct((b, t - skip * TM, d), F32),
        scratch_shapes=[pltpu.VMEM((TOP_K, TM, d), F32), pltpu.SemaphoreType.DMA(())],
        compiler_params=_cparams(("arbitrary", "arbitrary")),
        name="moe_combine_final" if final else "moe_combine",
    )(*args)


def _slot_plan(ridx, n_tok):
    e = ridx.reshape(-1)
    onehot = (e[:, None] == jnp.arange(N_EXPERTS, dtype=jnp.int32)[None, :]).astype(jnp.int32)
    incl = jnp.cumsum(onehot, axis=0)
    rank = jnp.sum((incl - onehot) * onehot, axis=1)
    counts = incl[-1]
    padded = (counts + TM - 1) // TM * TM
    pends = jnp.cumsum(padded)
    dest = (pends - padded)[e] + rank
    n_blocks = (n_tok * TOP_K + N_EXPERTS * (TM - 1) + TM - 1) // TM
    blk_start = jnp.arange(n_blocks, dtype=jnp.int32) * TM
    blk_expert = jnp.minimum(jnp.sum((pends[None, :] <= blk_start[:, None]).astype(jnp.int32), axis=1),
                             N_EXPERTS - 1)
    n_used = (pends[-1] // TM).astype(jnp.int32).reshape(1)
    return dest.astype(jnp.int32), blk_expert.astype(jnp.int32), n_used, n_blocks


def _rope_tables(t_len, ctx_len):
    half = GLA_DK // 2
    inv = ROPE_BASE ** (-np.arange(0, half, 2, dtype=np.float32) / half)
    lane = np.arange(GLA_DK)
    p = np.arange(t_len - ctx_len)
    pos = np.where(lane[None, :] < half, (p // GRID_W)[:, None], (p % GRID_W)[:, None]).astype(np.float32)
    ang = jnp.asarray(pos) * jnp.asarray(inv[(lane % half) % (half // 2)])[None, :]
    cos, sin = jnp.cos(ang), jnp.sin(ang)
    lower = jnp.asarray((lane % half) < half // 2)[None, :]
    pad = lambda a, v: jnp.concatenate([jnp.full((ctx_len, GLA_DK), v, F32), a], axis=0)
    return pad(cos, 1.0), pad(jnp.where(lower, -sin, 0.0), 0.0), pad(jnp.where(lower, 0.0, sin), 0.0)


def kernel(x, c, ctx, c_ctx, w_ada, b_ada, norm1_w, norm2_w, w_in, w_gk_up, b_gk, gla_norm_w, rpb,
           w_bo_gla, w_bo_na, w_out, w_router, b_router, w1, w3, w2, final_norm_w):
    bsz, seq, d = x.shape
    ctx_len = ctx.shape[1]
    depth = w_ada.shape[0]
    assert ctx_len == TM and seq % TM == 0 and TM % GRID_W == 0 and seq // GRID_W >= 3 * (TM // GRID_W)
    t = ctx_len + seq
    nt = t // TM
    n_tok = bsz * t

    xs = jnp.concatenate([ctx, x], axis=1)
    cvec = jnp.zeros((8, d), F32).at[:bsz].set(c).at[bsz].set(c_ctx)
    mods = _modulation(cvec, w_ada, b_ada).reshape(depth, 8, 6, d)
    cos, sa, sb = _rope_tables(t, ctx_len)
    wrt = w_router.T.astype(F32)
    brt = jnp.broadcast_to(b_router.astype(F32)[:, None], (N_EXPERTS, TM))
    gq_end = 2 * GLA_QK + 2 * GLA_V

    out = None
    for l in range(depth):
        mx = jnp.pad(mods[l, :bsz], ((0, 0), (0, 2), (0, 0)))
        mc = jnp.broadcast_to(jnp.pad(mods[l, bsz], ((0, 2), (0, 0)))[None], mx.shape)
        modl = jnp.stack([mc, mx], axis=1)
        wl = w_in[l]
        wp = jnp.concatenate([wl[:, :gq_end], wl[:, gq_end + 2 * GLA_LR:], wl[:, gq_end:gq_end + 2 * GLA_LR],
                              jnp.zeros((d, LR_PAD - 2 * GLA_LR), wl.dtype)], axis=1).astype(BF16)
        gq, gk, gv, gg, nq, nk, nv, m1, m2, lr = _inproj(xs, modl, norm1_w[l][None], wp, cos, sa, sb)

        def wup(dirn):
            w = jnp.zeros((LR_PAD, GLA_QK), F32)
            return w.at[dirn * GLA_LR:(dirn + 1) * GLA_LR].set(w_gk_up[l, dirn]).astype(BF16)

        ob = _gla_pass(True, gq, gk, gv, lr, wup(1), b_gk[l, 1][None])
        gy = _gla_pass(False, gq, gk, gv, lr, wup(0), b_gk[l, 0][None],
                       extra=(ob, gg, gla_norm_w[l][None]))
        ny = _na(nq, nk, nv, _na_bias_tables(rpb[l], seq // GRID_W))
        xs, hp, ridx, rw = _merge(gy, ny, m1, m2, xs, modl, norm2_w[l][None],
                                  w_bo_gla[l].astype(BF16), w_bo_na[l].astype(BF16), w_out[l].astype(BF16),
                                  wrt, brt)
        ridx2 = jnp.transpose(ridx[:, :TOP_K, :], (0, 2, 1))
        wcol = jnp.transpose(rw[:, :TOP_K, :], (0, 2, 1))
        dest, blk_expert, n_used, n_blocks = _slot_plan(ridx2, n_tok)
        dest3 = jnp.transpose(dest.reshape(bsz * nt, TM, TOP_K), (0, 2, 1))
        xsl = _dispatch(dest3, hp.reshape(n_tok, d), n_blocks * TM)
        y = _experts(blk_expert, n_used, xsl, w1, w3, w2, l)
        if l < depth - 1:
            xs = _combine(dest3, y, xs, wcol, modl)
        else:
            out = _combine(dest3, y, xs, wcol, modl, final_norm_w[None])
    return out
```

```python
import functools

import jax
import jax.numpy as jnp
import numpy as np
from jax import lax
from jax.experimental import pallas as pl
from jax.experimental.pallas import tpu as pltpu

F32 = jnp.float32
BF16 = jnp.bfloat16

EPS = 1e-6
GRID_W = 64
GLA_HEADS = 4
GLA_DK = 128
GLA_DV = 256
GLA_LR = 16
GLA_TAU = 16.0
GLA_CHUNK = 64
ROPE_BASE = 10000.0
NA_HEADS = 8
NA_HD = 64
WIN_R = 8
WIN_C = 16
N_EXPERTS = 16
N_GROUPS = 4
EXPERTS_PER_GROUP = N_EXPERTS // N_GROUPS
TOP_K = 2

GLA_QK = GLA_HEADS * GLA_DK
GLA_V = GLA_HEADS * GLA_DV
NA_W = NA_HEADS * NA_HD

TM = 256
LANES = 128
LR_PAD = LANES
NEG = -1e30
VMEM_LIMIT = 56 * 1024 * 1024

_NT = (((1,), (1,)), ((), ()))
_TN = (((0,), (0,)), ((), ()))


def _cparams(sem):
    return pltpu.CompilerParams(dimension_semantics=sem, vmem_limit_bytes=VMEM_LIMIT)


def _dot(a, b):
    return jnp.dot(a, b, preferred_element_type=F32)


def _dg(a, b, dims):
    return lax.dot_general(a, b, dims, preferred_element_type=F32)


def _split(a):
    hi = a.astype(BF16)
    lo = (a - hi.astype(F32)).astype(BF16)
    return hi, lo


def _sigmoid(x):
    return 1.0 / (1.0 + jnp.exp(-x))


def _norm_mod(x, w, shift, scale):
    y = x * lax.rsqrt(jnp.mean(x * x, axis=-1, keepdims=True) + EPS)
    return (y * w) * (1.0 + scale) + shift


def _mod_kernel(c_ref, w_ref, b_ref, o_ref):
    c = c_ref[...]
    s = c * _sigmoid(c)
    sh, sl = _split(s)
    wh, wl = _split(w_ref[0])
    o_ref[0] = _dot(sh, wh) + _dot(sl, wh) + _dot(sh, wl) + b_ref[0]


def _modulation(cvec, w_ada, b_ada):
    depth, d, n = w_ada.shape
    tn = 1024
    return pl.pallas_call(
        _mod_kernel,
        grid=(depth, n // tn),
        in_specs=[pl.BlockSpec((8, d), lambda l, j: (0, 0)),
                  pl.BlockSpec((1, d, tn), lambda l, j: (l, 0, j)),
                  pl.BlockSpec((1, 1, tn), lambda l, j: (l, 0, j))],
        out_specs=pl.BlockSpec((1, 8, tn), lambda l, j: (l, 0, j)),
        out_shape=jax.ShapeDtypeStruct((depth, 8, n), F32),
        compiler_params=_cparams(("arbitrary", "arbitrary")),
        name="modulation",
    )(cvec, w_ada, b_ada.reshape(depth, 1, n))


_IN_OUT = (("gq", GLA_QK, BF16), ("gk", GLA_QK, BF16), ("gv", GLA_V, BF16), ("gg", GLA_V, BF16),
           ("nq", NA_W, BF16), ("nk", NA_W, BF16), ("nv", NA_W, BF16),
           ("m1", None, BF16), ("m2", None, BF16), ("lr", LR_PAD, F32))


def _inproj_kernel(x_ref, mod_ref, nw_ref, w_ref, cos_ref, sa_ref, sb_ref, *o_refs):
    d = x_ref.shape[2]
    mod = mod_ref[0, 0]
    h = _norm_mod(x_ref[0], nw_ref[...], mod[0:1], mod[1:2]).astype(BF16)
    cos, sa, sb = cos_ref[...], sa_ref[...], sb_ref[...]

    def rope(r, scale):
        parts = []
        for hh in range(GLA_HEADS):
            xs = r[:, hh * GLA_DK:(hh + 1) * GLA_DK]
            y = xs * cos + pltpu.roll(xs, GLA_DK - 32, 1) * sa + pltpu.roll(xs, 32, 1) * sb
            parts.append(y * scale if scale != 1.0 else y)
        return jnp.concatenate(parts, axis=1)

    off = 0
    for (name, width, dt), o_ref in zip(_IN_OUT, o_refs):
        width = d if width is None else width
        r = _dot(h, w_ref[:, off:off + width])
        if name == "gq":
            r = rope(r, GLA_DK ** -0.5)
        elif name == "gk":
            r = rope(r, 1.0)
        elif name == "nq":
            r = r * (NA_HD ** -0.5)
        o_ref[0] = r.astype(dt)
        off += width


def _inproj(x, modl, nw, wp, cos, sa, sb):
    b, t, d = x.shape
    nt = t // TM
    widths = [d if w is None else w for _, w, _ in _IN_OUT]
    ncols = sum(widths)
    return pl.pallas_call(
        _inproj_kernel,
        grid=(b, nt),
        in_specs=[pl.BlockSpec((1, TM, d), lambda i, j: (i, j, 0)),
                  pl.BlockSpec((1, 1, 8, d), lambda i, j: (i, jnp.minimum(j, 1), 0, 0)),
                  pl.BlockSpec((1, d), lambda i, j: (0, 0)),
                  pl.BlockSpec((d, ncols), lambda i, j: (0, 0)),
                  pl.BlockSpec((TM, GLA_DK), lambda i, j: (j, 0)),
                  pl.BlockSpec((TM, GLA_DK), lambda i, j: (j, 0)),
                  pl.BlockSpec((TM, GLA_DK), lambda i, j: (j, 0))],
        out_specs=[pl.BlockSpec((1, TM, w), lambda i, j: (i, j, 0)) for w in widths],
        out_shape=[jax.ShapeDtypeStruct((b, t, w), dt) for w, (_, _, dt) in zip(widths, _IN_OUT)],
        compiler_params=_cparams(("arbitrary", "arbitrary")),
        name="inproj",
    )(x, modl, nw, wp, cos, sa, sb)


def _gla_kernel(reverse, finish, q_ref, k_ref, v_ref, lr_ref, wup_ref, bg_ref, *rest):
    if finish:
        ob_ref, g_ref, nw_ref, o_ref, st_ref = rest
    else:
        o_ref, st_ref = rest
    nchunk = TM // GLA_CHUNK

    @pl.when(pl.program_id(1) == 0)
    def _():
        st_ref[...] = jnp.zeros_like(st_ref)

    z = _dot(lr_ref[0].astype(BF16), wup_ref[...]) + bg_ref[...]
    logg = (jnp.minimum(z, 0.0) - jnp.log(1.0 + jnp.exp(-jnp.abs(z)))) * (1.0 / GLA_TAU)
    r = lax.broadcasted_iota(jnp.int32, (TM, TM), 0)
    s = lax.broadcasted_iota(jnp.int32, (TM, TM), 1)
    order = (s >= r) if reverse else (s <= r)
    tri = jnp.where(((r // GLA_CHUNK) == (s // GLA_CHUNK)) & order, 1.0, 0.0).astype(BF16)
    hi, lo = _split(logg)
    bc = _dot(tri, hi) + _dot(tri, lo)
    cmask = order[:GLA_CHUNK, :GLA_CHUNK]

    for c in (range(nchunk - 1, -1, -1) if reverse else range(nchunk)):
        rs = slice(c * GLA_CHUNK, (c + 1) * GLA_CHUNK)
        for hh in range(GLA_HEADS):
            ls = slice(hh * GLA_DK, (hh + 1) * GLA_DK)
            vs = slice(hh * GLA_DV, (hh + 1) * GLA_DV)
            b = bc[rs, ls]
            bend = b[0:1] if reverse else b[GLA_CHUNK - 1:GLA_CHUNK]
            qc = q_ref[0, rs, ls].astype(F32)
            kc = k_ref[0, rs, ls].astype(F32)
            vc = v_ref[0, rs, vs]
            kd = (kc * jnp.exp(bend - b)).astype(BF16)
            qi = (qc * jnp.exp(b)).astype(BF16)
            qa = (qc * jnp.exp(b - bend)).astype(BF16)
            att = jnp.where(cmask, _dg(qa, kd, _NT), 0.0).astype(BF16)
            st = st_ref[hh]
            o = _dg(qi, st.astype(BF16), _NT) + _dot(att, vc)
            st_ref[hh] = st * jnp.exp(bend) + _dg(vc, kd, _TN)
            if finish:
                o = o + ob_ref[0, rs, vs].astype(F32)
                o = o * lax.rsqrt(jnp.mean(o * o, axis=-1, keepdims=True) + EPS) * nw_ref[...]
                g = g_ref[0, rs, vs].astype(F32)
                o = o * (g * _sigmoid(g))
            o_ref[0, rs, vs] = o.astype(o_ref.dtype)


def _gla_pass(reverse, q, k, v, lr, wup, bg, extra=None):
    b, t, _ = q.shape
    nt = t // TM
    if reverse:
        tile = lambda i, j: (i, jnp.where(j == 0, 0, nt - j), 0)
    else:
        tile = lambda i, j: (i, j, 0)
    const = lambda i, j: (0, 0)
    in_specs = [pl.BlockSpec((1, TM, GLA_QK), tile), pl.BlockSpec((1, TM, GLA_QK), tile),
                pl.BlockSpec((1, TM, GLA_V), tile), pl.BlockSpec((1, TM, LR_PAD), tile),
                pl.BlockSpec((LR_PAD, GLA_QK), const), pl.BlockSpec((1, GLA_QK), const)]
    args = [q, k, v, lr, wup, bg]
    if extra is not None:
        ob, g, nw = extra
        in_specs += [pl.BlockSpec((1, TM, GLA_V), tile), pl.BlockSpec((1, TM, GLA_V), tile),
                     pl.BlockSpec((1, GLA_DV), const)]
        args += [ob, g, nw]
    return pl.pallas_call(
        functools.partial(_gla_kernel, reverse, extra is not None),
        grid=(b, nt),
        in_specs=in_specs,
        out_specs=pl.BlockSpec((1, TM, GLA_V), tile),
        out_shape=jax.ShapeDtypeStruct((b, t, GLA_V), BF16),
        scratch_shapes=[pltpu.VMEM((GLA_HEADS, GLA_DV, GLA_DK), F32)],
        compiler_params=_cparams(("arbitrary", "arbitrary")),
        name="gla_bwd" if reverse else "gla_fwd",
    )(*args)


def _na_bias_tables(rpb, rows):
    rpt = TM // GRID_W
    kr = min(WIN_R, rows)
    col = np.arange(GRID_W)
    cidx = np.clip(col[None, :] - col[:, None], -(WIN_C - 1), WIN_C - 1) + (WIN_C - 1)
    c_sel = (cidx[None] == np.arange(2 * WIN_C - 1)[:, None, None]).astype(np.float32)
    c0 = np.clip(col - WIN_C // 2, 0, GRID_W - WIN_C)
    c_ok = (col[None, :] >= c0[:, None]) & (col[None, :] < c0[:, None] + WIN_C)
    toep = jnp.einsum("hrd,dqk->hrqk", rpb.astype(F32), c_sel, precision=lax.Precision.HIGHEST)
    tabs = [jnp.full((NA_HEADS, TM, 3 * TM), NEG, F32)]
    for r_base, u_base in ((0, 0), (rpt, 0), (rows - rpt, rows - 3 * rpt)):
        rq = r_base + np.arange(rpt)[:, None]
        rk = u_base + np.arange(3 * rpt)[None, :]
        r0 = np.clip(rq - kr // 2, 0, rows - kr)
        r_ok = (rk >= r0) & (rk < r0 + kr)
        ridx = np.clip(rk - rq + (WIN_R - 1), 0, 2 * WIN_R - 2)
        r_sel = (ridx[None] == np.arange(2 * WIN_R - 1)[:, None, None]).astype(np.float32)
        bias = jnp.einsum("hrqk,rim->hiqmk", toep, r_sel, precision=lax.Precision.HIGHEST)
        valid = r_ok[:, None, :, None] & c_ok[None, :, None, :]
        tabs.append(jnp.where(valid[None], bias, NEG).reshape(NA_HEADS, TM, 3 * TM))
    return jnp.stack(tabs)


def _na_kernel(q_ref, kp_ref, kc_ref, kn_ref, kx_ref, vp_ref, vc_ref, vn_ref, vx_ref, bias_ref, o_ref):
    lane = lax.broadcasted_iota(jnp.int32, (TM, 2 * NA_HD), 1)
    first = lane < NA_HD
    k_refs = (kp_ref, kc_ref, kn_ref, kx_ref)
    v_refs = (vp_ref, vc_ref, vn_ref, vx_ref)
    for hp in range(NA_HEADS // 2):
        ls = slice(hp * 2 * NA_HD, (hp + 1) * 2 * NA_HD)
        q2 = q_ref[0, :, ls]
        ks = [kr[0, :, ls] for kr in k_refs]
        vs = [vr[0, :, ls] for vr in v_refs]
        outs = []
        for sub in range(2):
            hd = 2 * hp + sub
            qm = jnp.where(first if sub == 0 else jnp.logical_not(first), q2, jnp.zeros_like(q2))
            sc = []
            for i in range(4):
                s = _dg(qm, ks[i], _NT)
                if i < 3:
                    s = s + bias_ref[0, hd, :, i * TM:(i + 1) * TM]
                sc.append(s)
            m = jnp.max(sc[0], axis=-1, keepdims=True)
            for i in range(1, 4):
                m = jnp.maximum(m, jnp.max(sc[i], axis=-1, keepdims=True))
            l = jnp.zeros_like(m)
            acc = jnp.zeros((TM, 2 * NA_HD), F32)
            for i in range(4):
                p = jnp.exp(sc[i] - m)
                l = l + jnp.sum(p, axis=-1, keepdims=True)
                acc = acc + _dot(p.astype(BF16), vs[i])
            outs.append(acc / l)
        o_ref[0, :, ls] = jnp.where(first, outs[0], outs[1]).astype(o_ref.dtype)


def _na(q, k, v, bias):
    b, t, _ = q.shape
    nt = t // TM
    qt = lambda i, j: (i, j, 0)
    ctx = lambda i, j: (i, 0, 0)

    def near(o):
        return lambda i, j: (i, jnp.clip(j, 2, nt - 2) + o, 0)

    def kind(i, j):
        return (jnp.where(j == 0, 0, jnp.where(j == 1, 1, jnp.where(j == nt - 1, 3, 2))), 0, 0, 0)

    blk = lambda f: pl.BlockSpec((1, TM, NA_W), f)
    return pl.pallas_call(
        _na_kernel,
        grid=(b, nt),
        in_specs=[blk(qt), blk(near(-1)), blk(near(0)), blk(near(1)), blk(ctx),
                  blk(near(-1)), blk(near(0)), blk(near(1)), blk(ctx),
                  pl.BlockSpec((1, NA_HEADS, TM, 3 * TM), kind)],
        out_specs=blk(qt),
        out_shape=jax.ShapeDtypeStruct((b, t, NA_W), BF16),
        compiler_params=_cparams(("arbitrary", "arbitrary")),
        name="natten",
    )(q, k, k, k, k, v, v, v, v, bias)


def _route(sel, aff):
    rows = lambda a, e: a[e:e + 1, :]
    gscore = []
    for g in range(N_GROUPS):
        a, b, c, d = (rows(sel, EXPERTS_PER_GROUP * g + i) for i in range(EXPERTS_PER_GROUP))
        hi1, lo1 = jnp.maximum(a, b), jnp.minimum(a, b)
        hi2, lo2 = jnp.maximum(c, d), jnp.minimum(c, d)
        gscore.append(jnp.maximum(hi1, hi2) + jnp.maximum(jnp.minimum(hi1, hi2), jnp.maximum(lo1, lo2)))
    gbest = jnp.zeros_like(gscore[0], dtype=jnp.int32)
    gval = gscore[0]
    for g in range(1, N_GROUPS):
        better = gscore[g] > gval
        gbest = jnp.where(better, g, gbest)
        gval = jnp.where(better, gscore[g], gval)
    cs, ca = [], []
    for i in range(EXPERTS_PER_GROUP):
        s_i, a_i = rows(sel, i), rows(aff, i)
        for g in range(1, N_GROUPS):
            pick = gbest == g
            s_i = jnp.where(pick, rows(sel, EXPERTS_PER_GROUP * g + i), s_i)
            a_i = jnp.where(pick, rows(aff, EXPERTS_PER_GROUP * g + i), a_i)
        cs.append(s_i)
        ca.append(a_i)
    i1 = jnp.zeros_like(gbest)
    v1, w1 = cs[0], ca[0]
    for i in range(1, EXPERTS_PER_GROUP):
        better = cs[i] > v1
        i1 = jnp.where(better, i, i1)
        v1 = jnp.where(better, cs[i], v1)
        w1 = jnp.where(better, ca[i], w1)
    i2 = jnp.full_like(gbest, -1)
    v2 = jnp.full_like(v1, -jnp.inf)
    w2 = jnp.zeros_like(w1)
    for i in range(EXPERTS_PER_GROUP):
        better = (i1 != i) & ((cs[i] > v2) | (i2 < 0))
        i2 = jnp.where(better, i, i2)
        v2 = jnp.where(better, cs[i], v2)
        w2 = jnp.where(better, ca[i], w2)
    tot = w1 + w2
    base = gbest * EXPERTS_PER_GROUP
    return base + i1, base + i2, w1 / tot, w2 / tot


def _merge_kernel(gy_ref, ny_ref, m1_ref, m2_ref, x_ref, mod_ref, nw_ref, wbg_ref, wbn_ref, wo_ref,
                  wr_ref, br_ref, xo_ref, hp_ref, ri_ref, rw_ref):
    a = _dot(gy_ref[0], wbg_ref[...])
    b = _dot(ny_ref[0], wbn_ref[...])
    m = _sigmoid(m1_ref[0].astype(F32)) * a + _sigmoid(m2_ref[0].astype(F32)) * b
    y = _dot(m.astype(BF16), wo_ref[...])
    mod = mod_ref[0, 0]
    xn = x_ref[0] + mod[2:3] * y
    xo_ref[0] = xn
    h2 = _norm_mod(xn, nw_ref[...], mod[3:4], mod[4:5])
    hp_ref[0] = h2
    hh, hl = _split(h2)
    wh, wl = _split(wr_ref[...])
    logit = _dg(wh, hh, _NT) + _dg(wh, hl, _NT) + _dg(wl, hh, _NT)
    aff = _sigmoid(logit)
    i1, i2, w1, w2 = _route(aff + br_ref[...], aff)
    zi = jnp.zeros((6, TM), jnp.int32)
    ri_ref[0] = jnp.concatenate([i1, i2, zi], axis=0)
    rw_ref[0] = jnp.concatenate([w1, w2, zi.astype(F32)], axis=0)


def _merge(gy, ny, m1, m2, x, modl, nw2, wbg, wbn, wo, wrt, brt):
    b, t, d = x.shape
    nt = t // TM
    tile = lambda i, j: (i, j, 0)
    const = lambda i, j: (0, 0)
    return pl.pallas_call(
        _merge_kernel,
        grid=(b, nt),
        in_specs=[pl.BlockSpec((1, TM, GLA_V), tile), pl.BlockSpec((1, TM, NA_W), tile),
                  pl.BlockSpec((1, TM, d), tile), pl.BlockSpec((1, TM, d), tile),
                  pl.BlockSpec((1, TM, d), tile),
                  pl.BlockSpec((1, 1, 8, d), lambda i, j: (i, jnp.minimum(j, 1), 0, 0)),
                  pl.BlockSpec((1, d), const),
                  pl.BlockSpec((GLA_V, d), const), pl.BlockSpec((NA_W, d), const),
                  pl.BlockSpec((d, d), const),
                  pl.BlockSpec((N_EXPERTS, d), const), pl.BlockSpec((N_EXPERTS, TM), const)],
        out_specs=[pl.BlockSpec((1, TM, d), tile), pl.BlockSpec((1, TM, d), tile),
                   pl.BlockSpec((1, 8, TM), lambda i, j: (i, 0, j)),
                   pl.BlockSpec((1, 8, TM), lambda i, j: (i, 0, j))],
        out_shape=[jax.ShapeDtypeStruct((b, t, d), F32), jax.ShapeDtypeStruct((b, t, d), F32),
                   jax.ShapeDtypeStruct((b, 8, t), jnp.int32), jax.ShapeDtypeStruct((b, 8, t), F32)],
        compiler_params=_cparams(("arbitrary", "arbitrary")),
        name="merge_router",
    )(gy, ny, m1, m2, x, modl, nw2, wbg, wbn, wo, wrt, brt)


def _moe_kernel(be_ref, nu_ref, nv_ref, src_cur, src_nxt, dst_cur, dst_prv, h_ref, w1_ref, w3_ref, w2_ref,
                o_ref, xbuf, ybuf, w1b, w3b, w2b, gsem, ssem):
    i = pl.program_id(0)
    nu = nu_ref[0]
    buf = lax.rem(i, 2)
    used = i < nu

    def gather(idx_ref, blk, b, start):
        def body(r, carry):
            cp = pltpu.make_async_copy(h_ref.at[pl.ds(idx_ref[0, 0, r], 1)], xbuf.at[b, pl.ds(r, 1)], gsem.at[b])
            cp.start() if start else cp.wait()
            return carry
        lax.fori_loop(0, nv_ref[blk], body, 0)

    def scatter(idx_ref, blk, b, start):
        def body(r, carry):
            cp = pltpu.make_async_copy(ybuf.at[b, pl.ds(r, 1)], o_ref.at[pl.ds(idx_ref[0, 0, r], 1)], ssem.at[b])
            cp.start() if start else cp.wait()
            return carry
        lax.fori_loop(0, nv_ref[blk], body, 0)

    @pl.when(i == 0)
    def _():
        xbuf[...] = jnp.zeros_like(xbuf)
        gather(src_cur, i, buf, True)

    @pl.when(used)
    def _():
        gather(src_cur, i, buf, False)

    @pl.when(i + 1 < nu)
    def _():
        gather(src_nxt, i + 1, 1 - buf, True)

    fresh = jnp.logical_or(i == 0, be_ref[i] != be_ref[jnp.maximum(i - 1, 0)])

    @pl.when(jnp.logical_and(used, fresh))
    def _():
        for src, dst in ((w1_ref, w1b), (w3_ref, w3b), (w2_ref, w2b)):
            def body(r, carry, src=src, dst=dst):
                rows = pl.ds(pl.multiple_of(r * LANES, LANES), LANES)
                dst[rows, :] = src[0, 0, rows, :].astype(BF16)
                return carry
            lax.fori_loop(0, src.shape[2] // LANES, body, 0)

    @pl.when(used)
    def _():
        x = xbuf[buf].astype(BF16)
        a = _dot(x, w1b[...])
        b = _dot(x, w3b[...])
        hmid = (a * _sigmoid(a)) * b
        ybuf[buf] = _dot(hmid.astype(BF16), w2b[...])
        scatter(dst_cur, i, buf, True)

    @pl.when(jnp.logical_and(used, i > 0))
    def _():
        scatter(dst_prv, i - 1, 1 - buf, False)

    @pl.when(i == nu - 1)
    def _():
        scatter(dst_cur, i, buf, False)


def _moe(blk_expert, n_used, n_valid, slot_src, slot_dst, h, w1, w3, w2, layer):
    n, d = h.shape
    f = w1.shape[3]
    nb = blk_expert.shape[0]
    wsel = lambda i, be, nu, nv: (layer, be[i], 0, 0)
    smem = lambda f_: pl.BlockSpec((1, 1, TM), f_, memory_space=pltpu.SMEM)
    cur = lambda i, be, nu, nv: (i, 0, 0)
    nxt = lambda i, be, nu, nv: (jnp.minimum(i + 1, nb - 1), 0, 0)
    prv = lambda i, be, nu, nv: (jnp.maximum(i - 1, 0), 0, 0)
    return pl.pallas_call(
        _moe_kernel,
        grid_spec=pltpu.PrefetchScalarGridSpec(
            num_scalar_prefetch=3,
            grid=(nb,),
            in_specs=[smem(cur), smem(nxt), smem(cur), smem(prv),
                      pl.BlockSpec(memory_space=pl.ANY), pl.BlockSpec((1, 1, d, f), wsel),
                      pl.BlockSpec((1, 1, d, f), wsel), pl.BlockSpec((1, 1, f, d), wsel)],
            out_specs=pl.BlockSpec(memory_space=pl.ANY),
            scratch_shapes=[pltpu.VMEM((2, TM, d), F32), pltpu.VMEM((2, TM, d), F32),
                            pltpu.VMEM((d, f), BF16), pltpu.VMEM((d, f), BF16), pltpu.VMEM((f, d), BF16),
                            pltpu.SemaphoreType.DMA((2,)), pltpu.SemaphoreType.DMA((2,))]),
        out_shape=jax.ShapeDtypeStruct((TOP_K * n, d), F32),
        compiler_params=_cparams(("arbitrary",)),
        name="moe_experts",
    )(blk_expert, n_used, n_valid, slot_src, slot_src, slot_dst, slot_dst, h, w1, w3, w2)


def _combine_kernel(final, y_ref, x_ref, w_ref, mod_ref, *rest):
    if final:
        fw_ref, o_ref = rest
    else:
        (o_ref,) = rest
    w = w_ref[0]
    y = None
    for kk in range(TOP_K):
        yk = y_ref[kk, 0] * w[:, kk:kk + 1]
        y = yk if y is None else y + yk
    xn = x_ref[0] + mod_ref[0, 0][5:6] * y
    if final:
        xn = xn * lax.rsqrt(jnp.mean(xn * xn, axis=-1, keepdims=True) + EPS) * fw_ref[...]
    o_ref[0] = xn


def _combine(y, x, wcol, modl, final_w=None):
    b, t, d = x.shape
    nt = t // TM
    final = final_w is not None
    skip = 1 if final else 0
    tile = lambda i, j: (i, j + skip, 0)
    in_specs = [pl.BlockSpec((TOP_K, 1, TM, d), lambda i, j: (0, i, j + skip, 0)),
                pl.BlockSpec((1, TM, d), tile), pl.BlockSpec((1, TM, TOP_K), tile),
                pl.BlockSpec((1, 1, 8, d), lambda i, j: (i, jnp.minimum(j + skip, 1), 0, 0))]
    args = [y.reshape(TOP_K, b, t, d), x, wcol, modl]
    if final:
        in_specs.append(pl.BlockSpec((1, d), lambda i, j: (0, 0)))
        args.append(final_w)
    return pl.pallas_call(
        functools.partial(_combine_kernel, final),
        grid=(b, nt - skip),
        in_specs=in_specs,
        out_specs=pl.BlockSpec((1, TM, d), lambda i, j: (i, j, 0)),
        out_shape=jax.ShapeDtypeStruct((b, t - skip * TM, d), F32),
        compiler_params=_cparams(("arbitrary", "arbitrary")),
        name="moe_combine_final" if final else "moe_combine",
    )(*args)


def _slot_plan(ridx, n_tok):
    e = ridx.reshape(-1)
    onehot = (e[:, None] == jnp.arange(N_EXPERTS, dtype=jnp.int32)[None, :]).astype(jnp.int32)
    incl = jnp.cumsum(onehot, axis=0)
    rank = jnp.sum((incl - onehot) * onehot, axis=1)
    counts = incl[-1]
    padded = (counts + TM - 1) // TM * TM
    pends = jnp.cumsum(padded)
    pstarts = pends - padded
    dest = pstarts[e] + rank
    n_blocks = (n_tok * TOP_K + N_EXPERTS * (TM - 1) + TM - 1) // TM
    blk_start = jnp.arange(n_blocks, dtype=jnp.int32) * TM
    blk_expert = jnp.minimum(jnp.sum((pends[None, :] <= blk_start[:, None]).astype(jnp.int32), axis=1),
                             N_EXPERTS - 1)
    n_valid = jnp.clip(counts[blk_expert] - (blk_start - pstarts[blk_expert]), 0, TM)
    n_used = (pends[-1] // TM).astype(jnp.int32).reshape(1)
    pair = jnp.arange(n_tok * TOP_K, dtype=jnp.int32)
    row = (pair % TOP_K) * n_tok + pair // TOP_K
    slot_dst = jnp.zeros((n_blocks * TM,), jnp.int32).at[dest].set(row, unique_indices=True)
    slot_src = slot_dst % n_tok
    shape3 = (n_blocks, 1, TM)
    return (blk_expert.astype(jnp.int32), n_used, n_valid.astype(jnp.int32),
            slot_src.reshape(shape3), slot_dst.reshape(shape3))


def _rope_tables(t_len, ctx_len):
    half = GLA_DK // 2
    inv = ROPE_BASE ** (-np.arange(0, half, 2, dtype=np.float32) / half)
    lane = np.arange(GLA_DK)
    p = np.arange(t_len - ctx_len)
    pos = np.where(lane[None, :] < half, (p // GRID_W)[:, None], (p % GRID_W)[:, None]).astype(np.float32)
    ang = jnp.asarray(pos) * jnp.asarray(inv[(lane % half) % (half // 2)])[None, :]
    cos, sin = jnp.cos(ang), jnp.sin(ang)
    lower = jnp.asarray((lane % half) < half // 2)[None, :]
    pad = lambda a, v: jnp.concatenate([jnp.full((ctx_len, GLA_DK), v, F32), a], axis=0)
    return pad(cos, 1.0), pad(jnp.where(lower, -sin, 0.0), 0.0), pad(jnp.where(lower, 0.0, sin), 0.0)


def kernel(x, c, ctx, c_ctx, w_ada, b_ada, norm1_w, norm2_w, w_in, w_gk_up, b_gk, gla_norm_w, rpb,
           w_bo_gla, w_bo_na, w_out, w_router, b_router, w1, w3, w2, final_norm_w):
    bsz, seq, d = x.shape
    ctx_len = ctx.shape[1]
    depth = w_ada.shape[0]
    assert ctx_len == TM and seq % TM == 0 and TM % GRID_W == 0 and seq // GRID_W >= 3 * (TM // GRID_W)
    t = ctx_len + seq
    nt = t // TM
    n_tok = bsz * t

    xs = jnp.concatenate([ctx, x], axis=1)
    cvec = jnp.zeros((8, d), F32).at[:bsz].set(c).at[bsz].set(c_ctx)
    mods = _modulation(cvec, w_ada, b_ada).reshape(depth, 8, 6, d)
    cos, sa, sb = _rope_tables(t, ctx_len)
    wrt = w_router.T.astype(F32)
    brt = jnp.broadcast_to(b_router.astype(F32)[:, None], (N_EXPERTS, TM))
    gq_end = 2 * GLA_QK + 2 * GLA_V

    out = None
    for l in range(depth):
        mx = jnp.pad(mods[l, :bsz], ((0, 0), (0, 2), (0, 0)))
        mc = jnp.broadcast_to(jnp.pad(mods[l, bsz], ((0, 2), (0, 0)))[None], mx.shape)
        modl = jnp.stack([mc, mx], axis=1)
        wl = w_in[l]
        wp = jnp.concatenate([wl[:, :gq_end], wl[:, gq_end + 2 * GLA_LR:], wl[:, gq_end:gq_end + 2 * GLA_LR],
                              jnp.zeros((d, LR_PAD - 2 * GLA_LR), wl.dtype)], axis=1).astype(BF16)
        gq, gk, gv, gg, nq, nk, nv, m1, m2, lr = _inproj(xs, modl, norm1_w[l][None], wp, cos, sa, sb)

        def wup(dirn):
            w = jnp.zeros((LR_PAD, GLA_QK), F32)
            return w.at[dirn * GLA_LR:(dirn + 1) * GLA_LR].set(w_gk_up[l, dirn]).astype(BF16)

        ob = _gla_pass(True, gq, gk, gv, lr, wup(1), b_gk[l, 1][None])
        gy = _gla_pass(False, gq, gk, gv, lr, wup(0), b_gk[l, 0][None],
                       extra=(ob, gg, gla_norm_w[l][None]))
        ny = _na(nq, nk, nv, _na_bias_tables(rpb[l], seq // GRID_W))
        xs, hp, ridx, rw = _merge(gy, ny, m1, m2, xs, modl, norm2_w[l][None],
                                  w_bo_gla[l].astype(BF16), w_bo_na[l].astype(BF16), w_out[l].astype(BF16),
                                  wrt, brt)
        ridx2 = jnp.transpose(ridx[:, :TOP_K, :], (0, 2, 1))
        wcol = jnp.transpose(rw[:, :TOP_K, :], (0, 2, 1))
        blk_expert, n_used, n_valid, slot_src, slot_dst = _slot_plan(ridx2, n_tok)
        y = _moe(blk_expert, n_used, n_valid, slot_src, slot_dst, hp.reshape(n_tok, d), w1, w3, w2, l)
        if l < depth - 1:
            xs = _combine(y, xs, wcol, modl)
        else:
            out = _combine(y, xs, wcol, modl, final_norm_w[None])
    return out
```

```python
import functools

import jax
import jax.numpy as jnp
import numpy as np
from jax import lax
from jax.experimental import pallas as pl
from jax.experimental.pallas import tpu as pltpu

F32 = jnp.float32
BF16 = jnp.bfloat16

EPS = 1e-6
GRID_W = 64
GLA_HEADS = 4
GLA_DK = 128
GLA_DV = 256
GLA_LR = 16
GLA_TAU = 16.0
GLA_CHUNK = 64
ROPE_BASE = 10000.0
NA_HEADS = 8
NA_HD = 64
WIN_R = 8
WIN_C = 16
N_EXPERTS = 16
N_GROUPS = 4
EXPERTS_PER_GROUP = N_EXPERTS // N_GROUPS
TOP_K = 2

GLA_QK = GLA_HEADS * GLA_DK
GLA_V = GLA_HEADS * GLA_DV
NA_W = NA_HEADS * NA_HD

TM = 256
LANES = 128
LR_PAD = LANES
NEG = -1e30
VMEM_LIMIT = 56 * 1024 * 1024

_NT = (((1,), (1,)), ((), ()))
_TN = (((0,), (0,)), ((), ()))


def _cparams(sem):
    return pltpu.CompilerParams(dimension_semantics=sem, vmem_limit_bytes=VMEM_LIMIT)


def _dot(a, b):
    return jnp.dot(a, b, preferred_element_type=F32)


def _dg(a, b, dims):
    return lax.dot_general(a, b, dims, preferred_element_type=F32)


def _split(a):
    hi = a.astype(BF16)
    lo = (a - hi.astype(F32)).astype(BF16)
    return hi, lo


def _sigmoid(x):
    return 1.0 / (1.0 + jnp.exp(-x))


def _norm_mod(x, w, shift, scale):
    y = x * lax.rsqrt(jnp.mean(x * x, axis=-1, keepdims=True) + EPS)
    return (y * w) * (1.0 + scale) + shift


def _rows_to_tiles(ref, base, val):
    st = val.shape[1] // LANES
    for s in range(st):
        ref[pl.ds(base + s, val.shape[0], stride=st), :] = val[:, s * LANES:(s + 1) * LANES]


def _tiles_to_rows(ref, base, n, st):
    return jnp.concatenate([ref[pl.ds(base + s, n, stride=st), :] for s in range(st)], axis=1)


def _mod_kernel(c_ref, w_ref, b_ref, o_ref):
    c = c_ref[...]
    s = c * _sigmoid(c)
    sh, sl = _split(s)
    wh, wl = _split(w_ref[0])
    o_ref[0] = _dot(sh, wh) + _dot(sl, wh) + _dot(sh, wl) + b_ref[0]


def _modulation(cvec, w_ada, b_ada):
    depth, d, n = w_ada.shape
    tn = 1024
    return pl.pallas_call(
        _mod_kernel,
        grid=(depth, n // tn),
        in_specs=[pl.BlockSpec((8, d), lambda l, j: (0, 0)),
                  pl.BlockSpec((1, d, tn), lambda l, j: (l, 0, j)),
                  pl.BlockSpec((1, 1, tn), lambda l, j: (l, 0, j))],
        out_specs=pl.BlockSpec((1, 8, tn), lambda l, j: (l, 0, j)),
        out_shape=jax.ShapeDtypeStruct((depth, 8, n), F32),
        compiler_params=_cparams(("arbitrary", "arbitrary")),
        name="modulation",
    )(cvec, w_ada, b_ada.reshape(depth, 1, n))


_IN_OUT = (("gq", GLA_QK, BF16), ("gk", GLA_QK, BF16), ("gv", GLA_V, BF16), ("gg", GLA_V, BF16),
           ("nq", NA_W, BF16), ("nk", NA_W, BF16), ("nv", NA_W, BF16),
           ("m1", None, BF16), ("m2", None, BF16), ("lr", LR_PAD, F32))


def _inproj_kernel(x_ref, mod_ref, nw_ref, w_ref, cos_ref, sa_ref, sb_ref, *o_refs):
    d = x_ref.shape[2]
    mod = mod_ref[0, 0]
    h = _norm_mod(x_ref[0], nw_ref[...], mod[0:1], mod[1:2]).astype(BF16)
    cos, sa, sb = cos_ref[...], sa_ref[...], sb_ref[...]

    def rope(r, scale):
        parts = []
        for hh in range(GLA_HEADS):
            xs = r[:, hh * GLA_DK:(hh + 1) * GLA_DK]
            y = xs * cos + pltpu.roll(xs, GLA_DK - 32, 1) * sa + pltpu.roll(xs, 32, 1) * sb
            parts.append(y * scale if scale != 1.0 else y)
        return jnp.concatenate(parts, axis=1)

    off = 0
    for (name, width, dt), o_ref in zip(_IN_OUT, o_refs):
        width = d if width is None else width
        r = _dot(h, w_ref[:, off:off + width])
        if name == "gq":
            r = rope(r, GLA_DK ** -0.5)
        elif name == "gk":
            r = rope(r, 1.0)
        elif name == "nq":
            r = r * (NA_HD ** -0.5)
        o_ref[0] = r.astype(dt)
        off += width


def _inproj(x, modl, nw, wp, cos, sa, sb):
    b, t, d = x.shape
    nt = t // TM
    widths = [d if w is None else w for _, w, _ in _IN_OUT]
    ncols = sum(widths)
    return pl.pallas_call(
        _inproj_kernel,
        grid=(b, nt),
        in_specs=[pl.BlockSpec((1, TM, d), lambda i, j: (i, j, 0)),
                  pl.BlockSpec((1, 1, 8, d), lambda i, j: (i, jnp.minimum(j, 1), 0, 0)),
                  pl.BlockSpec((1, d), lambda i, j: (0, 0)),
                  pl.BlockSpec((d, ncols), lambda i, j: (0, 0)),
                  pl.BlockSpec((TM, GLA_DK), lambda i, j: (j, 0)),
                  pl.BlockSpec((TM, GLA_DK), lambda i, j: (j, 0)),
                  pl.BlockSpec((TM, GLA_DK), lambda i, j: (j, 0))],
        out_specs=[pl.BlockSpec((1, TM, w), lambda i, j: (i, j, 0)) for w in widths],
        out_shape=[jax.ShapeDtypeStruct((b, t, w), dt) for w, (_, _, dt) in zip(widths, _IN_OUT)],
        compiler_params=_cparams(("arbitrary", "arbitrary")),
        name="inproj",
    )(x, modl, nw, wp, cos, sa, sb)


def _gla_kernel(reverse, finish, q_ref, k_ref, v_ref, lr_ref, wup_ref, bg_ref, *rest):
    if finish:
        ob_ref, g_ref, nw_ref, o_ref, st_ref = rest
    else:
        o_ref, st_ref = rest
    nchunk = TM // GLA_CHUNK

    @pl.when(pl.program_id(1) == 0)
    def _():
        st_ref[...] = jnp.zeros_like(st_ref)

    z = _dot(lr_ref[0].astype(BF16), wup_ref[...]) + bg_ref[...]
    logg = (jnp.minimum(z, 0.0) - jnp.log(1.0 + jnp.exp(-jnp.abs(z)))) * (1.0 / GLA_TAU)
    r = lax.broadcasted_iota(jnp.int32, (TM, TM), 0)
    s = lax.broadcasted_iota(jnp.int32, (TM, TM), 1)
    order = (s >= r) if reverse else (s <= r)
    tri = jnp.where(((r // GLA_CHUNK) == (s // GLA_CHUNK)) & order, 1.0, 0.0).astype(BF16)
    hi, lo = _split(logg)
    bc = _dot(tri, hi) + _dot(tri, lo)
    cmask = order[:GLA_CHUNK, :GLA_CHUNK]

    for c in (range(nchunk - 1, -1, -1) if reverse else range(nchunk)):
        rs = slice(c * GLA_CHUNK, (c + 1) * GLA_CHUNK)
        for hh in range(GLA_HEADS):
            ls = slice(hh * GLA_DK, (hh + 1) * GLA_DK)
            vs = slice(hh * GLA_DV, (hh + 1) * GLA_DV)
            b = bc[rs, ls]
            bend = b[0:1] if reverse else b[GLA_CHUNK - 1:GLA_CHUNK]
            qc = q_ref[0, rs, ls].astype(F32)
            kc = k_ref[0, rs, ls].astype(F32)
            vc = v_ref[0, rs, vs]
            kd = (kc * jnp.exp(bend - b)).astype(BF16)
            qi = (qc * jnp.exp(b)).astype(BF16)
            qa = (qc * jnp.exp(b - bend)).astype(BF16)
            att = jnp.where(cmask, _dg(qa, kd, _NT), 0.0).astype(BF16)
            st = st_ref[hh]
            o = _dg(qi, st.astype(BF16), _NT) + _dot(att, vc)
            st_ref[hh] = st * jnp.exp(bend) + _dg(vc, kd, _TN)
            if finish:
                o = o + ob_ref[0, rs, vs].astype(F32)
                o = o * lax.rsqrt(jnp.mean(o * o, axis=-1, keepdims=True) + EPS) * nw_ref[...]
                g = g_ref[0, rs, vs].astype(F32)
                o = o * (g * _sigmoid(g))
            o_ref[0, rs, vs] = o.astype(o_ref.dtype)


def _gla_pass(reverse, q, k, v, lr, wup, bg, extra=None):
    b, t, _ = q.shape
    nt = t // TM
    if reverse:
        tile = lambda i, j: (i, jnp.where(j == 0, 0, nt - j), 0)
    else:
        tile = lambda i, j: (i, j, 0)
    const = lambda i, j: (0, 0)
    in_specs = [pl.BlockSpec((1, TM, GLA_QK), tile), pl.BlockSpec((1, TM, GLA_QK), tile),
                pl.BlockSpec((1, TM, GLA_V), tile), pl.BlockSpec((1, TM, LR_PAD), tile),
                pl.BlockSpec((LR_PAD, GLA_QK), const), pl.BlockSpec((1, GLA_QK), const)]
    args = [q, k, v, lr, wup, bg]
    if extra is not None:
        ob, g, nw = extra
        in_specs += [pl.BlockSpec((1, TM, GLA_V), tile), pl.BlockSpec((1, TM, GLA_V), tile),
                     pl.BlockSpec((1, GLA_DV), const)]
        args += [ob, g, nw]
    return pl.pallas_call(
        functools.partial(_gla_kernel, reverse, extra is not None),
        grid=(b, nt),
        in_specs=in_specs,
        out_specs=pl.BlockSpec((1, TM, GLA_V), tile),
        out_shape=jax.ShapeDtypeStruct((b, t, GLA_V), BF16),
        scratch_shapes=[pltpu.VMEM((GLA_HEADS, GLA_DV, GLA_DK), F32)],
        compiler_params=_cparams(("arbitrary", "arbitrary")),
        name="gla_bwd" if reverse else "gla_fwd",
    )(*args)


def _na_bias_tables(rpb, rows):
    rpt = TM // GRID_W
    kr = min(WIN_R, rows)
    col = np.arange(GRID_W)
    cidx = np.clip(col[None, :] - col[:, None], -(WIN_C - 1), WIN_C - 1) + (WIN_C - 1)
    c_sel = (cidx[None] == np.arange(2 * WIN_C - 1)[:, None, None]).astype(np.float32)
    c0 = np.clip(col - WIN_C // 2, 0, GRID_W - WIN_C)
    c_ok = (col[None, :] >= c0[:, None]) & (col[None, :] < c0[:, None] + WIN_C)
    toep = jnp.einsum("hrd,dqk->hrqk", rpb.astype(F32), c_sel, precision=lax.Precision.HIGHEST)
    tabs = [jnp.full((NA_HEADS, TM, 3 * TM), NEG, F32)]
    for r_base, u_base in ((0, 0), (rpt, 0), (rows - rpt, rows - 3 * rpt)):
        rq = r_base + np.arange(rpt)[:, None]
        rk = u_base + np.arange(3 * rpt)[None, :]
        r0 = np.clip(rq - kr // 2, 0, rows - kr)
        r_ok = (rk >= r0) & (rk < r0 + kr)
        ridx = np.clip(rk - rq + (WIN_R - 1), 0, 2 * WIN_R - 2)
        r_sel = (ridx[None] == np.arange(2 * WIN_R - 1)[:, None, None]).astype(np.float32)
        bias = jnp.einsum("hrqk,rim->hiqmk", toep, r_sel, precision=lax.Precision.HIGHEST)
        valid = r_ok[:, None, :, None] & c_ok[None, :, None, :]
        tabs.append(jnp.where(valid[None], bias, NEG).reshape(NA_HEADS, TM, 3 * TM))
    return jnp.stack(tabs)


def _na_kernel(q_ref, kp_ref, kc_ref, kn_ref, kx_ref, vp_ref, vc_ref, vn_ref, vx_ref, bias_ref, o_ref):
    lane = lax.broadcasted_iota(jnp.int32, (TM, 2 * NA_HD), 1)
    first = lane < NA_HD
    k_refs = (kp_ref, kc_ref, kn_ref, kx_ref)
    v_refs = (vp_ref, vc_ref, vn_ref, vx_ref)
    for hp in range(NA_HEADS // 2):
        ls = slice(hp * 2 * NA_HD, (hp + 1) * 2 * NA_HD)
        q2 = q_ref[0, :, ls]
        ks = [kr[0, :, ls] for kr in k_refs]
        vs = [vr[0, :, ls] for vr in v_refs]
        outs = []
        for sub in range(2):
            hd = 2 * hp + sub
            qm = jnp.where(first if sub == 0 else jnp.logical_not(first), q2, jnp.zeros_like(q2))
            sc = []
            for i in range(4):
                s = _dg(qm, ks[i], _NT)
                if i < 3:
                    s = s + bias_ref[0, hd, :, i * TM:(i + 1) * TM]
                sc.append(s)
            m = jnp.max(sc[0], axis=-1, keepdims=True)
            for i in range(1, 4):
                m = jnp.maximum(m, jnp.max(sc[i], axis=-1, keepdims=True))
            l = jnp.zeros_like(m)
            acc = jnp.zeros((TM, 2 * NA_HD), F32)
            for i in range(4):
                p = jnp.exp(sc[i] - m)
                l = l + jnp.sum(p, axis=-1, keepdims=True)
                acc = acc + _dot(p.astype(BF16), vs[i])
            outs.append(acc / l)
        o_ref[0, :, ls] = jnp.where(first, outs[0], outs[1]).astype(o_ref.dtype)


def _na(q, k, v, bias):
    b, t, _ = q.shape
    nt = t // TM
    qt = lambda i, j: (i, j, 0)
    ctx = lambda i, j: (i, 0, 0)

    def near(o):
        return lambda i, j: (i, jnp.clip(j, 2, nt - 2) + o, 0)

    def kind(i, j):
        return (jnp.where(j == 0, 0, jnp.where(j == 1, 1, jnp.where(j == nt - 1, 3, 2))), 0, 0, 0)

    blk = lambda f: pl.BlockSpec((1, TM, NA_W), f)
    return pl.pallas_call(
        _na_kernel,
        grid=(b, nt),
        in_specs=[blk(qt), blk(near(-1)), blk(near(0)), blk(near(1)), blk(ctx),
                  blk(near(-1)), blk(near(0)), blk(near(1)), blk(ctx),
                  pl.BlockSpec((1, NA_HEADS, TM, 3 * TM), kind)],
        out_specs=blk(qt),
        out_shape=jax.ShapeDtypeStruct((b, t, NA_W), BF16),
        compiler_params=_cparams(("arbitrary", "arbitrary")),
        name="natten",
    )(q, k, k, k, k, v, v, v, v, bias)


def _route(sel, aff):
    rows = lambda a, e: a[e:e + 1, :]
    gscore = []
    for g in range(N_GROUPS):
        a, b, c, d = (rows(sel, EXPERTS_PER_GROUP * g + i) for i in range(EXPERTS_PER_GROUP))
        hi1, lo1 = jnp.maximum(a, b), jnp.minimum(a, b)
        hi2, lo2 = jnp.maximum(c, d), jnp.minimum(c, d)
        gscore.append(jnp.maximum(hi1, hi2) + jnp.maximum(jnp.minimum(hi1, hi2), jnp.maximum(lo1, lo2)))
    gbest = jnp.zeros_like(gscore[0], dtype=jnp.int32)
    gval = gscore[0]
    for g in range(1, N_GROUPS):
        better = gscore[g] > gval
        gbest = jnp.where(better, g, gbest)
        gval = jnp.where(better, gscore[g], gval)
    cs, ca = [], []
    for i in range(EXPERTS_PER_GROUP):
        s_i, a_i = rows(sel, i), rows(aff, i)
        for g in range(1, N_GROUPS):
            pick = gbest == g
            s_i = jnp.where(pick, rows(sel, EXPERTS_PER_GROUP * g + i), s_i)
            a_i = jnp.where(pick, rows(aff, EXPERTS_PER_GROUP * g + i), a_i)
        cs.append(s_i)
        ca.append(a_i)
    i1 = jnp.zeros_like(gbest)
    v1, w1 = cs[0], ca[0]
    for i in range(1, EXPERTS_PER_GROUP):
        better = cs[i] > v1
        i1 = jnp.where(better, i, i1)
        v1 = jnp.where(better, cs[i], v1)
        w1 = jnp.where(better, ca[i], w1)
    i2 = jnp.full_like(gbest, -1)
    v2 = jnp.full_like(v1, -jnp.inf)
    w2 = jnp.zeros_like(w1)
    for i in range(EXPERTS_PER_GROUP):
        better = (i1 != i) & ((cs[i] > v2) | (i2 < 0))
        i2 = jnp.where(better, i, i2)
        v2 = jnp.where(better, cs[i], v2)
        w2 = jnp.where(better, ca[i], w2)
    tot = w1 + w2
    base = gbest * EXPERTS_PER_GROUP
    return base + i1, base + i2, w1 / tot, w2 / tot


def _merge_kernel(gy_ref, ny_ref, m1_ref, m2_ref, x_ref, mod_ref, nw_ref, wbg_ref, wbn_ref, wo_ref,
                  wr_ref, br_ref, xo_ref, hp_ref, ri_ref, rw_ref):
    a = _dot(gy_ref[0], wbg_ref[...])
    b = _dot(ny_ref[0], wbn_ref[...])
    m = _sigmoid(m1_ref[0].astype(F32)) * a + _sigmoid(m2_ref[0].astype(F32)) * b
    y = _dot(m.astype(BF16), wo_ref[...])
    mod = mod_ref[0, 0]
    xn = x_ref[0] + mod[2:3] * y
    xo_ref[0] = xn
    h2 = _norm_mod(xn, nw_ref[...], mod[3:4], mod[4:5])
    _rows_to_tiles(hp_ref, 0, h2)
    hh, hl = _split(h2)
    wh, wl = _split(wr_ref[...])
    logit = _dg(wh, hh, _NT) + _dg(wh, hl, _NT) + _dg(wl, hh, _NT)
    aff = _sigmoid(logit)
    i1, i2, w1, w2 = _route(aff + br_ref[...], aff)
    zi = jnp.zeros((6, TM), jnp.int32)
    ri_ref[0] = jnp.concatenate([i1, i2, zi], axis=0)
    rw_ref[0] = jnp.concatenate([w1, w2, zi.astype(F32)], axis=0)


def _merge(gy, ny, m1, m2, x, modl, nw2, wbg, wbn, wo, wrt, brt):
    b, t, d = x.shape
    nt = t // TM
    tile = lambda i, j: (i, j, 0)
    const = lambda i, j: (0, 0)
    return pl.pallas_call(
        _merge_kernel,
        grid=(b, nt),
        in_specs=[pl.BlockSpec((1, TM, GLA_V), tile), pl.BlockSpec((1, TM, NA_W), tile),
                  pl.BlockSpec((1, TM, d), tile), pl.BlockSpec((1, TM, d), tile),
                  pl.BlockSpec((1, TM, d), tile),
                  pl.BlockSpec((1, 1, 8, d), lambda i, j: (i, jnp.minimum(j, 1), 0, 0)),
                  pl.BlockSpec((1, d), const),
                  pl.BlockSpec((GLA_V, d), const), pl.BlockSpec((NA_W, d), const),
                  pl.BlockSpec((d, d), const),
                  pl.BlockSpec((N_EXPERTS, d), const), pl.BlockSpec((N_EXPERTS, TM), const)],
        out_specs=[pl.BlockSpec((1, TM, d), tile),
                   pl.BlockSpec((TM * d // LANES, LANES), lambda i, j: (i * nt + j, 0)),
                   pl.BlockSpec((1, 8, TM), lambda i, j: (i, 0, j)),
                   pl.BlockSpec((1, 8, TM), lambda i, j: (i, 0, j))],
        out_shape=[jax.ShapeDtypeStruct((b, t, d), F32), jax.ShapeDtypeStruct((b * t * d // LANES, LANES), F32),
                   jax.ShapeDtypeStruct((b, 8, t), jnp.int32), jax.ShapeDtypeStruct((b, 8, t), F32)],
        compiler_params=_cparams(("arbitrary", "arbitrary")),
        name="merge_router",
    )(gy, ny, m1, m2, x, modl, nw2, wbg, wbn, wo, wrt, brt)


def _moe_kernel(be_ref, nu_ref, src_cur, src_nxt, dst_cur, h_ref, w1_ref, w3_ref, w2_ref,
                o_ref, xbuf, ybuf, w1b, w3b, w2b, gsem, ssem):
    i = pl.program_id(0)
    nu = nu_ref[0]
    buf = lax.rem(i, 2)
    used = i < nu
    st = w1_ref.shape[2] // LANES
    blk = TM * st
    n_real = o_ref.shape[0] - 2 * blk

    def half(ref, b):
        return ref.at[pl.ds(pl.multiple_of(b * blk, blk), blk)]

    def token(ref, row):
        return ref.at[pl.ds(pl.multiple_of(row, st), st)]

    def start_gather(idx_ref, b):
        def body(r, carry):
            pltpu.make_async_copy(token(h_ref, idx_ref[0, 0, r]), token(xbuf, b * blk + r * st), gsem.at[b]).start()
            return carry
        lax.fori_loop(0, TM, body, 0, unroll=8)

    def start_scatter(idx_ref, b):
        def body(r, carry):
            pltpu.make_async_copy(token(ybuf, b * blk + r * st), token(o_ref, idx_ref[0, 0, r]), ssem.at[b]).start()
            return carry
        lax.fori_loop(0, TM, body, 0, unroll=8)

    def wait_gather(b):
        pltpu.make_async_copy(h_ref.at[pl.ds(0, blk)], half(xbuf, b), gsem.at[b]).wait()

    def wait_scatter(b):
        pltpu.make_async_copy(half(ybuf, b), o_ref.at[pl.ds(0, blk)], ssem.at[b]).wait()

    @pl.when(i == 0)
    def _():
        xbuf[...] = jnp.zeros_like(xbuf)
        for b in range(2):
            spare = pltpu.make_async_copy(half(xbuf, b), o_ref.at[pl.ds(n_real + b * blk, blk)], ssem.at[b])
            spare.start()
            spare.wait()
        start_gather(src_cur, buf)

    @pl.when(used)
    def _():
        wait_gather(buf)

    @pl.when(i + 1 < nu)
    def _():
        start_gather(src_nxt, 1 - buf)

    fresh = jnp.logical_or(i == 0, be_ref[i] != be_ref[jnp.maximum(i - 1, 0)])

    @pl.when(jnp.logical_and(used, fresh))
    def _():
        for src, dst in ((w1_ref, w1b), (w3_ref, w3b), (w2_ref, w2b)):
            def body(r, carry, src=src, dst=dst):
                rows = pl.ds(pl.multiple_of(r * LANES, LANES), LANES)
                dst[rows, :] = src[0, 0, rows, :].astype(BF16)
                return carry
            lax.fori_loop(0, src.shape[2] // LANES, body, 0)

    @pl.when(used)
    def _():
        x = _tiles_to_rows(xbuf, buf * blk, TM, st).astype(BF16)
        a = _dot(x, w1b[...])
        b = _dot(x, w3b[...])
        hmid = (a * _sigmoid(a)) * b
        _rows_to_tiles(ybuf, buf * blk, _dot(hmid.astype(BF16), w2b[...]))
        start_scatter(dst_cur, buf)

    @pl.when(jnp.logical_and(used, i > 0))
    def _():
        wait_scatter(1 - buf)

    @pl.when(i == nu - 1)
    def _():
        wait_scatter(buf)


def _moe(blk_expert, n_used, slot_src, slot_dst, h, w1, w3, w2, layer):
    d, f = w1.shape[2], w1.shape[3]
    st = d // LANES
    blk = TM * st
    n_rows = h.shape[0]
    nb = blk_expert.shape[0]
    wsel = lambda i, be, nu: (layer, be[i], 0, 0)
    smem = lambda f_: pl.BlockSpec((1, 1, TM), f_, memory_space=pltpu.SMEM)
    cur = lambda i, be, nu: (i, 0, 0)
    nxt = lambda i, be, nu: (jnp.minimum(i + 1, nb - 1), 0, 0)
    return pl.pallas_call(
        _moe_kernel,
        grid_spec=pltpu.PrefetchScalarGridSpec(
            num_scalar_prefetch=2,
            grid=(nb,),
            in_specs=[smem(cur), smem(nxt), smem(cur),
                      pl.BlockSpec(memory_space=pl.ANY), pl.BlockSpec((1, 1, d, f), wsel),
                      pl.BlockSpec((1, 1, d, f), wsel), pl.BlockSpec((1, 1, f, d), wsel)],
            out_specs=pl.BlockSpec(memory_space=pl.ANY),
            scratch_shapes=[pltpu.VMEM((2 * blk, LANES), F32), pltpu.VMEM((2 * blk, LANES), F32),
                            pltpu.VMEM((d, f), BF16), pltpu.VMEM((d, f), BF16), pltpu.VMEM((f, d), BF16),
                            pltpu.SemaphoreType.DMA((2,)), pltpu.SemaphoreType.DMA((2,))]),
        out_shape=jax.ShapeDtypeStruct((TOP_K * n_rows + 2 * blk, LANES), F32),
        compiler_params=_cparams(("arbitrary",)),
        name="moe_experts",
    )(blk_expert, n_used, slot_src, slot_src, slot_dst, h, w1, w3, w2)


def _combine_kernel(final, *refs):
    y_refs, (x_ref, w_ref, mod_ref), rest = refs[:TOP_K], refs[TOP_K:TOP_K + 3], refs[TOP_K + 3:]
    if final:
        fw_ref, o_ref = rest
    else:
        (o_ref,) = rest
    w = w_ref[0]
    y = None
    for kk in range(TOP_K):
        yk = _tiles_to_rows(y_refs[kk], 0, TM, x_ref.shape[2] // LANES) * w[:, kk:kk + 1]
        y = yk if y is None else y + yk
    xn = x_ref[0] + mod_ref[0, 0][5:6] * y
    if final:
        xn = xn * lax.rsqrt(jnp.mean(xn * xn, axis=-1, keepdims=True) + EPS) * fw_ref[...]
    o_ref[0] = xn


def _combine(y, x, wcol, modl, final_w=None):
    b, t, d = x.shape
    nt = t // TM
    final = final_w is not None
    skip = 1 if final else 0
    tile = lambda i, j: (i, j + skip, 0)
    def plane(kk):
        return pl.BlockSpec((TM * d // LANES, LANES), lambda i, j: ((kk * b + i) * nt + j + skip, 0))

    in_specs = [plane(kk) for kk in range(TOP_K)] + [
        pl.BlockSpec((1, TM, d), tile), pl.BlockSpec((1, TM, TOP_K), tile),
        pl.BlockSpec((1, 1, 8, d), lambda i, j: (i, jnp.minimum(j + skip, 1), 0, 0))]
    args = [y] * TOP_K + [x, wcol, modl]
    if final:
        in_specs.append(pl.BlockSpec((1, d), lambda i, j: (0, 0)))
        args.append(final_w)
    return pl.pallas_call(
        functools.partial(_combine_kernel, final),
        grid=(b, nt - skip),
        in_specs=in_specs,
        out_specs=pl.BlockSpec((1, TM, d), lambda i, j: (i, j, 0)),
        out_shape=jax.ShapeDtypeStruct((b, t - skip * TM, d), F32),
        compiler_params=_cparams(("arbitrary", "arbitrary")),
        name="moe_combine_final" if final else "moe_combine",
    )(*args)


def _slot_plan(ridx, n_tok, st):
    e = ridx.reshape(-1)
    onehot = (e[:, None] == jnp.arange(N_EXPERTS, dtype=jnp.int32)[None, :]).astype(jnp.int32)
    incl = jnp.cumsum(onehot, axis=0)
    rank = jnp.sum((incl - onehot) * onehot, axis=1)
    counts = incl[-1]
    padded = (counts + TM - 1) // TM * TM
    pends = jnp.cumsum(padded)
    pstarts = pends - padded
    dest = pstarts[e] + rank
    n_blocks = (n_tok * TOP_K + N_EXPERTS * (TM - 1) + TM - 1) // TM
    blk_start = jnp.arange(n_blocks, dtype=jnp.int32) * TM
    blk_expert = jnp.minimum(jnp.sum((pends[None, :] <= blk_start[:, None]).astype(jnp.int32), axis=1),
                             N_EXPERTS - 1)
    n_used = (pends[-1] // TM).astype(jnp.int32).reshape(1)
    pair = jnp.arange(n_tok * TOP_K, dtype=jnp.int32)
    row = (pair % TOP_K) * n_tok + pair // TOP_K
    slot = jnp.arange(n_blocks * TM, dtype=jnp.int32)
    spare = n_tok * TOP_K + ((slot // TM) % 2) * TM + slot % TM
    slot_dst = spare.at[dest].set(row, unique_indices=True)
    slot_src = jnp.where(slot_dst < n_tok * TOP_K, slot_dst % n_tok, 0)
    shape3 = (n_blocks, 1, TM)
    return blk_expert.astype(jnp.int32), n_used, (slot_src * st).reshape(shape3), (slot_dst * st).reshape(shape3)


def _rope_tables(t_len, ctx_len):
    half = GLA_DK // 2
    inv = ROPE_BASE ** (-np.arange(0, half, 2, dtype=np.float32) / half)
    lane = np.arange(GLA_DK)
    p = np.arange(t_len - ctx_len)
    pos = np.where(lane[None, :] < half, (p // GRID_W)[:, None], (p % GRID_W)[:, None]).astype(np.float32)
    ang = jnp.asarray(pos) * jnp.asarray(inv[(lane % half) % (half // 2)])[None, :]
    cos, sin = jnp.cos(ang), jnp.sin(ang)
    lower = jnp.asarray((lane % half) < half // 2)[None, :]
    pad = lambda a, v: jnp.concatenate([jnp.full((ctx_len, GLA_DK), v, F32), a], axis=0)
    return pad(cos, 1.0), pad(jnp.where(lower, -sin, 0.0), 0.0), pad(jnp.where(lower, 0.0, sin), 0.0)


def kernel(x, c, ctx, c_ctx, w_ada, b_ada, norm1_w, norm2_w, w_in, w_gk_up, b_gk, gla_norm_w, rpb,
           w_bo_gla, w_bo_na, w_out, w_router, b_router, w1, w3, w2, final_norm_w):
    bsz, seq, d = x.shape
    ctx_len = ctx.shape[1]
    depth = w_ada.shape[0]
    assert ctx_len == TM and seq % TM == 0 and TM % GRID_W == 0 and seq // GRID_W >= 3 * (TM // GRID_W)
    t = ctx_len + seq
    nt = t // TM
    n_tok = bsz * t

    xs = jnp.concatenate([ctx, x], axis=1)
    cvec = jnp.zeros((8, d), F32).at[:bsz].set(c).at[bsz].set(c_ctx)
    mods = _modulation(cvec, w_ada, b_ada).reshape(depth, 8, 6, d)
    cos, sa, sb = _rope_tables(t, ctx_len)
    wrt = w_router.T.astype(F32)
    brt = jnp.broadcast_to(b_router.astype(F32)[:, None], (N_EXPERTS, TM))
    gq_end = 2 * GLA_QK + 2 * GLA_V

    out = None
    for l in range(depth):
        mx = jnp.pad(mods[l, :bsz], ((0, 0), (0, 2), (0, 0)))
        mc = jnp.broadcast_to(jnp.pad(mods[l, bsz], ((0, 2), (0, 0)))[None], mx.shape)
        modl = jnp.stack([mc, mx], axis=1)
        wl = w_in[l]
        wp = jnp.concatenate([wl[:, :gq_end], wl[:, gq_end + 2 * GLA_LR:], wl[:, gq_end:gq_end + 2 * GLA_LR],
                              jnp.zeros((d, LR_PAD - 2 * GLA_LR), wl.dtype)], axis=1).astype(BF16)
        gq, gk, gv, gg, nq, nk, nv, m1, m2, lr = _inproj(xs, modl, norm1_w[l][None], wp, cos, sa, sb)

        def wup(dirn):
            w = jnp.zeros((LR_PAD, GLA_QK), F32)
            return w.at[dirn * GLA_LR:(dirn + 1) * GLA_LR].set(w_gk_up[l, dirn]).astype(BF16)

        ob = _gla_pass(True, gq, gk, gv, lr, wup(1), b_gk[l, 1][None])
        gy = _gla_pass(False, gq, gk, gv, lr, wup(0), b_gk[l, 0][None],
                       extra=(ob, gg, gla_norm_w[l][None]))
        ny = _na(nq, nk, nv, _na_bias_tables(rpb[l], seq // GRID_W))
        xs, hp, ridx, rw = _merge(gy, ny, m1, m2, xs, modl, norm2_w[l][None],
                                  w_bo_gla[l].astype(BF16), w_bo_na[l].astype(BF16), w_out[l].astype(BF16),
                                  wrt, brt)
        ridx2 = jnp.transpose(ridx[:, :TOP_K, :], (0, 2, 1))
        wcol = jnp.transpose(rw[:, :TOP_K, :], (0, 2, 1))
        blk_expert, n_used, slot_src, slot_dst = _slot_plan(ridx2, n_tok, d // LANES)
        y = _moe(blk_expert, n_used, slot_src, slot_dst, hp, w1, w3, w2, l)
        if l < depth - 1:
            xs = _combine(y, xs, wcol, modl)
        else:
            out = _combine(y, xs, wcol, modl, final_norm_w[None])
    return out
```

```python
import functools

import jax
import jax.numpy as jnp
import numpy as np
from jax import lax
from jax.experimental import pallas as pl
from jax.experimental.pallas import tpu as pltpu

F32 = jnp.float32
BF16 = jnp.bfloat16

EPS = 1e-6
GRID_W = 64
GLA_HEADS = 4
GLA_DK = 128
GLA_DV = 256
GLA_LR = 16
GLA_TAU = 16.0
GLA_CHUNK = 64
ROPE_BASE = 10000.0
NA_HEADS = 8
NA_HD = 64
WIN_R = 8
WIN_C = 16
N_EXPERTS = 16
N_GROUPS = 4
EXPERTS_PER_GROUP = N_EXPERTS // N_GROUPS
TOP_K = 2

GLA_QK = GLA_HEADS * GLA_DK
GLA_V = GLA_HEADS * GLA_DV
NA_W = NA_HEADS * NA_HD

TM = 256
LANES = 128
LR_PAD = LANES
NEG = -1e30
VMEM_LIMIT = 56 * 1024 * 1024

_NT = (((1,), (1,)), ((), ()))
_TN = (((0,), (0,)), ((), ()))


def _cparams(sem):
    return pltpu.CompilerParams(dimension_semantics=sem, vmem_limit_bytes=VMEM_LIMIT)


def _dot(a, b):
    return jnp.dot(a, b, preferred_element_type=F32)


def _dg(a, b, dims):
    return lax.dot_general(a, b, dims, preferred_element_type=F32)


def _split(a):
    hi = a.astype(BF16)
    lo = (a - hi.astype(F32)).astype(BF16)
    return hi, lo


def _sigmoid(x):
    return 1.0 / (1.0 + jnp.exp(-x))


def _norm_mod(x, w, shift, scale):
    y = x * lax.rsqrt(jnp.mean(x * x, axis=-1, keepdims=True) + EPS)
    return (y * w) * (1.0 + scale) + shift


def _rows_to_tiles(ref, base, val):
    st = val.shape[1] // LANES
    for s in range(st):
        ref[pl.ds(base + s, val.shape[0], stride=st), :] = val[:, s * LANES:(s + 1) * LANES]


def _tiles_to_rows(ref, base, n, st):
    return jnp.concatenate([ref[pl.ds(base + s, n, stride=st), :] for s in range(st)], axis=1)


def _mod_kernel(c_ref, w_ref, b_ref, o_ref):
    c = c_ref[...]
    s = c * _sigmoid(c)
    sh, sl = _split(s)
    wh, wl = _split(w_ref[0])
    o_ref[0] = _dot(sh, wh) + _dot(sl, wh) + _dot(sh, wl) + b_ref[0]


def _modulation(cvec, w_ada, b_ada):
    depth, d, n = w_ada.shape
    tn = 1024
    return pl.pallas_call(
        _mod_kernel,
        grid=(depth, n // tn),
        in_specs=[pl.BlockSpec((8, d), lambda l, j: (0, 0)),
                  pl.BlockSpec((1, d, tn), lambda l, j: (l, 0, j)),
                  pl.BlockSpec((1, 1, tn), lambda l, j: (l, 0, j))],
        out_specs=pl.BlockSpec((1, 8, tn), lambda l, j: (l, 0, j)),
        out_shape=jax.ShapeDtypeStruct((depth, 8, n), F32),
        compiler_params=_cparams(("arbitrary", "arbitrary")),
        name="modulation",
    )(cvec, w_ada, b_ada.reshape(depth, 1, n))


_IN_OUT = (("gq", GLA_QK, BF16), ("gk", GLA_QK, BF16), ("gv", GLA_V, BF16), ("gg", GLA_V, BF16),
           ("nq", NA_W, BF16), ("nk", NA_W, BF16), ("nv", NA_W, BF16),
           ("m1", None, BF16), ("m2", None, BF16), ("lr", LR_PAD, F32))


def _inproj_kernel(x_ref, mod_ref, nw_ref, w_ref, cos_ref, sa_ref, sb_ref, *o_refs):
    d = x_ref.shape[2]
    mod = mod_ref[0, 0]
    h = _norm_mod(x_ref[0], nw_ref[...], mod[0:1], mod[1:2]).astype(BF16)
    cos, sa, sb = cos_ref[...], sa_ref[...], sb_ref[...]

    def rope(r, scale):
        parts = []
        for hh in range(GLA_HEADS):
            xs = r[:, hh * GLA_DK:(hh + 1) * GLA_DK]
            y = xs * cos + pltpu.roll(xs, GLA_DK - 32, 1) * sa + pltpu.roll(xs, 32, 1) * sb
            parts.append(y * scale if scale != 1.0 else y)
        return jnp.concatenate(parts, axis=1)

    off = 0
    for (name, width, dt), o_ref in zip(_IN_OUT, o_refs):
        width = d if width is None else width
        r = _dot(h, w_ref[:, off:off + width])
        if name == "gq":
            r = rope(r, GLA_DK ** -0.5)
        elif name == "gk":
            r = rope(r, 1.0)
        elif name == "nq":
            r = r * (NA_HD ** -0.5)
        o_ref[0] = r.astype(dt)
        off += width


def _inproj(x, modl, nw, wp, cos, sa, sb):
    b, t, d = x.shape
    nt = t // TM
    widths = [d if w is None else w for _, w, _ in _IN_OUT]
    ncols = sum(widths)
    return pl.pallas_call(
        _inproj_kernel,
        grid=(b, nt),
        in_specs=[pl.BlockSpec((1, TM, d), lambda i, j: (i, j, 0)),
                  pl.BlockSpec((1, 1, 8, d), lambda i, j: (i, jnp.minimum(j, 1), 0, 0)),
                  pl.BlockSpec((1, d), lambda i, j: (0, 0)),
                  pl.BlockSpec((d, ncols), lambda i, j: (0, 0)),
                  pl.BlockSpec((TM, GLA_DK), lambda i, j: (j, 0)),
                  pl.BlockSpec((TM, GLA_DK), lambda i, j: (j, 0)),
                  pl.BlockSpec((TM, GLA_DK), lambda i, j: (j, 0))],
        out_specs=[pl.BlockSpec((1, TM, w), lambda i, j: (i, j, 0)) for w in widths],
        out_shape=[jax.ShapeDtypeStruct((b, t, w), dt) for w, (_, _, dt) in zip(widths, _IN_OUT)],
        compiler_params=_cparams(("arbitrary", "arbitrary")),
        name="inproj",
    )(x, modl, nw, wp, cos, sa, sb)


def _gla_kernel(reverse, finish, q_ref, k_ref, v_ref, lr_ref, wup_ref, bg_ref, *rest):
    if finish:
        ob_ref, g_ref, nw_ref, o_ref, st_ref = rest
    else:
        o_ref, st_ref = rest
    nchunk = TM // GLA_CHUNK

    @pl.when(pl.program_id(1) == 0)
    def _():
        st_ref[...] = jnp.zeros_like(st_ref)

    z = _dot(lr_ref[0].astype(BF16), wup_ref[...]) + bg_ref[...]
    logg = (jnp.minimum(z, 0.0) - jnp.log(1.0 + jnp.exp(-jnp.abs(z)))) * (1.0 / GLA_TAU)
    r = lax.broadcasted_iota(jnp.int32, (TM, TM), 0)
    s = lax.broadcasted_iota(jnp.int32, (TM, TM), 1)
    order = (s >= r) if reverse else (s <= r)
    tri = jnp.where(((r // GLA_CHUNK) == (s // GLA_CHUNK)) & order, 1.0, 0.0).astype(BF16)
    hi, lo = _split(logg)
    bc = _dot(tri, hi) + _dot(tri, lo)
    cmask = order[:GLA_CHUNK, :GLA_CHUNK]

    for c in (range(nchunk - 1, -1, -1) if reverse else range(nchunk)):
        rs = slice(c * GLA_CHUNK, (c + 1) * GLA_CHUNK)
        for hh in range(GLA_HEADS):
            ls = slice(hh * GLA_DK, (hh + 1) * GLA_DK)
            vs = slice(hh * GLA_DV, (hh + 1) * GLA_DV)
            b = bc[rs, ls]
            bend = b[0:1] if reverse else b[GLA_CHUNK - 1:GLA_CHUNK]
            qc = q_ref[0, rs, ls].astype(F32)
            kc = k_ref[0, rs, ls].astype(F32)
            vc = v_ref[0, rs, vs]
            kd = (kc * jnp.exp(bend - b)).astype(BF16)
            qi = (qc * jnp.exp(b)).astype(BF16)
            qa = (qc * jnp.exp(b - bend)).astype(BF16)
            att = jnp.where(cmask, _dg(qa, kd, _NT), 0.0).astype(BF16)
            st = st_ref[hh]
            o = _dg(qi, st.astype(BF16), _NT) + _dot(att, vc)
            st_ref[hh] = st * jnp.exp(bend) + _dg(vc, kd, _TN)
            if finish:
                o = o + ob_ref[0, rs, vs].astype(F32)
                o = o * lax.rsqrt(jnp.mean(o * o, axis=-1, keepdims=True) + EPS) * nw_ref[...]
                g = g_ref[0, rs, vs].astype(F32)
                o = o * (g * _sigmoid(g))
            o_ref[0, rs, vs] = o.astype(o_ref.dtype)


def _gla_pass(reverse, q, k, v, lr, wup, bg, extra=None):
    b, t, _ = q.shape
    nt = t // TM
    if reverse:
        tile = lambda i, j: (i, jnp.where(j == 0, 0, nt - j), 0)
    else:
        tile = lambda i, j: (i, j, 0)
    const = lambda i, j: (0, 0)
    in_specs = [pl.BlockSpec((1, TM, GLA_QK), tile), pl.BlockSpec((1, TM, GLA_QK), tile),
                pl.BlockSpec((1, TM, GLA_V), tile), pl.BlockSpec((1, TM, LR_PAD), tile),
                pl.BlockSpec((LR_PAD, GLA_QK), const), pl.BlockSpec((1, GLA_QK), const)]
    args = [q, k, v, lr, wup, bg]
    if extra is not None:
        ob, g, nw = extra
        in_specs += [pl.BlockSpec((1, TM, GLA_V), tile), pl.BlockSpec((1, TM, GLA_V), tile),
                     pl.BlockSpec((1, GLA_DV), const)]
        args += [ob, g, nw]
    return pl.pallas_call(
        functools.partial(_gla_kernel, reverse, extra is not None),
        grid=(b, nt),
        in_specs=in_specs,
        out_specs=pl.BlockSpec((1, TM, GLA_V), tile),
        out_shape=jax.ShapeDtypeStruct((b, t, GLA_V), BF16),
        scratch_shapes=[pltpu.VMEM((GLA_HEADS, GLA_DV, GLA_DK), F32)],
        compiler_params=_cparams(("arbitrary", "arbitrary")),
        name="gla_bwd" if reverse else "gla_fwd",
    )(*args)


def _na_bias_tables(rpb, rows):
    rpt = TM // GRID_W
    kr = min(WIN_R, rows)
    col = np.arange(GRID_W)
    cidx = np.clip(col[None, :] - col[:, None], -(WIN_C - 1), WIN_C - 1) + (WIN_C - 1)
    c_sel = (cidx[None] == np.arange(2 * WIN_C - 1)[:, None, None]).astype(np.float32)
    c0 = np.clip(col - WIN_C // 2, 0, GRID_W - WIN_C)
    c_ok = (col[None, :] >= c0[:, None]) & (col[None, :] < c0[:, None] + WIN_C)
    toep = jnp.einsum("hrd,dqk->hrqk", rpb.astype(F32), c_sel, precision=lax.Precision.HIGHEST)
    tabs = [jnp.full((NA_HEADS, TM, 3 * TM), NEG, F32)]
    for r_base, u_base in ((0, 0), (rpt, 0), (rows - rpt, rows - 3 * rpt)):
        rq = r_base + np.arange(rpt)[:, None]
        rk = u_base + np.arange(3 * rpt)[None, :]
        r0 = np.clip(rq - kr // 2, 0, rows - kr)
        r_ok = (rk >= r0) & (rk < r0 + kr)
        ridx = np.clip(rk - rq + (WIN_R - 1), 0, 2 * WIN_R - 2)
        r_sel = (ridx[None] == np.arange(2 * WIN_R - 1)[:, None, None]).astype(np.float32)
        bias = jnp.einsum("hrqk,rim->hiqmk", toep, r_sel, precision=lax.Precision.HIGHEST)
        valid = r_ok[:, None, :, None] & c_ok[None, :, None, :]
        tabs.append(jnp.where(valid[None], bias, NEG).reshape(NA_HEADS, TM, 3 * TM))
    return jnp.stack(tabs)


def _na_kernel(q_ref, kp_ref, kc_ref, kn_ref, kx_ref, vp_ref, vc_ref, vn_ref, vx_ref, bias_ref, o_ref):
    lane = lax.broadcasted_iota(jnp.int32, (TM, 2 * NA_HD), 1)
    first = lane < NA_HD
    k_refs = (kp_ref, kc_ref, kn_ref, kx_ref)
    v_refs = (vp_ref, vc_ref, vn_ref, vx_ref)
    for hp in range(NA_HEADS // 2):
        ls = slice(hp * 2 * NA_HD, (hp + 1) * 2 * NA_HD)
        q2 = q_ref[0, :, ls]
        ks = [kr[0, :, ls] for kr in k_refs]
        vs = [vr[0, :, ls] for vr in v_refs]
        outs = []
        for sub in range(2):
            hd = 2 * hp + sub
            qm = jnp.where(first if sub == 0 else jnp.logical_not(first), q2, jnp.zeros_like(q2))
            sc = []
            for i in range(4):
                s = _dg(qm, ks[i], _NT)
                if i < 3:
                    s = s + bias_ref[0, hd, :, i * TM:(i + 1) * TM]
                sc.append(s)
            m = jnp.max(sc[0], axis=-1, keepdims=True)
            for i in range(1, 4):
                m = jnp.maximum(m, jnp.max(sc[i], axis=-1, keepdims=True))
            l = jnp.zeros_like(m)
            acc = jnp.zeros((TM, 2 * NA_HD), F32)
            for i in range(4):
                p = jnp.exp(sc[i] - m)
                l = l + jnp.sum(p, axis=-1, keepdims=True)
                acc = acc + _dot(p.astype(BF16), vs[i])
            outs.append(acc / l)
        o_ref[0, :, ls] = jnp.where(first, outs[0], outs[1]).astype(o_ref.dtype)


def _na(q, k, v, bias):
    b, t, _ = q.shape
    nt = t // TM
    qt = lambda i, j: (i, j, 0)
    ctx = lambda i, j: (i, 0, 0)

    def near(o):
        return lambda i, j: (i, jnp.clip(j, 2, nt - 2) + o, 0)

    def kind(i, j):
        return (jnp.where(j == 0, 0, jnp.where(j == 1, 1, jnp.where(j == nt - 1, 3, 2))), 0, 0, 0)

    blk = lambda f: pl.BlockSpec((1, TM, NA_W), f)
    return pl.pallas_call(
        _na_kernel,
        grid=(b, nt),
        in_specs=[blk(qt), blk(near(-1)), blk(near(0)), blk(near(1)), blk(ctx),
                  blk(near(-1)), blk(near(0)), blk(near(1)), blk(ctx),
                  pl.BlockSpec((1, NA_HEADS, TM, 3 * TM), kind)],
        out_specs=blk(qt),
        out_shape=jax.ShapeDtypeStruct((b, t, NA_W), BF16),
        compiler_params=_cparams(("arbitrary", "arbitrary")),
        name="natten",
    )(q, k, k, k, k, v, v, v, v, bias)


def _route(sel, aff):
    rows = lambda a, e: a[e:e + 1, :]
    gscore = []
    for g in range(N_GROUPS):
        a, b, c, d = (rows(sel, EXPERTS_PER_GROUP * g + i) for i in range(EXPERTS_PER_GROUP))
        hi1, lo1 = jnp.maximum(a, b), jnp.minimum(a, b)
        hi2, lo2 = jnp.maximum(c, d), jnp.minimum(c, d)
        gscore.append(jnp.maximum(hi1, hi2) + jnp.maximum(jnp.minimum(hi1, hi2), jnp.maximum(lo1, lo2)))
    gbest = jnp.zeros_like(gscore[0], dtype=jnp.int32)
    gval = gscore[0]
    for g in range(1, N_GROUPS):
        better = gscore[g] > gval
        gbest = jnp.where(better, g, gbest)
        gval = jnp.where(better, gscore[g], gval)
    cs, ca = [], []
    for i in range(EXPERTS_PER_GROUP):
        s_i, a_i = rows(sel, i), rows(aff, i)
        for g in range(1, N_GROUPS):
            pick = gbest == g
            s_i = jnp.where(pick, rows(sel, EXPERTS_PER_GROUP * g + i), s_i)
            a_i = jnp.where(pick, rows(aff, EXPERTS_PER_GROUP * g + i), a_i)
        cs.append(s_i)
        ca.append(a_i)
    i1 = jnp.zeros_like(gbest)
    v1, w1 = cs[0], ca[0]
    for i in range(1, EXPERTS_PER_GROUP):
        better = cs[i] > v1
        i1 = jnp.where(better, i, i1)
        v1 = jnp.where(better, cs[i], v1)
        w1 = jnp.where(better, ca[i], w1)
    i2 = jnp.full_like(gbest, -1)
    v2 = jnp.full_like(v1, -jnp.inf)
    w2 = jnp.zeros_like(w1)
    for i in range(EXPERTS_PER_GROUP):
        better = (i1 != i) & ((cs[i] > v2) | (i2 < 0))
        i2 = jnp.where(better, i, i2)
        v2 = jnp.where(better, cs[i], v2)
        w2 = jnp.where(better, ca[i], w2)
    tot = w1 + w2
    base = gbest * EXPERTS_PER_GROUP
    return base + i1, base + i2, w1 / tot, w2 / tot


def _merge_kernel(gy_ref, ny_ref, m1_ref, m2_ref, x_ref, mod_ref, nw_ref, wbg_ref, wbn_ref, wo_ref,
                  wr_ref, br_ref, xo_ref, hp_ref, ri_ref, rw_ref):
    a = _dot(gy_ref[0], wbg_ref[...])
    b = _dot(ny_ref[0], wbn_ref[...])
    m = _sigmoid(m1_ref[0].astype(F32)) * a + _sigmoid(m2_ref[0].astype(F32)) * b
    y = _dot(m.astype(BF16), wo_ref[...])
    mod = mod_ref[0, 0]
    xn = x_ref[0] + mod[2:3] * y
    xo_ref[0] = xn
    h2 = _norm_mod(xn, nw_ref[...], mod[3:4], mod[4:5])
    _rows_to_tiles(hp_ref, 0, h2)
    hh, hl = _split(h2)
    wh, wl = _split(wr_ref[...])
    logit = _dg(wh, hh, _NT) + _dg(wh, hl, _NT) + _dg(wl, hh, _NT)
    aff = _sigmoid(logit)
    i1, i2, w1, w2 = _route(aff + br_ref[...], aff)
    zi = jnp.zeros((6, TM), jnp.int32)
    ri_ref[0] = jnp.concatenate([i1, i2, zi], axis=0)
    rw_ref[0] = jnp.concatenate([w1, w2, zi.astype(F32)], axis=0)


def _merge(gy, ny, m1, m2, x, modl, nw2, wbg, wbn, wo, wrt, brt):
    b, t, d = x.shape
    nt = t // TM
    tile = lambda i, j: (i, j, 0)
    const = lambda i, j: (0, 0)
    return pl.pallas_call(
        _merge_kernel,
        grid=(b, nt),
        in_specs=[pl.BlockSpec((1, TM, GLA_V), tile), pl.BlockSpec((1, TM, NA_W), tile),
                  pl.BlockSpec((1, TM, d), tile), pl.BlockSpec((1, TM, d), tile),
                  pl.BlockSpec((1, TM, d), tile),
                  pl.BlockSpec((1, 1, 8, d), lambda i, j: (i, jnp.minimum(j, 1), 0, 0)),
                  pl.BlockSpec((1, d), const),
                  pl.BlockSpec((GLA_V, d), const), pl.BlockSpec((NA_W, d), const),
                  pl.BlockSpec((d, d), const),
                  pl.BlockSpec((N_EXPERTS, d), const), pl.BlockSpec((N_EXPERTS, TM), const)],
        out_specs=[pl.BlockSpec((1, TM, d), tile),
                   pl.BlockSpec((TM * d // LANES, LANES), lambda i, j: (i * nt + j, 0)),
                   pl.BlockSpec((1, 8, TM), lambda i, j: (i, 0, j)),
                   pl.BlockSpec((1, 8, TM), lambda i, j: (i, 0, j))],
        out_shape=[jax.ShapeDtypeStruct((b, t, d), F32), jax.ShapeDtypeStruct((b * t * d // LANES, LANES), F32),
                   jax.ShapeDtypeStruct((b, 8, t), jnp.int32), jax.ShapeDtypeStruct((b, 8, t), F32)],
        compiler_params=_cparams(("arbitrary", "arbitrary")),
        name="merge_router",
    )(gy, ny, m1, m2, x, modl, nw2, wbg, wbn, wo, wrt, brt)


def _moe_kernel(be_ref, nu_ref, src_cur, src_nxt, dst_cur, dst_prv, h_ref, w1_ref, w3_ref, w2_ref,
                o_ref, xbuf, ybuf, w1b, w3b, w2b, gsem, ssem):
    i = pl.program_id(0)
    nu = nu_ref[0]
    xb = lax.rem(i, 2)
    yb = lax.rem(i, 3)
    used = i < nu
    st = w1_ref.shape[2] // LANES
    blk = TM * st
    n_real = o_ref.shape[0] - 2 * blk

    def part(ref, b):
        return ref.at[pl.ds(pl.multiple_of(b * blk, blk), blk)]

    def token(ref, row):
        return ref.at[pl.ds(pl.multiple_of(row, st), st)]

    def start_gather(idx_ref, b):
        for r in range(TM):
            pltpu.make_async_copy(token(h_ref, idx_ref[0, 0, r]), token(xbuf, b * blk + r * st), gsem.at[b]).start()

    def start_scatter(idx_ref, b):
        for r in range(TM):
            pltpu.make_async_copy(token(ybuf, b * blk + r * st), token(o_ref, idx_ref[0, 0, r]), ssem.at[b]).start()

    def wait_gather(b):
        pltpu.make_async_copy(h_ref.at[pl.ds(0, blk)], part(xbuf, b), gsem.at[b]).wait()

    def wait_scatter(b):
        pltpu.make_async_copy(part(ybuf, b), o_ref.at[pl.ds(0, blk)], ssem.at[b]).wait()

    @pl.when(i == 0)
    def _():
        xbuf[...] = jnp.zeros_like(xbuf)
        ybuf[...] = jnp.zeros_like(ybuf)
        for b in range(2):
            spare = pltpu.make_async_copy(part(xbuf, b), o_ref.at[pl.ds(n_real + b * blk, blk)], ssem.at[b])
            spare.start()
            spare.wait()
        start_gather(src_cur, xb)

    fresh = jnp.logical_or(i == 0, be_ref[i] != be_ref[jnp.maximum(i - 1, 0)])

    @pl.when(jnp.logical_and(used, fresh))
    def _():
        for src, dst in ((w1_ref, w1b), (w3_ref, w3b), (w2_ref, w2b)):
            def body(r, carry, src=src, dst=dst):
                rows = pl.ds(pl.multiple_of(r * LANES, LANES), LANES)
                dst[rows, :] = src[0, 0, rows, :].astype(BF16)
                return carry
            lax.fori_loop(0, src.shape[2] // LANES, body, 0)

    @pl.when(jnp.logical_and(used, i > 0))
    def _():
        wait_scatter(lax.rem(i + 1, 3))

    @pl.when(used)
    def _():
        wait_gather(xb)
        x = _tiles_to_rows(xbuf, xb * blk, TM, st).astype(BF16)
        start_gather(src_nxt, 1 - xb)
        start_scatter(dst_prv, lax.rem(i + 2, 3))
        a = _dot(x, w1b[...])
        b = _dot(x, w3b[...])
        hmid = (a * _sigmoid(a)) * b
        _rows_to_tiles(ybuf, yb * blk, _dot(hmid.astype(BF16), w2b[...]))

    @pl.when(i == nu - 1)
    def _():
        start_scatter(dst_cur, yb)
        wait_scatter(yb)
        wait_scatter(lax.rem(i + 2, 3))
        wait_gather(1 - xb)


def _moe(blk_expert, n_used, slot_src, slot_dst, h, w1, w3, w2, layer):
    d, f = w1.shape[2], w1.shape[3]
    st = d // LANES
    blk = TM * st
    n_rows = h.shape[0]
    nb = blk_expert.shape[0]
    wsel = lambda i, be, nu: (layer, be[i], 0, 0)
    smem = lambda f_: pl.BlockSpec((1, 1, TM), f_, memory_space=pltpu.SMEM)
    cur = lambda i, be, nu: (i, 0, 0)
    nxt = lambda i, be, nu: (jnp.minimum(i + 1, nb - 1), 0, 0)
    prv = lambda i, be, nu: (jnp.maximum(i - 1, 0), 0, 0)
    return pl.pallas_call(
        _moe_kernel,
        grid_spec=pltpu.PrefetchScalarGridSpec(
            num_scalar_prefetch=2,
            grid=(nb,),
            in_specs=[smem(cur), smem(nxt), smem(cur), smem(prv),
                      pl.BlockSpec(memory_space=pl.ANY), pl.BlockSpec((1, 1, d, f), wsel),
                      pl.BlockSpec((1, 1, d, f), wsel), pl.BlockSpec((1, 1, f, d), wsel)],
            out_specs=pl.BlockSpec(memory_space=pl.ANY),
            scratch_shapes=[pltpu.VMEM((2 * blk, LANES), F32), pltpu.VMEM((3 * blk, LANES), F32),
                            pltpu.VMEM((d, f), BF16), pltpu.VMEM((d, f), BF16), pltpu.VMEM((f, d), BF16),
                            pltpu.SemaphoreType.DMA((2,)), pltpu.SemaphoreType.DMA((3,))]),
        out_shape=jax.ShapeDtypeStruct((TOP_K * n_rows + 2 * blk, LANES), F32),
        compiler_params=_cparams(("arbitrary",)),
        name="moe_experts",
    )(blk_expert, n_used, slot_src, slot_src, slot_dst, slot_dst, h, w1, w3, w2)


def _combine_kernel(final, *refs):
    y_refs, (x_ref, w_ref, mod_ref), rest = refs[:TOP_K], refs[TOP_K:TOP_K + 3], refs[TOP_K + 3:]
    if final:
        fw_ref, o_ref = rest
    else:
        (o_ref,) = rest
    w = w_ref[0]
    y = None
    for kk in range(TOP_K):
        yk = _tiles_to_rows(y_refs[kk], 0, TM, x_ref.shape[2] // LANES) * w[:, kk:kk + 1]
        y = yk if y is None else y + yk
    xn = x_ref[0] + mod_ref[0, 0][5:6] * y
    if final:
        xn = xn * lax.rsqrt(jnp.mean(xn * xn, axis=-1, keepdims=True) + EPS) * fw_ref[...]
    o_ref[0] = xn


def _combine(y, x, wcol, modl, final_w=None):
    b, t, d = x.shape
    nt = t // TM
    final = final_w is not None
    skip = 1 if final else 0
    tile = lambda i, j: (i, j + skip, 0)
    def plane(kk):
        return pl.BlockSpec((TM * d // LANES, LANES), lambda i, j: ((kk * b + i) * nt + j + skip, 0))

    in_specs = [plane(kk) for kk in range(TOP_K)] + [
        pl.BlockSpec((1, TM, d), tile), pl.BlockSpec((1, TM, TOP_K), tile),
        pl.BlockSpec((1, 1, 8, d), lambda i, j: (i, jnp.minimum(j + skip, 1), 0, 0))]
    args = [y] * TOP_K + [x, wcol, modl]
    if final:
        in_specs.append(pl.BlockSpec((1, d), lambda i, j: (0, 0)))
        args.append(final_w)
    return pl.pallas_call(
        functools.partial(_combine_kernel, final),
        grid=(b, nt - skip),
        in_specs=in_specs,
        out_specs=pl.BlockSpec((1, TM, d), lambda i, j: (i, j, 0)),
        out_shape=jax.ShapeDtypeStruct((b, t - skip * TM, d), F32),
        compiler_params=_cparams(("arbitrary", "arbitrary")),
        name="moe_combine_final" if final else "moe_combine",
    )(*args)


def _slot_plan(ridx, n_tok, st):
    e = ridx.reshape(-1)
    onehot = (e[:, None] == jnp.arange(N_EXPERTS, dtype=jnp.int32)[None, :]).astype(jnp.int32)
    incl = jnp.cumsum(onehot, axis=0)
    rank = jnp.sum((incl - onehot) * onehot, axis=1)
    counts = incl[-1]
    padded = (counts + TM - 1) // TM * TM
    pends = jnp.cumsum(padded)
    pstarts = pends - padded
    dest = pstarts[e] + rank
    n_blocks = (n_tok * TOP_K + N_EXPERTS * (TM - 1) + TM - 1) // TM
    blk_start = jnp.arange(n_blocks, dtype=jnp.int32) * TM
    blk_expert = jnp.minimum(jnp.sum((pends[None, :] <= blk_start[:, None]).astype(jnp.int32), axis=1),
                             N_EXPERTS - 1)
    n_used = (pends[-1] // TM).astype(jnp.int32).reshape(1)
    pair = jnp.arange(n_tok * TOP_K, dtype=jnp.int32)
    row = (pair % TOP_K) * n_tok + pair // TOP_K
    slot = jnp.arange(n_blocks * TM, dtype=jnp.int32)
    spare = n_tok * TOP_K + ((slot // TM) % 2) * TM + slot % TM
    slot_dst = spare.at[dest].set(row, unique_indices=True)
    slot_src = jnp.where(slot_dst < n_tok * TOP_K, slot_dst % n_tok, 0)
    shape3 = (n_blocks, 1, TM)
    return blk_expert.astype(jnp.int32), n_used, (slot_src * st).reshape(shape3), (slot_dst * st).reshape(shape3)


def _rope_tables(t_len, ctx_len):
    half = GLA_DK // 2
    inv = ROPE_BASE ** (-np.arange(0, half, 2, dtype=np.float32) / half)
    lane = np.arange(GLA_DK)
    p = np.arange(t_len - ctx_len)
    pos = np.where(lane[None, :] < half, (p // GRID_W)[:, None], (p % GRID_W)[:, None]).astype(np.float32)
    ang = jnp.asarray(pos) * jnp.asarray(inv[(lane % half) % (half // 2)])[None, :]
    cos, sin = jnp.cos(ang), jnp.sin(ang)
    lower = jnp.asarray((lane % half) < half // 2)[None, :]
    pad = lambda a, v: jnp.concatenate([jnp.full((ctx_len, GLA_DK), v, F32), a], axis=0)
    return pad(cos, 1.0), pad(jnp.where(lower, -sin, 0.0), 0.0), pad(jnp.where(lower, 0.0, sin), 0.0)


def kernel(x, c, ctx, c_ctx, w_ada, b_ada, norm1_w, norm2_w, w_in, w_gk_up, b_gk, gla_norm_w, rpb,
           w_bo_gla, w_bo_na, w_out, w_router, b_router, w1, w3, w2, final_norm_w):
    bsz, seq, d = x.shape
    ctx_len = ctx.shape[1]
    depth = w_ada.shape[0]
    assert ctx_len == TM and seq % TM == 0 and TM % GRID_W == 0 and seq // GRID_W >= 3 * (TM // GRID_W)
    t = ctx_len + seq
    nt = t // TM
    n_tok = bsz * t

    xs = jnp.concatenate([ctx, x], axis=1)
    cvec = jnp.zeros((8, d), F32).at[:bsz].set(c).at[bsz].set(c_ctx)
    mods = _modulation(cvec, w_ada, b_ada).reshape(depth, 8, 6, d)
    cos, sa, sb = _rope_tables(t, ctx_len)
    wrt = w_router.T.astype(F32)
    brt = jnp.broadcast_to(b_router.astype(F32)[:, None], (N_EXPERTS, TM))
    gq_end = 2 * GLA_QK + 2 * GLA_V

    out = None
    for l in range(depth):
        mx = jnp.pad(mods[l, :bsz], ((0, 0), (0, 2), (0, 0)))
        mc = jnp.broadcast_to(jnp.pad(mods[l, bsz], ((0, 2), (0, 0)))[None], mx.shape)
        modl = jnp.stack([mc, mx], axis=1)
        wl = w_in[l]
        wp = jnp.concatenate([wl[:, :gq_end], wl[:, gq_end + 2 * GLA_LR:], wl[:, gq_end:gq_end + 2 * GLA_LR],
                              jnp.zeros((d, LR_PAD - 2 * GLA_LR), wl.dtype)], axis=1).astype(BF16)
        gq, gk, gv, gg, nq, nk, nv, m1, m2, lr = _inproj(xs, modl, norm1_w[l][None], wp, cos, sa, sb)

        def wup(dirn):
            w = jnp.zeros((LR_PAD, GLA_QK), F32)
            return w.at[dirn * GLA_LR:(dirn + 1) * GLA_LR].set(w_gk_up[l, dirn]).astype(BF16)

        ob = _gla_pass(True, gq, gk, gv, lr, wup(1), b_gk[l, 1][None])
        gy = _gla_pass(False, gq, gk, gv, lr, wup(0), b_gk[l, 0][None],
                       extra=(ob, gg, gla_norm_w[l][None]))
        ny = _na(nq, nk, nv, _na_bias_tables(rpb[l], seq // GRID_W))
        xs, hp, ridx, rw = _merge(gy, ny, m1, m2, xs, modl, norm2_w[l][None],
                                  w_bo_gla[l].astype(BF16), w_bo_na[l].astype(BF16), w_out[l].astype(BF16),
                                  wrt, brt)
        ridx2 = jnp.transpose(ridx[:, :TOP_K, :], (0, 2, 1))
        wcol = jnp.transpose(rw[:, :TOP_K, :], (0, 2, 1))
        blk_expert, n_used, slot_src, slot_dst = _slot_plan(ridx2, n_tok, d // LANES)
        y = _moe(blk_expert, n_used, slot_src, slot_dst, hp, w1, w3, w2, l)
        if l < depth - 1:
            xs = _combine(y, xs, wcol, modl)
        else:
            out = _combine(y, xs, wcol, modl, final_norm_w[None])
    return out
```

```python
import functools

import jax
import jax.numpy as jnp
import numpy as np
from jax import lax
from jax.experimental import pallas as pl
from jax.experimental.pallas import tpu as pltpu

F32 = jnp.float32
BF16 = jnp.bfloat16

EPS = 1e-6
GRID_W = 64
GLA_HEADS = 4
GLA_DK = 128
GLA_DV = 256
GLA_LR = 16
GLA_TAU = 16.0
GLA_CHUNK = 64
ROPE_BASE = 10000.0
NA_HEADS = 8
NA_HD = 64
WIN_R = 8
WIN_C = 16
N_EXPERTS = 16
N_GROUPS = 4
EXPERTS_PER_GROUP = N_EXPERTS // N_GROUPS
TOP_K = 2

GLA_QK = GLA_HEADS * GLA_DK
GLA_V = GLA_HEADS * GLA_DV
NA_W = NA_HEADS * NA_HD

TM = 256
LANES = 128
LR_PAD = LANES
NEG = -1e30
VMEM_LIMIT = 56 * 1024 * 1024

_NT = (((1,), (1,)), ((), ()))
_TN = (((0,), (0,)), ((), ()))


def _cparams(sem):
    return pltpu.CompilerParams(dimension_semantics=sem, vmem_limit_bytes=VMEM_LIMIT)


def _dot(a, b):
    return jnp.dot(a, b, preferred_element_type=F32)


def _dg(a, b, dims):
    return lax.dot_general(a, b, dims, preferred_element_type=F32)


def _split(a):
    hi = a.astype(BF16)
    lo = (a - hi.astype(F32)).astype(BF16)
    return hi, lo


def _sigmoid(x):
    return 1.0 / (1.0 + jnp.exp(-x))


def _norm_mod(x, w, shift, scale):
    y = x * lax.rsqrt(jnp.mean(x * x, axis=-1, keepdims=True) + EPS)
    return (y * w) * (1.0 + scale) + shift


def _rows_to_tiles(ref, base, val):
    st = val.shape[1] // LANES
    for s in range(st):
        ref[pl.ds(base + s, val.shape[0], stride=st), :] = val[:, s * LANES:(s + 1) * LANES]


def _tiles_to_rows(ref, base, n, st):
    return jnp.concatenate([ref[pl.ds(base + s, n, stride=st), :] for s in range(st)], axis=1)


def _mod_kernel(c_ref, w_ref, b_ref, o_ref):
    c = c_ref[...]
    s = c * _sigmoid(c)
    sh, sl = _split(s)
    wh, wl = _split(w_ref[0])
    o_ref[0] = _dot(sh, wh) + _dot(sl, wh) + _dot(sh, wl) + b_ref[0]


def _modulation(cvec, w_ada, b_ada):
    depth, d, n = w_ada.shape
    tn = 1024
    return pl.pallas_call(
        _mod_kernel,
        grid=(depth, n // tn),
        in_specs=[pl.BlockSpec((8, d), lambda l, j: (0, 0)),
                  pl.BlockSpec((1, d, tn), lambda l, j: (l, 0, j)),
                  pl.BlockSpec((1, 1, tn), lambda l, j: (l, 0, j))],
        out_specs=pl.BlockSpec((1, 8, tn), lambda l, j: (l, 0, j)),
        out_shape=jax.ShapeDtypeStruct((depth, 8, n), F32),
        compiler_params=_cparams(("arbitrary", "arbitrary")),
        name="modulation",
    )(cvec, w_ada, b_ada.reshape(depth, 1, n))


_IN_OUT = (("gq", GLA_QK, BF16), ("gk", GLA_QK, BF16), ("gv", GLA_V, BF16), ("gg", GLA_V, BF16),
           ("nq", NA_W, BF16), ("nk", NA_W, BF16), ("nv", NA_W, BF16),
           ("m1", None, BF16), ("m2", None, BF16), ("lr", LR_PAD, F32))


def _inproj_kernel(x_ref, mod_ref, nw_ref, w_ref, cos_ref, sa_ref, sb_ref, *o_refs):
    d = x_ref.shape[2]
    mod = mod_ref[0, 0]
    h = _norm_mod(x_ref[0], nw_ref[...], mod[0:1], mod[1:2]).astype(BF16)
    cos, sa, sb = cos_ref[...], sa_ref[...], sb_ref[...]

    def rope(r, scale):
        parts = []
        for hh in range(GLA_HEADS):
            xs = r[:, hh * GLA_DK:(hh + 1) * GLA_DK]
            y = xs * cos + pltpu.roll(xs, GLA_DK - 32, 1) * sa + pltpu.roll(xs, 32, 1) * sb
            parts.append(y * scale if scale != 1.0 else y)
        return jnp.concatenate(parts, axis=1)

    off = 0
    for (name, width, dt), o_ref in zip(_IN_OUT, o_refs):
        width = d if width is None else width
        r = _dot(h, w_ref[:, off:off + width])
        if name == "gq":
            r = rope(r, GLA_DK ** -0.5)
        elif name == "gk":
            r = rope(r, 1.0)
        elif name == "nq":
            r = r * (NA_HD ** -0.5)
        o_ref[0] = r.astype(dt)
        off += width


def _inproj(x, modl, nw, wp, cos, sa, sb):
    b, t, d = x.shape
    nt = t // TM
    widths = [d if w is None else w for _, w, _ in _IN_OUT]
    ncols = sum(widths)
    return pl.pallas_call(
        _inproj_kernel,
        grid=(b, nt),
        in_specs=[pl.BlockSpec((1, TM, d), lambda i, j: (i, j, 0)),
                  pl.BlockSpec((1, 1, 8, d), lambda i, j: (i, jnp.minimum(j, 1), 0, 0)),
                  pl.BlockSpec((1, d), lambda i, j: (0, 0)),
                  pl.BlockSpec((d, ncols), lambda i, j: (0, 0)),
                  pl.BlockSpec((TM, GLA_DK), lambda i, j: (j, 0)),
                  pl.BlockSpec((TM, GLA_DK), lambda i, j: (j, 0)),
                  pl.BlockSpec((TM, GLA_DK), lambda i, j: (j, 0))],
        out_specs=[pl.BlockSpec((1, TM, w), lambda i, j: (i, j, 0)) for w in widths],
        out_shape=[jax.ShapeDtypeStruct((b, t, w), dt) for w, (_, _, dt) in zip(widths, _IN_OUT)],
        compiler_params=_cparams(("arbitrary", "arbitrary")),
        name="inproj",
    )(x, modl, nw, wp, cos, sa, sb)


def _gla_kernel(reverse, finish, q_ref, k_ref, v_ref, lr_ref, wup_ref, bg_ref, *rest):
    if finish:
        ob_ref, g_ref, nw_ref, o_ref, st_ref = rest
    else:
        o_ref, st_ref = rest
    nchunk = TM // GLA_CHUNK

    @pl.when(pl.program_id(1) == 0)
    def _():
        st_ref[...] = jnp.zeros_like(st_ref)

    z = _dot(lr_ref[0].astype(BF16), wup_ref[...]) + bg_ref[...]
    logg = (jnp.minimum(z, 0.0) - jnp.log(1.0 + jnp.exp(-jnp.abs(z)))) * (1.0 / GLA_TAU)
    r = lax.broadcasted_iota(jnp.int32, (TM, TM), 0)
    s = lax.broadcasted_iota(jnp.int32, (TM, TM), 1)
    order = (s >= r) if reverse else (s <= r)
    tri = jnp.where(((r // GLA_CHUNK) == (s // GLA_CHUNK)) & order, 1.0, 0.0).astype(BF16)
    hi, lo = _split(logg)
    bc = _dot(tri, hi) + _dot(tri, lo)
    cmask = order[:GLA_CHUNK, :GLA_CHUNK]

    for c in (range(nchunk - 1, -1, -1) if reverse else range(nchunk)):
        rs = slice(c * GLA_CHUNK, (c + 1) * GLA_CHUNK)
        for hh in range(GLA_HEADS):
            ls = slice(hh * GLA_DK, (hh + 1) * GLA_DK)
            vs = slice(hh * GLA_DV, (hh + 1) * GLA_DV)
            b = bc[rs, ls]
            bend = b[0:1] if reverse else b[GLA_CHUNK - 1:GLA_CHUNK]
            qc = q_ref[0, rs, ls].astype(F32)
            kc = k_ref[0, rs, ls].astype(F32)
            vc = v_ref[0, rs, vs]
            kd = (kc * jnp.exp(bend - b)).astype(BF16)
            qi = (qc * jnp.exp(b)).astype(BF16)
            qa = (qc * jnp.exp(b - bend)).astype(BF16)
            att = jnp.where(cmask, _dg(qa, kd, _NT), 0.0).astype(BF16)
            st = st_ref[hh]
            o = _dg(qi, st.astype(BF16), _NT) + _dot(att, vc)
            st_ref[hh] = st * jnp.exp(bend) + _dg(vc, kd, _TN)
            if finish:
                o = o + ob_ref[0, rs, vs].astype(F32)
                o = o * lax.rsqrt(jnp.mean(o * o, axis=-1, keepdims=True) + EPS) * nw_ref[...]
                g = g_ref[0, rs, vs].astype(F32)
                o = o * (g * _sigmoid(g))
            o_ref[0, rs, vs] = o.astype(o_ref.dtype)


def _gla_pass(reverse, q, k, v, lr, wup, bg, extra=None):
    b, t, _ = q.shape
    nt = t // TM
    if reverse:
        tile = lambda i, j: (i, jnp.where(j == 0, 0, nt - j), 0)
    else:
        tile = lambda i, j: (i, j, 0)
    const = lambda i, j: (0, 0)
    in_specs = [pl.BlockSpec((1, TM, GLA_QK), tile), pl.BlockSpec((1, TM, GLA_QK), tile),
                pl.BlockSpec((1, TM, GLA_V), tile), pl.BlockSpec((1, TM, LR_PAD), tile),
                pl.BlockSpec((LR_PAD, GLA_QK), const), pl.BlockSpec((1, GLA_QK), const)]
    args = [q, k, v, lr, wup, bg]
    if extra is not None:
        ob, g, nw = extra
        in_specs += [pl.BlockSpec((1, TM, GLA_V), tile), pl.BlockSpec((1, TM, GLA_V), tile),
                     pl.BlockSpec((1, GLA_DV), const)]
        args += [ob, g, nw]
    return pl.pallas_call(
        functools.partial(_gla_kernel, reverse, extra is not None),
        grid=(b, nt),
        in_specs=in_specs,
        out_specs=pl.BlockSpec((1, TM, GLA_V), tile),
        out_shape=jax.ShapeDtypeStruct((b, t, GLA_V), BF16),
        scratch_shapes=[pltpu.VMEM((GLA_HEADS, GLA_DV, GLA_DK), F32)],
        compiler_params=_cparams(("arbitrary", "arbitrary")),
        name="gla_bwd" if reverse else "gla_fwd",
    )(*args)


def _na_bias_tables(rpb, rows):
    rpt = TM // GRID_W
    kr = min(WIN_R, rows)
    col = np.arange(GRID_W)
    cidx = np.clip(col[None, :] - col[:, None], -(WIN_C - 1), WIN_C - 1) + (WIN_C - 1)
    c_sel = (cidx[None] == np.arange(2 * WIN_C - 1)[:, None, None]).astype(np.float32)
    c0 = np.clip(col - WIN_C // 2, 0, GRID_W - WIN_C)
    c_ok = (col[None, :] >= c0[:, None]) & (col[None, :] < c0[:, None] + WIN_C)
    toep = jnp.einsum("hrd,dqk->hrqk", rpb.astype(F32), c_sel, precision=lax.Precision.HIGHEST)
    tabs = [jnp.full((NA_HEADS, TM, 3 * TM), NEG, F32)]
    for r_base, u_base in ((0, 0), (rpt, 0), (rows - rpt, rows - 3 * rpt)):
        rq = r_base + np.arange(rpt)[:, None]
        rk = u_base + np.arange(3 * rpt)[None, :]
        r0 = np.clip(rq - kr // 2, 0, rows - kr)
        r_ok = (rk >= r0) & (rk < r0 + kr)
        ridx = np.clip(rk - rq + (WIN_R - 1), 0, 2 * WIN_R - 2)
        r_sel = (ridx[None] == np.arange(2 * WIN_R - 1)[:, None, None]).astype(np.float32)
        bias = jnp.einsum("hrqk,rim->hiqmk", toep, r_sel, precision=lax.Precision.HIGHEST)
        valid = r_ok[:, None, :, None] & c_ok[None, :, None, :]
        tabs.append(jnp.where(valid[None], bias, NEG).reshape(NA_HEADS, TM, 3 * TM))
    return jnp.stack(tabs)


def _na_kernel(q_ref, kp_ref, kc_ref, kn_ref, kx_ref, vp_ref, vc_ref, vn_ref, vx_ref, bias_ref, o_ref):
    lane = lax.broadcasted_iota(jnp.int32, (TM, 2 * NA_HD), 1)
    first = lane < NA_HD
    k_refs = (kp_ref, kc_ref, kn_ref, kx_ref)
    v_refs = (vp_ref, vc_ref, vn_ref, vx_ref)
    for hp in range(NA_HEADS // 2):
        ls = slice(hp * 2 * NA_HD, (hp + 1) * 2 * NA_HD)
        q2 = q_ref[0, :, ls]
        ks = [kr[0, :, ls] for kr in k_refs]
        vs = [vr[0, :, ls] for vr in v_refs]
        outs = []
        for sub in range(2):
            hd = 2 * hp + sub
            qm = jnp.where(first if sub == 0 else jnp.logical_not(first), q2, jnp.zeros_like(q2))
            sc = []
            for i in range(4):
                s = _dg(qm, ks[i], _NT)
                if i < 3:
                    s = s + bias_ref[0, hd, :, i * TM:(i + 1) * TM]
                sc.append(s)
            m = jnp.max(sc[0], axis=-1, keepdims=True)
            for i in range(1, 4):
                m = jnp.maximum(m, jnp.max(sc[i], axis=-1, keepdims=True))
            l = jnp.zeros_like(m)
            acc = jnp.zeros((TM, 2 * NA_HD), F32)
            for i in range(4):
                p = jnp.exp(sc[i] - m)
                l = l + jnp.sum(p, axis=-1, keepdims=True)
                acc = acc + _dot(p.astype(BF16), vs[i])
            outs.append(acc / l)
        o_ref[0, :, ls] = jnp.where(first, outs[0], outs[1]).astype(o_ref.dtype)


def _na(q, k, v, bias):
    b, t, _ = q.shape
    nt = t // TM
    qt = lambda i, j: (i, j, 0)
    ctx = lambda i, j: (i, 0, 0)

    def near(o):
        return lambda i, j: (i, jnp.clip(j, 2, nt - 2) + o, 0)

    def kind(i, j):
        return (jnp.where(j == 0, 0, jnp.where(j == 1, 1, jnp.where(j == nt - 1, 3, 2))), 0, 0, 0)

    blk = lambda f: pl.BlockSpec((1, TM, NA_W), f)
    return pl.pallas_call(
        _na_kernel,
        grid=(b, nt),
        in_specs=[blk(qt), blk(near(-1)), blk(near(0)), blk(near(1)), blk(ctx),
                  blk(near(-1)), blk(near(0)), blk(near(1)), blk(ctx),
                  pl.BlockSpec((1, NA_HEADS, TM, 3 * TM), kind)],
        out_specs=blk(qt),
        out_shape=jax.ShapeDtypeStruct((b, t, NA_W), BF16),
        compiler_params=_cparams(("arbitrary", "arbitrary")),
        name="natten",
    )(q, k, k, k, k, v, v, v, v, bias)


def _route(sel, aff):
    rows = lambda a, e: a[e:e + 1, :]
    gscore = []
    for g in range(N_GROUPS):
        a, b, c, d = (rows(sel, EXPERTS_PER_GROUP * g + i) for i in range(EXPERTS_PER_GROUP))
        hi1, lo1 = jnp.maximum(a, b), jnp.minimum(a, b)
        hi2, lo2 = jnp.maximum(c, d), jnp.minimum(c, d)
        gscore.append(jnp.maximum(hi1, hi2) + jnp.maximum(jnp.minimum(hi1, hi2), jnp.maximum(lo1, lo2)))
    gbest = jnp.zeros_like(gscore[0], dtype=jnp.int32)
    gval = gscore[0]
    for g in range(1, N_GROUPS):
        better = gscore[g] > gval
        gbest = jnp.where(better, g, gbest)
        gval = jnp.where(better, gscore[g], gval)
    cs, ca = [], []
    for i in range(EXPERTS_PER_GROUP):
        s_i, a_i = rows(sel, i), rows(aff, i)
        for g in range(1, N_GROUPS):
            pick = gbest == g
            s_i = jnp.where(pick, rows(sel, EXPERTS_PER_GROUP * g + i), s_i)
            a_i = jnp.where(pick, rows(aff, EXPERTS_PER_GROUP * g + i), a_i)
        cs.append(s_i)
        ca.append(a_i)
    i1 = jnp.zeros_like(gbest)
    v1, w1 = cs[0], ca[0]
    for i in range(1, EXPERTS_PER_GROUP):
        better = cs[i] > v1
        i1 = jnp.where(better, i, i1)
        v1 = jnp.where(better, cs[i], v1)
        w1 = jnp.where(better, ca[i], w1)
    i2 = jnp.full_like(gbest, -1)
    v2 = jnp.full_like(v1, -jnp.inf)
    w2 = jnp.zeros_like(w1)
    for i in range(EXPERTS_PER_GROUP):
        better = (i1 != i) & ((cs[i] > v2) | (i2 < 0))
        i2 = jnp.where(better, i, i2)
        v2 = jnp.where(better, cs[i], v2)
        w2 = jnp.where(better, ca[i], w2)
    tot = w1 + w2
    base = gbest * EXPERTS_PER_GROUP
    return base + i1, base + i2, w1 / tot, w2 / tot


def _merge_kernel(gy_ref, ny_ref, m1_ref, m2_ref, x_ref, mod_ref, nw_ref, wbg_ref, wbn_ref, wo_ref,
                  wr_ref, br_ref, xo_ref, hp_ref, ri_ref, rw_ref):
    a = _dot(gy_ref[0], wbg_ref[...])
    b = _dot(ny_ref[0], wbn_ref[...])
    m = _sigmoid(m1_ref[0].astype(F32)) * a + _sigmoid(m2_ref[0].astype(F32)) * b
    y = _dot(m.astype(BF16), wo_ref[...])
    mod = mod_ref[0, 0]
    xn = x_ref[0] + mod[2:3] * y
    xo_ref[0] = xn
    h2 = _norm_mod(xn, nw_ref[...], mod[3:4], mod[4:5])
    _rows_to_tiles(hp_ref, 0, h2)
    hh, hl = _split(h2)
    wh, wl = _split(wr_ref[...])
    logit = _dg(wh, hh, _NT) + _dg(wh, hl, _NT) + _dg(wl, hh, _NT)
    aff = _sigmoid(logit)
    i1, i2, w1, w2 = _route(aff + br_ref[...], aff)
    zi = jnp.zeros((6, TM), jnp.int32)
    ri_ref[0] = jnp.concatenate([i1, i2, zi], axis=0)
    rw_ref[0] = jnp.concatenate([w1, w2, zi.astype(F32)], axis=0)


def _merge(gy, ny, m1, m2, x, modl, nw2, wbg, wbn, wo, wrt, brt):
    b, t, d = x.shape
    nt = t // TM
    tile = lambda i, j: (i, j, 0)
    const = lambda i, j: (0, 0)
    return pl.pallas_call(
        _merge_kernel,
        grid=(b, nt),
        in_specs=[pl.BlockSpec((1, TM, GLA_V), tile), pl.BlockSpec((1, TM, NA_W), tile),
                  pl.BlockSpec((1, TM, d), tile), pl.BlockSpec((1, TM, d), tile),
                  pl.BlockSpec((1, TM, d), tile),
                  pl.BlockSpec((1, 1, 8, d), lambda i, j: (i, jnp.minimum(j, 1), 0, 0)),
                  pl.BlockSpec((1, d), const),
                  pl.BlockSpec((GLA_V, d), const), pl.BlockSpec((NA_W, d), const),
                  pl.BlockSpec((d, d), const),
                  pl.BlockSpec((N_EXPERTS, d), const), pl.BlockSpec((N_EXPERTS, TM), const)],
        out_specs=[pl.BlockSpec((1, TM, d), tile),
                   pl.BlockSpec((TM * d // LANES, LANES), lambda i, j: (i * nt + j, 0)),
                   pl.BlockSpec((1, 8, TM), lambda i, j: (i, 0, j)),
                   pl.BlockSpec((1, 8, TM), lambda i, j: (i, 0, j))],
        out_shape=[jax.ShapeDtypeStruct((b, t, d), F32), jax.ShapeDtypeStruct((b * t * d // LANES, LANES), F32),
                   jax.ShapeDtypeStruct((b, 8, t), jnp.int32), jax.ShapeDtypeStruct((b, 8, t), F32)],
        compiler_params=_cparams(("arbitrary", "arbitrary")),
        name="merge_router",
    )(gy, ny, m1, m2, x, modl, nw2, wbg, wbn, wo, wrt, brt)


def _moe_kernel(be_ref, nu_ref, src_cur, src_nxt, dst_cur, dst_prv, h_ref, w1_ref, w3_ref, w2_ref,
                o_ref, xbuf, ybuf, w1b, w3b, w2b, gsem, ssem):
    i = pl.program_id(0)
    nu = nu_ref[0]
    xb = lax.rem(i, 2)
    yb = lax.rem(i, 3)
    used = i < nu
    st = w1_ref.shape[2] // LANES
    blk = TM * st
    n_real = o_ref.shape[0] - 2 * blk

    def part(ref, b):
        return ref.at[pl.ds(pl.multiple_of(b * blk, blk), blk)]

    def token(ref, row):
        return ref.at[pl.ds(pl.multiple_of(row, st), st)]

    def start_gather(idx_ref, b):
        for r in range(TM):
            pltpu.make_async_copy(token(h_ref, idx_ref[0, 0, r]), token(xbuf, b * blk + r * st),
                                  gsem.at[b]).start(priority=r % 2)

    def start_scatter(idx_ref, b):
        for r in range(TM):
            pltpu.make_async_copy(token(ybuf, b * blk + r * st), token(o_ref, idx_ref[0, 0, r]),
                                  ssem.at[b]).start(priority=r % 2)

    def wait_gather(b):
        pltpu.make_async_copy(h_ref.at[pl.ds(0, blk)], part(xbuf, b), gsem.at[b]).wait()

    def wait_scatter(b):
        pltpu.make_async_copy(part(ybuf, b), o_ref.at[pl.ds(0, blk)], ssem.at[b]).wait()

    @pl.when(i == 0)
    def _():
        xbuf[...] = jnp.zeros_like(xbuf)
        ybuf[...] = jnp.zeros_like(ybuf)
        for b in range(2):
            spare = pltpu.make_async_copy(part(xbuf, b), o_ref.at[pl.ds(n_real + b * blk, blk)], ssem.at[b])
            spare.start()
            spare.wait()
        start_gather(src_cur, xb)

    fresh = jnp.logical_or(i == 0, be_ref[i] != be_ref[jnp.maximum(i - 1, 0)])

    @pl.when(jnp.logical_and(used, fresh))
    def _():
        for src, dst in ((w1_ref, w1b), (w3_ref, w3b), (w2_ref, w2b)):
            def body(r, carry, src=src, dst=dst):
                rows = pl.ds(pl.multiple_of(r * LANES, LANES), LANES)
                dst[rows, :] = src[0, 0, rows, :].astype(BF16)
                return carry
            lax.fori_loop(0, src.shape[2] // LANES, body, 0)

    @pl.when(jnp.logical_and(used, i > 0))
    def _():
        wait_scatter(lax.rem(i + 1, 3))

    @pl.when(used)
    def _():
        wait_gather(xb)
        x = _tiles_to_rows(xbuf, xb * blk, TM, st).astype(BF16)
        start_gather(src_nxt, 1 - xb)
        start_scatter(dst_prv, lax.rem(i + 2, 3))
        a = _dot(x, w1b[...])
        b = _dot(x, w3b[...])
        hmid = (a * _sigmoid(a)) * b
        _rows_to_tiles(ybuf, yb * blk, _dot(hmid.astype(BF16), w2b[...]))

    @pl.when(i == nu - 1)
    def _():
        start_scatter(dst_cur, yb)
        wait_scatter(yb)
        wait_scatter(lax.rem(i + 2, 3))
        wait_gather(1 - xb)


def _moe(blk_expert, n_used, slot_src, slot_dst, h, w1, w3, w2, layer):
    d, f = w1.shape[2], w1.shape[3]
    st = d // LANES
    blk = TM * st
    n_rows = h.shape[0]
    nb = blk_expert.shape[0]
    wsel = lambda i, be, nu: (layer, be[i], 0, 0)
    smem = lambda f_: pl.BlockSpec((1, 1, TM), f_, memory_space=pltpu.SMEM)
    cur = lambda i, be, nu: (i, 0, 0)
    nxt = lambda i, be, nu: (jnp.minimum(i + 1, nb - 1), 0, 0)
    prv = lambda i, be, nu: (jnp.maximum(i - 1, 0), 0, 0)
    return pl.pallas_call(
        _moe_kernel,
        grid_spec=pltpu.PrefetchScalarGridSpec(
            num_scalar_prefetch=2,
            grid=(nb,),
            in_specs=[smem(cur), smem(nxt), smem(cur), smem(prv),
                      pl.BlockSpec(memory_space=pl.ANY), pl.BlockSpec((1, 1, d, f), wsel),
                      pl.BlockSpec((1, 1, d, f), wsel), pl.BlockSpec((1, 1, f, d), wsel)],
            out_specs=pl.BlockSpec(memory_space=pl.ANY),
            scratch_shapes=[pltpu.VMEM((2 * blk, LANES), F32), pltpu.VMEM((3 * blk, LANES), F32),
                            pltpu.VMEM((d, f), BF16), pltpu.VMEM((d, f), BF16), pltpu.VMEM((f, d), BF16),
                            pltpu.SemaphoreType.DMA((2,)), pltpu.SemaphoreType.DMA((3,))]),
        out_shape=jax.ShapeDtypeStruct((TOP_K * n_rows + 2 * blk, LANES), F32),
        compiler_params=_cparams(("arbitrary",)),
        name="moe_experts",
    )(blk_expert, n_used, slot_src, slot_src, slot_dst, slot_dst, h, w1, w3, w2)


def _combine_kernel(final, *refs):
    y_refs, (x_ref, w_ref, mod_ref), rest = refs[:TOP_K], refs[TOP_K:TOP_K + 3], refs[TOP_K + 3:]
    if final:
        fw_ref, o_ref = rest
    else:
        (o_ref,) = rest
    w = w_ref[0]
    y = None
    for kk in range(TOP_K):
        yk = _tiles_to_rows(y_refs[kk], 0, TM, x_ref.shape[2] // LANES) * w[:, kk:kk + 1]
        y = yk if y is None else y + yk
    xn = x_ref[0] + mod_ref[0, 0][5:6] * y
    if final:
        xn = xn * lax.rsqrt(jnp.mean(xn * xn, axis=-1, keepdims=True) + EPS) * fw_ref[...]
    o_ref[0] = xn


def _combine(y, x, wcol, modl, final_w=None):
    b, t, d = x.shape
    nt = t // TM
    final = final_w is not None
    skip = 1 if final else 0
    tile = lambda i, j: (i, j + skip, 0)
    def plane(kk):
        return pl.BlockSpec((TM * d // LANES, LANES), lambda i, j: ((kk * b + i) * nt + j + skip, 0))

    in_specs = [plane(kk) for kk in range(TOP_K)] + [
        pl.BlockSpec((1, TM, d), tile), pl.BlockSpec((1, TM, TOP_K), tile),
        pl.BlockSpec((1, 1, 8, d), lambda i, j: (i, jnp.minimum(j + skip, 1), 0, 0))]
    args = [y] * TOP_K + [x, wcol, modl]
    if final:
        in_specs.append(pl.BlockSpec((1, d), lambda i, j: (0, 0)))
        args.append(final_w)
    return pl.pallas_call(
        functools.partial(_combine_kernel, final),
        grid=(b, nt - skip),
        in_specs=in_specs,
        out_specs=pl.BlockSpec((1, TM, d), lambda i, j: (i, j, 0)),
        out_shape=jax.ShapeDtypeStruct((b, t - skip * TM, d), F32),
        compiler_params=_cparams(("arbitrary", "arbitrary")),
        name="moe_combine_final" if final else "moe_combine",
    )(*args)


def _slot_plan(ridx, n_tok, st):
    e = ridx.reshape(-1)
    onehot = (e[:, None] == jnp.arange(N_EXPERTS, dtype=jnp.int32)[None, :]).astype(jnp.int32)
    incl = jnp.cumsum(onehot, axis=0)
    rank = jnp.sum((incl - onehot) * onehot, axis=1)
    counts = incl[-1]
    padded = (counts + TM - 1) // TM * TM
    pends = jnp.cumsum(padded)
    pstarts = pends - padded
    dest = pstarts[e] + rank
    n_blocks = (n_tok * TOP_K + N_EXPERTS * (TM - 1) + TM - 1) // TM
    blk_start = jnp.arange(n_blocks, dtype=jnp.int32) * TM
    blk_expert = jnp.minimum(jnp.sum((pends[None, :] <= blk_start[:, None]).astype(jnp.int32), axis=1),
                             N_EXPERTS - 1)
    n_used = (pends[-1] // TM).astype(jnp.int32).reshape(1)
    pair = jnp.arange(n_tok * TOP_K, dtype=jnp.int32)
    row = (pair % TOP_K) * n_tok + pair // TOP_K
    slot = jnp.arange(n_blocks * TM, dtype=jnp.int32)
    spare = n_tok * TOP_K + ((slot // TM) % 2) * TM + slot % TM
    slot_dst = spare.at[dest].set(row, unique_indices=True)
    slot_src = jnp.where(slot_dst < n_tok * TOP_K, slot_dst % n_tok, 0)
    shape3 = (n_blocks, 1, TM)
    return blk_expert.astype(jnp.int32), n_used, (slot_src * st).reshape(shape3), (slot_dst * st).reshape(shape3)


def _rope_tables(t_len, ctx_len):
    half = GLA_DK // 2
    inv = ROPE_BASE ** (-np.arange(0, half, 2, dtype=np.float32) / half)
    lane = np.arange(GLA_DK)
    p = np.arange(t_len - ctx_len)
    pos = np.where(lane[None, :] < half, (p // GRID_W)[:, None], (p % GRID_W)[:, None]).astype(np.float32)
    ang = jnp.asarray(pos) * jnp.asarray(inv[(lane % half) % (half // 2)])[None, :]
    cos, sin = jnp.cos(ang), jnp.sin(ang)
    lower = jnp.asarray((lane % half) < half // 2)[None, :]
    pad = lambda a, v: jnp.concatenate([jnp.full((ctx_len, GLA_DK), v, F32), a], axis=0)
    return pad(cos, 1.0), pad(jnp.where(lower, -sin, 0.0), 0.0), pad(jnp.where(lower, 0.0, sin), 0.0)


def kernel(x, c, ctx, c_ctx, w_ada, b_ada, norm1_w, norm2_w, w_in, w_gk_up, b_gk, gla_norm_w, rpb,
           w_bo_gla, w_bo_na, w_out, w_router, b_router, w1, w3, w2, final_norm_w):
    bsz, seq, d = x.shape
    ctx_len = ctx.shape[1]
    depth = w_ada.shape[0]
    assert ctx_len == TM and seq % TM == 0 and TM % GRID_W == 0 and seq // GRID_W >= 3 * (TM // GRID_W)
    t = ctx_len + seq
    nt = t // TM
    n_tok = bsz * t

    xs = jnp.concatenate([ctx, x], axis=1)
    cvec = jnp.zeros((8, d), F32).at[:bsz].set(c).at[bsz].set(c_ctx)
    mods = _modulation(cvec, w_ada, b_ada).reshape(depth, 8, 6, d)
    cos, sa, sb = _rope_tables(t, ctx_len)
    wrt = w_router.T.astype(F32)
    brt = jnp.broadcast_to(b_router.astype(F32)[:, None], (N_EXPERTS, TM))
    gq_end = 2 * GLA_QK + 2 * GLA_V

    out = None
    for l in range(depth):
        mx = jnp.pad(mods[l, :bsz], ((0, 0), (0, 2), (0, 0)))
        mc = jnp.broadcast_to(jnp.pad(mods[l, bsz], ((0, 2), (0, 0)))[None], mx.shape)
        modl = jnp.stack([mc, mx], axis=1)
        wl = w_in[l]
        wp = jnp.concatenate([wl[:, :gq_end], wl[:, gq_end + 2 * GLA_LR:], wl[:, gq_end:gq_end + 2 * GLA_LR],
                              jnp.zeros((d, LR_PAD - 2 * GLA_LR), wl.dtype)], axis=1).astype(BF16)
        gq, gk, gv, gg, nq, nk, nv, m1, m2, lr = _inproj(xs, modl, norm1_w[l][None], wp, cos, sa, sb)

        def wup(dirn):
            w = jnp.zeros((LR_PAD, GLA_QK), F32)
            return w.at[dirn * GLA_LR:(dirn + 1) * GLA_LR].set(w_gk_up[l, dirn]).astype(BF16)

        ob = _gla_pass(True, gq, gk, gv, lr, wup(1), b_gk[l, 1][None])
        gy = _gla_pass(False, gq, gk, gv, lr, wup(0), b_gk[l, 0][None],
                       extra=(ob, gg, gla_norm_w[l][None]))
        ny = _na(nq, nk, nv, _na_bias_tables(rpb[l], seq // GRID_W))
        xs, hp, ridx, rw = _merge(gy, ny, m1, m2, xs, modl, norm2_w[l][None],
                                  w_bo_gla[l].astype(BF16), w_bo_na[l].astype(BF16), w_out[l].astype(BF16),
                                  wrt, brt)
        ridx2 = jnp.transpose(ridx[:, :TOP_K, :], (0, 2, 1))
        wcol = jnp.transpose(rw[:, :TOP_K, :], (0, 2, 1))
        blk_expert, n_used, slot_src, slot_dst = _slot_plan(ridx2, n_tok, d // LANES)
        y = _moe(blk_expert, n_used, slot_src, slot_dst, hp, w1, w3, w2, l)
        if l < depth - 1:
            xs = _combine(y, xs, wcol, modl)
        else:
            out = _combine(y, xs, wcol, modl, final_norm_w[None])
    return out
```

```python
import functools

import jax
import jax.numpy as jnp
import numpy as np
from jax import lax
from jax.experimental import pallas as pl
from jax.experimental.pallas import tpu as pltpu

F32 = jnp.float32
BF16 = jnp.bfloat16

EPS = 1e-6
GRID_W = 64
GLA_HEADS = 4
GLA_DK = 128
GLA_DV = 256
GLA_LR = 16
GLA_TAU = 16.0
GLA_CHUNK = 64
ROPE_BASE = 10000.0
NA_HEADS = 8
NA_HD = 64
WIN_R = 8
WIN_C = 16
N_EXPERTS = 16
N_GROUPS = 4
EXPERTS_PER_GROUP = N_EXPERTS // N_GROUPS
TOP_K = 2

GLA_QK = GLA_HEADS * GLA_DK
GLA_V = GLA_HEADS * GLA_DV
NA_W = NA_HEADS * NA_HD

TM = 256
LANES = 128
LR_PAD = LANES
NEG = -1e30
VMEM_LIMIT = 56 * 1024 * 1024

_NT = (((1,), (1,)), ((), ()))
_TN = (((0,), (0,)), ((), ()))


def _cparams(sem):
    return pltpu.CompilerParams(dimension_semantics=sem, vmem_limit_bytes=VMEM_LIMIT)


def _dot(a, b):
    return jnp.dot(a, b, preferred_element_type=F32)


def _dg(a, b, dims):
    return lax.dot_general(a, b, dims, preferred_element_type=F32)


def _split(a):
    hi = a.astype(BF16)
    lo = (a - hi.astype(F32)).astype(BF16)
    return hi, lo


def _sigmoid(x):
    return 1.0 / (1.0 + jnp.exp(-x))


def _norm_mod(x, w, shift, scale):
    y = x * lax.rsqrt(jnp.mean(x * x, axis=-1, keepdims=True) + EPS)
    return (y * w) * (1.0 + scale) + shift


def _rows_to_tiles(ref, base, val):
    st = val.shape[1] // LANES
    for s in range(st):
        ref[pl.ds(base + s, val.shape[0], stride=st), :] = val[:, s * LANES:(s + 1) * LANES]


def _tiles_to_rows(ref, base, n, st):
    return jnp.concatenate([ref[pl.ds(base + s, n, stride=st), :] for s in range(st)], axis=1)


def _mod_kernel(c_ref, w_ref, b_ref, o_ref):
    c = c_ref[...]
    s = c * _sigmoid(c)
    sh, sl = _split(s)
    wh, wl = _split(w_ref[0])
    o_ref[0] = _dot(sh, wh) + _dot(sl, wh) + _dot(sh, wl) + b_ref[0]


def _modulation(cvec, w_ada, b_ada):
    depth, d, n = w_ada.shape
    tn = 1024
    return pl.pallas_call(
        _mod_kernel,
        grid=(depth, n // tn),
        in_specs=[pl.BlockSpec((8, d), lambda l, j: (0, 0)),
                  pl.BlockSpec((1, d, tn), lambda l, j: (l, 0, j)),
                  pl.BlockSpec((1, 1, tn), lambda l, j: (l, 0, j))],
        out_specs=pl.BlockSpec((1, 8, tn), lambda l, j: (l, 0, j)),
        out_shape=jax.ShapeDtypeStruct((depth, 8, n), F32),
        compiler_params=_cparams(("arbitrary", "arbitrary")),
        name="modulation",
    )(cvec, w_ada, b_ada.reshape(depth, 1, n))


_IN_OUT = (("gq", GLA_QK, BF16), ("gk", GLA_QK, BF16), ("gv", GLA_V, BF16), ("gg", GLA_V, BF16),
           ("nq", NA_W, BF16), ("nk", NA_W, BF16), ("nv", NA_W, BF16),
           ("m1", None, BF16), ("m2", None, BF16), ("lr", LR_PAD, F32))


def _inproj_kernel(x_ref, mod_ref, nw_ref, w_ref, cos_ref, sa_ref, sb_ref, *o_refs):
    d = x_ref.shape[2]
    mod = mod_ref[0, 0]
    h = _norm_mod(x_ref[0], nw_ref[...], mod[0:1], mod[1:2]).astype(BF16)
    cos, sa, sb = cos_ref[...], sa_ref[...], sb_ref[...]

    def rope(r, scale):
        parts = []
        for hh in range(GLA_HEADS):
            xs = r[:, hh * GLA_DK:(hh + 1) * GLA_DK]
            y = xs * cos + pltpu.roll(xs, GLA_DK - 32, 1) * sa + pltpu.roll(xs, 32, 1) * sb
            parts.append(y * scale if scale != 1.0 else y)
        return jnp.concatenate(parts, axis=1)

    off = 0
    for (name, width, dt), o_ref in zip(_IN_OUT, o_refs):
        width = d if width is None else width
        r = _dot(h, w_ref[:, off:off + width])
        if name == "gq":
            r = rope(r, GLA_DK ** -0.5)
        elif name == "gk":
            r = rope(r, 1.0)
        elif name == "nq":
            r = r * (NA_HD ** -0.5)
        o_ref[0] = r.astype(dt)
        off += width


def _inproj(x, modl, nw, wp, cos, sa, sb):
    b, t, d = x.shape
    nt = t // TM
    widths = [d if w is None else w for _, w, _ in _IN_OUT]
    ncols = sum(widths)
    return pl.pallas_call(
        _inproj_kernel,
        grid=(b, nt),
        in_specs=[pl.BlockSpec((1, TM, d), lambda i, j: (i, j, 0)),
                  pl.BlockSpec((1, 1, 8, d), lambda i, j: (i, jnp.minimum(j, 1), 0, 0)),
                  pl.BlockSpec((1, d), lambda i, j: (0, 0)),
                  pl.BlockSpec((d, ncols), lambda i, j: (0, 0)),
                  pl.BlockSpec((TM, GLA_DK), lambda i, j: (j, 0)),
                  pl.BlockSpec((TM, GLA_DK), lambda i, j: (j, 0)),
                  pl.BlockSpec((TM, GLA_DK), lambda i, j: (j, 0))],
        out_specs=[pl.BlockSpec((1, TM, w), lambda i, j: (i, j, 0)) for w in widths],
        out_shape=[jax.ShapeDtypeStruct((b, t, w), dt) for w, (_, _, dt) in zip(widths, _IN_OUT)],
        compiler_params=_cparams(("arbitrary", "arbitrary")),
        name="inproj",
    )(x, modl, nw, wp, cos, sa, sb)


def _gla_kernel(reverse, finish, q_ref, k_ref, v_ref, lr_ref, wup_ref, bg_ref, *rest):
    if finish:
        ob_ref, g_ref, nw_ref, o_ref, st_ref = rest
    else:
        o_ref, st_ref = rest
    nchunk = TM // GLA_CHUNK

    @pl.when(pl.program_id(1) == 0)
    def _():
        st_ref[...] = jnp.zeros_like(st_ref)

    z = _dot(lr_ref[0].astype(BF16), wup_ref[...]) + bg_ref[...]
    logg = (jnp.minimum(z, 0.0) - jnp.log(1.0 + jnp.exp(-jnp.abs(z)))) * (1.0 / GLA_TAU)
    r = lax.broadcasted_iota(jnp.int32, (TM, TM), 0)
    s = lax.broadcasted_iota(jnp.int32, (TM, TM), 1)
    order = (s >= r) if reverse else (s <= r)
    tri = jnp.where(((r // GLA_CHUNK) == (s // GLA_CHUNK)) & order, 1.0, 0.0).astype(BF16)
    hi, lo = _split(logg)
    bc = _dot(tri, hi) + _dot(tri, lo)
    cmask = order[:GLA_CHUNK, :GLA_CHUNK]

    for c in (range(nchunk - 1, -1, -1) if reverse else range(nchunk)):
        rs = slice(c * GLA_CHUNK, (c + 1) * GLA_CHUNK)
        for hh in range(GLA_HEADS):
            ls = slice(hh * GLA_DK, (hh + 1) * GLA_DK)
            vs = slice(hh * GLA_DV, (hh + 1) * GLA_DV)
            b = bc[rs, ls]
            bend = b[0:1] if reverse else b[GLA_CHUNK - 1:GLA_CHUNK]
            qc = q_ref[0, rs, ls].astype(F32)
            kc = k_ref[0, rs, ls].astype(F32)
            vc = v_ref[0, rs, vs]
            kd = (kc * jnp.exp(bend - b)).astype(BF16)
            qi = (qc * jnp.exp(b)).astype(BF16)
            qa = (qc * jnp.exp(b - bend)).astype(BF16)
            att = jnp.where(cmask, _dg(qa, kd, _NT), 0.0).astype(BF16)
            st = st_ref[hh]
            o = _dg(qi, st.astype(BF16), _NT) + _dot(att, vc)
            st_ref[hh] = st * jnp.exp(bend) + _dg(vc, kd, _TN)
            if finish:
                o = o + ob_ref[0, rs, vs].astype(F32)
                o = o * lax.rsqrt(jnp.mean(o * o, axis=-1, keepdims=True) + EPS) * nw_ref[...]
                g = g_ref[0, rs, vs].astype(F32)
                o = o * (g * _sigmoid(g))
            o_ref[0, rs, vs] = o.astype(o_ref.dtype)


def _gla_pass(reverse, q, k, v, lr, wup, bg, extra=None):
    b, t, _ = q.shape
    nt = t // TM
    if reverse:
        tile = lambda i, j: (i, jnp.where(j == 0, 0, nt - j), 0)
    else:
        tile = lambda i, j: (i, j, 0)
    const = lambda i, j: (0, 0)
    in_specs = [pl.BlockSpec((1, TM, GLA_QK), tile), pl.BlockSpec((1, TM, GLA_QK), tile),
                pl.BlockSpec((1, TM, GLA_V), tile), pl.BlockSpec((1, TM, LR_PAD), tile),
                pl.BlockSpec((LR_PAD, GLA_QK), const), pl.BlockSpec((1, GLA_QK), const)]
    args = [q, k, v, lr, wup, bg]
    if extra is not None:
        ob, g, nw = extra
        in_specs += [pl.BlockSpec((1, TM, GLA_V), tile), pl.BlockSpec((1, TM, GLA_V), tile),
                     pl.BlockSpec((1, GLA_DV), const)]
        args += [ob, g, nw]
    return pl.pallas_call(
        functools.partial(_gla_kernel, reverse, extra is not None),
        grid=(b, nt),
        in_specs=in_specs,
        out_specs=pl.BlockSpec((1, TM, GLA_V), tile),
        out_shape=jax.ShapeDtypeStruct((b, t, GLA_V), BF16),
        scratch_shapes=[pltpu.VMEM((GLA_HEADS, GLA_DV, GLA_DK), F32)],
        compiler_params=_cparams(("arbitrary", "arbitrary")),
        name="gla_bwd" if reverse else "gla_fwd",
    )(*args)


def _na_bias_tables(rpb, rows):
    rpt = TM // GRID_W
    kr = min(WIN_R, rows)
    col = np.arange(GRID_W)
    cidx = np.clip(col[None, :] - col[:, None], -(WIN_C - 1), WIN_C - 1) + (WIN_C - 1)
    c_sel = (cidx[None] == np.arange(2 * WIN_C - 1)[:, None, None]).astype(np.float32)
    c0 = np.clip(col - WIN_C // 2, 0, GRID_W - WIN_C)
    c_ok = (col[None, :] >= c0[:, None]) & (col[None, :] < c0[:, None] + WIN_C)
    toep = jnp.einsum("hrd,dqk->hrqk", rpb.astype(F32), c_sel, precision=lax.Precision.HIGHEST)
    toep = jnp.where(c_ok[None, None], toep, NEG)
    masked = jnp.full((NA_HEADS, GRID_W, GRID_W), NEG, F32)
    tabs = [jnp.full((NA_HEADS, TM, 3 * TM), NEG, F32)]
    for r_base, u_base in ((0, 0), (rpt, 0), (rows - rpt, rows - 3 * rpt)):
        blocks = []
        for i in range(rpt):
            rq = r_base + i
            r0 = min(max(rq - kr // 2, 0), rows - kr)
            row = [toep[:, u_base + m - rq + (WIN_R - 1)] if r0 <= u_base + m < r0 + kr else masked
                   for m in range(3 * rpt)]
            blocks.append(jnp.concatenate(row, axis=-1))
        tabs.append(jnp.concatenate(blocks, axis=-2))
    return jnp.stack(tabs)


def _na_kernel(q_ref, kp_ref, kc_ref, kn_ref, kx_ref, vp_ref, vc_ref, vn_ref, vx_ref, bias_ref, o_ref):
    lane = lax.broadcasted_iota(jnp.int32, (TM, 2 * NA_HD), 1)
    first = lane < NA_HD
    k_refs = (kp_ref, kc_ref, kn_ref, kx_ref)
    v_refs = (vp_ref, vc_ref, vn_ref, vx_ref)
    for hp in range(NA_HEADS // 2):
        ls = slice(hp * 2 * NA_HD, (hp + 1) * 2 * NA_HD)
        q2 = q_ref[0, :, ls]
        ks = [kr[0, :, ls] for kr in k_refs]
        vs = [vr[0, :, ls] for vr in v_refs]
        outs = []
        for sub in range(2):
            hd = 2 * hp + sub
            qm = jnp.where(first if sub == 0 else jnp.logical_not(first), q2, jnp.zeros_like(q2))
            sc = []
            for i in range(4):
                s = _dg(qm, ks[i], _NT)
                if i < 3:
                    s = s + bias_ref[0, hd, :, i * TM:(i + 1) * TM]
                sc.append(s)
            m = jnp.max(sc[0], axis=-1, keepdims=True)
            for i in range(1, 4):
                m = jnp.maximum(m, jnp.max(sc[i], axis=-1, keepdims=True))
            l = jnp.zeros_like(m)
            acc = jnp.zeros((TM, 2 * NA_HD), F32)
            for i in range(4):
                p = jnp.exp(sc[i] - m)
                l = l + jnp.sum(p, axis=-1, keepdims=True)
                acc = acc + _dot(p.astype(BF16), vs[i])
            outs.append(acc / l)
        o_ref[0, :, ls] = jnp.where(first, outs[0], outs[1]).astype(o_ref.dtype)


def _na(q, k, v, bias):
    b, t, _ = q.shape
    nt = t // TM
    qt = lambda i, j: (i, j, 0)
    ctx = lambda i, j: (i, 0, 0)

    def near(o):
        return lambda i, j: (i, jnp.clip(j, 2, nt - 2) + o, 0)

    def kind(i, j):
        return (jnp.where(j == 0, 0, jnp.where(j == 1, 1, jnp.where(j == nt - 1, 3, 2))), 0, 0, 0)

    blk = lambda f: pl.BlockSpec((1, TM, NA_W), f)
    return pl.pallas_call(
        _na_kernel,
        grid=(b, nt),
        in_specs=[blk(qt), blk(near(-1)), blk(near(0)), blk(near(1)), blk(ctx),
                  blk(near(-1)), blk(near(0)), blk(near(1)), blk(ctx),
                  pl.BlockSpec((1, NA_HEADS, TM, 3 * TM), kind)],
        out_specs=blk(qt),
        out_shape=jax.ShapeDtypeStruct((b, t, NA_W), BF16),
        compiler_params=_cparams(("arbitrary", "arbitrary")),
        name="natten",
    )(q, k, k, k, k, v, v, v, v, bias)


def _route(sel, aff):
    rows = lambda a, e: a[e:e + 1, :]
    gscore = []
    for g in range(N_GROUPS):
        a, b, c, d = (rows(sel, EXPERTS_PER_GROUP * g + i) for i in range(EXPERTS_PER_GROUP))
        hi1, lo1 = jnp.maximum(a, b), jnp.minimum(a, b)
        hi2, lo2 = jnp.maximum(c, d), jnp.minimum(c, d)
        gscore.append(jnp.maximum(hi1, hi2) + jnp.maximum(jnp.minimum(hi1, hi2), jnp.maximum(lo1, lo2)))
    gbest = jnp.zeros_like(gscore[0], dtype=jnp.int32)
    gval = gscore[0]
    for g in range(1, N_GROUPS):
        better = gscore[g] > gval
        gbest = jnp.where(better, g, gbest)
        gval = jnp.where(better, gscore[g], gval)
    cs, ca = [], []
    for i in range(EXPERTS_PER_GROUP):
        s_i, a_i = rows(sel, i), rows(aff, i)
        for g in range(1, N_GROUPS):
            pick = gbest == g
            s_i = jnp.where(pick, rows(sel, EXPERTS_PER_GROUP * g + i), s_i)
            a_i = jnp.where(pick, rows(aff, EXPERTS_PER_GROUP * g + i), a_i)
        cs.append(s_i)
        ca.append(a_i)
    i1 = jnp.zeros_like(gbest)
    v1, w1 = cs[0], ca[0]
    for i in range(1, EXPERTS_PER_GROUP):
        better = cs[i] > v1
        i1 = jnp.where(better, i, i1)
        v1 = jnp.where(better, cs[i], v1)
        w1 = jnp.where(better, ca[i], w1)
    i2 = jnp.full_like(gbest, -1)
    v2 = jnp.full_like(v1, -jnp.inf)
    w2 = jnp.zeros_like(w1)
    for i in range(EXPERTS_PER_GROUP):
        better = (i1 != i) & ((cs[i] > v2) | (i2 < 0))
        i2 = jnp.where(better, i, i2)
        v2 = jnp.where(better, cs[i], v2)
        w2 = jnp.where(better, ca[i], w2)
    tot = w1 + w2
    base = gbest * EXPERTS_PER_GROUP
    return base + i1, base + i2, w1 / tot, w2 / tot


def _merge_kernel(gy_ref, ny_ref, m1_ref, m2_ref, x_ref, mod_ref, nw_ref, wbg_ref, wbn_ref, wo_ref,
                  wr_ref, br_ref, xo_ref, hp_ref, ri_ref, rw_ref):
    a = _dot(gy_ref[0], wbg_ref[...])
    b = _dot(ny_ref[0], wbn_ref[...])
    m = _sigmoid(m1_ref[0].astype(F32)) * a + _sigmoid(m2_ref[0].astype(F32)) * b
    y = _dot(m.astype(BF16), wo_ref[...])
    mod = mod_ref[0, 0]
    xn = x_ref[0] + mod[2:3] * y
    xo_ref[0] = xn
    h2 = _norm_mod(xn, nw_ref[...], mod[3:4], mod[4:5])
    _rows_to_tiles(hp_ref, 0, h2)
    hh, hl = _split(h2)
    wh, wl = _split(wr_ref[...])
    logit = _dg(wh, hh, _NT) + _dg(wh, hl, _NT) + _dg(wl, hh, _NT)
    aff = _sigmoid(logit)
    i1, i2, w1, w2 = _route(aff + br_ref[...], aff)
    zi = jnp.zeros((6, TM), jnp.int32)
    ri_ref[0] = jnp.concatenate([i1, i2, zi], axis=0)
    rw_ref[0] = jnp.concatenate([w1, w2, zi.astype(F32)], axis=0)


def _merge(gy, ny, m1, m2, x, modl, nw2, wbg, wbn, wo, wrt, brt):
    b, t, d = x.shape
    nt = t // TM
    tile = lambda i, j: (i, j, 0)
    const = lambda i, j: (0, 0)
    return pl.pallas_call(
        _merge_kernel,
        grid=(b, nt),
        in_specs=[pl.BlockSpec((1, TM, GLA_V), tile), pl.BlockSpec((1, TM, NA_W), tile),
                  pl.BlockSpec((1, TM, d), tile), pl.BlockSpec((1, TM, d), tile),
                  pl.BlockSpec((1, TM, d), tile),
                  pl.BlockSpec((1, 1, 8, d), lambda i, j: (i, jnp.minimum(j, 1), 0, 0)),
                  pl.BlockSpec((1, d), const),
                  pl.BlockSpec((GLA_V, d), const), pl.BlockSpec((NA_W, d), const),
                  pl.BlockSpec((d, d), const),
                  pl.BlockSpec((N_EXPERTS, d), const), pl.BlockSpec((N_EXPERTS, TM), const)],
        out_specs=[pl.BlockSpec((1, TM, d), tile),
                   pl.BlockSpec((TM * d // LANES, LANES), lambda i, j: (i * nt + j, 0)),
                   pl.BlockSpec((1, 8, TM), lambda i, j: (i, 0, j)),
                   pl.BlockSpec((1, 8, TM), lambda i, j: (i, 0, j))],
        out_shape=[jax.ShapeDtypeStruct((b, t, d), F32), jax.ShapeDtypeStruct((b * t * d // LANES, LANES), F32),
                   jax.ShapeDtypeStruct((b, 8, t), jnp.int32), jax.ShapeDtypeStruct((b, 8, t), F32)],
        compiler_params=_cparams(("arbitrary", "arbitrary")),
        name="merge_router",
    )(gy, ny, m1, m2, x, modl, nw2, wbg, wbn, wo, wrt, brt)


def _moe_kernel(be_ref, nu_ref, src_cur, src_nxt, src_nx2, dst_cur, dst_prv, h_ref, w1_ref, w3_ref, w2_ref,
                o_ref, xbuf, ybuf, w1b, w3b, w2b, gsem, ssem):
    i = pl.program_id(0)
    nu = nu_ref[0]
    xb = lax.rem(i, 3)
    yb = lax.rem(i, 3)
    used = i < nu
    st = w1_ref.shape[2] // LANES
    blk = TM * st
    n_real = o_ref.shape[0] - 2 * blk

    def part(ref, b):
        return ref.at[pl.ds(pl.multiple_of(b * blk, blk), blk)]

    def token(ref, row):
        return ref.at[pl.ds(pl.multiple_of(row, st), st)]

    def start_gather(idx_ref, b):
        for r in range(TM):
            pltpu.make_async_copy(token(h_ref, idx_ref[0, 0, r]), token(xbuf, b * blk + r * st),
                                  gsem.at[b]).start(priority=r % 2)

    def start_scatter(idx_ref, b):
        for r in range(TM):
            pltpu.make_async_copy(token(ybuf, b * blk + r * st), token(o_ref, idx_ref[0, 0, r]),
                                  ssem.at[b]).start(priority=r % 2)

    def wait_gather(b):
        pltpu.make_async_copy(h_ref.at[pl.ds(0, blk)], part(xbuf, b), gsem.at[b]).wait()

    def wait_scatter(b):
        pltpu.make_async_copy(part(ybuf, b), o_ref.at[pl.ds(0, blk)], ssem.at[b]).wait()

    @pl.when(i == 0)
    def _():
        xbuf[...] = jnp.zeros_like(xbuf)
        ybuf[...] = jnp.zeros_like(ybuf)
        for b in range(2):
            spare = pltpu.make_async_copy(part(xbuf, b), o_ref.at[pl.ds(n_real + b * blk, blk)], ssem.at[b])
            spare.start()
            spare.wait()
        start_gather(src_cur, 0)
        start_gather(src_nxt, 1)

    fresh = jnp.logical_or(i == 0, be_ref[i] != be_ref[jnp.maximum(i - 1, 0)])

    @pl.when(jnp.logical_and(used, fresh))
    def _():
        for src, dst in ((w1_ref, w1b), (w3_ref, w3b), (w2_ref, w2b)):
            def body(r, carry, src=src, dst=dst):
                rows = pl.ds(pl.multiple_of(r * LANES, LANES), LANES)
                dst[rows, :] = src[0, 0, rows, :].astype(BF16)
                return carry
            lax.fori_loop(0, src.shape[2] // LANES, body, 0)

    @pl.when(jnp.logical_and(used, i > 0))
    def _():
        wait_scatter(lax.rem(i + 1, 3))

    @pl.when(used)
    def _():
        wait_gather(xb)
        x = _tiles_to_rows(xbuf, xb * blk, TM, st).astype(BF16)
        start_gather(src_nx2, lax.rem(i + 2, 3))
        start_scatter(dst_prv, lax.rem(i + 2, 3))
        a = _dot(x, w1b[...])
        b = _dot(x, w3b[...])
        hmid = (a * _sigmoid(a)) * b
        _rows_to_tiles(ybuf, yb * blk, _dot(hmid.astype(BF16), w2b[...]))

    @pl.when(i == nu - 1)
    def _():
        start_scatter(dst_cur, yb)
        wait_scatter(yb)
        wait_scatter(lax.rem(i + 2, 3))
        wait_gather(lax.rem(i + 1, 3))
        wait_gather(lax.rem(i + 2, 3))


def _moe(blk_expert, n_used, slot_src, slot_dst, h, w1, w3, w2, layer):
    d, f = w1.shape[2], w1.shape[3]
    st = d // LANES
    blk = TM * st
    n_rows = h.shape[0]
    nb = blk_expert.shape[0]
    wsel = lambda i, be, nu: (layer, be[i], 0, 0)
    smem = lambda f_: pl.BlockSpec((1, 1, TM), f_, memory_space=pltpu.SMEM)
    cur = lambda i, be, nu: (i, 0, 0)
    nxt = lambda i, be, nu: (jnp.minimum(i + 1, nb - 1), 0, 0)
    nx2 = lambda i, be, nu: (jnp.minimum(i + 2, nb - 1), 0, 0)
    prv = lambda i, be, nu: (jnp.maximum(i - 1, 0), 0, 0)
    return pl.pallas_call(
        _moe_kernel,
        grid_spec=pltpu.PrefetchScalarGridSpec(
            num_scalar_prefetch=2,
            grid=(nb,),
            in_specs=[smem(cur), smem(nxt), smem(nx2), smem(cur), smem(prv),
                      pl.BlockSpec(memory_space=pl.ANY), pl.BlockSpec((1, 1, d, f), wsel),
                      pl.BlockSpec((1, 1, d, f), wsel), pl.BlockSpec((1, 1, f, d), wsel)],
            out_specs=pl.BlockSpec(memory_space=pl.ANY),
            scratch_shapes=[pltpu.VMEM((3 * blk, LANES), F32), pltpu.VMEM((3 * blk, LANES), F32),
                            pltpu.VMEM((d, f), BF16), pltpu.VMEM((d, f), BF16), pltpu.VMEM((f, d), BF16),
                            pltpu.SemaphoreType.DMA((3,)), pltpu.SemaphoreType.DMA((3,))]),
        out_shape=jax.ShapeDtypeStruct((TOP_K * n_rows + 2 * blk, LANES), F32),
        compiler_params=_cparams(("arbitrary",)),
        name="moe_experts",
    )(blk_expert, n_used, slot_src, slot_src, slot_src, slot_dst, slot_dst, h, w1, w3, w2)


def _combine_kernel(final, *refs):
    y_refs, (x_ref, w_ref, mod_ref), rest = refs[:TOP_K], refs[TOP_K:TOP_K + 3], refs[TOP_K + 3:]
    if final:
        fw_ref, o_ref = rest
    else:
        (o_ref,) = rest
    w = w_ref[0]
    y = None
    for kk in range(TOP_K):
        yk = _tiles_to_rows(y_refs[kk], 0, TM, x_ref.shape[2] // LANES) * w[:, kk:kk + 1]
        y = yk if y is None else y + yk
    xn = x_ref[0] + mod_ref[0, 0][5:6] * y
    if final:
        xn = xn * lax.rsqrt(jnp.mean(xn * xn, axis=-1, keepdims=True) + EPS) * fw_ref[...]
    o_ref[0] = xn


def _combine(y, x, wcol, modl, final_w=None):
    b, t, d = x.shape
    nt = t // TM
    final = final_w is not None
    skip = 1 if final else 0
    tile = lambda i, j: (i, j + skip, 0)
    def plane(kk):
        return pl.BlockSpec((TM * d // LANES, LANES), lambda i, j: ((kk * b + i) * nt + j + skip, 0))

    in_specs = [plane(kk) for kk in range(TOP_K)] + [
        pl.BlockSpec((1, TM, d), tile), pl.BlockSpec((1, TM, TOP_K), tile),
        pl.BlockSpec((1, 1, 8, d), lambda i, j: (i, jnp.minimum(j + skip, 1), 0, 0))]
    args = [y] * TOP_K + [x, wcol, modl]
    if final:
        in_specs.append(pl.BlockSpec((1, d), lambda i, j: (0, 0)))
        args.append(final_w)
    return pl.pallas_call(
        functools.partial(_combine_kernel, final),
        grid=(b, nt - skip),
        in_specs=in_specs,
        out_specs=pl.BlockSpec((1, TM, d), lambda i, j: (i, j, 0)),
        out_shape=jax.ShapeDtypeStruct((b, t - skip * TM, d), F32),
        compiler_params=_cparams(("arbitrary", "arbitrary")),
        name="moe_combine_final" if final else "moe_combine",
    )(*args)


def _slot_plan(ridx, n_tok, st):
    e = ridx.reshape(-1)
    onehot = (e[:, None] == jnp.arange(N_EXPERTS, dtype=jnp.int32)[None, :]).astype(jnp.int32)
    incl = jnp.cumsum(onehot, axis=0)
    rank = jnp.sum((incl - onehot) * onehot, axis=1)
    counts = incl[-1]
    padded = (counts + TM - 1) // TM * TM
    pends = jnp.cumsum(padded)
    pstarts = pends - padded
    dest = pstarts[e] + rank
    n_blocks = (n_tok * TOP_K + N_EXPERTS * (TM - 1) + TM - 1) // TM
    blk_start = jnp.arange(n_blocks, dtype=jnp.int32) * TM
    blk_expert = jnp.minimum(jnp.sum((pends[None, :] <= blk_start[:, None]).astype(jnp.int32), axis=1),
                             N_EXPERTS - 1)
    n_used = (pends[-1] // TM).astype(jnp.int32).reshape(1)
    pair = jnp.arange(n_tok * TOP_K, dtype=jnp.int32)
    row = (pair % TOP_K) * n_tok + pair // TOP_K
    slot = jnp.arange(n_blocks * TM, dtype=jnp.int32)
    spare = n_tok * TOP_K + ((slot // TM) % 2) * TM + slot % TM
    slot_dst = spare.at[dest].set(row, unique_indices=True)
    slot_src = jnp.where(slot_dst < n_tok * TOP_K, slot_dst % n_tok, 0)
    shape3 = (n_blocks, 1, TM)
    return blk_expert.astype(jnp.int32), n_used, (slot_src * st).reshape(shape3), (slot_dst * st).reshape(shape3)


def _rope_tables(t_len, ctx_len):
    half = GLA_DK // 2
    inv = ROPE_BASE ** (-np.arange(0, half, 2, dtype=np.float32) / half)
    lane = np.arange(GLA_DK)
    p = np.arange(t_len - ctx_len)
    pos = np.where(lane[None, :] < half, (p // GRID_W)[:, None], (p % GRID_W)[:, None]).astype(np.float32)
    ang = jnp.asarray(pos) * jnp.asarray(inv[(lane % half) % (half // 2)])[None, :]
    cos, sin = jnp.cos(ang), jnp.sin(ang)
    lower = jnp.asarray((lane % half) < half // 2)[None, :]
    pad = lambda a, v: jnp.concatenate([jnp.full((ctx_len, GLA_DK), v, F32), a], axis=0)
    return pad(cos, 1.0), pad(jnp.where(lower, -sin, 0.0), 0.0), pad(jnp.where(lower, 0.0, sin), 0.0)


def kernel(x, c, ctx, c_ctx, w_ada, b_ada, norm1_w, norm2_w, w_in, w_gk_up, b_gk, gla_norm_w, rpb,
           w_bo_gla, w_bo_na, w_out, w_router, b_router, w1, w3, w2, final_norm_w):
    bsz, seq, d = x.shape
    ctx_len = ctx.shape[1]
    depth = w_ada.shape[0]
    assert ctx_len == TM and seq % TM == 0 and TM % GRID_W == 0 and seq // GRID_W >= 3 * (TM // GRID_W)
    t = ctx_len + seq
    nt = t // TM
    n_tok = bsz * t

    xs = jnp.concatenate([ctx, x], axis=1)
    cvec = jnp.zeros((8, d), F32).at[:bsz].set(c).at[bsz].set(c_ctx)
    mods = _modulation(cvec, w_ada, b_ada).reshape(depth, 8, 6, d)
    cos, sa, sb = _rope_tables(t, ctx_len)
    wrt = w_router.T.astype(F32)
    brt = jnp.broadcast_to(b_router.astype(F32)[:, None], (N_EXPERTS, TM))
    gq_end = 2 * GLA_QK + 2 * GLA_V

    out = None
    for l in range(depth):
        mx = jnp.pad(mods[l, :bsz], ((0, 0), (0, 2), (0, 0)))
        mc = jnp.broadcast_to(jnp.pad(mods[l, bsz], ((0, 2), (0, 0)))[None], mx.shape)
        modl = jnp.stack([mc, mx], axis=1)
        wl = w_in[l]
        wp = jnp.concatenate([wl[:, :gq_end], wl[:, gq_end + 2 * GLA_LR:], wl[:, gq_end:gq_end + 2 * GLA_LR],
                              jnp.zeros((d, LR_PAD - 2 * GLA_LR), wl.dtype)], axis=1).astype(BF16)
        gq, gk, gv, gg, nq, nk, nv, m1, m2, lr = _inproj(xs, modl, norm1_w[l][None], wp, cos, sa, sb)

        def wup(dirn):
            w = jnp.zeros((LR_PAD, GLA_QK), F32)
            return w.at[dirn * GLA_LR:(dirn + 1) * GLA_LR].set(w_gk_up[l, dirn]).astype(BF16)

        ob = _gla_pass(True, gq, gk, gv, lr, wup(1), b_gk[l, 1][None])
        gy = _gla_pass(False, gq, gk, gv, lr, wup(0), b_gk[l, 0][None],
                       extra=(ob, gg, gla_norm_w[l][None]))
        ny = _na(nq, nk, nv, _na_bias_tables(rpb[l], seq // GRID_W))
        xs, hp, ridx, rw = _merge(gy, ny, m1, m2, xs, modl, norm2_w[l][None],
                                  w_bo_gla[l].astype(BF16), w_bo_na[l].astype(BF16), w_out[l].astype(BF16),
                                  wrt, brt)
        ridx2 = jnp.transpose(ridx[:, :TOP_K, :], (0, 2, 1))
        wcol = jnp.transpose(rw[:, :TOP_K, :], (0, 2, 1))
        blk_expert, n_used, slot_src, slot_dst = _slot_plan(ridx2, n_tok, d // LANES)
        y = _moe(blk_expert, n_used, slot_src, slot_dst, hp, w1, w3, w2, l)
        if l < depth - 1:
            xs = _combine(y, xs, wcol, modl)
        else:
            out = _combine(y, xs, wcol, modl, final_norm_w[None])
    return out
```

```python
import functools

import jax
import jax.numpy as jnp
import numpy as np
from jax import lax
from jax.experimental import pallas as pl
from jax.experimental.pallas import tpu as pltpu

F32 = jnp.float32
BF16 = jnp.bfloat16

EPS = 1e-6
GRID_W = 64
GLA_HEADS = 4
GLA_DK = 128
GLA_DV = 256
GLA_LR = 16
GLA_TAU = 16.0
GLA_CHUNK = 64
ROPE_BASE = 10000.0
NA_HEADS = 8
NA_HD = 64
WIN_R = 8
WIN_C = 16
N_EXPERTS = 16
N_GROUPS = 4
EXPERTS_PER_GROUP = N_EXPERTS // N_GROUPS
TOP_K = 2

GLA_QK = GLA_HEADS * GLA_DK
GLA_V = GLA_HEADS * GLA_DV
NA_W = NA_HEADS * NA_HD

TM = 256
LANES = 128
LR_PAD = LANES
NEG = -1e30
VMEM_LIMIT = 56 * 1024 * 1024

_NT = (((1,), (1,)), ((), ()))
_TN = (((0,), (0,)), ((), ()))


def _cparams(sem):
    return pltpu.CompilerParams(dimension_semantics=sem, vmem_limit_bytes=VMEM_LIMIT)


def _dot(a, b):
    return jnp.dot(a, b, preferred_element_type=F32)


def _dg(a, b, dims):
    return lax.dot_general(a, b, dims, preferred_element_type=F32)


def _split(a):
    hi = a.astype(BF16)
    lo = (a - hi.astype(F32)).astype(BF16)
    return hi, lo


def _sigmoid(x):
    return 1.0 / (1.0 + jnp.exp(-x))


def _norm_mod(x, w, shift, scale):
    y = x * lax.rsqrt(jnp.mean(x * x, axis=-1, keepdims=True) + EPS)
    return (y * w) * (1.0 + scale) + shift


def _rows_to_tiles(ref, base, val):
    st = val.shape[1] // LANES
    for s in range(st):
        ref[pl.ds(base + s, val.shape[0], stride=st), :] = val[:, s * LANES:(s + 1) * LANES]


def _tiles_to_rows(ref, base, n, st):
    return jnp.concatenate([ref[pl.ds(base + s, n, stride=st), :] for s in range(st)], axis=1)


def _mod_kernel(c_ref, w_ref, b_ref, o_ref):
    c = c_ref[...]
    s = c * _sigmoid(c)
    sh, sl = _split(s)
    wh, wl = _split(w_ref[0])
    o_ref[0] = _dot(sh, wh) + _dot(sl, wh) + _dot(sh, wl) + b_ref[0]


def _modulation(cvec, w_ada, b_ada):
    depth, d, n = w_ada.shape
    tn = 1024
    return pl.pallas_call(
        _mod_kernel,
        grid=(depth, n // tn),
        in_specs=[pl.BlockSpec((8, d), lambda l, j: (0, 0)),
                  pl.BlockSpec((1, d, tn), lambda l, j: (l, 0, j)),
                  pl.BlockSpec((1, 1, tn), lambda l, j: (l, 0, j))],
        out_specs=pl.BlockSpec((1, 8, tn), lambda l, j: (l, 0, j)),
        out_shape=jax.ShapeDtypeStruct((depth, 8, n), F32),
        compiler_params=_cparams(("arbitrary", "arbitrary")),
        name="modulation",
    )(cvec, w_ada, b_ada.reshape(depth, 1, n))


_IN_OUT = (("gq", GLA_QK, BF16), ("gk", GLA_QK, BF16), ("gv", GLA_V, BF16), ("gg", GLA_V, BF16),
           ("nq", NA_W, BF16), ("nk", NA_W, BF16), ("nv", NA_W, BF16),
           ("m1", None, BF16), ("m2", None, BF16), ("lr", LR_PAD, F32))


def _inproj_kernel(x_ref, mod_ref, nw_ref, w_ref, cos_ref, sa_ref, sb_ref, *o_refs):
    d = x_ref.shape[2]
    mod = mod_ref[0, 0]
    h = _norm_mod(x_ref[0], nw_ref[0], mod[0:1], mod[1:2]).astype(BF16)
    cos, sa, sb = cos_ref[...], sa_ref[...], sb_ref[...]

    def rope(r, scale):
        parts = []
        for hh in range(GLA_HEADS):
            xs = r[:, hh * GLA_DK:(hh + 1) * GLA_DK]
            y = xs * cos + pltpu.roll(xs, GLA_DK - 32, 1) * sa + pltpu.roll(xs, 32, 1) * sb
            parts.append(y * scale if scale != 1.0 else y)
        return jnp.concatenate(parts, axis=1)

    off = 0
    for (name, width, dt), o_ref in zip(_IN_OUT, o_refs):
        width = d if width is None else width
        r = _dot(h, w_ref[0, :, off:off + width])
        if name == "gq":
            r = rope(r, GLA_DK ** -0.5)
        elif name == "gk":
            r = rope(r, 1.0)
        elif name == "nq":
            r = r * (NA_HD ** -0.5)
        o_ref[0] = r.astype(dt)
        off += width


def _layer_spec(arr, layer, *lead):
    tail = arr.shape[1 + len(lead):]
    index = (layer,) + lead + (0,) * len(tail)
    return pl.BlockSpec((1,) * (1 + len(lead)) + tail, lambda *_: index)


def _mod_spec(mods, layer, ctx_row, skip=0):
    return pl.BlockSpec((1, 1) + mods.shape[2:], lambda i, j: (layer, jnp.where(j + skip == 0, ctx_row, i), 0, 0))


def _inproj(x, mods, nw, wp, cos, sa, sb, layer):
    b, t, d = x.shape
    nt = t // TM
    widths = [d if w is None else w for _, w, _ in _IN_OUT]
    return pl.pallas_call(
        _inproj_kernel,
        grid=(b, nt),
        in_specs=[pl.BlockSpec((1, TM, d), lambda i, j: (i, j, 0)),
                  _mod_spec(mods, layer, b), _layer_spec(nw, layer), _layer_spec(wp, layer),
                  pl.BlockSpec((TM, GLA_DK), lambda i, j: (j, 0)),
                  pl.BlockSpec((TM, GLA_DK), lambda i, j: (j, 0)),
                  pl.BlockSpec((TM, GLA_DK), lambda i, j: (j, 0))],
        out_specs=[pl.BlockSpec((1, TM, w), lambda i, j: (i, j, 0)) for w in widths],
        out_shape=[jax.ShapeDtypeStruct((b, t, w), dt) for w, (_, _, dt) in zip(widths, _IN_OUT)],
        compiler_params=_cparams(("arbitrary", "arbitrary")),
        name="inproj",
    )(x, mods, nw, wp, cos, sa, sb)


def _gla_kernel(reverse, finish, q_ref, k_ref, v_ref, lr_ref, wup_ref, bg_ref, *rest):
    if finish:
        ob_ref, g_ref, nw_ref, o_ref, st_ref = rest
    else:
        o_ref, st_ref = rest
    nchunk = TM // GLA_CHUNK

    @pl.when(pl.program_id(1) == 0)
    def _():
        st_ref[...] = jnp.zeros_like(st_ref)

    z = _dot(lr_ref[0].astype(BF16), wup_ref[0, 0]) + bg_ref[0, 0]
    logg = (jnp.minimum(z, 0.0) - jnp.log(1.0 + jnp.exp(-jnp.abs(z)))) * (1.0 / GLA_TAU)
    r = lax.broadcasted_iota(jnp.int32, (TM, TM), 0)
    s = lax.broadcasted_iota(jnp.int32, (TM, TM), 1)
    order = (s >= r) if reverse else (s <= r)
    tri = jnp.where(((r // GLA_CHUNK) == (s // GLA_CHUNK)) & order, 1.0, 0.0).astype(BF16)
    hi, lo = _split(logg)
    bc = _dot(tri, hi) + _dot(tri, lo)
    cmask = order[:GLA_CHUNK, :GLA_CHUNK]

    for c in (range(nchunk - 1, -1, -1) if reverse else range(nchunk)):
        rs = slice(c * GLA_CHUNK, (c + 1) * GLA_CHUNK)
        for hh in range(GLA_HEADS):
            ls = slice(hh * GLA_DK, (hh + 1) * GLA_DK)
            vs = slice(hh * GLA_DV, (hh + 1) * GLA_DV)
            b = bc[rs, ls]
            bend = b[0:1] if reverse else b[GLA_CHUNK - 1:GLA_CHUNK]
            qc = q_ref[0, rs, ls].astype(F32)
            kc = k_ref[0, rs, ls].astype(F32)
            vc = v_ref[0, rs, vs]
            kd = (kc * jnp.exp(bend - b)).astype(BF16)
            qi = (qc * jnp.exp(b)).astype(BF16)
            qa = (qc * jnp.exp(b - bend)).astype(BF16)
            att = jnp.where(cmask, _dg(qa, kd, _NT), 0.0).astype(BF16)
            st = st_ref[hh]
            o = _dg(qi, st.astype(BF16), _NT) + _dot(att, vc)
            st_ref[hh] = st * jnp.exp(bend) + _dg(vc, kd, _TN)
            if finish:
                o = o + ob_ref[0, rs, vs].astype(F32)
                o = o * lax.rsqrt(jnp.mean(o * o, axis=-1, keepdims=True) + EPS) * nw_ref[0]
                g = g_ref[0, rs, vs].astype(F32)
                o = o * (g * _sigmoid(g))
            o_ref[0, rs, vs] = o.astype(o_ref.dtype)


def _gla_pass(reverse, q, k, v, lr, wup, bg, layer, extra=None):
    b, t, _ = q.shape
    nt = t // TM
    dirn = 1 if reverse else 0
    if reverse:
        tile = lambda i, j: (i, jnp.where(j == 0, 0, nt - j), 0)
    else:
        tile = lambda i, j: (i, j, 0)
    in_specs = [pl.BlockSpec((1, TM, GLA_QK), tile), pl.BlockSpec((1, TM, GLA_QK), tile),
                pl.BlockSpec((1, TM, GLA_V), tile), pl.BlockSpec((1, TM, LR_PAD), tile),
                _layer_spec(wup, layer, dirn), _layer_spec(bg, layer, dirn)]
    args = [q, k, v, lr, wup, bg]
    if extra is not None:
        ob, g, nw = extra
        in_specs += [pl.BlockSpec((1, TM, GLA_V), tile), pl.BlockSpec((1, TM, GLA_V), tile),
                     _layer_spec(nw, layer)]
        args += [ob, g, nw]
    return pl.pallas_call(
        functools.partial(_gla_kernel, reverse, extra is not None),
        grid=(b, nt),
        in_specs=in_specs,
        out_specs=pl.BlockSpec((1, TM, GLA_V), tile),
        out_shape=jax.ShapeDtypeStruct((b, t, GLA_V), BF16),
        scratch_shapes=[pltpu.VMEM((GLA_HEADS, GLA_DV, GLA_DK), F32)],
        compiler_params=_cparams(("arbitrary", "arbitrary")),
        name="gla_bwd" if reverse else "gla_fwd",
    )(*args)


def _na_bias_tables(rpb, rows):
    rpt = TM // GRID_W
    kr = min(WIN_R, rows)
    col = np.arange(GRID_W)
    cidx = np.clip(col[None, :] - col[:, None], -(WIN_C - 1), WIN_C - 1) + (WIN_C - 1)
    c_sel = (cidx[None] == np.arange(2 * WIN_C - 1)[:, None, None]).astype(np.float32)
    c0 = np.clip(col - WIN_C // 2, 0, GRID_W - WIN_C)
    c_ok = (col[None, :] >= c0[:, None]) & (col[None, :] < c0[:, None] + WIN_C)
    toep = jnp.einsum("lhrd,dqk->lhrqk", rpb.astype(F32), c_sel, precision=lax.Precision.HIGHEST)
    toep = jnp.where(c_ok, toep, NEG)
    depth = rpb.shape[0]
    masked = jnp.full((depth, NA_HEADS, GRID_W, GRID_W), NEG, F32)
    tabs = [jnp.full((depth, NA_HEADS, TM, 3 * TM), NEG, F32)]
    for r_base, u_base in ((0, 0), (rpt, 0), (rows - rpt, rows - 3 * rpt)):
        blocks = []
        for i in range(rpt):
            rq = r_base + i
            r0 = min(max(rq - kr // 2, 0), rows - kr)
            row = [toep[:, :, u_base + m - rq + (WIN_R - 1)] if r0 <= u_base + m < r0 + kr else masked
                   for m in range(3 * rpt)]
            blocks.append(jnp.concatenate(row, axis=-1))
        tabs.append(jnp.concatenate(blocks, axis=-2))
    return jnp.stack(tabs, axis=1)


def _na_kernel(q_ref, kp_ref, kc_ref, kn_ref, kx_ref, vp_ref, vc_ref, vn_ref, vx_ref, bias_ref, o_ref):
    lane = lax.broadcasted_iota(jnp.int32, (TM, 2 * NA_HD), 1)
    first = lane < NA_HD
    k_refs = (kp_ref, kc_ref, kn_ref, kx_ref)
    v_refs = (vp_ref, vc_ref, vn_ref, vx_ref)
    for hp in range(NA_HEADS // 2):
        ls = slice(hp * 2 * NA_HD, (hp + 1) * 2 * NA_HD)
        q2 = q_ref[0, :, ls]
        ks = [kr[0, :, ls] for kr in k_refs]
        vs = [vr[0, :, ls] for vr in v_refs]
        outs = []
        for sub in range(2):
            hd = 2 * hp + sub
            qm = jnp.where(first if sub == 0 else jnp.logical_not(first), q2, jnp.zeros_like(q2))
            sc = []
            for i in range(4):
                s = _dg(qm, ks[i], _NT)
                if i < 3:
                    s = s + bias_ref[0, 0, hd, :, i * TM:(i + 1) * TM]
                sc.append(s)
            m = jnp.max(sc[0], axis=-1, keepdims=True)
            for i in range(1, 4):
                m = jnp.maximum(m, jnp.max(sc[i], axis=-1, keepdims=True))
            l = jnp.zeros_like(m)
            acc = jnp.zeros((TM, 2 * NA_HD), F32)
            for i in range(4):
                p = jnp.exp(sc[i] - m)
                l = l + jnp.sum(p, axis=-1, keepdims=True)
                acc = acc + _dot(p.astype(BF16), vs[i])
            outs.append(acc / l)
        o_ref[0, :, ls] = jnp.where(first, outs[0], outs[1]).astype(o_ref.dtype)


def _na(q, k, v, bias, layer):
    b, t, _ = q.shape
    nt = t // TM
    qt = lambda i, j: (i, j, 0)
    ctx = lambda i, j: (i, 0, 0)

    def near(o):
        return lambda i, j: (i, jnp.clip(j, 2, nt - 2) + o, 0)

    def kind(i, j):
        return (layer, jnp.where(j == 0, 0, jnp.where(j == 1, 1, jnp.where(j == nt - 1, 3, 2))), 0, 0, 0)

    blk = lambda f: pl.BlockSpec((1, TM, NA_W), f)
    return pl.pallas_call(
        _na_kernel,
        grid=(b, nt),
        in_specs=[blk(qt), blk(near(-1)), blk(near(0)), blk(near(1)), blk(ctx),
                  blk(near(-1)), blk(near(0)), blk(near(1)), blk(ctx),
                  pl.BlockSpec((1, 1, NA_HEADS, TM, 3 * TM), kind)],
        out_specs=blk(qt),
        out_shape=jax.ShapeDtypeStruct((b, t, NA_W), BF16),
        compiler_params=_cparams(("arbitrary", "arbitrary")),
        name="natten",
    )(q, k, k, k, k, v, v, v, v, bias)


def _route(sel, aff):
    rows = lambda a, e: a[e:e + 1, :]
    gscore = []
    for g in range(N_GROUPS):
        a, b, c, d = (rows(sel, EXPERTS_PER_GROUP * g + i) for i in range(EXPERTS_PER_GROUP))
        hi1, lo1 = jnp.maximum(a, b), jnp.minimum(a, b)
        hi2, lo2 = jnp.maximum(c, d), jnp.minimum(c, d)
        gscore.append(jnp.maximum(hi1, hi2) + jnp.maximum(jnp.minimum(hi1, hi2), jnp.maximum(lo1, lo2)))
    gbest = jnp.zeros_like(gscore[0], dtype=jnp.int32)
    gval = gscore[0]
    for g in range(1, N_GROUPS):
        better = gscore[g] > gval
        gbest = jnp.where(better, g, gbest)
        gval = jnp.where(better, gscore[g], gval)
    cs, ca = [], []
    for i in range(EXPERTS_PER_GROUP):
        s_i, a_i = rows(sel, i), rows(aff, i)
        for g in range(1, N_GROUPS):
            pick = gbest == g
            s_i = jnp.where(pick, rows(sel, EXPERTS_PER_GROUP * g + i), s_i)
            a_i = jnp.where(pick, rows(aff, EXPERTS_PER_GROUP * g + i), a_i)
        cs.append(s_i)
        ca.append(a_i)
    i1 = jnp.zeros_like(gbest)
    v1, w1 = cs[0], ca[0]
    for i in range(1, EXPERTS_PER_GROUP):
        better = cs[i] > v1
        i1 = jnp.where(better, i, i1)
        v1 = jnp.where(better, cs[i], v1)
        w1 = jnp.where(better, ca[i], w1)
    i2 = jnp.full_like(gbest, -1)
    v2 = jnp.full_like(v1, -jnp.inf)
    w2 = jnp.zeros_like(w1)
    for i in range(EXPERTS_PER_GROUP):
        better = (i1 != i) & ((cs[i] > v2) | (i2 < 0))
        i2 = jnp.where(better, i, i2)
        v2 = jnp.where(better, cs[i], v2)
        w2 = jnp.where(better, ca[i], w2)
    tot = w1 + w2
    base = gbest * EXPERTS_PER_GROUP
    return base + i1, base + i2, w1 / tot, w2 / tot


def _merge_kernel(gy_ref, ny_ref, m1_ref, m2_ref, x_ref, mod_ref, nw_ref, wbg_ref, wbn_ref, wo_ref,
                  wr_ref, br_ref, xo_ref, hp_ref, ri_ref, rw_ref):
    a = _dot(gy_ref[0], wbg_ref[0])
    b = _dot(ny_ref[0], wbn_ref[0])
    m = _sigmoid(m1_ref[0].astype(F32)) * a + _sigmoid(m2_ref[0].astype(F32)) * b
    y = _dot(m.astype(BF16), wo_ref[0])
    mod = mod_ref[0, 0]
    xn = x_ref[0] + mod[2:3] * y
    xo_ref[0] = xn
    h2 = _norm_mod(xn, nw_ref[0], mod[3:4], mod[4:5])
    _rows_to_tiles(hp_ref, 0, h2)
    hh, hl = _split(h2)
    wh, wl = _split(wr_ref[...])
    logit = _dg(wh, hh, _NT) + _dg(wh, hl, _NT) + _dg(wl, hh, _NT)
    aff = _sigmoid(logit)
    i1, i2, w1, w2 = _route(aff + br_ref[...], aff)
    zi = jnp.zeros((6, TM), jnp.int32)
    ri_ref[0] = jnp.concatenate([i1, i2, zi], axis=0)
    rw_ref[0] = jnp.concatenate([w1, w2, zi.astype(F32)], axis=0)


def _merge(gy, ny, m1, m2, x, mods, nw2, wbg, wbn, wo, wrt, brt, layer):
    b, t, d = x.shape
    nt = t // TM
    tile = lambda i, j: (i, j, 0)
    const = lambda i, j: (0, 0)
    return pl.pallas_call(
        _merge_kernel,
        grid=(b, nt),
        in_specs=[pl.BlockSpec((1, TM, GLA_V), tile), pl.BlockSpec((1, TM, NA_W), tile),
                  pl.BlockSpec((1, TM, d), tile), pl.BlockSpec((1, TM, d), tile),
                  pl.BlockSpec((1, TM, d), tile),
                  _mod_spec(mods, layer, b), _layer_spec(nw2, layer),
                  _layer_spec(wbg, layer), _layer_spec(wbn, layer), _layer_spec(wo, layer),
                  pl.BlockSpec((N_EXPERTS, d), const), pl.BlockSpec((N_EXPERTS, TM), const)],
        out_specs=[pl.BlockSpec((1, TM, d), tile),
                   pl.BlockSpec((TM * d // LANES, LANES), lambda i, j: (i * nt + j, 0)),
                   pl.BlockSpec((1, 8, TM), lambda i, j: (i, 0, j)),
                   pl.BlockSpec((1, 8, TM), lambda i, j: (i, 0, j))],
        out_shape=[jax.ShapeDtypeStruct((b, t, d), F32), jax.ShapeDtypeStruct((b * t * d // LANES, LANES), F32),
                   jax.ShapeDtypeStruct((b, 8, t), jnp.int32), jax.ShapeDtypeStruct((b, 8, t), F32)],
        compiler_params=_cparams(("arbitrary", "arbitrary")),
        name="merge_router",
    )(gy, ny, m1, m2, x, mods, nw2, wbg, wbn, wo, wrt, brt)


def _moe_kernel(be_ref, nu_ref, src_cur, src_nxt, src_nx2, dst_cur, dst_prv, h_ref, w1_ref, w3_ref, w2_ref,
                o_ref, xbuf, ybuf, w1b, w3b, w2b, gsem, ssem):
    i = pl.program_id(0)
    nu = nu_ref[0]
    xb = lax.rem(i, 3)
    yb = lax.rem(i, 3)
    used = i < nu
    st = w1_ref.shape[2] // LANES
    blk = TM * st
    n_real = o_ref.shape[0] - 2 * blk

    def part(ref, b):
        return ref.at[pl.ds(pl.multiple_of(b * blk, blk), blk)]

    def token(ref, row):
        return ref.at[pl.ds(pl.multiple_of(row, st), st)]

    def start_gather(idx_ref, b):
        for r in range(TM):
            pltpu.make_async_copy(token(h_ref, idx_ref[0, 0, r]), token(xbuf, b * blk + r * st),
                                  gsem.at[b]).start(priority=r % 2)

    def start_scatter(idx_ref, b):
        for r in range(TM):
            pltpu.make_async_copy(token(ybuf, b * blk + r * st), token(o_ref, idx_ref[0, 0, r]),
                                  ssem.at[b]).start(priority=r % 2)

    def wait_gather(b):
        pltpu.make_async_copy(h_ref.at[pl.ds(0, blk)], part(xbuf, b), gsem.at[b]).wait()

    def wait_scatter(b):
        pltpu.make_async_copy(part(ybuf, b), o_ref.at[pl.ds(0, blk)], ssem.at[b]).wait()

    @pl.when(i == 0)
    def _():
        xbuf[...] = jnp.zeros_like(xbuf)
        ybuf[...] = jnp.zeros_like(ybuf)
        for b in range(2):
            spare = pltpu.make_async_copy(part(xbuf, b), o_ref.at[pl.ds(n_real + b * blk, blk)], ssem.at[b])
            spare.start()
            spare.wait()
        start_gather(src_cur, 0)
        start_gather(src_nxt, 1)

    fresh = jnp.logical_or(i == 0, be_ref[i] != be_ref[jnp.maximum(i - 1, 0)])

    @pl.when(jnp.logical_and(used, fresh))
    def _():
        for src, dst in ((w1_ref, w1b), (w3_ref, w3b), (w2_ref, w2b)):
            def body(r, carry, src=src, dst=dst):
                rows = pl.ds(pl.multiple_of(r * LANES, LANES), LANES)
                dst[rows, :] = src[0, 0, rows, :].astype(BF16)
                return carry
            lax.fori_loop(0, src.shape[2] // LANES, body, 0)

    @pl.when(jnp.logical_and(used, i > 0))
    def _():
        wait_scatter(lax.rem(i + 1, 3))

    @pl.when(used)
    def _():
        wait_gather(xb)
        x = _tiles_to_rows(xbuf, xb * blk, TM, st).astype(BF16)
        start_gather(src_nx2, lax.rem(i + 2, 3))
        start_scatter(dst_prv, lax.rem(i + 2, 3))
        a = _dot(x, w1b[...])
        b = _dot(x, w3b[...])
        hmid = (a * _sigmoid(a)) * b
        _rows_to_tiles(ybuf, yb * blk, _dot(hmid.astype(BF16), w2b[...]))

    @pl.when(i == nu - 1)
    def _():
        start_scatter(dst_cur, yb)
        wait_scatter(yb)
        wait_scatter(lax.rem(i + 2, 3))
        wait_gather(lax.rem(i + 1, 3))
        wait_gather(lax.rem(i + 2, 3))


def _moe(blk_expert, n_used, slot_src, slot_dst, h, w1, w3, w2, layer):
    d, f = w1.shape[2], w1.shape[3]
    st = d // LANES
    blk = TM * st
    n_rows = h.shape[0]
    nb = blk_expert.shape[0]
    wsel = lambda i, be, nu: (layer, be[i], 0, 0)
    smem = lambda f_: pl.BlockSpec((1, 1, TM), f_, memory_space=pltpu.SMEM)
    cur = lambda i, be, nu: (i, 0, 0)
    nxt = lambda i, be, nu: (jnp.minimum(i + 1, nb - 1), 0, 0)
    nx2 = lambda i, be, nu: (jnp.minimum(i + 2, nb - 1), 0, 0)
    prv = lambda i, be, nu: (jnp.maximum(i - 1, 0), 0, 0)
    return pl.pallas_call(
        _moe_kernel,
        grid_spec=pltpu.PrefetchScalarGridSpec(
            num_scalar_prefetch=2,
            grid=(nb,),
            in_specs=[smem(cur), smem(nxt), smem(nx2), smem(cur), smem(prv),
                      pl.BlockSpec(memory_space=pl.ANY), pl.BlockSpec((1, 1, d, f), wsel),
                      pl.BlockSpec((1, 1, d, f), wsel), pl.BlockSpec((1, 1, f, d), wsel)],
            out_specs=pl.BlockSpec(memory_space=pl.ANY),
            scratch_shapes=[pltpu.VMEM((3 * blk, LANES), F32), pltpu.VMEM((3 * blk, LANES), F32),
                            pltpu.VMEM((d, f), BF16), pltpu.VMEM((d, f), BF16), pltpu.VMEM((f, d), BF16),
                            pltpu.SemaphoreType.DMA((3,)), pltpu.SemaphoreType.DMA((3,))]),
        out_shape=jax.ShapeDtypeStruct((TOP_K * n_rows + 2 * blk, LANES), F32),
        compiler_params=_cparams(("arbitrary",)),
        name="moe_experts",
    )(blk_expert, n_used, slot_src, slot_src, slot_src, slot_dst, slot_dst, h, w1, w3, w2)


def _combine_kernel(final, *refs):
    y_refs, (x_ref, w_ref, mod_ref), rest = refs[:TOP_K], refs[TOP_K:TOP_K + 3], refs[TOP_K + 3:]
    if final:
        fw_ref, o_ref = rest
    else:
        (o_ref,) = rest
    w = w_ref[0]
    y = None
    for kk in range(TOP_K):
        yk = _tiles_to_rows(y_refs[kk], 0, TM, x_ref.shape[2] // LANES) * w[:, kk:kk + 1]
        y = yk if y is None else y + yk
    xn = x_ref[0] + mod_ref[0, 0][5:6] * y
    if final:
        xn = xn * lax.rsqrt(jnp.mean(xn * xn, axis=-1, keepdims=True) + EPS) * fw_ref[...]
    o_ref[0] = xn


def _combine(y, x, wcol, mods, layer, final_w=None):
    b, t, d = x.shape
    nt = t // TM
    final = final_w is not None
    skip = 1 if final else 0
    tile = lambda i, j: (i, j + skip, 0)

    def plane(kk):
        return pl.BlockSpec((TM * d // LANES, LANES), lambda i, j: ((kk * b + i) * nt + j + skip, 0))

    in_specs = [plane(kk) for kk in range(TOP_K)] + [
        pl.BlockSpec((1, TM, d), tile), pl.BlockSpec((1, TM, TOP_K), tile), _mod_spec(mods, layer, b, skip)]
    args = [y] * TOP_K + [x, wcol, mods]
    if final:
        in_specs.append(pl.BlockSpec((1, d), lambda i, j: (0, 0)))
        args.append(final_w)
    return pl.pallas_call(
        functools.partial(_combine_kernel, final),
        grid=(b, nt - skip),
        in_specs=in_specs,
        out_specs=pl.BlockSpec((1, TM, d), lambda i, j: (i, j, 0)),
        out_shape=jax.ShapeDtypeStruct((b, t - skip * TM, d), F32),
        compiler_params=_cparams(("arbitrary", "arbitrary")),
        name="moe_combine_final" if final else "moe_combine",
    )(*args)


def _invert_kernel(dest_ref, spare_ref, out_ref):
    pltpu.sync_copy(spare_ref, out_ref)

    def place(p, carry):
        out_ref[dest_ref[p]] = p
        return carry

    lax.fori_loop(0, dest_ref.shape[0], place, 0, unroll=16)


def _invert_slots(dest, n_slots):
    spare = dest.shape[0] + np.arange(n_slots, dtype=np.int32) % (2 * TM)
    return pl.pallas_call(
        _invert_kernel,
        in_specs=[pl.BlockSpec(memory_space=pltpu.SMEM), pl.BlockSpec(memory_space=pl.ANY)],
        out_specs=pl.BlockSpec(memory_space=pltpu.SMEM),
        out_shape=jax.ShapeDtypeStruct((n_slots,), jnp.int32),
        name="slot_invert",
    )(dest, jnp.asarray(spare))


def _slot_plan(ridx, n_tok, st):
    e = jnp.transpose(ridx[:, :TOP_K, :], (1, 0, 2)).reshape(-1)
    onehot = (e[:, None] == jnp.arange(N_EXPERTS, dtype=jnp.int32)[None, :]).astype(jnp.int32)
    incl = jnp.cumsum(onehot, axis=0)
    rank = jnp.sum((incl - onehot) * onehot, axis=1)
    counts = incl[-1]
    padded = (counts + TM - 1) // TM * TM
    pends = jnp.cumsum(padded)
    pstarts = pends - padded
    dest = pstarts[e] + rank
    n_blocks = (n_tok * TOP_K + N_EXPERTS * (TM - 1) + TM - 1) // TM
    blk_start = jnp.arange(n_blocks, dtype=jnp.int32) * TM
    blk_expert = jnp.minimum(jnp.sum((pends[None, :] <= blk_start[:, None]).astype(jnp.int32), axis=1),
                             N_EXPERTS - 1)
    n_used = (pends[-1] // TM).astype(jnp.int32).reshape(1)
    slot_dst = _invert_slots(dest.astype(jnp.int32), n_blocks * TM)
    slot_src = jnp.where(slot_dst < n_tok * TOP_K, slot_dst % n_tok, 0)
    shape3 = (n_blocks, 1, TM)
    return blk_expert.astype(jnp.int32), n_used, (slot_src * st).reshape(shape3), (slot_dst * st).reshape(shape3)


def _rope_tables(t_len, ctx_len):
    half = GLA_DK // 2
    inv = ROPE_BASE ** (-np.arange(0, half, 2, dtype=np.float32) / half)
    lane = np.arange(GLA_DK)
    p = np.arange(t_len - ctx_len)
    pos = np.where(lane[None, :] < half, (p // GRID_W)[:, None], (p % GRID_W)[:, None]).astype(np.float32)
    ang = jnp.asarray(pos) * jnp.asarray(inv[(lane % half) % (half // 2)])[None, :]
    cos, sin = jnp.cos(ang), jnp.sin(ang)
    lower = jnp.asarray((lane % half) < half // 2)[None, :]
    pad = lambda a, v: jnp.concatenate([jnp.full((ctx_len, GLA_DK), v, F32), a], axis=0)
    return pad(cos, 1.0), pad(jnp.where(lower, -sin, 0.0), 0.0), pad(jnp.where(lower, 0.0, sin), 0.0)


def kernel(x, c, ctx, c_ctx, w_ada, b_ada, norm1_w, norm2_w, w_in, w_gk_up, b_gk, gla_norm_w, rpb,
           w_bo_gla, w_bo_na, w_out, w_router, b_router, w1, w3, w2, final_norm_w):
    bsz, seq, d = x.shape
    ctx_len = ctx.shape[1]
    depth = w_ada.shape[0]
    assert ctx_len == TM and seq % TM == 0 and TM % GRID_W == 0 and seq // GRID_W >= 3 * (TM // GRID_W)
    t = ctx_len + seq
    nt = t // TM
    n_tok = bsz * t

    xs = jnp.concatenate([ctx, x], axis=1)
    cvec = jnp.concatenate([c, c_ctx[None], jnp.zeros((8 - bsz - 1, d), F32)], axis=0)
    mods = _modulation(cvec, w_ada, b_ada).reshape(depth, 8, 6, d)
    cos, sa, sb = _rope_tables(t, ctx_len)
    wrt = w_router.T.astype(F32)
    brt = jnp.broadcast_to(b_router.astype(F32)[:, None], (N_EXPERTS, TM))
    gq_end = 2 * GLA_QK + 2 * GLA_V
    wp = jnp.concatenate([w_in[:, :, :gq_end], w_in[:, :, gq_end + 2 * GLA_LR:],
                          w_in[:, :, gq_end:gq_end + 2 * GLA_LR],
                          jnp.zeros((depth, d, LR_PAD - 2 * GLA_LR), w_in.dtype)], axis=2).astype(BF16)
    wup = jnp.stack([jnp.pad(w_gk_up[:, dirn], ((0, 0), (dirn * GLA_LR, LR_PAD - (dirn + 1) * GLA_LR), (0, 0)))
                     for dirn in range(2)], axis=1).astype(BF16)
    bg = b_gk.reshape(depth, 2, 1, GLA_QK)
    nw1, nw2 = norm1_w.reshape(depth, 1, d), norm2_w.reshape(depth, 1, d)
    gnw = gla_norm_w.reshape(depth, 1, GLA_DV)
    wbg, wbn, wo = w_bo_gla.astype(BF16), w_bo_na.astype(BF16), w_out.astype(BF16)
    bias = _na_bias_tables(rpb, seq // GRID_W)

    out = None
    for l in range(depth):
        gq, gk, gv, gg, nq, nk, nv, m1, m2, lr = _inproj(xs, mods, nw1, wp, cos, sa, sb, l)
        ob = _gla_pass(True, gq, gk, gv, lr, wup, bg, l)
        gy = _gla_pass(False, gq, gk, gv, lr, wup, bg, l, extra=(ob, gg, gnw))
        ny = _na(nq, nk, nv, bias, l)
        xs, hp, ridx, rw = _merge(gy, ny, m1, m2, xs, mods, nw2, wbg, wbn, wo, wrt, brt, l)
        wcol = jnp.transpose(rw[:, :TOP_K, :], (0, 2, 1))
        blk_expert, n_used, slot_src, slot_dst = _slot_plan(ridx, n_tok, d // LANES)
        y = _moe(blk_expert, n_used, slot_src, slot_dst, hp, w1, w3, w2, l)
        if l < depth - 1:
            xs = _combine(y, xs, wcol, mods, l)
        else:
            out = _combine(y, xs, wcol, mods, l, final_norm_w[None])
    return out
```

```python
import functools

import jax
import jax.numpy as jnp
import numpy as np
from jax import lax
from jax.experimental import pallas as pl
from jax.experimental.pallas import tpu as pltpu

F32 = jnp.float32
BF16 = jnp.bfloat16

EPS = 1e-6
GRID_W = 64
GLA_HEADS = 4
GLA_DK = 128
GLA_DV = 256
GLA_LR = 16
GLA_TAU = 16.0
GLA_CHUNK = 64
ROPE_BASE = 10000.0
NA_HEADS = 8
NA_HD = 64
WIN_R = 8
WIN_C = 16
N_EXPERTS = 16
N_GROUPS = 4
EXPERTS_PER_GROUP = N_EXPERTS // N_GROUPS
TOP_K = 2

GLA_QK = GLA_HEADS * GLA_DK
GLA_V = GLA_HEADS * GLA_DV
NA_W = NA_HEADS * NA_HD

TM = 256
LANES = 128
LR_PAD = LANES
NEG = -1e30
VMEM_LIMIT = 56 * 1024 * 1024

_NT = (((1,), (1,)), ((), ()))
_TN = (((0,), (0,)), ((), ()))


def _cparams(sem):
    return pltpu.CompilerParams(dimension_semantics=sem, vmem_limit_bytes=VMEM_LIMIT)


def _dot(a, b):
    return jnp.dot(a, b, preferred_element_type=F32)


def _dg(a, b, dims):
    return lax.dot_general(a, b, dims, preferred_element_type=F32)


def _split(a):
    hi = a.astype(BF16)
    lo = (a - hi.astype(F32)).astype(BF16)
    return hi, lo


def _sigmoid(x):
    return 1.0 / (1.0 + jnp.exp(-x))


def _norm_mod(x, w, shift, scale):
    y = x * lax.rsqrt(jnp.mean(x * x, axis=-1, keepdims=True) + EPS)
    return (y * w) * (1.0 + scale) + shift


def _rows_to_tiles(ref, base, val):
    st = val.shape[1] // LANES
    for s in range(st):
        ref[pl.ds(base + s, val.shape[0], stride=st), :] = val[:, s * LANES:(s + 1) * LANES]


def _tiles_to_rows(ref, base, n, st):
    return jnp.concatenate([ref[pl.ds(base + s, n, stride=st), :] for s in range(st)], axis=1)


def _mod_kernel(c_ref, w_ref, b_ref, o_ref):
    c = c_ref[...]
    s = c * _sigmoid(c)
    sh, sl = _split(s)
    wh, wl = _split(w_ref[0])
    o_ref[0] = _dot(sh, wh) + _dot(sl, wh) + _dot(sh, wl) + b_ref[0]


def _modulation(cvec, w_ada, b_ada):
    depth, d, n = w_ada.shape
    tn = 1024
    return pl.pallas_call(
        _mod_kernel,
        grid=(depth, n // tn),
        in_specs=[pl.BlockSpec((8, d), lambda l, j: (0, 0)),
                  pl.BlockSpec((1, d, tn), lambda l, j: (l, 0, j)),
                  pl.BlockSpec((1, 1, tn), lambda l, j: (l, 0, j))],
        out_specs=pl.BlockSpec((1, 8, tn), lambda l, j: (l, 0, j)),
        out_shape=jax.ShapeDtypeStruct((depth, 8, n), F32),
        compiler_params=_cparams(("arbitrary", "arbitrary")),
        name="modulation",
    )(cvec, w_ada, b_ada.reshape(depth, 1, n))


_IN_OUT = (("gq", GLA_QK, BF16), ("gk", GLA_QK, BF16), ("gv", GLA_V, BF16), ("gg", GLA_V, BF16),
           ("nq", NA_W, BF16), ("nk", NA_W, BF16), ("nv", NA_W, BF16),
           ("m1", None, BF16), ("m2", None, BF16), ("lr", LR_PAD, F32))


_IN_SPLIT = (4, 9, 10)


def _inproj_kernel(x_ref, mod_ref, nw_ref, wa_ref, wb_ref, wc_ref, cos_ref, sa_ref, sb_ref, *o_refs):
    d = x_ref.shape[2]
    mod = mod_ref[0, 0]
    h = _norm_mod(x_ref[0], nw_ref[0], mod[0:1], mod[1:2]).astype(BF16)
    cos, sa, sb = cos_ref[...], sa_ref[...], sb_ref[...]

    def rope(r, scale):
        parts = []
        for hh in range(GLA_HEADS):
            xs = r[:, hh * GLA_DK:(hh + 1) * GLA_DK]
            y = xs * cos + pltpu.roll(xs, GLA_DK - 32, 1) * sa + pltpu.roll(xs, 32, 1) * sb
            parts.append(y * scale if scale != 1.0 else y)
        return jnp.concatenate(parts, axis=1)

    off = 0
    for gi, ((name, width, dt), o_ref) in enumerate(zip(_IN_OUT, o_refs)):
        width = d if width is None else width
        if gi in _IN_SPLIT:
            off = 0
        w_ref = wa_ref if gi < _IN_SPLIT[0] else (wb_ref if gi < _IN_SPLIT[1] else wc_ref)
        r = _dot(h, w_ref[0, :, off:off + width])
        if name == "gq":
            r = rope(r, GLA_DK ** -0.5)
        elif name == "gk":
            r = rope(r, 1.0)
        elif name == "nq":
            r = r * (NA_HD ** -0.5)
        o_ref[0] = r.astype(dt)
        off += width


def _layer_spec(arr, layer, *lead):
    tail = arr.shape[1 + len(lead):]
    index = (layer,) + lead + (0,) * len(tail)
    return pl.BlockSpec((1,) * (1 + len(lead)) + tail, lambda *_: index)


def _mod_spec(mods, layer, ctx_row, skip=0):
    return pl.BlockSpec((1, 1) + mods.shape[2:], lambda i, j: (layer, jnp.where(j + skip == 0, ctx_row, i), 0, 0))


def _inproj(x, mods, nw, wp, cos, sa, sb, layer):
    b, t, d = x.shape
    nt = t // TM
    widths = [d if w is None else w for _, w, _ in _IN_OUT]
    return pl.pallas_call(
        _inproj_kernel,
        grid=(b, nt),
        in_specs=[pl.BlockSpec((1, TM, d), lambda i, j: (i, j, 0)),
                  _mod_spec(mods, layer, b), _layer_spec(nw, layer)] + [_layer_spec(w, layer) for w in wp] + [
                  pl.BlockSpec((TM, GLA_DK), lambda i, j: (j, 0)),
                  pl.BlockSpec((TM, GLA_DK), lambda i, j: (j, 0)),
                  pl.BlockSpec((TM, GLA_DK), lambda i, j: (j, 0))],
        out_specs=[pl.BlockSpec((1, TM, w), lambda i, j: (i, j, 0)) for w in widths],
        out_shape=[jax.ShapeDtypeStruct((b, t, w), dt) for w, (_, _, dt) in zip(widths, _IN_OUT)],
        compiler_params=_cparams(("arbitrary", "arbitrary")),
        name="inproj",
    )(x, mods, nw, *wp, cos, sa, sb)


def _gla_kernel(reverse, finish, q_ref, k_ref, v_ref, lr_ref, wup_ref, bg_ref, *rest):
    if finish:
        ob_ref, g_ref, nw_ref, o_ref, st_ref = rest
    else:
        o_ref, st_ref = rest
    nchunk = TM // GLA_CHUNK
    nb = q_ref.shape[0]

    @pl.when(pl.program_id(0) == 0)
    def _():
        st_ref[...] = jnp.zeros_like(st_ref)

    r = lax.broadcasted_iota(jnp.int32, (TM, TM), 0)
    s = lax.broadcasted_iota(jnp.int32, (TM, TM), 1)
    order = (s >= r) if reverse else (s <= r)
    tri = jnp.where(((r // GLA_CHUNK) == (s // GLA_CHUNK)) & order, 1.0, 0.0).astype(BF16)
    cmask = order[:GLA_CHUNK, :GLA_CHUNK]
    bcs = []
    for bi in range(nb):
        z = _dot(lr_ref[bi].astype(BF16), wup_ref[0, 0]) + bg_ref[0, 0]
        logg = (jnp.minimum(z, 0.0) - jnp.log(1.0 + jnp.exp(-jnp.abs(z)))) * (1.0 / GLA_TAU)
        hi, lo = _split(logg)
        bcs.append(_dot(tri, hi) + _dot(tri, lo))

    for c in (range(nchunk - 1, -1, -1) if reverse else range(nchunk)):
        rs = slice(c * GLA_CHUNK, (c + 1) * GLA_CHUNK)
        for bi in range(nb):
            for hh in range(GLA_HEADS):
                ls = slice(hh * GLA_DK, (hh + 1) * GLA_DK)
                vs = slice(hh * GLA_DV, (hh + 1) * GLA_DV)
                b = bcs[bi][rs, ls]
                bend = b[0:1] if reverse else b[GLA_CHUNK - 1:GLA_CHUNK]
                qc = q_ref[bi, rs, ls].astype(F32)
                kc = k_ref[bi, rs, ls].astype(F32)
                vc = v_ref[bi, rs, vs]
                kd = (kc * jnp.exp(bend - b)).astype(BF16)
                qi = (qc * jnp.exp(b)).astype(BF16)
                qa = (qc * jnp.exp(b - bend)).astype(BF16)
                att = jnp.where(cmask, _dg(qa, kd, _NT), 0.0).astype(BF16)
                st = st_ref[bi * GLA_HEADS + hh]
                o = _dg(qi, st.astype(BF16), _NT) + _dot(att, vc)
                st_ref[bi * GLA_HEADS + hh] = st * jnp.exp(bend) + _dg(vc, kd, _TN)
                if finish:
                    o = o + ob_ref[bi, rs, vs].astype(F32)
                    o = o * lax.rsqrt(jnp.mean(o * o, axis=-1, keepdims=True) + EPS) * nw_ref[0]
                    g = g_ref[bi, rs, vs].astype(F32)
                    o = o * (g * _sigmoid(g))
                o_ref[bi, rs, vs] = o.astype(o_ref.dtype)


def _gla_pass(reverse, q, k, v, lr, wup, bg, layer, extra=None):
    b, t, _ = q.shape
    nt = t // TM
    dirn = 1 if reverse else 0
    if reverse:
        tile = lambda j: (0, jnp.where(j == 0, 0, nt - j), 0)
    else:
        tile = lambda j: (0, j, 0)
    in_specs = [pl.BlockSpec((b, TM, GLA_QK), tile), pl.BlockSpec((b, TM, GLA_QK), tile),
                pl.BlockSpec((b, TM, GLA_V), tile), pl.BlockSpec((b, TM, LR_PAD), tile),
                _layer_spec(wup, layer, dirn), _layer_spec(bg, layer, dirn)]
    args = [q, k, v, lr, wup, bg]
    if extra is not None:
        ob, g, nw = extra
        in_specs += [pl.BlockSpec((b, TM, GLA_V), tile), pl.BlockSpec((b, TM, GLA_V), tile),
                     _layer_spec(nw, layer)]
        args += [ob, g, nw]
    return pl.pallas_call(
        functools.partial(_gla_kernel, reverse, extra is not None),
        grid=(nt,),
        in_specs=in_specs,
        out_specs=pl.BlockSpec((b, TM, GLA_V), tile),
        out_shape=jax.ShapeDtypeStruct((b, t, GLA_V), BF16),
        scratch_shapes=[pltpu.VMEM((b * GLA_HEADS, GLA_DV, GLA_DK), F32)],
        compiler_params=_cparams(("arbitrary",)),
        name="gla_bwd" if reverse else "gla_fwd",
    )(*args)


def _na_bias_tables(rpb, rows):
    rpt = TM // GRID_W
    kr = min(WIN_R, rows)
    col = np.arange(GRID_W)
    cidx = np.clip(col[None, :] - col[:, None], -(WIN_C - 1), WIN_C - 1) + (WIN_C - 1)
    c_sel = (cidx[None] == np.arange(2 * WIN_C - 1)[:, None, None]).astype(np.float32)
    c0 = np.clip(col - WIN_C // 2, 0, GRID_W - WIN_C)
    c_ok = (col[None, :] >= c0[:, None]) & (col[None, :] < c0[:, None] + WIN_C)
    toep = jnp.einsum("lhrd,dqk->lhrqk", rpb.astype(F32), c_sel, precision=lax.Precision.HIGHEST)
    toep = jnp.where(c_ok, toep, NEG)
    depth = rpb.shape[0]
    masked = jnp.full((depth, NA_HEADS, GRID_W, GRID_W), NEG, F32)
    tabs = [jnp.full((depth, NA_HEADS, TM, 3 * TM), NEG, F32)]
    for r_base, u_base in ((0, 0), (rpt, 0), (rows - rpt, rows - 3 * rpt)):
        blocks = []
        for i in range(rpt):
            rq = r_base + i
            r0 = min(max(rq - kr // 2, 0), rows - kr)
            row = [toep[:, :, u_base + m - rq + (WIN_R - 1)] if r0 <= u_base + m < r0 + kr else masked
                   for m in range(3 * rpt)]
            blocks.append(jnp.concatenate(row, axis=-1))
        tabs.append(jnp.concatenate(blocks, axis=-2))
    return jnp.stack(tabs, axis=1)


def _na_kernel(q_ref, kp_ref, kc_ref, kn_ref, kx_ref, vp_ref, vc_ref, vn_ref, vx_ref, bias_ref, o_ref):
    lane = lax.broadcasted_iota(jnp.int32, (TM, 2 * NA_HD), 1)
    first = lane < NA_HD
    k_refs = (kp_ref, kc_ref, kn_ref, kx_ref)
    v_refs = (vp_ref, vc_ref, vn_ref, vx_ref)
    for hp in range(NA_HEADS // 2):
        ls = slice(hp * 2 * NA_HD, (hp + 1) * 2 * NA_HD)
        q2 = q_ref[0, :, ls]
        ks = [kr[0, :, ls] for kr in k_refs]
        vs = [vr[0, :, ls] for vr in v_refs]
        outs = []
        for sub in range(2):
            hd = 2 * hp + sub
            qm = jnp.where(first if sub == 0 else jnp.logical_not(first), q2, jnp.zeros_like(q2))
            sc = []
            for i in range(4):
                s = _dg(qm, ks[i], _NT)
                if i < 3:
                    s = s + bias_ref[0, 0, hd, :, i * TM:(i + 1) * TM]
                sc.append(s)
            m = jnp.max(sc[0], axis=-1, keepdims=True)
            for i in range(1, 4):
                m = jnp.maximum(m, jnp.max(sc[i], axis=-1, keepdims=True))
            l = jnp.zeros_like(m)
            acc = jnp.zeros((TM, 2 * NA_HD), F32)
            for i in range(4):
                p = jnp.exp(sc[i] - m)
                l = l + jnp.sum(p, axis=-1, keepdims=True)
                acc = acc + _dot(p.astype(BF16), vs[i])
            outs.append(acc / l)
        o_ref[0, :, ls] = jnp.where(first, outs[0], outs[1]).astype(o_ref.dtype)


def _na(q, k, v, bias, layer):
    b, t, _ = q.shape
    nt = t // TM
    qt = lambda i, j: (i, j, 0)
    ctx = lambda i, j: (i, 0, 0)

    def near(o):
        return lambda i, j: (i, jnp.clip(j, 2, nt - 2) + o, 0)

    def kind(i, j):
        return (layer, jnp.where(j == 0, 0, jnp.where(j == 1, 1, jnp.where(j == nt - 1, 3, 2))), 0, 0, 0)

    blk = lambda f: pl.BlockSpec((1, TM, NA_W), f)
    return pl.pallas_call(
        _na_kernel,
        grid=(b, nt),
        in_specs=[blk(qt), blk(near(-1)), blk(near(0)), blk(near(1)), blk(ctx),
                  blk(near(-1)), blk(near(0)), blk(near(1)), blk(ctx),
                  pl.BlockSpec((1, 1, NA_HEADS, TM, 3 * TM), kind)],
        out_specs=blk(qt),
        out_shape=jax.ShapeDtypeStruct((b, t, NA_W), BF16),
        compiler_params=_cparams(("arbitrary", "arbitrary")),
        name="natten",
    )(q, k, k, k, k, v, v, v, v, bias)


def _route(sel, aff):
    rows = lambda a, e: a[e:e + 1, :]
    gscore = []
    for g in range(N_GROUPS):
        a, b, c, d = (rows(sel, EXPERTS_PER_GROUP * g + i) for i in range(EXPERTS_PER_GROUP))
        hi1, lo1 = jnp.maximum(a, b), jnp.minimum(a, b)
        hi2, lo2 = jnp.maximum(c, d), jnp.minimum(c, d)
        gscore.append(jnp.maximum(hi1, hi2) + jnp.maximum(jnp.minimum(hi1, hi2), jnp.maximum(lo1, lo2)))
    gbest = jnp.zeros_like(gscore[0], dtype=jnp.int32)
    gval = gscore[0]
    for g in range(1, N_GROUPS):
        better = gscore[g] > gval
        gbest = jnp.where(better, g, gbest)
        gval = jnp.where(better, gscore[g], gval)
    cs, ca = [], []
    for i in range(EXPERTS_PER_GROUP):
        s_i, a_i = rows(sel, i), rows(aff, i)
        for g in range(1, N_GROUPS):
            pick = gbest == g
            s_i = jnp.where(pick, rows(sel, EXPERTS_PER_GROUP * g + i), s_i)
            a_i = jnp.where(pick, rows(aff, EXPERTS_PER_GROUP * g + i), a_i)
        cs.append(s_i)
        ca.append(a_i)
    i1 = jnp.zeros_like(gbest)
    v1, w1 = cs[0], ca[0]
    for i in range(1, EXPERTS_PER_GROUP):
        better = cs[i] > v1
        i1 = jnp.where(better, i, i1)
        v1 = jnp.where(better, cs[i], v1)
        w1 = jnp.where(better, ca[i], w1)
    i2 = jnp.full_like(gbest, -1)
    v2 = jnp.full_like(v1, -jnp.inf)
    w2 = jnp.zeros_like(w1)
    for i in range(EXPERTS_PER_GROUP):
        better = (i1 != i) & ((cs[i] > v2) | (i2 < 0))
        i2 = jnp.where(better, i, i2)
        v2 = jnp.where(better, cs[i], v2)
        w2 = jnp.where(better, ca[i], w2)
    tot = w1 + w2
    base = gbest * EXPERTS_PER_GROUP
    return base + i1, base + i2, w1 / tot, w2 / tot


def _merge_kernel(gy_ref, ny_ref, m1_ref, m2_ref, x_ref, mod_ref, nw_ref, wbg_ref, wbn_ref, wo_ref,
                  wr_ref, br_ref, xo_ref, hp_ref, ri_ref, rw_ref):
    nb = x_ref.shape[0]
    is_ctx = pl.program_id(0) == 0
    wh, wl = _split(wr_ref[...])
    zi = jnp.zeros((6, TM), jnp.int32)
    for bi in range(nb):
        a = _dot(gy_ref[bi], wbg_ref[0])
        b = _dot(ny_ref[bi], wbn_ref[0])
        m = _sigmoid(m1_ref[bi].astype(F32)) * a + _sigmoid(m2_ref[bi].astype(F32)) * b
        y = _dot(m.astype(BF16), wo_ref[0])
        mod = jnp.where(is_ctx, mod_ref[0, nb], mod_ref[0, bi])
        xn = x_ref[bi] + mod[2:3] * y
        xo_ref[bi] = xn
        h2 = _norm_mod(xn, nw_ref[0], mod[3:4], mod[4:5])
        _rows_to_tiles(hp_ref.at[bi], 0, h2)
        hh, hl = _split(h2)
        logit = _dg(wh, hh, _NT) + _dg(wh, hl, _NT) + _dg(wl, hh, _NT)
        aff = _sigmoid(logit)
        i1, i2, w1, w2 = _route(aff + br_ref[...], aff)
        ri_ref[bi] = jnp.concatenate([i1, i2, zi], axis=0)
        rw_ref[bi] = jnp.concatenate([w1, w2, zi.astype(F32)], axis=0)


def _merge(gy, ny, m1, m2, x, mods, nw2, wbg, wbn, wo, wrt, brt, layer):
    b, t, d = x.shape
    nt = t // TM
    rows = TM * d // LANES
    tile = lambda j: (0, j, 0)
    const = lambda j: (0, 0)
    xo, hp, ri, rw = pl.pallas_call(
        _merge_kernel,
        grid=(nt,),
        in_specs=[pl.BlockSpec((b, TM, GLA_V), tile), pl.BlockSpec((b, TM, NA_W), tile),
                  pl.BlockSpec((b, TM, d), tile), pl.BlockSpec((b, TM, d), tile),
                  pl.BlockSpec((b, TM, d), tile),
                  _layer_spec(mods, layer), _layer_spec(nw2, layer),
                  _layer_spec(wbg, layer), _layer_spec(wbn, layer), _layer_spec(wo, layer),
                  pl.BlockSpec((N_EXPERTS, d), const), pl.BlockSpec((N_EXPERTS, TM), const)],
        out_specs=[pl.BlockSpec((b, TM, d), tile), pl.BlockSpec((b, rows, LANES), tile),
                   pl.BlockSpec((b, 8, TM), lambda j: (0, 0, j)),
                   pl.BlockSpec((b, 8, TM), lambda j: (0, 0, j))],
        out_shape=[jax.ShapeDtypeStruct((b, t, d), F32), jax.ShapeDtypeStruct((b, nt * rows, LANES), F32),
                   jax.ShapeDtypeStruct((b, 8, t), jnp.int32), jax.ShapeDtypeStruct((b, 8, t), F32)],
        compiler_params=_cparams(("arbitrary",)),
        name="merge_router",
    )(gy, ny, m1, m2, x, mods, nw2, wbg, wbn, wo, wrt, brt)
    return xo, hp.reshape(b * nt * rows, LANES), ri, rw


def _moe_kernel(be_ref, nu_ref, src_cur, src_nxt, src_nx2, dst_cur, dst_prv, h_ref, w1_ref, w3_ref, w2_ref,
                o_ref, xbuf, ybuf, w1b, w3b, w2b, gsem, ssem):
    i = pl.program_id(0)
    nu = nu_ref[0]
    xb = lax.rem(i, 3)
    yb = lax.rem(i, 3)
    used = i < nu
    st = w1_ref.shape[2] // LANES
    blk = TM * st
    n_real = o_ref.shape[0] - 2 * blk

    def part(ref, b):
        return ref.at[pl.ds(pl.multiple_of(b * blk, blk), blk)]

    def token(ref, row):
        return ref.at[pl.ds(pl.multiple_of(row, st), st)]

    def start_gather(idx_ref, b):
        for r in range(TM):
            pltpu.make_async_copy(token(h_ref, idx_ref[0, 0, r]), token(xbuf, b * blk + r * st),
                                  gsem.at[b]).start(priority=r % 2)

    def start_scatter(idx_ref, b):
        for r in range(TM):
            pltpu.make_async_copy(token(ybuf, b * blk + r * st), token(o_ref, idx_ref[0, 0, r]),
                                  ssem.at[b]).start(priority=r % 2)

    def wait_gather(b):
        pltpu.make_async_copy(h_ref.at[pl.ds(0, blk)], part(xbuf, b), gsem.at[b]).wait()

    def wait_scatter(b):
        pltpu.make_async_copy(part(ybuf, b), o_ref.at[pl.ds(0, blk)], ssem.at[b]).wait()

    @pl.when(i == 0)
    def _():
        xbuf[...] = jnp.zeros_like(xbuf)
        ybuf[...] = jnp.zeros_like(ybuf)
        for b in range(2):
            spare = pltpu.make_async_copy(part(xbuf, b), o_ref.at[pl.ds(n_real + b * blk, blk)], ssem.at[b])
            spare.start()
            spare.wait()
        start_gather(src_cur, 0)
        start_gather(src_nxt, 1)

    fresh = jnp.logical_or(i == 0, be_ref[i] != be_ref[jnp.maximum(i - 1, 0)])

    @pl.when(jnp.logical_and(used, fresh))
    def _():
        for src, dst in ((w1_ref, w1b), (w3_ref, w3b), (w2_ref, w2b)):
            def body(r, carry, src=src, dst=dst):
                rows = pl.ds(pl.multiple_of(r * LANES, LANES), LANES)
                dst[rows, :] = src[0, 0, rows, :].astype(BF16)
                return carry
            lax.fori_loop(0, src.shape[2] // LANES, body, 0)

    @pl.when(jnp.logical_and(used, i > 0))
    def _():
        wait_scatter(lax.rem(i + 1, 3))

    @pl.when(used)
    def _():
        wait_gather(xb)
        x = _tiles_to_rows(xbuf, xb * blk, TM, st).astype(BF16)
        start_gather(src_nx2, lax.rem(i + 2, 3))
        start_scatter(dst_prv, lax.rem(i + 2, 3))
        a = _dot(x, w1b[...])
        b = _dot(x, w3b[...])
        hmid = (a * _sigmoid(a)) * b
        _rows_to_tiles(ybuf, yb * blk, _dot(hmid.astype(BF16), w2b[...]))

    @pl.when(i == nu - 1)
    def _():
        start_scatter(dst_cur, yb)
        wait_scatter(yb)
        wait_scatter(lax.rem(i + 2, 3))
        wait_gather(lax.rem(i + 1, 3))
        wait_gather(lax.rem(i + 2, 3))


def _moe(blk_expert, n_used, slot_src, slot_dst, h, w1, w3, w2, layer):
    d, f = w1.shape[2], w1.shape[3]
    st = d // LANES
    blk = TM * st
    n_rows = h.shape[0]
    nb = blk_expert.shape[0]
    wsel = lambda i, be, nu: (layer, be[i], 0, 0)
    smem = lambda f_: pl.BlockSpec((1, 1, TM), f_, memory_space=pltpu.SMEM)
    cur = lambda i, be, nu: (i, 0, 0)
    nxt = lambda i, be, nu: (jnp.minimum(i + 1, nb - 1), 0, 0)
    nx2 = lambda i, be, nu: (jnp.minimum(i + 2, nb - 1), 0, 0)
    prv = lambda i, be, nu: (jnp.maximum(i - 1, 0), 0, 0)
    return pl.pallas_call(
        _moe_kernel,
        grid_spec=pltpu.PrefetchScalarGridSpec(
            num_scalar_prefetch=2,
            grid=(nb,),
            in_specs=[smem(cur), smem(nxt), smem(nx2), smem(cur), smem(prv),
                      pl.BlockSpec(memory_space=pl.ANY), pl.BlockSpec((1, 1, d, f), wsel),
                      pl.BlockSpec((1, 1, d, f), wsel), pl.BlockSpec((1, 1, f, d), wsel)],
            out_specs=pl.BlockSpec(memory_space=pl.ANY),
            scratch_shapes=[pltpu.VMEM((3 * blk, LANES), F32), pltpu.VMEM((3 * blk, LANES), F32),
                            pltpu.VMEM((d, f), BF16), pltpu.VMEM((d, f), BF16), pltpu.VMEM((f, d), BF16),
                            pltpu.SemaphoreType.DMA((3,)), pltpu.SemaphoreType.DMA((3,))]),
        out_shape=jax.ShapeDtypeStruct((TOP_K * n_rows + 2 * blk, LANES), F32),
        compiler_params=_cparams(("arbitrary",)),
        name="moe_experts",
    )(blk_expert, n_used, slot_src, slot_src, slot_src, slot_dst, slot_dst, h, w1, w3, w2)


def _combine_kernel(final, *refs):
    y_refs, (x_ref, w_ref, mod_ref), rest = refs[:TOP_K], refs[TOP_K:TOP_K + 3], refs[TOP_K + 3:]
    if final:
        fw_ref, o_ref = rest
    else:
        (o_ref,) = rest
    w = w_ref[0]
    y = None
    for kk in range(TOP_K):
        yk = _tiles_to_rows(y_refs[kk], 0, TM, x_ref.shape[2] // LANES) * w[:, kk:kk + 1]
        y = yk if y is None else y + yk
    xn = x_ref[0] + mod_ref[0, 0][5:6] * y
    if final:
        xn = xn * lax.rsqrt(jnp.mean(xn * xn, axis=-1, keepdims=True) + EPS) * fw_ref[...]
    o_ref[0] = xn


def _combine(y, x, wcol, mods, layer, final_w=None):
    b, t, d = x.shape
    nt = t // TM
    final = final_w is not None
    skip = 1 if final else 0
    tile = lambda i, j: (i, j + skip, 0)

    def plane(kk):
        return pl.BlockSpec((TM * d // LANES, LANES), lambda i, j: ((kk * b + i) * nt + j + skip, 0))

    in_specs = [plane(kk) for kk in range(TOP_K)] + [
        pl.BlockSpec((1, TM, d), tile), pl.BlockSpec((1, TM, TOP_K), tile), _mod_spec(mods, layer, b, skip)]
    args = [y] * TOP_K + [x, wcol, mods]
    if final:
        in_specs.append(pl.BlockSpec((1, d), lambda i, j: (0, 0)))
        args.append(final_w)
    return pl.pallas_call(
        functools.partial(_combine_kernel, final),
        grid=(b, nt - skip),
        in_specs=in_specs,
        out_specs=pl.BlockSpec((1, TM, d), lambda i, j: (i, j, 0)),
        out_shape=jax.ShapeDtypeStruct((b, t - skip * TM, d), F32),
        compiler_params=_cparams(("arbitrary", "arbitrary")),
        name="moe_combine_final" if final else "moe_combine",
    )(*args)


def _invert_kernel(dest_ref, spare_ref, out_ref):
    pltpu.sync_copy(spare_ref, out_ref)

    def place(p, carry):
        out_ref[dest_ref[p]] = p
        return carry

    lax.fori_loop(0, dest_ref.shape[0], place, 0, unroll=16)


def _invert_slots(dest, n_slots):
    spare = dest.shape[0] + np.arange(n_slots, dtype=np.int32) % (2 * TM)
    return pl.pallas_call(
        _invert_kernel,
        in_specs=[pl.BlockSpec(memory_space=pltpu.SMEM), pl.BlockSpec(memory_space=pl.ANY)],
        out_specs=pl.BlockSpec(memory_space=pltpu.SMEM),
        out_shape=jax.ShapeDtypeStruct((n_slots,), jnp.int32),
        name="slot_invert",
    )(dest, jnp.asarray(spare))


def _slot_plan(ridx, n_tok, st):
    e = jnp.transpose(ridx[:, :TOP_K, :], (1, 0, 2)).reshape(-1)
    onehot = (e[:, None] == jnp.arange(N_EXPERTS, dtype=jnp.int32)[None, :]).astype(jnp.int32)
    incl = jnp.cumsum(onehot, axis=0)
    rank = jnp.sum((incl - onehot) * onehot, axis=1)
    counts = incl[-1]
    padded = (counts + TM - 1) // TM * TM
    pends = jnp.cumsum(padded)
    pstarts = pends - padded
    dest = pstarts[e] + rank
    n_blocks = (n_tok * TOP_K + N_EXPERTS * (TM - 1) + TM - 1) // TM
    blk_start = jnp.arange(n_blocks, dtype=jnp.int32) * TM
    blk_expert = jnp.minimum(jnp.sum((pends[None, :] <= blk_start[:, None]).astype(jnp.int32), axis=1),
                             N_EXPERTS - 1)
    n_used = (pends[-1] // TM).astype(jnp.int32).reshape(1)
    slot_dst = _invert_slots(dest.astype(jnp.int32), n_blocks * TM)
    slot_src = jnp.where(slot_dst < n_tok * TOP_K, slot_dst % n_tok, 0)
    shape3 = (n_blocks, 1, TM)
    return blk_expert.astype(jnp.int32), n_used, (slot_src * st).reshape(shape3), (slot_dst * st).reshape(shape3)


def _rope_tables(t_len, ctx_len):
    half = GLA_DK // 2
    inv = ROPE_BASE ** (-np.arange(0, half, 2, dtype=np.float32) / half)
    lane = np.arange(GLA_DK)
    p = np.arange(t_len - ctx_len)
    pos = np.where(lane[None, :] < half, (p // GRID_W)[:, None], (p % GRID_W)[:, None]).astype(np.float32)
    ang = jnp.asarray(pos) * jnp.asarray(inv[(lane % half) % (half // 2)])[None, :]
    cos, sin = jnp.cos(ang), jnp.sin(ang)
    lower = jnp.asarray((lane % half) < half // 2)[None, :]
    pad = lambda a, v: jnp.concatenate([jnp.full((ctx_len, GLA_DK), v, F32), a], axis=0)
    return pad(cos, 1.0), pad(jnp.where(lower, -sin, 0.0), 0.0), pad(jnp.where(lower, 0.0, sin), 0.0)


def kernel(x, c, ctx, c_ctx, w_ada, b_ada, norm1_w, norm2_w, w_in, w_gk_up, b_gk, gla_norm_w, rpb,
           w_bo_gla, w_bo_na, w_out, w_router, b_router, w1, w3, w2, final_norm_w):
    bsz, seq, d = x.shape
    ctx_len = ctx.shape[1]
    depth = w_ada.shape[0]
    assert ctx_len == TM and seq % TM == 0 and TM % GRID_W == 0 and seq // GRID_W >= 3 * (TM // GRID_W)
    t = ctx_len + seq
    nt = t // TM
    n_tok = bsz * t

    xs = jnp.concatenate([ctx, x], axis=1)
    cvec = jnp.concatenate([c, c_ctx[None], jnp.zeros((8 - bsz - 1, d), F32)], axis=0)
    mods = _modulation(cvec, w_ada, b_ada).reshape(depth, 8, 6, d)
    cos, sa, sb = _rope_tables(t, ctx_len)
    wrt = w_router.T.astype(F32)
    brt = jnp.broadcast_to(b_router.astype(F32)[:, None], (N_EXPERTS, TM))
    gq_end = 2 * GLA_QK + 2 * GLA_V
    wp = (w_in[:, :, :gq_end].astype(BF16), w_in[:, :, gq_end + 2 * GLA_LR:].astype(BF16),
          jnp.pad(w_in[:, :, gq_end:gq_end + 2 * GLA_LR], ((0, 0), (0, 0), (0, LR_PAD - 2 * GLA_LR))).astype(BF16))
    wup = jnp.stack([jnp.pad(w_gk_up[:, dirn], ((0, 0), (dirn * GLA_LR, LR_PAD - (dirn + 1) * GLA_LR), (0, 0)))
                     for dirn in range(2)], axis=1).astype(BF16)
    bg = b_gk.reshape(depth, 2, 1, GLA_QK)
    nw1, nw2 = norm1_w.reshape(depth, 1, d), norm2_w.reshape(depth, 1, d)
    gnw = gla_norm_w.reshape(depth, 1, GLA_DV)
    wbg, wbn, wo = w_bo_gla.astype(BF16), w_bo_na.astype(BF16), w_out.astype(BF16)
    bias = _na_bias_tables(rpb, seq // GRID_W)

    out = None
    for l in range(depth):
        gq, gk, gv, gg, nq, nk, nv, m1, m2, lr = _inproj(xs, mods, nw1, wp, cos, sa, sb, l)
        ob = _gla_pass(True, gq, gk, gv, lr, wup, bg, l)
        gy = _gla_pass(False, gq, gk, gv, lr, wup, bg, l, extra=(ob, gg, gnw))
        ny = _na(nq, nk, nv, bias, l)
        xs, hp, ridx, rw = _merge(gy, ny, m1, m2, xs, mods, nw2, wbg, wbn, wo, wrt, brt, l)
        wcol = jnp.transpose(rw[:, :TOP_K, :], (0, 2, 1))
        blk_expert, n_used, slot_src, slot_dst = _slot_plan(ridx, n_tok, d // LANES)
        y = _moe(blk_expert, n_used, slot_src, slot_dst, hp, w1, w3, w2, l)
        if l < depth - 1:
            xs = _combine(y, xs, wcol, mods, l)
        else:
            out = _combine(y, xs, wcol, mods, l, final_norm_w[None])
    return out
```

```python
import functools

import jax
import jax.numpy as jnp
import numpy as np
from jax import lax
from jax.experimental import pallas as pl
from jax.experimental.pallas import tpu as pltpu

F32 = jnp.float32
BF16 = jnp.bfloat16

EPS = 1e-6
GRID_W = 64
GLA_HEADS = 4
GLA_DK = 128
GLA_DV = 256
GLA_LR = 16
GLA_TAU = 16.0
GLA_CHUNK = 64
ROPE_BASE = 10000.0
NA_HEADS = 8
NA_HD = 64
WIN_R = 8
WIN_C = 16
N_EXPERTS = 16
N_GROUPS = 4
EXPERTS_PER_GROUP = N_EXPERTS // N_GROUPS
TOP_K = 2

GLA_QK = GLA_HEADS * GLA_DK
GLA_V = GLA_HEADS * GLA_DV
NA_W = NA_HEADS * NA_HD

TM = 256
LANES = 128
LR_PAD = LANES
NEG = -1e30
VMEM_LIMIT = 56 * 1024 * 1024

_NT = (((1,), (1,)), ((), ()))
_TN = (((0,), (0,)), ((), ()))


def _cparams(sem):
    return pltpu.CompilerParams(dimension_semantics=sem, vmem_limit_bytes=VMEM_LIMIT)


def _dot(a, b):
    return jnp.dot(a, b, preferred_element_type=F32)


def _dg(a, b, dims):
    return lax.dot_general(a, b, dims, preferred_element_type=F32)


def _split(a):
    hi = a.astype(BF16)
    lo = (a - hi.astype(F32)).astype(BF16)
    return hi, lo


def _sigmoid(x):
    return 1.0 / (1.0 + jnp.exp(-x))


def _norm_mod(x, w, shift, scale):
    y = x * lax.rsqrt(jnp.mean(x * x, axis=-1, keepdims=True) + EPS)
    return (y * w) * (1.0 + scale) + shift


def _rows_to_tiles(ref, base, val):
    st = val.shape[1] // LANES
    for s in range(st):
        ref[pl.ds(base + s, val.shape[0], stride=st), :] = val[:, s * LANES:(s + 1) * LANES]


def _tiles_to_rows(ref, base, n, st):
    return jnp.concatenate([ref[pl.ds(base + s, n, stride=st), :] for s in range(st)], axis=1)


def _mod_kernel(c_ref, w_ref, b_ref, o_ref):
    c = c_ref[...]
    s = c * _sigmoid(c)
    sh, sl = _split(s)
    wh, wl = _split(w_ref[0])
    o_ref[0] = _dot(sh, wh) + _dot(sl, wh) + _dot(sh, wl) + b_ref[0]


def _modulation(cvec, w_ada, b_ada):
    depth, d, n = w_ada.shape
    tn = 1024
    return pl.pallas_call(
        _mod_kernel,
        grid=(depth, n // tn),
        in_specs=[pl.BlockSpec((8, d), lambda l, j: (0, 0)),
                  pl.BlockSpec((1, d, tn), lambda l, j: (l, 0, j)),
                  pl.BlockSpec((1, 1, tn), lambda l, j: (l, 0, j))],
        out_specs=pl.BlockSpec((1, 8, tn), lambda l, j: (l, 0, j)),
        out_shape=jax.ShapeDtypeStruct((depth, 8, n), F32),
        compiler_params=_cparams(("arbitrary", "arbitrary")),
        name="modulation",
    )(cvec, w_ada, b_ada.reshape(depth, 1, n))


_IN_OUT = (("gq", GLA_QK, BF16), ("gk", GLA_QK, BF16), ("gv", GLA_V, BF16), ("gg", GLA_V, BF16),
           ("nq", NA_W, BF16), ("nk", NA_W, BF16), ("nv", NA_W, BF16),
           ("m1", None, BF16), ("m2", None, BF16), ("lr", LR_PAD, F32))


_IN_SPLIT = (4, 9, 10)


def _inproj_kernel(x_ref, mod_ref, nw_ref, wa_ref, wb_ref, wc_ref, cos_ref, sa_ref, sb_ref, *o_refs):
    d = x_ref.shape[2]
    mod = mod_ref[0, 0]
    h = _norm_mod(x_ref[0], nw_ref[0], mod[0:1], mod[1:2]).astype(BF16)
    cos, sa, sb = cos_ref[...], sa_ref[...], sb_ref[...]

    def rope(r, scale):
        parts = []
        for hh in range(GLA_HEADS):
            xs = r[:, hh * GLA_DK:(hh + 1) * GLA_DK]
            y = xs * cos + pltpu.roll(xs, GLA_DK - 32, 1) * sa + pltpu.roll(xs, 32, 1) * sb
            parts.append(y * scale if scale != 1.0 else y)
        return jnp.concatenate(parts, axis=1)

    off = 0
    for gi, ((name, width, dt), o_ref) in enumerate(zip(_IN_OUT, o_refs)):
        width = d if width is None else width
        if gi in _IN_SPLIT:
            off = 0
        w_ref = wa_ref if gi < _IN_SPLIT[0] else (wb_ref if gi < _IN_SPLIT[1] else wc_ref)
        r = _dot(h, w_ref[0, :, off:off + width])
        if name == "gq":
            r = rope(r, GLA_DK ** -0.5)
        elif name == "gk":
            r = rope(r, 1.0)
        elif name == "nq":
            r = r * (NA_HD ** -0.5)
        o_ref[0] = r.astype(dt)
        off += width


def _layer_spec(arr, layer, *lead):
    tail = arr.shape[1 + len(lead):]
    index = (layer,) + lead + (0,) * len(tail)
    return pl.BlockSpec((1,) * (1 + len(lead)) + tail, lambda *_: index)


def _mod_spec(mods, layer, ctx_row, skip=0):
    return pl.BlockSpec((1, 1) + mods.shape[2:], lambda i, j: (layer, jnp.where(j + skip == 0, ctx_row, i), 0, 0))


def _inproj(x, mods, nw, wp, cos, sa, sb, layer):
    b, t, d = x.shape
    nt = t // TM
    widths = [d if w is None else w for _, w, _ in _IN_OUT]
    return pl.pallas_call(
        _inproj_kernel,
        grid=(b, nt),
        in_specs=[pl.BlockSpec((1, TM, d), lambda i, j: (i, j, 0)),
                  _mod_spec(mods, layer, b), _layer_spec(nw, layer)] + [_layer_spec(w, layer) for w in wp] + [
                  pl.BlockSpec((TM, GLA_DK), lambda i, j: (j, 0)),
                  pl.BlockSpec((TM, GLA_DK), lambda i, j: (j, 0)),
                  pl.BlockSpec((TM, GLA_DK), lambda i, j: (j, 0))],
        out_specs=[pl.BlockSpec((1, TM, w), lambda i, j: (i, j, 0)) for w in widths],
        out_shape=[jax.ShapeDtypeStruct((b, t, w), dt) for w, (_, _, dt) in zip(widths, _IN_OUT)],
        compiler_params=_cparams(("arbitrary", "arbitrary")),
        name="inproj",
    )(x, mods, nw, *wp, cos, sa, sb)


def _gla_kernel(reverse, finish, q_ref, k_ref, v_ref, lr_ref, wup_ref, bg_ref, *rest):
    if finish:
        ob_ref, g_ref, nw_ref, o_ref, st_ref = rest
    else:
        o_ref, st_ref = rest
    nchunk = TM // GLA_CHUNK
    nb = q_ref.shape[0]

    @pl.when(pl.program_id(0) == 0)
    def _():
        st_ref[...] = jnp.zeros_like(st_ref)

    r = lax.broadcasted_iota(jnp.int32, (TM, TM), 0)
    s = lax.broadcasted_iota(jnp.int32, (TM, TM), 1)
    order = (s >= r) if reverse else (s <= r)
    tri = jnp.where(((r // GLA_CHUNK) == (s // GLA_CHUNK)) & order, 1.0, 0.0).astype(BF16)
    cmask = order[:GLA_CHUNK, :GLA_CHUNK]
    bcs = []
    for bi in range(nb):
        z = _dot(lr_ref[bi].astype(BF16), wup_ref[0, 0]) + bg_ref[0, 0]
        logg = (jnp.minimum(z, 0.0) - jnp.log(1.0 + jnp.exp(-jnp.abs(z)))) * (1.0 / GLA_TAU)
        hi, lo = _split(logg)
        bcs.append(_dot(tri, hi) + _dot(tri, lo))

    for c in (range(nchunk - 1, -1, -1) if reverse else range(nchunk)):
        rs = slice(c * GLA_CHUNK, (c + 1) * GLA_CHUNK)
        for bi in range(nb):
            for hh in range(GLA_HEADS):
                ls = slice(hh * GLA_DK, (hh + 1) * GLA_DK)
                vs = slice(hh * GLA_DV, (hh + 1) * GLA_DV)
                b = bcs[bi][rs, ls]
                bend = b[0:1] if reverse else b[GLA_CHUNK - 1:GLA_CHUNK]
                qc = q_ref[bi, rs, ls].astype(F32)
                kc = k_ref[bi, rs, ls].astype(F32)
                vc = v_ref[bi, rs, vs]
                kd = (kc * jnp.exp(bend - b)).astype(BF16)
                qi = (qc * jnp.exp(b)).astype(BF16)
                qa = (qc * jnp.exp(b - bend)).astype(BF16)
                att = jnp.where(cmask, _dg(qa, kd, _NT), 0.0).astype(BF16)
                st = st_ref[bi * GLA_HEADS + hh]
                o = _dg(qi, st.astype(BF16), _NT) + _dot(att, vc)
                st_ref[bi * GLA_HEADS + hh] = st * jnp.exp(bend) + _dg(vc, kd, _TN)
                if finish:
                    o = o + ob_ref[bi, rs, vs].astype(F32)
                    o = o * lax.rsqrt(jnp.mean(o * o, axis=-1, keepdims=True) + EPS) * nw_ref[0]
                    g = g_ref[bi, rs, vs].astype(F32)
                    o = o * (g * _sigmoid(g))
                o_ref[bi, rs, vs] = o.astype(o_ref.dtype)


def _gla_pass(reverse, q, k, v, lr, wup, bg, layer, extra=None):
    b, t, _ = q.shape
    nt = t // TM
    dirn = 1 if reverse else 0
    if reverse:
        tile = lambda j: (0, jnp.where(j == 0, 0, nt - j), 0)
    else:
        tile = lambda j: (0, j, 0)
    in_specs = [pl.BlockSpec((b, TM, GLA_QK), tile), pl.BlockSpec((b, TM, GLA_QK), tile),
                pl.BlockSpec((b, TM, GLA_V), tile), pl.BlockSpec((b, TM, LR_PAD), tile),
                _layer_spec(wup, layer, dirn), _layer_spec(bg, layer, dirn)]
    args = [q, k, v, lr, wup, bg]
    if extra is not None:
        ob, g, nw = extra
        in_specs += [pl.BlockSpec((b, TM, GLA_V), tile), pl.BlockSpec((b, TM, GLA_V), tile),
                     _layer_spec(nw, layer)]
        args += [ob, g, nw]
    return pl.pallas_call(
        functools.partial(_gla_kernel, reverse, extra is not None),
        grid=(nt,),
        in_specs=in_specs,
        out_specs=pl.BlockSpec((b, TM, GLA_V), tile),
        out_shape=jax.ShapeDtypeStruct((b, t, GLA_V), BF16),
        scratch_shapes=[pltpu.VMEM((b * GLA_HEADS, GLA_DV, GLA_DK), F32)],
        compiler_params=_cparams(("arbitrary",)),
        name="gla_bwd" if reverse else "gla_fwd",
    )(*args)


def _na_bias_tables(rpb, rows):
    rpt = TM // GRID_W
    kr = min(WIN_R, rows)
    col = np.arange(GRID_W)
    cidx = np.clip(col[None, :] - col[:, None], -(WIN_C - 1), WIN_C - 1) + (WIN_C - 1)
    c_sel = (cidx[None] == np.arange(2 * WIN_C - 1)[:, None, None]).astype(np.float32)
    c0 = np.clip(col - WIN_C // 2, 0, GRID_W - WIN_C)
    c_ok = (col[None, :] >= c0[:, None]) & (col[None, :] < c0[:, None] + WIN_C)
    toep = jnp.einsum("lhrd,dqk->lhrqk", rpb.astype(F32), c_sel, precision=lax.Precision.HIGHEST)
    toep = jnp.where(c_ok, toep, NEG)
    depth = rpb.shape[0]
    masked = jnp.full((depth, NA_HEADS, GRID_W, GRID_W), NEG, F32)
    tabs = [jnp.full((depth, NA_HEADS, TM, 3 * TM), NEG, F32)]
    for r_base, u_base in ((0, 0), (rpt, 0), (rows - rpt, rows - 3 * rpt)):
        blocks = []
        for i in range(rpt):
            rq = r_base + i
            r0 = min(max(rq - kr // 2, 0), rows - kr)
            row = [toep[:, :, u_base + m - rq + (WIN_R - 1)] if r0 <= u_base + m < r0 + kr else masked
                   for m in range(3 * rpt)]
            blocks.append(jnp.concatenate(row, axis=-1))
        tabs.append(jnp.concatenate(blocks, axis=-2))
    return jnp.stack(tabs, axis=1)


def _na_kernel(q_ref, kp_ref, kc_ref, kn_ref, kx_ref, vp_ref, vc_ref, vn_ref, vx_ref, bias_ref, o_ref):
    lane = lax.broadcasted_iota(jnp.int32, (TM, 2 * NA_HD), 1)
    first = lane < NA_HD
    k_refs = (kp_ref, kc_ref, kn_ref, kx_ref)
    v_refs = (vp_ref, vc_ref, vn_ref, vx_ref)
    for bi in range(q_ref.shape[0]):
        for hp in range(NA_HEADS // 2):
            ls = slice(hp * 2 * NA_HD, (hp + 1) * 2 * NA_HD)
            q2 = q_ref[bi, :, ls]
            ks = [kr[bi, :, ls] for kr in k_refs]
            vs = [vr[bi, :, ls] for vr in v_refs]
            outs = []
            for sub in range(2):
                hd = 2 * hp + sub
                own = first if sub == 0 else jnp.logical_not(first)
                qm = jnp.where(own, q2, jnp.zeros_like(q2))
                sc = []
                for i in range(4):
                    s = _dg(qm, ks[i], _NT)
                    if i < 3:
                        s = s + bias_ref[0, 0, hd, :, i * TM:(i + 1) * TM]
                    sc.append(s)
                m = jnp.max(sc[0], axis=-1, keepdims=True)
                for i in range(1, 4):
                    m = jnp.maximum(m, jnp.max(sc[i], axis=-1, keepdims=True))
                acc = jnp.zeros((TM, 2 * NA_HD), F32)
                for i in range(4):
                    p = jnp.exp((sc[i] - m).astype(BF16))
                    acc = acc + _dot(p, jnp.where(own, vs[i], jnp.ones_like(vs[i])))
                l = acc[:, NA_HD:NA_HD + 1] if sub == 0 else acc[:, 0:1]
                outs.append(acc / l)
            o_ref[bi, :, ls] = jnp.where(first, outs[0], outs[1]).astype(o_ref.dtype)


def _na(q, k, v, bias, layer):
    b, t, _ = q.shape
    nt = t // TM
    qt = lambda j: (0, j, 0)
    ctx = lambda j: (0, 0, 0)

    def near(o):
        return lambda j: (0, jnp.clip(j, 2, nt - 2) + o, 0)

    def kind(j):
        return (layer, jnp.where(j == 0, 0, jnp.where(j == 1, 1, jnp.where(j == nt - 1, 3, 2))), 0, 0, 0)

    blk = lambda f: pl.BlockSpec((b, TM, NA_W), f)
    return pl.pallas_call(
        _na_kernel,
        grid=(nt,),
        in_specs=[blk(qt), blk(near(-1)), blk(near(0)), blk(near(1)), blk(ctx),
                  blk(near(-1)), blk(near(0)), blk(near(1)), blk(ctx),
                  pl.BlockSpec((1, 1, NA_HEADS, TM, 3 * TM), kind)],
        out_specs=blk(qt),
        out_shape=jax.ShapeDtypeStruct((b, t, NA_W), BF16),
        compiler_params=_cparams(("arbitrary",)),
        name="natten",
    )(q, k, k, k, k, v, v, v, v, bias)


def _route(sel, aff):
    rows = lambda a, e: a[e:e + 1, :]
    gscore = []
    for g in range(N_GROUPS):
        a, b, c, d = (rows(sel, EXPERTS_PER_GROUP * g + i) for i in range(EXPERTS_PER_GROUP))
        hi1, lo1 = jnp.maximum(a, b), jnp.minimum(a, b)
        hi2, lo2 = jnp.maximum(c, d), jnp.minimum(c, d)
        gscore.append(jnp.maximum(hi1, hi2) + jnp.maximum(jnp.minimum(hi1, hi2), jnp.maximum(lo1, lo2)))
    gbest = jnp.zeros_like(gscore[0], dtype=jnp.int32)
    gval = gscore[0]
    for g in range(1, N_GROUPS):
        better = gscore[g] > gval
        gbest = jnp.where(better, g, gbest)
        gval = jnp.where(better, gscore[g], gval)
    cs, ca = [], []
    for i in range(EXPERTS_PER_GROUP):
        s_i, a_i = rows(sel, i), rows(aff, i)
        for g in range(1, N_GROUPS):
            pick = gbest == g
            s_i = jnp.where(pick, rows(sel, EXPERTS_PER_GROUP * g + i), s_i)
            a_i = jnp.where(pick, rows(aff, EXPERTS_PER_GROUP * g + i), a_i)
        cs.append(s_i)
        ca.append(a_i)
    i1 = jnp.zeros_like(gbest)
    v1, w1 = cs[0], ca[0]
    for i in range(1, EXPERTS_PER_GROUP):
        better = cs[i] > v1
        i1 = jnp.where(better, i, i1)
        v1 = jnp.where(better, cs[i], v1)
        w1 = jnp.where(better, ca[i], w1)
    i2 = jnp.full_like(gbest, -1)
    v2 = jnp.full_like(v1, -jnp.inf)
    w2 = jnp.zeros_like(w1)
    for i in range(EXPERTS_PER_GROUP):
        better = (i1 != i) & ((cs[i] > v2) | (i2 < 0))
        i2 = jnp.where(better, i, i2)
        v2 = jnp.where(better, cs[i], v2)
        w2 = jnp.where(better, ca[i], w2)
    tot = w1 + w2
    base = gbest * EXPERTS_PER_GROUP
    return base + i1, base + i2, w1 / tot, w2 / tot


def _merge_kernel(gy_ref, ny_ref, m1_ref, m2_ref, x_ref, mod_ref, nw_ref, wbg_ref, wbn_ref, wo_ref,
                  wr_ref, br_ref, xo_ref, hp_ref, ri_ref, rw_ref):
    nb = x_ref.shape[0]
    is_ctx = pl.program_id(0) == 0
    wh, wl = _split(wr_ref[...])
    zi = jnp.zeros((6, TM), jnp.int32)
    for bi in range(nb):
        a = _dot(gy_ref[bi], wbg_ref[0])
        b = _dot(ny_ref[bi], wbn_ref[0])
        m = _sigmoid(m1_ref[bi].astype(F32)) * a + _sigmoid(m2_ref[bi].astype(F32)) * b
        y = _dot(m.astype(BF16), wo_ref[0])
        mod = jnp.where(is_ctx, mod_ref[0, nb], mod_ref[0, bi])
        xn = x_ref[bi] + mod[2:3] * y
        xo_ref[bi] = xn
        h2 = _norm_mod(xn, nw_ref[0], mod[3:4], mod[4:5])
        _rows_to_tiles(hp_ref.at[bi], 0, h2)
        hh, hl = _split(h2)
        logit = _dg(wh, hh, _NT) + _dg(wh, hl, _NT) + _dg(wl, hh, _NT)
        aff = _sigmoid(logit)
        i1, i2, w1, w2 = _route(aff + br_ref[...], aff)
        ri_ref[bi] = jnp.concatenate([i1, i2, zi], axis=0)
        rw_ref[bi] = jnp.concatenate([w1, w2, zi.astype(F32)], axis=0)


def _merge(gy, ny, m1, m2, x, mods, nw2, wbg, wbn, wo, wrt, brt, layer):
    b, t, d = x.shape
    nt = t // TM
    rows = TM * d // LANES
    tile = lambda j: (0, j, 0)
    const = lambda j: (0, 0)
    xo, hp, ri, rw = pl.pallas_call(
        _merge_kernel,
        grid=(nt,),
        in_specs=[pl.BlockSpec((b, TM, GLA_V), tile), pl.BlockSpec((b, TM, NA_W), tile),
                  pl.BlockSpec((b, TM, d), tile), pl.BlockSpec((b, TM, d), tile),
                  pl.BlockSpec((b, TM, d), tile),
                  _layer_spec(mods, layer), _layer_spec(nw2, layer),
                  _layer_spec(wbg, layer), _layer_spec(wbn, layer), _layer_spec(wo, layer),
                  pl.BlockSpec((N_EXPERTS, d), const), pl.BlockSpec((N_EXPERTS, TM), const)],
        out_specs=[pl.BlockSpec((b, TM, d), tile), pl.BlockSpec((b, rows, LANES), tile),
                   pl.BlockSpec((b, 8, TM), lambda j: (0, 0, j)),
                   pl.BlockSpec((b, 8, TM), lambda j: (0, 0, j))],
        out_shape=[jax.ShapeDtypeStruct((b, t, d), F32), jax.ShapeDtypeStruct((b, nt * rows, LANES), F32),
                   jax.ShapeDtypeStruct((b, 8, t), jnp.int32), jax.ShapeDtypeStruct((b, 8, t), F32)],
        compiler_params=_cparams(("arbitrary",)),
        name="merge_router",
    )(gy, ny, m1, m2, x, mods, nw2, wbg, wbn, wo, wrt, brt)
    return xo, hp.reshape(b * nt * rows, LANES), ri, rw


def _moe_kernel(be_ref, nu_ref, src_cur, src_nxt, src_nx2, dst_cur, dst_prv, h_ref, w1_ref, w3_ref, w2_ref,
                o_ref, xbuf, ybuf, w1b, w3b, w2b, gsem, ssem):
    i = pl.program_id(0)
    nu = nu_ref[0]
    xb = lax.rem(i, 3)
    yb = lax.rem(i, 3)
    used = i < nu
    st = w1_ref.shape[2] // LANES
    blk = TM * st
    n_real = o_ref.shape[0] - 2 * blk

    def part(ref, b):
        return ref.at[pl.ds(pl.multiple_of(b * blk, blk), blk)]

    def token(ref, row):
        return ref.at[pl.ds(pl.multiple_of(row, st), st)]

    def start_gather(idx_ref, b):
        for r in range(TM):
            pltpu.make_async_copy(token(h_ref, idx_ref[0, 0, r]), token(xbuf, b * blk + r * st),
                                  gsem.at[b]).start(priority=r % 2)

    def start_scatter(idx_ref, b):
        for r in range(TM):
            pltpu.make_async_copy(token(ybuf, b * blk + r * st), token(o_ref, idx_ref[0, 0, r]),
                                  ssem.at[b]).start(priority=r % 2)

    def wait_gather(b):
        pltpu.make_async_copy(h_ref.at[pl.ds(0, blk)], part(xbuf, b), gsem.at[b]).wait()

    def wait_scatter(b):
        pltpu.make_async_copy(part(ybuf, b), o_ref.at[pl.ds(0, blk)], ssem.at[b]).wait()

    @pl.when(i == 0)
    def _():
        xbuf[...] = jnp.zeros_like(xbuf)
        ybuf[...] = jnp.zeros_like(ybuf)
        for b in range(2):
            spare = pltpu.make_async_copy(part(xbuf, b), o_ref.at[pl.ds(n_real + b * blk, blk)], ssem.at[b])
            spare.start()
            spare.wait()
        start_gather(src_cur, 0)
        start_gather(src_nxt, 1)

    fresh = jnp.logical_or(i == 0, be_ref[i] != be_ref[jnp.maximum(i - 1, 0)])

    @pl.when(jnp.logical_and(used, fresh))
    def _():
        for src, dst in ((w1_ref, w1b), (w3_ref, w3b), (w2_ref, w2b)):
            def body(r, carry, src=src, dst=dst):
                rows = pl.ds(pl.multiple_of(r * LANES, LANES), LANES)
                dst[rows, :] = src[0, 0, rows, :].astype(BF16)
                return carry
            lax.fori_loop(0, src.shape[2] // LANES, body, 0)

    @pl.when(jnp.logical_and(used, i > 0))
    def _():
        wait_scatter(lax.rem(i + 1, 3))

    @pl.when(used)
    def _():
        wait_gather(xb)
        x = _tiles_to_rows(xbuf, xb * blk, TM, st).astype(BF16)
        start_gather(src_nx2, lax.rem(i + 2, 3))
        start_scatter(dst_prv, lax.rem(i + 2, 3))
        a = _dot(x, w1b[...])
        b = _dot(x, w3b[...])
        hmid = (a * _sigmoid(a)) * b
        _rows_to_tiles(ybuf, yb * blk, _dot(hmid.astype(BF16), w2b[...]))

    @pl.when(i == nu - 1)
    def _():
        start_scatter(dst_cur, yb)
        wait_scatter(yb)
        wait_scatter(lax.rem(i + 2, 3))
        wait_gather(lax.rem(i + 1, 3))
        wait_gather(lax.rem(i + 2, 3))


def _moe(blk_expert, n_used, slot_src, slot_dst, h, w1, w3, w2, layer):
    d, f = w1.shape[2], w1.shape[3]
    st = d // LANES
    blk = TM * st
    n_rows = h.shape[0]
    nb = blk_expert.shape[0]
    wsel = lambda i, be, nu: (layer, be[i], 0, 0)
    smem = lambda f_: pl.BlockSpec((1, 1, TM), f_, memory_space=pltpu.SMEM)
    cur = lambda i, be, nu: (i, 0, 0)
    nxt = lambda i, be, nu: (jnp.minimum(i + 1, nb - 1), 0, 0)
    nx2 = lambda i, be, nu: (jnp.minimum(i + 2, nb - 1), 0, 0)
    prv = lambda i, be, nu: (jnp.maximum(i - 1, 0), 0, 0)
    return pl.pallas_call(
        _moe_kernel,
        grid_spec=pltpu.PrefetchScalarGridSpec(
            num_scalar_prefetch=2,
            grid=(nb,),
            in_specs=[smem(cur), smem(nxt), smem(nx2), smem(cur), smem(prv),
                      pl.BlockSpec(memory_space=pl.ANY), pl.BlockSpec((1, 1, d, f), wsel),
                      pl.BlockSpec((1, 1, d, f), wsel), pl.BlockSpec((1, 1, f, d), wsel)],
            out_specs=pl.BlockSpec(memory_space=pl.ANY),
            scratch_shapes=[pltpu.VMEM((3 * blk, LANES), F32), pltpu.VMEM((3 * blk, LANES), F32),
                            pltpu.VMEM((d, f), BF16), pltpu.VMEM((d, f), BF16), pltpu.VMEM((f, d), BF16),
                            pltpu.SemaphoreType.DMA((3,)), pltpu.SemaphoreType.DMA((3,))]),
        out_shape=jax.ShapeDtypeStruct((TOP_K * n_rows + 2 * blk, LANES), F32),
        compiler_params=_cparams(("arbitrary",)),
        name="moe_experts",
    )(blk_expert, n_used, slot_src, slot_src, slot_src, slot_dst, slot_dst, h, w1, w3, w2)


def _combine_kernel(final, *refs):
    y_refs, (x_ref, w_ref, mod_ref), rest = refs[:TOP_K], refs[TOP_K:TOP_K + 3], refs[TOP_K + 3:]
    if final:
        fw_ref, o_ref = rest
    else:
        (o_ref,) = rest
    w = w_ref[0]
    y = None
    for kk in range(TOP_K):
        yk = _tiles_to_rows(y_refs[kk], 0, TM, x_ref.shape[2] // LANES) * w[:, kk:kk + 1]
        y = yk if y is None else y + yk
    xn = x_ref[0] + mod_ref[0, 0][5:6] * y
    if final:
        xn = xn * lax.rsqrt(jnp.mean(xn * xn, axis=-1, keepdims=True) + EPS) * fw_ref[...]
    o_ref[0] = xn


def _combine(y, x, wcol, mods, layer, final_w=None):
    b, t, d = x.shape
    nt = t // TM
    final = final_w is not None
    skip = 1 if final else 0
    tile = lambda i, j: (i, j + skip, 0)

    def plane(kk):
        return pl.BlockSpec((TM * d // LANES, LANES), lambda i, j: ((kk * b + i) * nt + j + skip, 0))

    in_specs = [plane(kk) for kk in range(TOP_K)] + [
        pl.BlockSpec((1, TM, d), tile), pl.BlockSpec((1, TM, TOP_K), tile), _mod_spec(mods, layer, b, skip)]
    args = [y] * TOP_K + [x, wcol, mods]
    if final:
        in_specs.append(pl.BlockSpec((1, d), lambda i, j: (0, 0)))
        args.append(final_w)
    return pl.pallas_call(
        functools.partial(_combine_kernel, final),
        grid=(b, nt - skip),
        in_specs=in_specs,
        out_specs=pl.BlockSpec((1, TM, d), lambda i, j: (i, j, 0)),
        out_shape=jax.ShapeDtypeStruct((b, t - skip * TM, d), F32),
        compiler_params=_cparams(("arbitrary", "arbitrary")),
        name="moe_combine_final" if final else "moe_combine",
    )(*args)


def _invert_kernel(dest_ref, spare_ref, out_ref):
    pltpu.sync_copy(spare_ref, out_ref)

    def place(p, carry):
        out_ref[dest_ref[p]] = p
        return carry

    lax.fori_loop(0, dest_ref.shape[0], place, 0, unroll=16)


def _invert_slots(dest, n_slots):
    spare = dest.shape[0] + np.arange(n_slots, dtype=np.int32) % (2 * TM)
    return pl.pallas_call(
        _invert_kernel,
        in_specs=[pl.BlockSpec(memory_space=pltpu.SMEM), pl.BlockSpec(memory_space=pl.ANY)],
        out_specs=pl.BlockSpec(memory_space=pltpu.SMEM),
        out_shape=jax.ShapeDtypeStruct((n_slots,), jnp.int32),
        name="slot_invert",
    )(dest, jnp.asarray(spare))


def _slot_plan(ridx, n_tok, st):
    e = jnp.transpose(ridx[:, :TOP_K, :], (1, 0, 2)).reshape(-1)
    onehot = (e[:, None] == jnp.arange(N_EXPERTS, dtype=jnp.int32)[None, :]).astype(jnp.int32)
    incl = jnp.cumsum(onehot, axis=0)
    rank = jnp.sum((incl - onehot) * onehot, axis=1)
    counts = incl[-1]
    padded = (counts + TM - 1) // TM * TM
    pends = jnp.cumsum(padded)
    pstarts = pends - padded
    dest = pstarts[e] + rank
    n_blocks = (n_tok * TOP_K + N_EXPERTS * (TM - 1) + TM - 1) // TM
    blk_start = jnp.arange(n_blocks, dtype=jnp.int32) * TM
    blk_expert = jnp.minimum(jnp.sum((pends[None, :] <= blk_start[:, None]).astype(jnp.int32), axis=1),
                             N_EXPERTS - 1)
    n_used = (pends[-1] // TM).astype(jnp.int32).reshape(1)
    slot_dst = _invert_slots(dest.astype(jnp.int32), n_blocks * TM)
    slot_src = jnp.where(slot_dst < n_tok * TOP_K, slot_dst % n_tok, 0)
    shape3 = (n_blocks, 1, TM)
    return blk_expert.astype(jnp.int32), n_used, (slot_src * st).reshape(shape3), (slot_dst * st).reshape(shape3)


def _rope_tables(t_len, ctx_len):
    half = GLA_DK // 2
    inv = ROPE_BASE ** (-np.arange(0, half, 2, dtype=np.float32) / half)
    lane = np.arange(GLA_DK)
    p = np.arange(t_len - ctx_len)
    pos = np.where(lane[None, :] < half, (p // GRID_W)[:, None], (p % GRID_W)[:, None]).astype(np.float32)
    ang = jnp.asarray(pos) * jnp.asarray(inv[(lane % half) % (half // 2)])[None, :]
    cos, sin = jnp.cos(ang), jnp.sin(ang)
    lower = jnp.asarray((lane % half) < half // 2)[None, :]
    pad = lambda a, v: jnp.concatenate([jnp.full((ctx_len, GLA_DK), v, F32), a], axis=0)
    return pad(cos, 1.0), pad(jnp.where(lower, -sin, 0.0), 0.0), pad(jnp.where(lower, 0.0, sin), 0.0)


def kernel(x, c, ctx, c_ctx, w_ada, b_ada, norm1_w, norm2_w, w_in, w_gk_up, b_gk, gla_norm_w, rpb,
           w_bo_gla, w_bo_na, w_out, w_router, b_router, w1, w3, w2, final_norm_w):
    bsz, seq, d = x.shape
    ctx_len = ctx.shape[1]
    depth = w_ada.shape[0]
    assert ctx_len == TM and seq % TM == 0 and TM % GRID_W == 0 and seq // GRID_W >= 3 * (TM // GRID_W)
    t = ctx_len + seq
    nt = t // TM
    n_tok = bsz * t

    xs = jnp.concatenate([ctx, x], axis=1)
    cvec = jnp.concatenate([c, c_ctx[None], jnp.zeros((8 - bsz - 1, d), F32)], axis=0)
    mods = _modulation(cvec, w_ada, b_ada).reshape(depth, 8, 6, d)
    cos, sa, sb = _rope_tables(t, ctx_len)
    wrt = w_router.T.astype(F32)
    brt = jnp.broadcast_to(b_router.astype(F32)[:, None], (N_EXPERTS, TM))
    gq_end = 2 * GLA_QK + 2 * GLA_V
    wp = (w_in[:, :, :gq_end].astype(BF16), w_in[:, :, gq_end + 2 * GLA_LR:].astype(BF16),
          jnp.pad(w_in[:, :, gq_end:gq_end + 2 * GLA_LR], ((0, 0), (0, 0), (0, LR_PAD - 2 * GLA_LR))).astype(BF16))
    wup = jnp.stack([jnp.pad(w_gk_up[:, dirn], ((0, 0), (dirn * GLA_LR, LR_PAD - (dirn + 1) * GLA_LR), (0, 0)))
                     for dirn in range(2)], axis=1).astype(BF16)
    bg = b_gk.reshape(depth, 2, 1, GLA_QK)
    nw1, nw2 = norm1_w.reshape(depth, 1, d), norm2_w.reshape(depth, 1, d)
    gnw = gla_norm_w.reshape(depth, 1, GLA_DV)
    wbg, wbn, wo = w_bo_gla.astype(BF16), w_bo_na.astype(BF16), w_out.astype(BF16)
    bias = _na_bias_tables(rpb, seq // GRID_W)

    out = None
    for l in range(depth):
        gq, gk, gv, gg, nq, nk, nv, m1, m2, lr = _inproj(xs, mods, nw1, wp, cos, sa, sb, l)
        ob = _gla_pass(True, gq, gk, gv, lr, wup, bg, l)
        gy = _gla_pass(False, gq, gk, gv, lr, wup, bg, l, extra=(ob, gg, gnw))
        ny = _na(nq, nk, nv, bias, l)
        xs, hp, ridx, rw = _merge(gy, ny, m1, m2, xs, mods, nw2, wbg, wbn, wo, wrt, brt, l)
        wcol = jnp.transpose(rw[:, :TOP_K, :], (0, 2, 1))
        blk_expert, n_used, slot_src, slot_dst = _slot_plan(ridx, n_tok, d // LANES)
        y = _moe(blk_expert, n_used, slot_src, slot_dst, hp, w1, w3, w2, l)
        if l < depth - 1:
            xs = _combine(y, xs, wcol, mods, l)
        else:
            out = _combine(y, xs, wcol, mods, l, final_norm_w[None])
    return out
```

```python
import functools

import jax
import jax.numpy as jnp
import numpy as np
from jax import lax
from jax.experimental import pallas as pl
from jax.experimental.pallas import tpu as pltpu

F32 = jnp.float32
BF16 = jnp.bfloat16

EPS = 1e-6
GRID_W = 64
GLA_HEADS = 4
GLA_DK = 128
GLA_DV = 256
GLA_LR = 16
GLA_TAU = 16.0
GLA_CHUNK = 64
ROPE_BASE = 10000.0
NA_HEADS = 8
NA_HD = 64
WIN_R = 8
WIN_C = 16
N_EXPERTS = 16
N_GROUPS = 4
EXPERTS_PER_GROUP = N_EXPERTS // N_GROUPS
TOP_K = 2

GLA_QK = GLA_HEADS * GLA_DK
GLA_V = GLA_HEADS * GLA_DV
NA_W = NA_HEADS * NA_HD

TM = 256
LANES = 128
LR_PAD = LANES
NEG = -1e30
VMEM_LIMIT = 56 * 1024 * 1024

_NT = (((1,), (1,)), ((), ()))
_TN = (((0,), (0,)), ((), ()))


def _cparams(sem):
    return pltpu.CompilerParams(dimension_semantics=sem, vmem_limit_bytes=VMEM_LIMIT)


def _dot(a, b):
    return jnp.dot(a, b, preferred_element_type=F32)


def _dg(a, b, dims):
    return lax.dot_general(a, b, dims, preferred_element_type=F32)


def _split(a):
    hi = a.astype(BF16)
    lo = (a - hi.astype(F32)).astype(BF16)
    return hi, lo


def _sigmoid(x):
    return 1.0 / (1.0 + jnp.exp(-x))


def _norm_mod(x, w, shift, scale):
    y = x * lax.rsqrt(jnp.mean(x * x, axis=-1, keepdims=True) + EPS)
    return (y * w) * (1.0 + scale) + shift


def _rows_to_tiles(ref, base, val):
    st = val.shape[1] // LANES
    for s in range(st):
        ref[pl.ds(base + s, val.shape[0], stride=st), :] = val[:, s * LANES:(s + 1) * LANES]


def _tiles_to_rows(ref, base, n, st):
    return jnp.concatenate([ref[pl.ds(base + s, n, stride=st), :] for s in range(st)], axis=1)


def _mod_kernel(c_ref, w_ref, b_ref, o_ref):
    c = c_ref[...]
    s = c * _sigmoid(c)
    sh, sl = _split(s)
    wh, wl = _split(w_ref[0])
    o_ref[0] = _dot(sh, wh) + _dot(sl, wh) + _dot(sh, wl) + b_ref[0]


def _modulation(cvec, w_ada, b_ada):
    depth, d, n = w_ada.shape
    tn = 1024
    return pl.pallas_call(
        _mod_kernel,
        grid=(depth, n // tn),
        in_specs=[pl.BlockSpec((8, d), lambda l, j: (0, 0)),
                  pl.BlockSpec((1, d, tn), lambda l, j: (l, 0, j)),
                  pl.BlockSpec((1, 1, tn), lambda l, j: (l, 0, j))],
        out_specs=pl.BlockSpec((1, 8, tn), lambda l, j: (l, 0, j)),
        out_shape=jax.ShapeDtypeStruct((depth, 8, n), F32),
        compiler_params=_cparams(("arbitrary", "arbitrary")),
        name="modulation",
    )(cvec, w_ada, b_ada.reshape(depth, 1, n))


_IN_OUT = (("gq", GLA_QK, BF16), ("gk", GLA_QK, BF16), ("gv", GLA_V, BF16), ("gg", GLA_V, BF16),
           ("nq", NA_W, BF16), ("nk", NA_W, BF16), ("nv", NA_W, BF16),
           ("m1", None, BF16), ("m2", None, BF16), ("lr", LR_PAD, F32))


def _moe_residual(x, y_refs, w, mod):
    y = None
    for kk, y_ref in enumerate(y_refs):
        yk = _tiles_to_rows(y_ref, 0, x.shape[0], x.shape[1] // LANES) * w[:, kk:kk + 1]
        y = yk if y is None else y + yk
    return x + mod[5:6] * y


_IN_SPLIT = (4, 9, 10)


def _inproj_kernel(fused, *refs):
    if fused:
        y_refs, (w_ref, pmod_ref), refs = refs[:TOP_K], refs[TOP_K:TOP_K + 2], refs[TOP_K + 2:]
    x_ref, mod_ref, nw_ref, wa_ref, wb_ref, wc_ref, cos_ref, sa_ref, sb_ref = refs[:9]
    o_refs = refs[9:]
    d = x_ref.shape[2]
    x = x_ref[0]
    if fused:
        xo_ref, o_refs = o_refs[0], o_refs[1:]
        x = _moe_residual(x, y_refs, w_ref[0], pmod_ref[0, 0])
        xo_ref[0] = x
    mod = mod_ref[0, 0]
    h = _norm_mod(x, nw_ref[0], mod[0:1], mod[1:2]).astype(BF16)
    cos, sa, sb = cos_ref[...], sa_ref[...], sb_ref[...]

    def rope(r, scale):
        parts = []
        for hh in range(GLA_HEADS):
            xs = r[:, hh * GLA_DK:(hh + 1) * GLA_DK]
            y = xs * cos + pltpu.roll(xs, GLA_DK - 32, 1) * sa + pltpu.roll(xs, 32, 1) * sb
            parts.append(y * scale if scale != 1.0 else y)
        return jnp.concatenate(parts, axis=1)

    off = 0
    for gi, ((name, width, dt), o_ref) in enumerate(zip(_IN_OUT, o_refs)):
        width = d if width is None else width
        if gi in _IN_SPLIT:
            off = 0
        w_ref = wa_ref if gi < _IN_SPLIT[0] else (wb_ref if gi < _IN_SPLIT[1] else wc_ref)
        r = _dot(h, w_ref[0, :, off:off + width])
        if name == "gq":
            r = rope(r, GLA_DK ** -0.5)
        elif name == "gk":
            r = rope(r, 1.0)
        elif name == "nq":
            r = r * (NA_HD ** -0.5)
        o_ref[0] = r.astype(dt)
        off += width


def _layer_spec(arr, layer, *lead):
    tail = arr.shape[1 + len(lead):]
    index = (layer,) + lead + (0,) * len(tail)
    return pl.BlockSpec((1,) * (1 + len(lead)) + tail, lambda *_: index)


def _mod_spec(mods, layer, ctx_row, skip=0):
    return pl.BlockSpec((1, 1) + mods.shape[2:], lambda i, j: (layer, jnp.where(j + skip == 0, ctx_row, i), 0, 0))


def _plane_specs(b, nt, d, skip):
    def plane(kk):
        return pl.BlockSpec((TM * d // LANES, LANES), lambda i, j: ((kk * b + i) * nt + j + skip, 0))
    return [plane(kk) for kk in range(TOP_K)]


def _inproj(x, mods, nw, wp, cos, sa, sb, layer, moe=None):
    b, t, d = x.shape
    nt = t // TM
    widths = [d if w is None else w for _, w, _ in _IN_OUT]
    tile = lambda i, j: (i, j, 0)
    in_specs = [pl.BlockSpec((1, TM, d), tile),
                _mod_spec(mods, layer, b), _layer_spec(nw, layer)] + [_layer_spec(w, layer) for w in wp] + [
                pl.BlockSpec((TM, GLA_DK), lambda i, j: (j, 0)),
                pl.BlockSpec((TM, GLA_DK), lambda i, j: (j, 0)),
                pl.BlockSpec((TM, GLA_DK), lambda i, j: (j, 0))]
    args = [x, mods, nw, *wp, cos, sa, sb]
    out_specs = [pl.BlockSpec((1, TM, w), tile) for w in widths]
    out_shape = [jax.ShapeDtypeStruct((b, t, w), dt) for w, (_, _, dt) in zip(widths, _IN_OUT)]
    if moe is not None:
        y, wcol = moe
        in_specs = _plane_specs(b, nt, d, 0) + [pl.BlockSpec((1, TM, TOP_K), tile),
                                                _mod_spec(mods, layer - 1, b)] + in_specs
        args = [y] * TOP_K + [wcol, mods] + args
        out_specs = [pl.BlockSpec((1, TM, d), tile)] + out_specs
        out_shape = [jax.ShapeDtypeStruct((b, t, d), F32)] + out_shape
    return pl.pallas_call(
        functools.partial(_inproj_kernel, moe is not None),
        grid=(b, nt),
        in_specs=in_specs,
        out_specs=out_specs,
        out_shape=out_shape,
        compiler_params=_cparams(("arbitrary", "arbitrary")),
        name="inproj",
    )(*args)


def _gla_kernel(reverse, finish, q_ref, k_ref, v_ref, lr_ref, wup_ref, bg_ref, *rest):
    if finish:
        ob_ref, g_ref, nw_ref, o_ref, st_ref = rest
    else:
        o_ref, st_ref = rest
    nchunk = TM // GLA_CHUNK
    nb = q_ref.shape[0]

    @pl.when(pl.program_id(0) == 0)
    def _():
        st_ref[...] = jnp.zeros_like(st_ref)

    r = lax.broadcasted_iota(jnp.int32, (TM, TM), 0)
    s = lax.broadcasted_iota(jnp.int32, (TM, TM), 1)
    order = (s >= r) if reverse else (s <= r)
    tri = jnp.where(((r // GLA_CHUNK) == (s // GLA_CHUNK)) & order, 1.0, 0.0).astype(BF16)
    cmask = order[:GLA_CHUNK, :GLA_CHUNK]
    bcs = []
    for bi in range(nb):
        z = _dot(lr_ref[bi].astype(BF16), wup_ref[0, 0]) + bg_ref[0, 0]
        logg = (jnp.minimum(z, 0.0) - jnp.log(1.0 + jnp.exp(-jnp.abs(z)))) * (1.0 / GLA_TAU)
        hi, lo = _split(logg)
        bcs.append(_dot(tri, hi) + _dot(tri, lo))

    for c in (range(nchunk - 1, -1, -1) if reverse else range(nchunk)):
        rs = slice(c * GLA_CHUNK, (c + 1) * GLA_CHUNK)
        for bi in range(nb):
            for hh in range(GLA_HEADS):
                ls = slice(hh * GLA_DK, (hh + 1) * GLA_DK)
                vs = slice(hh * GLA_DV, (hh + 1) * GLA_DV)
                b = bcs[bi][rs, ls]
                bend = b[0:1] if reverse else b[GLA_CHUNK - 1:GLA_CHUNK]
                qc = q_ref[bi, rs, ls].astype(F32)
                kc = k_ref[bi, rs, ls].astype(F32)
                vc = v_ref[bi, rs, vs]
                kd = (kc * jnp.exp(bend - b)).astype(BF16)
                qi = (qc * jnp.exp(b)).astype(BF16)
                qa = (qc * jnp.exp(b - bend)).astype(BF16)
                att = jnp.where(cmask, _dg(qa, kd, _NT), 0.0).astype(BF16)
                st = st_ref[bi * GLA_HEADS + hh]
                o = _dg(qi, st.astype(BF16), _NT) + _dot(att, vc)
                st_ref[bi * GLA_HEADS + hh] = st * jnp.exp(bend) + _dg(vc, kd, _TN)
                if finish:
                    o = o + ob_ref[bi, rs, vs].astype(F32)
                    o = o * lax.rsqrt(jnp.mean(o * o, axis=-1, keepdims=True) + EPS) * nw_ref[0]
                    g = g_ref[bi, rs, vs].astype(F32)
                    o = o * (g * _sigmoid(g))
                o_ref[bi, rs, vs] = o.astype(o_ref.dtype)


def _gla_pass(reverse, q, k, v, lr, wup, bg, layer, extra=None):
    b, t, _ = q.shape
    nt = t // TM
    dirn = 1 if reverse else 0
    if reverse:
        tile = lambda j: (0, jnp.where(j == 0, 0, nt - j), 0)
    else:
        tile = lambda j: (0, j, 0)
    in_specs = [pl.BlockSpec((b, TM, GLA_QK), tile), pl.BlockSpec((b, TM, GLA_QK), tile),
                pl.BlockSpec((b, TM, GLA_V), tile), pl.BlockSpec((b, TM, LR_PAD), tile),
                _layer_spec(wup, layer, dirn), _layer_spec(bg, layer, dirn)]
    args = [q, k, v, lr, wup, bg]
    if extra is not None:
        ob, g, nw = extra
        in_specs += [pl.BlockSpec((b, TM, GLA_V), tile), pl.BlockSpec((b, TM, GLA_V), tile),
                     _layer_spec(nw, layer)]
        args += [ob, g, nw]
    return pl.pallas_call(
        functools.partial(_gla_kernel, reverse, extra is not None),
        grid=(nt,),
        in_specs=in_specs,
        out_specs=pl.BlockSpec((b, TM, GLA_V), tile),
        out_shape=jax.ShapeDtypeStruct((b, t, GLA_V), BF16),
        scratch_shapes=[pltpu.VMEM((b * GLA_HEADS, GLA_DV, GLA_DK), F32)],
        compiler_params=_cparams(("arbitrary",)),
        name="gla_bwd" if reverse else "gla_fwd",
    )(*args)


def _na_bias_tables(rpb, rows):
    rpt = TM // GRID_W
    kr = min(WIN_R, rows)
    col = np.arange(GRID_W)
    cidx = np.clip(col[None, :] - col[:, None], -(WIN_C - 1), WIN_C - 1) + (WIN_C - 1)
    c_sel = (cidx[None] == np.arange(2 * WIN_C - 1)[:, None, None]).astype(np.float32)
    c0 = np.clip(col - WIN_C // 2, 0, GRID_W - WIN_C)
    c_ok = (col[None, :] >= c0[:, None]) & (col[None, :] < c0[:, None] + WIN_C)
    toep = jnp.einsum("lhrd,dqk->lhrqk", rpb.astype(F32), c_sel, precision=lax.Precision.HIGHEST)
    toep = jnp.where(c_ok, toep, NEG)
    depth = rpb.shape[0]
    masked = jnp.full((depth, NA_HEADS, GRID_W, GRID_W), NEG, F32)
    tabs = [jnp.full((depth, NA_HEADS, TM, 3 * TM), NEG, F32)]
    for r_base, u_base in ((0, 0), (rpt, 0), (rows - rpt, rows - 3 * rpt)):
        blocks = []
        for i in range(rpt):
            rq = r_base + i
            r0 = min(max(rq - kr // 2, 0), rows - kr)
            row = [toep[:, :, u_base + m - rq + (WIN_R - 1)] if r0 <= u_base + m < r0 + kr else masked
                   for m in range(3 * rpt)]
            blocks.append(jnp.concatenate(row, axis=-1))
        tabs.append(jnp.concatenate(blocks, axis=-2))
    return jnp.stack(tabs, axis=1)


def _na_kernel(q_ref, kp_ref, kc_ref, kn_ref, kx_ref, vp_ref, vc_ref, vn_ref, vx_ref, bias_ref, o_ref):
    lane = lax.broadcasted_iota(jnp.int32, (TM, 2 * NA_HD), 1)
    first = lane < NA_HD
    k_refs = (kp_ref, kc_ref, kn_ref, kx_ref)
    v_refs = (vp_ref, vc_ref, vn_ref, vx_ref)
    for bi in range(q_ref.shape[0]):
        for hp in range(NA_HEADS // 2):
            ls = slice(hp * 2 * NA_HD, (hp + 1) * 2 * NA_HD)
            q2 = q_ref[bi, :, ls]
            ks = [kr[bi, :, ls] for kr in k_refs]
            vs = [vr[bi, :, ls] for vr in v_refs]
            outs = []
            for sub in range(2):
                hd = 2 * hp + sub
                own = first if sub == 0 else jnp.logical_not(first)
                qm = jnp.where(own, q2, jnp.zeros_like(q2))
                sc = []
                for i in range(4):
                    s = _dg(qm, ks[i], _NT)
                    if i < 3:
                        s = s + bias_ref[0, 0, hd, :, i * TM:(i + 1) * TM]
                    sc.append(s)
                m = jnp.max(sc[0], axis=-1, keepdims=True)
                for i in range(1, 4):
                    m = jnp.maximum(m, jnp.max(sc[i], axis=-1, keepdims=True))
                acc = jnp.zeros((TM, 2 * NA_HD), F32)
                for i in range(4):
                    p = jnp.exp((sc[i] - m).astype(BF16))
                    acc = acc + _dot(p, jnp.where(own, vs[i], jnp.ones_like(vs[i])))
                l = acc[:, NA_HD:NA_HD + 1] if sub == 0 else acc[:, 0:1]
                outs.append(acc / l)
            o_ref[bi, :, ls] = jnp.where(first, outs[0], outs[1]).astype(o_ref.dtype)


def _na(q, k, v, bias, layer):
    b, t, _ = q.shape
    nt = t // TM
    qt = lambda j: (0, j, 0)
    ctx = lambda j: (0, 0, 0)

    def near(o):
        return lambda j: (0, jnp.clip(j, 2, nt - 2) + o, 0)

    def kind(j):
        return (layer, jnp.where(j == 0, 0, jnp.where(j == 1, 1, jnp.where(j == nt - 1, 3, 2))), 0, 0, 0)

    blk = lambda f: pl.BlockSpec((b, TM, NA_W), f)
    return pl.pallas_call(
        _na_kernel,
        grid=(nt,),
        in_specs=[blk(qt), blk(near(-1)), blk(near(0)), blk(near(1)), blk(ctx),
                  blk(near(-1)), blk(near(0)), blk(near(1)), blk(ctx),
                  pl.BlockSpec((1, 1, NA_HEADS, TM, 3 * TM), kind)],
        out_specs=blk(qt),
        out_shape=jax.ShapeDtypeStruct((b, t, NA_W), BF16),
        compiler_params=_cparams(("arbitrary",)),
        name="natten",
    )(q, k, k, k, k, v, v, v, v, bias)


def _route(sel, aff):
    rows = lambda a, e: a[e:e + 1, :]
    gscore = []
    for g in range(N_GROUPS):
        a, b, c, d = (rows(sel, EXPERTS_PER_GROUP * g + i) for i in range(EXPERTS_PER_GROUP))
        hi1, lo1 = jnp.maximum(a, b), jnp.minimum(a, b)
        hi2, lo2 = jnp.maximum(c, d), jnp.minimum(c, d)
        gscore.append(jnp.maximum(hi1, hi2) + jnp.maximum(jnp.minimum(hi1, hi2), jnp.maximum(lo1, lo2)))
    gbest = jnp.zeros_like(gscore[0], dtype=jnp.int32)
    gval = gscore[0]
    for g in range(1, N_GROUPS):
        better = gscore[g] > gval
        gbest = jnp.where(better, g, gbest)
        gval = jnp.where(better, gscore[g], gval)
    cs, ca = [], []
    for i in range(EXPERTS_PER_GROUP):
        s_i, a_i = rows(sel, i), rows(aff, i)
        for g in range(1, N_GROUPS):
            pick = gbest == g
            s_i = jnp.where(pick, rows(sel, EXPERTS_PER_GROUP * g + i), s_i)
            a_i = jnp.where(pick, rows(aff, EXPERTS_PER_GROUP * g + i), a_i)
        cs.append(s_i)
        ca.append(a_i)
    i1 = jnp.zeros_like(gbest)
    v1, w1 = cs[0], ca[0]
    for i in range(1, EXPERTS_PER_GROUP):
        better = cs[i] > v1
        i1 = jnp.where(better, i, i1)
        v1 = jnp.where(better, cs[i], v1)
        w1 = jnp.where(better, ca[i], w1)
    i2 = jnp.full_like(gbest, -1)
    v2 = jnp.full_like(v1, -jnp.inf)
    w2 = jnp.zeros_like(w1)
    for i in range(EXPERTS_PER_GROUP):
        better = (i1 != i) & ((cs[i] > v2) | (i2 < 0))
        i2 = jnp.where(better, i, i2)
        v2 = jnp.where(better, cs[i], v2)
        w2 = jnp.where(better, ca[i], w2)
    tot = w1 + w2
    base = gbest * EXPERTS_PER_GROUP
    return base + i1, base + i2, w1 / tot, w2 / tot


def _merge_kernel(gy_ref, ny_ref, m1_ref, m2_ref, x_ref, mod_ref, nw_ref, wbg_ref, wbn_ref, wo_ref,
                  wr_ref, br_ref, xo_ref, hp_ref, ri_ref, rw_ref):
    nb = x_ref.shape[0]
    is_ctx = pl.program_id(0) == 0
    wh, wl = _split(wr_ref[...])
    zi = jnp.zeros((6, TM), jnp.int32)
    for bi in range(nb):
        a = _dot(gy_ref[bi], wbg_ref[0])
        b = _dot(ny_ref[bi], wbn_ref[0])
        m = _sigmoid(m1_ref[bi].astype(F32)) * a + _sigmoid(m2_ref[bi].astype(F32)) * b
        y = _dot(m.astype(BF16), wo_ref[0])
        mod = jnp.where(is_ctx, mod_ref[0, nb], mod_ref[0, bi])
        xn = x_ref[bi] + mod[2:3] * y
        xo_ref[bi] = xn
        h2 = _norm_mod(xn, nw_ref[0], mod[3:4], mod[4:5])
        _rows_to_tiles(hp_ref.at[bi], 0, h2)
        hh, hl = _split(h2)
        logit = _dg(wh, hh, _NT) + _dg(wh, hl, _NT) + _dg(wl, hh, _NT)
        aff = _sigmoid(logit)
        i1, i2, w1, w2 = _route(aff + br_ref[...], aff)
        ri_ref[bi] = jnp.concatenate([i1, i2, zi], axis=0)
        rw_ref[bi] = jnp.concatenate([w1, w2, zi.astype(F32)], axis=0)


def _merge(gy, ny, m1, m2, x, mods, nw2, wbg, wbn, wo, wrt, brt, layer):
    b, t, d = x.shape
    nt = t // TM
    rows = TM * d // LANES
    tile = lambda j: (0, j, 0)
    const = lambda j: (0, 0)
    xo, hp, ri, rw = pl.pallas_call(
        _merge_kernel,
        grid=(nt,),
        in_specs=[pl.BlockSpec((b, TM, GLA_V), tile), pl.BlockSpec((b, TM, NA_W), tile),
                  pl.BlockSpec((b, TM, d), tile), pl.BlockSpec((b, TM, d), tile),
                  pl.BlockSpec((b, TM, d), tile),
                  _layer_spec(mods, layer), _layer_spec(nw2, layer),
                  _layer_spec(wbg, layer), _layer_spec(wbn, layer), _layer_spec(wo, layer),
                  pl.BlockSpec((N_EXPERTS, d), const), pl.BlockSpec((N_EXPERTS, TM), const)],
        out_specs=[pl.BlockSpec((b, TM, d), tile), pl.BlockSpec((b, rows, LANES), tile),
                   pl.BlockSpec((b, 8, TM), lambda j: (0, 0, j)),
                   pl.BlockSpec((b, 8, TM), lambda j: (0, 0, j))],
        out_shape=[jax.ShapeDtypeStruct((b, t, d), F32), jax.ShapeDtypeStruct((b, nt * rows, LANES), F32),
                   jax.ShapeDtypeStruct((b, 8, t), jnp.int32), jax.ShapeDtypeStruct((b, 8, t), F32)],
        compiler_params=_cparams(("arbitrary",)),
        name="merge_router",
    )(gy, ny, m1, m2, x, mods, nw2, wbg, wbn, wo, wrt, brt)
    return xo, hp.reshape(b * nt * rows, LANES), ri, rw


def _moe_kernel(be_ref, nu_ref, src_cur, src_nxt, src_nx2, dst_cur, dst_prv, h_ref, w1_ref, w3_ref, w2_ref,
                o_ref, xbuf, ybuf, w1b, w3b, w2b, gsem, ssem):
    i = pl.program_id(0)
    nu = nu_ref[0]
    xb = lax.rem(i, 3)
    yb = lax.rem(i, 3)
    used = i < nu
    st = w1_ref.shape[2] // LANES
    blk = TM * st
    n_real = o_ref.shape[0] - 2 * blk

    def part(ref, b):
        return ref.at[pl.ds(pl.multiple_of(b * blk, blk), blk)]

    def token(ref, row):
        return ref.at[pl.ds(pl.multiple_of(row, st), st)]

    def start_gather(idx_ref, b):
        for r in range(TM):
            pltpu.make_async_copy(token(h_ref, idx_ref[0, 0, r]), token(xbuf, b * blk + r * st),
                                  gsem.at[b]).start(priority=r % 2)

    def start_scatter(idx_ref, b):
        for r in range(TM):
            pltpu.make_async_copy(token(ybuf, b * blk + r * st), token(o_ref, idx_ref[0, 0, r]),
                                  ssem.at[b]).start(priority=r % 2)

    def wait_gather(b):
        pltpu.make_async_copy(h_ref.at[pl.ds(0, blk)], part(xbuf, b), gsem.at[b]).wait()

    def wait_scatter(b):
        pltpu.make_async_copy(part(ybuf, b), o_ref.at[pl.ds(0, blk)], ssem.at[b]).wait()

    @pl.when(i == 0)
    def _():
        xbuf[...] = jnp.zeros_like(xbuf)
        ybuf[...] = jnp.zeros_like(ybuf)
        for b in range(2):
            spare = pltpu.make_async_copy(part(xbuf, b), o_ref.at[pl.ds(n_real + b * blk, blk)], ssem.at[b])
            spare.start()
            spare.wait()
        start_gather(src_cur, 0)
        start_gather(src_nxt, 1)

    fresh = jnp.logical_or(i == 0, be_ref[i] != be_ref[jnp.maximum(i - 1, 0)])

    @pl.when(jnp.logical_and(used, fresh))
    def _():
        for src, dst in ((w1_ref, w1b), (w3_ref, w3b), (w2_ref, w2b)):
            def body(r, carry, src=src, dst=dst):
                rows = pl.ds(pl.multiple_of(r * LANES, LANES), LANES)
                dst[rows, :] = src[0, 0, rows, :].astype(BF16)
                return carry
            lax.fori_loop(0, src.shape[2] // LANES, body, 0)

    @pl.when(jnp.logical_and(used, i > 0))
    def _():
        wait_scatter(lax.rem(i + 1, 3))

    @pl.when(used)
    def _():
        wait_gather(xb)
        x = _tiles_to_rows(xbuf, xb * blk, TM, st).astype(BF16)
        start_gather(src_nx2, lax.rem(i + 2, 3))
        start_scatter(dst_prv, lax.rem(i + 2, 3))
        a = _dot(x, w1b[...])
        b = _dot(x, w3b[...])
        hmid = (a * _sigmoid(a)) * b
        _rows_to_tiles(ybuf, yb * blk, _dot(hmid.astype(BF16), w2b[...]))

    @pl.when(i == nu - 1)
    def _():
        start_scatter(dst_cur, yb)
        wait_scatter(yb)
        wait_scatter(lax.rem(i + 2, 3))
        wait_gather(lax.rem(i + 1, 3))
        wait_gather(lax.rem(i + 2, 3))


def _moe(blk_expert, n_used, slot_src, slot_dst, h, w1, w3, w2, layer):
    d, f = w1.shape[2], w1.shape[3]
    st = d // LANES
    blk = TM * st
    n_rows = h.shape[0]
    nb = blk_expert.shape[0]
    wsel = lambda i, be, nu: (layer, be[i], 0, 0)
    smem = lambda f_: pl.BlockSpec((1, 1, TM), f_, memory_space=pltpu.SMEM)
    cur = lambda i, be, nu: (i, 0, 0)
    nxt = lambda i, be, nu: (jnp.minimum(i + 1, nb - 1), 0, 0)
    nx2 = lambda i, be, nu: (jnp.minimum(i + 2, nb - 1), 0, 0)
    prv = lambda i, be, nu: (jnp.maximum(i - 1, 0), 0, 0)
    return pl.pallas_call(
        _moe_kernel,
        grid_spec=pltpu.PrefetchScalarGridSpec(
            num_scalar_prefetch=2,
            grid=(nb,),
            in_specs=[smem(cur), smem(nxt), smem(nx2), smem(cur), smem(prv),
                      pl.BlockSpec(memory_space=pl.ANY), pl.BlockSpec((1, 1, d, f), wsel),
                      pl.BlockSpec((1, 1, d, f), wsel), pl.BlockSpec((1, 1, f, d), wsel)],
            out_specs=pl.BlockSpec(memory_space=pl.ANY),
            scratch_shapes=[pltpu.VMEM((3 * blk, LANES), F32), pltpu.VMEM((3 * blk, LANES), F32),
                            pltpu.VMEM((d, f), BF16), pltpu.VMEM((d, f), BF16), pltpu.VMEM((f, d), BF16),
                            pltpu.SemaphoreType.DMA((3,)), pltpu.SemaphoreType.DMA((3,))]),
        out_shape=jax.ShapeDtypeStruct((TOP_K * n_rows + 2 * blk, LANES), F32),
        compiler_params=_cparams(("arbitrary",)),
        name="moe_experts",
    )(blk_expert, n_used, slot_src, slot_src, slot_src, slot_dst, slot_dst, h, w1, w3, w2)


def _final_kernel(*refs):
    y_refs, (x_ref, w_ref, mod_ref, fw_ref, o_ref) = refs[:TOP_K], refs[TOP_K:]
    xn = _moe_residual(x_ref[0], y_refs, w_ref[0], mod_ref[0, 0])
    o_ref[0] = xn * lax.rsqrt(jnp.mean(xn * xn, axis=-1, keepdims=True) + EPS) * fw_ref[...]


def _final(y, x, wcol, mods, layer, final_w):
    b, t, d = x.shape
    nt = t // TM
    tile = lambda i, j: (i, j + 1, 0)
    return pl.pallas_call(
        _final_kernel,
        grid=(b, nt - 1),
        in_specs=_plane_specs(b, nt, d, 1) + [
            pl.BlockSpec((1, TM, d), tile), pl.BlockSpec((1, TM, TOP_K), tile), _mod_spec(mods, layer, b, 1),
            pl.BlockSpec((1, d), lambda i, j: (0, 0))],
        out_specs=pl.BlockSpec((1, TM, d), lambda i, j: (i, j, 0)),
        out_shape=jax.ShapeDtypeStruct((b, t - TM, d), F32),
        compiler_params=_cparams(("arbitrary", "arbitrary")),
        name="moe_combine_final",
    )(*([y] * TOP_K), x, wcol, mods, final_w)


def _invert_kernel(dest_ref, spare_ref, out_ref):
    pltpu.sync_copy(spare_ref, out_ref)

    def place(p, carry):
        out_ref[dest_ref[p]] = p
        return carry

    lax.fori_loop(0, dest_ref.shape[0], place, 0, unroll=16)


def _invert_slots(dest, n_slots):
    spare = dest.shape[0] + np.arange(n_slots, dtype=np.int32) % (2 * TM)
    return pl.pallas_call(
        _invert_kernel,
        in_specs=[pl.BlockSpec(memory_space=pltpu.SMEM), pl.BlockSpec(memory_space=pl.ANY)],
        out_specs=pl.BlockSpec(memory_space=pltpu.SMEM),
        out_shape=jax.ShapeDtypeStruct((n_slots,), jnp.int32),
        name="slot_invert",
    )(dest, jnp.asarray(spare))


def _slot_plan(ridx, n_tok, st):
    e = jnp.transpose(ridx[:, :TOP_K, :], (1, 0, 2)).reshape(-1)
    onehot = (e[:, None] == jnp.arange(N_EXPERTS, dtype=jnp.int32)[None, :]).astype(jnp.int32)
    incl = jnp.cumsum(onehot, axis=0)
    rank = jnp.sum((incl - onehot) * onehot, axis=1)
    counts = incl[-1]
    padded = (counts + TM - 1) // TM * TM
    pends = jnp.cumsum(padded)
    pstarts = pends - padded
    dest = pstarts[e] + rank
    n_blocks = (n_tok * TOP_K + N_EXPERTS * (TM - 1) + TM - 1) // TM
    blk_start = jnp.arange(n_blocks, dtype=jnp.int32) * TM
    blk_expert = jnp.minimum(jnp.sum((pends[None, :] <= blk_start[:, None]).astype(jnp.int32), axis=1),
                             N_EXPERTS - 1)
    n_used = (pends[-1] // TM).astype(jnp.int32).reshape(1)
    slot_dst = _invert_slots(dest.astype(jnp.int32), n_blocks * TM)
    slot_src = jnp.where(slot_dst < n_tok * TOP_K, slot_dst % n_tok, 0)
    shape3 = (n_blocks, 1, TM)
    return blk_expert.astype(jnp.int32), n_used, (slot_src * st).reshape(shape3), (slot_dst * st).reshape(shape3)


def _rope_tables(t_len, ctx_len):
    half = GLA_DK // 2
    inv = ROPE_BASE ** (-np.arange(0, half, 2, dtype=np.float32) / half)
    lane = np.arange(GLA_DK)
    p = np.arange(t_len - ctx_len)
    pos = np.where(lane[None, :] < half, (p // GRID_W)[:, None], (p % GRID_W)[:, None]).astype(np.float32)
    ang = jnp.asarray(pos) * jnp.asarray(inv[(lane % half) % (half // 2)])[None, :]
    cos, sin = jnp.cos(ang), jnp.sin(ang)
    lower = jnp.asarray((lane % half) < half // 2)[None, :]
    pad = lambda a, v: jnp.concatenate([jnp.full((ctx_len, GLA_DK), v, F32), a], axis=0)
    return pad(cos, 1.0), pad(jnp.where(lower, -sin, 0.0), 0.0), pad(jnp.where(lower, 0.0, sin), 0.0)


def kernel(x, c, ctx, c_ctx, w_ada, b_ada, norm1_w, norm2_w, w_in, w_gk_up, b_gk, gla_norm_w, rpb,
           w_bo_gla, w_bo_na, w_out, w_router, b_router, w1, w3, w2, final_norm_w):
    bsz, seq, d = x.shape
    ctx_len = ctx.shape[1]
    depth = w_ada.shape[0]
    assert ctx_len == TM and seq % TM == 0 and TM % GRID_W == 0 and seq // GRID_W >= 3 * (TM // GRID_W)
    t = ctx_len + seq
    nt = t // TM
    n_tok = bsz * t

    xs = jnp.concatenate([ctx, x], axis=1)
    cvec = jnp.concatenate([c, c_ctx[None], jnp.zeros((8 - bsz - 1, d), F32)], axis=0)
    mods = _modulation(cvec, w_ada, b_ada).reshape(depth, 8, 6, d)
    cos, sa, sb = _rope_tables(t, ctx_len)
    wrt = w_router.T.astype(F32)
    brt = jnp.broadcast_to(b_router.astype(F32)[:, None], (N_EXPERTS, TM))
    gq_end = 2 * GLA_QK + 2 * GLA_V
    wp = (w_in[:, :, :gq_end].astype(BF16), w_in[:, :, gq_end + 2 * GLA_LR:].astype(BF16),
          jnp.pad(w_in[:, :, gq_end:gq_end + 2 * GLA_LR], ((0, 0), (0, 0), (0, LR_PAD - 2 * GLA_LR))).astype(BF16))
    wup = jnp.stack([jnp.pad(w_gk_up[:, dirn], ((0, 0), (dirn * GLA_LR, LR_PAD - (dirn + 1) * GLA_LR), (0, 0)))
                     for dirn in range(2)], axis=1).astype(BF16)
    bg = b_gk.reshape(depth, 2, 1, GLA_QK)
    nw1, nw2 = norm1_w.reshape(depth, 1, d), norm2_w.reshape(depth, 1, d)
    gnw = gla_norm_w.reshape(depth, 1, GLA_DV)
    wbg, wbn, wo = w_bo_gla.astype(BF16), w_bo_na.astype(BF16), w_out.astype(BF16)
    bias = _na_bias_tables(rpb, seq // GRID_W)

    moe = None
    for l in range(depth):
        outs = _inproj(xs, mods, nw1, wp, cos, sa, sb, l, moe)
        if moe is not None:
            xs, outs = outs[0], outs[1:]
        gq, gk, gv, gg, nq, nk, nv, m1, m2, lr = outs
        ob = _gla_pass(True, gq, gk, gv, lr, wup, bg, l)
        gy = _gla_pass(False, gq, gk, gv, lr, wup, bg, l, extra=(ob, gg, gnw))
        ny = _na(nq, nk, nv, bias, l)
        xs, hp, ridx, rw = _merge(gy, ny, m1, m2, xs, mods, nw2, wbg, wbn, wo, wrt, brt, l)
        wcol = jnp.transpose(rw[:, :TOP_K, :], (0, 2, 1))
        blk_expert, n_used, slot_src, slot_dst = _slot_plan(ridx, n_tok, d // LANES)
        y = _moe(blk_expert, n_used, slot_src, slot_dst, hp, w1, w3, w2, l)
        moe = (y, wcol)
    return _final(moe[0], xs, moe[1], mods, depth - 1, final_norm_w[None])
```

```python
import functools

import jax
import jax.numpy as jnp
import numpy as np
from jax import lax
from jax.experimental import pallas as pl
from jax.experimental.pallas import tpu as pltpu

F32 = jnp.float32
BF16 = jnp.bfloat16

EPS = 1e-6
GRID_W = 64
GLA_HEADS = 4
GLA_DK = 128
GLA_DV = 256
GLA_LR = 16
GLA_TAU = 16.0
GLA_CHUNK = 64
ROPE_BASE = 10000.0
NA_HEADS = 8
NA_HD = 64
WIN_R = 8
WIN_C = 16
N_EXPERTS = 16
N_GROUPS = 4
EXPERTS_PER_GROUP = N_EXPERTS // N_GROUPS
TOP_K = 2

GLA_QK = GLA_HEADS * GLA_DK
GLA_V = GLA_HEADS * GLA_DV
NA_W = NA_HEADS * NA_HD

TM = 256
LANES = 128
LR_PAD = LANES
NEG = -1e30
VMEM_LIMIT = 56 * 1024 * 1024

_NT = (((1,), (1,)), ((), ()))
_TN = (((0,), (0,)), ((), ()))


def _cparams(sem):
    return pltpu.CompilerParams(dimension_semantics=sem, vmem_limit_bytes=VMEM_LIMIT)


def _dot(a, b):
    return jnp.dot(a, b, preferred_element_type=F32)


def _dg(a, b, dims):
    return lax.dot_general(a, b, dims, preferred_element_type=F32)


def _split(a):
    hi = a.astype(BF16)
    lo = (a - hi.astype(F32)).astype(BF16)
    return hi, lo


def _sigmoid(x):
    return 1.0 / (1.0 + jnp.exp(-x))


def _norm_mod(x, w, shift, scale):
    y = x * lax.rsqrt(jnp.mean(x * x, axis=-1, keepdims=True) + EPS)
    return (y * w) * (1.0 + scale) + shift


def _rows_to_tiles(ref, base, val):
    st = val.shape[1] // LANES
    for s in range(st):
        ref[pl.ds(base + s, val.shape[0], stride=st), :] = val[:, s * LANES:(s + 1) * LANES]


def _tiles_to_rows(ref, base, n, st):
    return jnp.concatenate([ref[pl.ds(base + s, n, stride=st), :] for s in range(st)], axis=1)


def _mod_kernel(c_ref, w_ref, b_ref, o_ref):
    c = c_ref[...]
    s = c * _sigmoid(c)
    sh, sl = _split(s)
    wh, wl = _split(w_ref[0])
    o_ref[0] = _dot(sh, wh) + _dot(sl, wh) + _dot(sh, wl) + b_ref[0]


def _modulation(cvec, w_ada, b_ada):
    depth, d, n = w_ada.shape
    tn = 1024
    return pl.pallas_call(
        _mod_kernel,
        grid=(depth, n // tn),
        in_specs=[pl.BlockSpec((8, d), lambda l, j: (0, 0)),
                  pl.BlockSpec((1, d, tn), lambda l, j: (l, 0, j)),
                  pl.BlockSpec((1, 1, tn), lambda l, j: (l, 0, j))],
        out_specs=pl.BlockSpec((1, 8, tn), lambda l, j: (l, 0, j)),
        out_shape=jax.ShapeDtypeStruct((depth, 8, n), F32),
        compiler_params=_cparams(("arbitrary", "arbitrary")),
        name="modulation",
    )(cvec, w_ada, b_ada.reshape(depth, 1, n))


_IN_OUT = (("gq", GLA_QK, BF16), ("gk", GLA_QK, BF16), ("gv", GLA_V, BF16), ("gg", GLA_V, BF16),
           ("nq", NA_W, BF16), ("nk", NA_W, BF16), ("nv", NA_W, BF16),
           ("m1", None, BF16), ("m2", None, BF16), ("lr", LR_PAD, F32))


def _moe_residual(x, y_refs, w, mod):
    y = None
    for kk, y_ref in enumerate(y_refs):
        yk = _tiles_to_rows(y_ref, 0, x.shape[0], x.shape[1] // LANES) * w[:, kk:kk + 1]
        y = yk if y is None else y + yk
    return x + mod[5:6] * y


_IN_SPLIT = (4, 9, 10)


def _inproj_kernel(fused, nb, *refs):
    if fused:
        n_planes = nb * TOP_K
        y_refs, (w_ref, pmod_ref), refs = refs[:n_planes], refs[n_planes:n_planes + 2], refs[n_planes + 2:]
    x_ref, mod_ref, nw_ref, wa_ref, wb_ref, wc_ref, cos_ref, sa_ref, sb_ref = refs[:9]
    o_refs = refs[9:]
    d = x_ref.shape[2]
    is_ctx = pl.program_id(0) == 0
    if fused:
        xo_ref, o_refs = o_refs[0], o_refs[1:]
    hs = []
    for bi in range(nb):
        x = x_ref[bi]
        if fused:
            pmod = jnp.where(is_ctx, pmod_ref[0, nb], pmod_ref[0, bi])
            x = _moe_residual(x, y_refs[bi * TOP_K:(bi + 1) * TOP_K], w_ref[bi], pmod)
            xo_ref[bi] = x
        mod = jnp.where(is_ctx, mod_ref[0, nb], mod_ref[0, bi])
        hs.append(_norm_mod(x, nw_ref[0], mod[0:1], mod[1:2]).astype(BF16))
    h = jnp.concatenate(hs, axis=0)
    cos, sa, sb = cos_ref[...], sa_ref[...], sb_ref[...]

    def rope(r, scale):
        parts = []
        for hh in range(GLA_HEADS):
            xs = r[:, hh * GLA_DK:(hh + 1) * GLA_DK]
            y = xs * cos + pltpu.roll(xs, GLA_DK - 32, 1) * sa + pltpu.roll(xs, 32, 1) * sb
            parts.append(y * scale if scale != 1.0 else y)
        return jnp.concatenate(parts, axis=1)

    off = 0
    for gi, ((name, width, dt), o_ref) in enumerate(zip(_IN_OUT, o_refs)):
        width = d if width is None else width
        if gi in _IN_SPLIT:
            off = 0
        w_ref = wa_ref if gi < _IN_SPLIT[0] else (wb_ref if gi < _IN_SPLIT[1] else wc_ref)
        rr = _dot(h, w_ref[0, :, off:off + width])
        for bi in range(nb):
            r = rr[bi * TM:(bi + 1) * TM]
            if name == "gq":
                r = rope(r, GLA_DK ** -0.5)
            elif name == "gk":
                r = rope(r, 1.0)
            elif name == "nq":
                r = r * (NA_HD ** -0.5)
            o_ref[bi] = r.astype(dt)
        off += width


def _layer_spec(arr, layer, *lead, single=False):
    tail = arr.shape[1 + len(lead):]
    index = (layer,) + lead + (0,) * len(tail)
    mode = dict(pipeline_mode=pl.Buffered(1)) if single else {}
    return pl.BlockSpec((1,) * (1 + len(lead)) + tail, lambda *_: index, **mode)


def _mod_spec(mods, layer, ctx_row, skip=0):
    return pl.BlockSpec((1, 1) + mods.shape[2:], lambda i, j: (layer, jnp.where(j + skip == 0, ctx_row, i), 0, 0))


def _plane_specs(b, nt, d, skip):
    def plane(kk):
        return pl.BlockSpec((TM * d // LANES, LANES), lambda i, j: ((kk * b + i) * nt + j + skip, 0))
    return [plane(kk) for kk in range(TOP_K)]


def _inproj(x, mods, nw, wp, cos, sa, sb, layer, moe=None):
    b, t, d = x.shape
    nt = t // TM
    widths = [d if w is None else w for _, w, _ in _IN_OUT]
    tile = lambda j: (0, j, 0)
    table = pl.BlockSpec((TM, GLA_DK), lambda j: (j, 0))
    in_specs = [pl.BlockSpec((b, TM, d), tile), _layer_spec(mods, layer), _layer_spec(nw, layer)] + [
        _layer_spec(w, layer, single=True) for w in wp] + [table, table, table]
    args = [x, mods, nw, *wp, cos, sa, sb]
    out_specs = [pl.BlockSpec((b, TM, w), tile) for w in widths]
    out_shape = [jax.ShapeDtypeStruct((b, t, w), dt) for w, (_, _, dt) in zip(widths, _IN_OUT)]
    if moe is not None:
        y, wcol = moe
        planes = [pl.BlockSpec((TM * d // LANES, LANES), lambda j, r=(kk * b + bi) * nt: (r + j, 0))
                  for bi in range(b) for kk in range(TOP_K)]
        in_specs = planes + [pl.BlockSpec((b, TM, TOP_K), tile), _layer_spec(mods, layer - 1)] + in_specs
        args = [y] * len(planes) + [wcol, mods] + args
        out_specs = [pl.BlockSpec((b, TM, d), tile)] + out_specs
        out_shape = [jax.ShapeDtypeStruct((b, t, d), F32)] + out_shape
    return pl.pallas_call(
        functools.partial(_inproj_kernel, moe is not None, b),
        grid=(nt,),
        in_specs=in_specs,
        out_specs=out_specs,
        out_shape=out_shape,
        compiler_params=_cparams(("arbitrary",)),
        name="inproj",
    )(*args)


def _gla_kernel(reverse, finish, q_ref, k_ref, v_ref, lr_ref, wup_ref, bg_ref, *rest):
    if finish:
        ob_ref, g_ref, nw_ref, o_ref, st_ref = rest
    else:
        o_ref, st_ref = rest
    nchunk = TM // GLA_CHUNK
    nb = q_ref.shape[0]

    @pl.when(pl.program_id(0) == 0)
    def _():
        st_ref[...] = jnp.zeros_like(st_ref)

    r = lax.broadcasted_iota(jnp.int32, (TM, TM), 0)
    s = lax.broadcasted_iota(jnp.int32, (TM, TM), 1)
    order = (s >= r) if reverse else (s <= r)
    tri = jnp.where(((r // GLA_CHUNK) == (s // GLA_CHUNK)) & order, 1.0, 0.0).astype(BF16)
    cmask = order[:GLA_CHUNK, :GLA_CHUNK]
    bcs = []
    for bi in range(nb):
        z = _dot(lr_ref[bi].astype(BF16), wup_ref[0, 0]) + bg_ref[0, 0]
        logg = (jnp.minimum(z, 0.0) - jnp.log(1.0 + jnp.exp(-jnp.abs(z)))) * (1.0 / GLA_TAU)
        hi, lo = _split(logg)
        bcs.append(_dot(tri, hi) + _dot(tri, lo))

    for c in (range(nchunk - 1, -1, -1) if reverse else range(nchunk)):
        rs = slice(c * GLA_CHUNK, (c + 1) * GLA_CHUNK)
        for bi in range(nb):
            for hh in range(GLA_HEADS):
                ls = slice(hh * GLA_DK, (hh + 1) * GLA_DK)
                vs = slice(hh * GLA_DV, (hh + 1) * GLA_DV)
                b = bcs[bi][rs, ls]
                bend = b[0:1] if reverse else b[GLA_CHUNK - 1:GLA_CHUNK]
                qc = q_ref[bi, rs, ls].astype(F32)
                kc = k_ref[bi, rs, ls].astype(F32)
                vc = v_ref[bi, rs, vs]
                kd = (kc * jnp.exp(bend - b)).astype(BF16)
                qi = (qc * jnp.exp(b)).astype(BF16)
                qa = (qc * jnp.exp(b - bend)).astype(BF16)
                att = jnp.where(cmask, _dg(qa, kd, _NT), 0.0).astype(BF16)
                st = st_ref[bi * GLA_HEADS + hh]
                o = _dg(qi, st.astype(BF16), _NT) + _dot(att, vc)
                st_ref[bi * GLA_HEADS + hh] = st * jnp.exp(bend) + _dg(vc, kd, _TN)
                if finish:
                    o = o + ob_ref[bi, rs, vs].astype(F32)
                    o = o * lax.rsqrt(jnp.mean(o * o, axis=-1, keepdims=True) + EPS) * nw_ref[0]
                    g = g_ref[bi, rs, vs].astype(F32)
                    o = o * (g * _sigmoid(g))
                o_ref[bi, rs, vs] = o.astype(o_ref.dtype)


def _gla_pass(reverse, q, k, v, lr, wup, bg, layer, extra=None):
    b, t, _ = q.shape
    nt = t // TM
    dirn = 1 if reverse else 0
    if reverse:
        tile = lambda j: (0, jnp.where(j == 0, 0, nt - j), 0)
    else:
        tile = lambda j: (0, j, 0)
    in_specs = [pl.BlockSpec((b, TM, GLA_QK), tile), pl.BlockSpec((b, TM, GLA_QK), tile),
                pl.BlockSpec((b, TM, GLA_V), tile), pl.BlockSpec((b, TM, LR_PAD), tile),
                _layer_spec(wup, layer, dirn), _layer_spec(bg, layer, dirn)]
    args = [q, k, v, lr, wup, bg]
    if extra is not None:
        ob, g, nw = extra
        in_specs += [pl.BlockSpec((b, TM, GLA_V), tile), pl.BlockSpec((b, TM, GLA_V), tile),
                     _layer_spec(nw, layer)]
        args += [ob, g, nw]
    return pl.pallas_call(
        functools.partial(_gla_kernel, reverse, extra is not None),
        grid=(nt,),
        in_specs=in_specs,
        out_specs=pl.BlockSpec((b, TM, GLA_V), tile),
        out_shape=jax.ShapeDtypeStruct((b, t, GLA_V), BF16),
        scratch_shapes=[pltpu.VMEM((b * GLA_HEADS, GLA_DV, GLA_DK), F32)],
        compiler_params=_cparams(("arbitrary",)),
        name="gla_bwd" if reverse else "gla_fwd",
    )(*args)


def _na_bias_tables(rpb, rows):
    rpt = TM // GRID_W
    kr = min(WIN_R, rows)
    col = np.arange(GRID_W)
    cidx = np.clip(col[None, :] - col[:, None], -(WIN_C - 1), WIN_C - 1) + (WIN_C - 1)
    c_sel = (cidx[None] == np.arange(2 * WIN_C - 1)[:, None, None]).astype(np.float32)
    c0 = np.clip(col - WIN_C // 2, 0, GRID_W - WIN_C)
    c_ok = (col[None, :] >= c0[:, None]) & (col[None, :] < c0[:, None] + WIN_C)
    toep = jnp.einsum("lhrd,dqk->lhrqk", rpb.astype(F32), c_sel, precision=lax.Precision.HIGHEST)
    toep = jnp.where(c_ok, toep, NEG)
    depth = rpb.shape[0]
    masked = jnp.full((depth, NA_HEADS, GRID_W, GRID_W), NEG, F32)
    tabs = [jnp.full((depth, NA_HEADS, TM, 3 * TM), NEG, F32)]
    for r_base, u_base in ((0, 0), (rpt, 0), (rows - rpt, rows - 3 * rpt)):
        blocks = []
        for i in range(rpt):
            rq = r_base + i
            r0 = min(max(rq - kr // 2, 0), rows - kr)
            row = [toep[:, :, u_base + m - rq + (WIN_R - 1)] if r0 <= u_base + m < r0 + kr else masked
                   for m in range(3 * rpt)]
            blocks.append(jnp.concatenate(row, axis=-1))
        tabs.append(jnp.concatenate(blocks, axis=-2))
    return jnp.stack(tabs, axis=1)


def _na_kernel(q_ref, kp_ref, kc_ref, kn_ref, kx_ref, vp_ref, vc_ref, vn_ref, vx_ref, bias_ref, o_ref):
    lane = lax.broadcasted_iota(jnp.int32, (TM, 2 * NA_HD), 1)
    first = lane < NA_HD
    k_refs = (kp_ref, kc_ref, kn_ref, kx_ref)
    v_refs = (vp_ref, vc_ref, vn_ref, vx_ref)
    for bi in range(q_ref.shape[0]):
        for hp in range(NA_HEADS // 2):
            ls = slice(hp * 2 * NA_HD, (hp + 1) * 2 * NA_HD)
            q2 = q_ref[bi, :, ls]
            ks = [kr[bi, :, ls] for kr in k_refs]
            vs = [vr[bi, :, ls] for vr in v_refs]
            outs = []
            for sub in range(2):
                hd = 2 * hp + sub
                own = first if sub == 0 else jnp.logical_not(first)
                qm = jnp.where(own, q2, jnp.zeros_like(q2))
                sc = []
                for i in range(4):
                    s = _dg(qm, ks[i], _NT)
                    if i < 3:
                        s = s + bias_ref[0, 0, hd, :, i * TM:(i + 1) * TM]
                    sc.append(s)
                m = jnp.max(sc[0], axis=-1, keepdims=True)
                for i in range(1, 4):
                    m = jnp.maximum(m, jnp.max(sc[i], axis=-1, keepdims=True))
                acc = jnp.zeros((TM, 2 * NA_HD), F32)
                for i in range(4):
                    p = jnp.exp((sc[i] - m).astype(BF16))
                    acc = acc + _dot(p, jnp.where(own, vs[i], jnp.ones_like(vs[i])))
                l = acc[:, NA_HD:NA_HD + 1] if sub == 0 else acc[:, 0:1]
                outs.append(acc / l)
            o_ref[bi, :, ls] = jnp.where(first, outs[0], outs[1]).astype(o_ref.dtype)


def _na(q, k, v, bias, layer):
    b, t, _ = q.shape
    nt = t // TM
    qt = lambda j: (0, j, 0)
    ctx = lambda j: (0, 0, 0)

    def near(o):
        return lambda j: (0, jnp.clip(j, 2, nt - 2) + o, 0)

    def kind(j):
        return (layer, jnp.where(j == 0, 0, jnp.where(j == 1, 1, jnp.where(j == nt - 1, 3, 2))), 0, 0, 0)

    blk = lambda f: pl.BlockSpec((b, TM, NA_W), f)
    return pl.pallas_call(
        _na_kernel,
        grid=(nt,),
        in_specs=[blk(qt), blk(near(-1)), blk(near(0)), blk(near(1)), blk(ctx),
                  blk(near(-1)), blk(near(0)), blk(near(1)), blk(ctx),
                  pl.BlockSpec((1, 1, NA_HEADS, TM, 3 * TM), kind)],
        out_specs=blk(qt),
        out_shape=jax.ShapeDtypeStruct((b, t, NA_W), BF16),
        compiler_params=_cparams(("arbitrary",)),
        name="natten",
    )(q, k, k, k, k, v, v, v, v, bias)


def _route(sel, aff):
    rows = lambda a, e: a[e:e + 1, :]
    gscore = []
    for g in range(N_GROUPS):
        a, b, c, d = (rows(sel, EXPERTS_PER_GROUP * g + i) for i in range(EXPERTS_PER_GROUP))
        hi1, lo1 = jnp.maximum(a, b), jnp.minimum(a, b)
        hi2, lo2 = jnp.maximum(c, d), jnp.minimum(c, d)
        gscore.append(jnp.maximum(hi1, hi2) + jnp.maximum(jnp.minimum(hi1, hi2), jnp.maximum(lo1, lo2)))
    gbest = jnp.zeros_like(gscore[0], dtype=jnp.int32)
    gval = gscore[0]
    for g in range(1, N_GROUPS):
        better = gscore[g] > gval
        gbest = jnp.where(better, g, gbest)
        gval = jnp.where(better, gscore[g], gval)
    cs, ca = [], []
    for i in range(EXPERTS_PER_GROUP):
        s_i, a_i = rows(sel, i), rows(aff, i)
        for g in range(1, N_GROUPS):
            pick = gbest == g
            s_i = jnp.where(pick, rows(sel, EXPERTS_PER_GROUP * g + i), s_i)
            a_i = jnp.where(pick, rows(aff, EXPERTS_PER_GROUP * g + i), a_i)
        cs.append(s_i)
        ca.append(a_i)
    i1 = jnp.zeros_like(gbest)
    v1, w1 = cs[0], ca[0]
    for i in range(1, EXPERTS_PER_GROUP):
        better = cs[i] > v1
        i1 = jnp.where(better, i, i1)
        v1 = jnp.where(better, cs[i], v1)
        w1 = jnp.where(better, ca[i], w1)
    i2 = jnp.full_like(gbest, -1)
    v2 = jnp.full_like(v1, -jnp.inf)
    w2 = jnp.zeros_like(w1)
    for i in range(EXPERTS_PER_GROUP):
        better = (i1 != i) & ((cs[i] > v2) | (i2 < 0))
        i2 = jnp.where(better, i, i2)
        v2 = jnp.where(better, cs[i], v2)
        w2 = jnp.where(better, ca[i], w2)
    tot = w1 + w2
    base = gbest * EXPERTS_PER_GROUP
    return base + i1, base + i2, w1 / tot, w2 / tot


def _merge_kernel(gy_ref, ny_ref, m1_ref, m2_ref, x_ref, mod_ref, nw_ref, wbg_ref, wbn_ref, wo_ref,
                  wr_ref, br_ref, xo_ref, hp_ref, ri_ref, rw_ref):
    nb = x_ref.shape[0]
    is_ctx = pl.program_id(0) == 0
    wh, wl = _split(wr_ref[...])
    zi = jnp.zeros((6, TM), jnp.int32)
    for bi in range(nb):
        a = _dot(gy_ref[bi], wbg_ref[0])
        b = _dot(ny_ref[bi], wbn_ref[0])
        m = _sigmoid(m1_ref[bi].astype(F32)) * a + _sigmoid(m2_ref[bi].astype(F32)) * b
        y = _dot(m.astype(BF16), wo_ref[0])
        mod = jnp.where(is_ctx, mod_ref[0, nb], mod_ref[0, bi])
        xn = x_ref[bi] + mod[2:3] * y
        xo_ref[bi] = xn
        h2 = _norm_mod(xn, nw_ref[0], mod[3:4], mod[4:5])
        _rows_to_tiles(hp_ref.at[bi], 0, h2)
        hh, hl = _split(h2)
        logit = _dg(wh, hh, _NT) + _dg(wh, hl, _NT) + _dg(wl, hh, _NT)
        aff = _sigmoid(logit)
        i1, i2, w1, w2 = _route(aff + br_ref[...], aff)
        ri_ref[bi] = jnp.concatenate([i1, i2, zi], axis=0)
        rw_ref[bi] = jnp.concatenate([w1, w2, zi.astype(F32)], axis=0)


def _merge(gy, ny, m1, m2, x, mods, nw2, wbg, wbn, wo, wrt, brt, layer):
    b, t, d = x.shape
    nt = t // TM
    rows = TM * d // LANES
    tile = lambda j: (0, j, 0)
    const = lambda j: (0, 0)
    xo, hp, ri, rw = pl.pallas_call(
        _merge_kernel,
        grid=(nt,),
        in_specs=[pl.BlockSpec((b, TM, GLA_V), tile), pl.BlockSpec((b, TM, NA_W), tile),
                  pl.BlockSpec((b, TM, d), tile), pl.BlockSpec((b, TM, d), tile),
                  pl.BlockSpec((b, TM, d), tile),
                  _layer_spec(mods, layer), _layer_spec(nw2, layer),
                  _layer_spec(wbg, layer), _layer_spec(wbn, layer), _layer_spec(wo, layer),
                  pl.BlockSpec((N_EXPERTS, d), const), pl.BlockSpec((N_EXPERTS, TM), const)],
        out_specs=[pl.BlockSpec((b, TM, d), tile), pl.BlockSpec((b, rows, LANES), tile),
                   pl.BlockSpec((b, 8, TM), lambda j: (0, 0, j)),
                   pl.BlockSpec((b, 8, TM), lambda j: (0, 0, j))],
        out_shape=[jax.ShapeDtypeStruct((b, t, d), F32), jax.ShapeDtypeStruct((b, nt * rows, LANES), F32),
                   jax.ShapeDtypeStruct((b, 8, t), jnp.int32), jax.ShapeDtypeStruct((b, 8, t), F32)],
        compiler_params=_cparams(("arbitrary",)),
        name="merge_router",
    )(gy, ny, m1, m2, x, mods, nw2, wbg, wbn, wo, wrt, brt)
    return xo, hp.reshape(b * nt * rows, LANES), ri, rw


def _moe_kernel(be_ref, nu_ref, src_cur, src_nxt, src_nx2, dst_cur, dst_prv, h_ref, w1_ref, w3_ref, w2_ref,
                o_ref, xbuf, ybuf, w1b, w3b, w2b, gsem, ssem):
    i = pl.program_id(0)
    nu = nu_ref[0]
    xb = lax.rem(i, 3)
    yb = lax.rem(i, 3)
    used = i < nu
    st = w1_ref.shape[2] // LANES
    blk = TM * st
    n_real = o_ref.shape[0] - 2 * blk

    def part(ref, b):
        return ref.at[pl.ds(pl.multiple_of(b * blk, blk), blk)]

    def token(ref, row):
        return ref.at[pl.ds(pl.multiple_of(row, st), st)]

    def start_gather(idx_ref, b):
        for r in range(TM):
            pltpu.make_async_copy(token(h_ref, idx_ref[0, 0, r]), token(xbuf, b * blk + r * st),
                                  gsem.at[b]).start(priority=r % 2)

    def start_scatter(idx_ref, b):
        for r in range(TM):
            pltpu.make_async_copy(token(ybuf, b * blk + r * st), token(o_ref, idx_ref[0, 0, r]),
                                  ssem.at[b]).start(priority=r % 2)

    def wait_gather(b):
        pltpu.make_async_copy(h_ref.at[pl.ds(0, blk)], part(xbuf, b), gsem.at[b]).wait()

    def wait_scatter(b):
        pltpu.make_async_copy(part(ybuf, b), o_ref.at[pl.ds(0, blk)], ssem.at[b]).wait()

    @pl.when(i == 0)
    def _():
        xbuf[...] = jnp.zeros_like(xbuf)
        ybuf[...] = jnp.zeros_like(ybuf)
        for b in range(2):
            spare = pltpu.make_async_copy(part(xbuf, b), o_ref.at[pl.ds(n_real + b * blk, blk)], ssem.at[b])
            spare.start()
            spare.wait()
        start_gather(src_cur, 0)
        start_gather(src_nxt, 1)

    fresh = jnp.logical_or(i == 0, be_ref[i] != be_ref[jnp.maximum(i - 1, 0)])

    @pl.when(jnp.logical_and(used, fresh))
    def _():
        for src, dst in ((w1_ref, w1b), (w3_ref, w3b), (w2_ref, w2b)):
            def body(r, carry, src=src, dst=dst):
                rows = pl.ds(pl.multiple_of(r * LANES, LANES), LANES)
                dst[rows, :] = src[0, 0, rows, :].astype(BF16)
                return carry
            lax.fori_loop(0, src.shape[2] // LANES, body, 0)

    @pl.when(jnp.logical_and(used, i > 0))
    def _():
        wait_scatter(lax.rem(i + 1, 3))

    @pl.when(used)
    def _():
        wait_gather(xb)
        x = _tiles_to_rows(xbuf, xb * blk, TM, st).astype(BF16)
        start_gather(src_nx2, lax.rem(i + 2, 3))
        start_scatter(dst_prv, lax.rem(i + 2, 3))
        a = _dot(x, w1b[...])
        b = _dot(x, w3b[...])
        hmid = (a * _sigmoid(a)) * b
        _rows_to_tiles(ybuf, yb * blk, _dot(hmid.astype(BF16), w2b[...]))

    @pl.when(i == nu - 1)
    def _():
        start_scatter(dst_cur, yb)
        wait_scatter(yb)
        wait_scatter(lax.rem(i + 2, 3))
        wait_gather(lax.rem(i + 1, 3))
        wait_gather(lax.rem(i + 2, 3))


def _moe(blk_expert, n_used, slot_src, slot_dst, h, w1, w3, w2, layer):
    d, f = w1.shape[2], w1.shape[3]
    st = d // LANES
    blk = TM * st
    n_rows = h.shape[0]
    nb = blk_expert.shape[0]
    wsel = lambda i, be, nu: (layer, be[i], 0, 0)
    smem = lambda f_: pl.BlockSpec((1, 1, TM), f_, memory_space=pltpu.SMEM)
    cur = lambda i, be, nu: (i, 0, 0)
    nxt = lambda i, be, nu: (jnp.minimum(i + 1, nb - 1), 0, 0)
    nx2 = lambda i, be, nu: (jnp.minimum(i + 2, nb - 1), 0, 0)
    prv = lambda i, be, nu: (jnp.maximum(i - 1, 0), 0, 0)
    return pl.pallas_call(
        _moe_kernel,
        grid_spec=pltpu.PrefetchScalarGridSpec(
            num_scalar_prefetch=2,
            grid=(nb,),
            in_specs=[smem(cur), smem(nxt), smem(nx2), smem(cur), smem(prv),
                      pl.BlockSpec(memory_space=pl.ANY), pl.BlockSpec((1, 1, d, f), wsel),
                      pl.BlockSpec((1, 1, d, f), wsel), pl.BlockSpec((1, 1, f, d), wsel)],
            out_specs=pl.BlockSpec(memory_space=pl.ANY),
            scratch_shapes=[pltpu.VMEM((3 * blk, LANES), F32), pltpu.VMEM((3 * blk, LANES), F32),
                            pltpu.VMEM((d, f), BF16), pltpu.VMEM((d, f), BF16), pltpu.VMEM((f, d), BF16),
                            pltpu.SemaphoreType.DMA((3,)), pltpu.SemaphoreType.DMA((3,))]),
        out_shape=jax.ShapeDtypeStruct((TOP_K * n_rows + 2 * blk, LANES), F32),
        compiler_params=_cparams(("arbitrary",)),
        name="moe_experts",
    )(blk_expert, n_used, slot_src, slot_src, slot_src, slot_dst, slot_dst, h, w1, w3, w2)


def _final_kernel(*refs):
    y_refs, (x_ref, w_ref, mod_ref, fw_ref, o_ref) = refs[:TOP_K], refs[TOP_K:]
    xn = _moe_residual(x_ref[0], y_refs, w_ref[0], mod_ref[0, 0])
    o_ref[0] = xn * lax.rsqrt(jnp.mean(xn * xn, axis=-1, keepdims=True) + EPS) * fw_ref[...]


def _final(y, x, wcol, mods, layer, final_w):
    b, t, d = x.shape
    nt = t // TM
    tile = lambda i, j: (i, j + 1, 0)
    return pl.pallas_call(
        _final_kernel,
        grid=(b, nt - 1),
        in_specs=_plane_specs(b, nt, d, 1) + [
            pl.BlockSpec((1, TM, d), tile), pl.BlockSpec((1, TM, TOP_K), tile), _mod_spec(mods, layer, b, 1),
            pl.BlockSpec((1, d), lambda i, j: (0, 0))],
        out_specs=pl.BlockSpec((1, TM, d), lambda i, j: (i, j, 0)),
        out_shape=jax.ShapeDtypeStruct((b, t - TM, d), F32),
        compiler_params=_cparams(("arbitrary", "arbitrary")),
        name="moe_combine_final",
    )(*([y] * TOP_K), x, wcol, mods, final_w)


def _invert_kernel(dest_ref, spare_ref, out_ref):
    pltpu.sync_copy(spare_ref, out_ref)

    def place(p, carry):
        out_ref[dest_ref[p]] = p
        return carry

    lax.fori_loop(0, dest_ref.shape[0], place, 0, unroll=16)


def _invert_slots(dest, n_slots):
    spare = dest.shape[0] + np.arange(n_slots, dtype=np.int32) % (2 * TM)
    return pl.pallas_call(
        _invert_kernel,
        in_specs=[pl.BlockSpec(memory_space=pltpu.SMEM), pl.BlockSpec(memory_space=pl.ANY)],
        out_specs=pl.BlockSpec(memory_space=pltpu.SMEM),
        out_shape=jax.ShapeDtypeStruct((n_slots,), jnp.int32),
        name="slot_invert",
    )(dest, jnp.asarray(spare))


def _slot_plan(ridx, n_tok, st):
    e = jnp.transpose(ridx[:, :TOP_K, :], (1, 0, 2)).reshape(-1)
    onehot = (e[:, None] == jnp.arange(N_EXPERTS, dtype=jnp.int32)[None, :]).astype(jnp.int32)
    incl = jnp.cumsum(onehot, axis=0)
    rank = jnp.sum((incl - onehot) * onehot, axis=1)
    counts = incl[-1]
    padded = (counts + TM - 1) // TM * TM
    pends = jnp.cumsum(padded)
    pstarts = pends - padded
    dest = pstarts[e] + rank
    n_blocks = (n_tok * TOP_K + N_EXPERTS * (TM - 1) + TM - 1) // TM
    blk_start = jnp.arange(n_blocks, dtype=jnp.int32) * TM
    blk_expert = jnp.minimum(jnp.sum((pends[None, :] <= blk_start[:, None]).astype(jnp.int32), axis=1),
                             N_EXPERTS - 1)
    n_used = (pends[-1] // TM).astype(jnp.int32).reshape(1)
    slot_dst = _invert_slots(dest.astype(jnp.int32), n_blocks * TM)
    slot_src = jnp.where(slot_dst < n_tok * TOP_K, slot_dst % n_tok, 0)
    shape3 = (n_blocks, 1, TM)
    return blk_expert.astype(jnp.int32), n_used, (slot_src * st).reshape(shape3), (slot_dst * st).reshape(shape3)


def _rope_tables(t_len, ctx_len):
    half = GLA_DK // 2
    inv = ROPE_BASE ** (-np.arange(0, half, 2, dtype=np.float32) / half)
    lane = np.arange(GLA_DK)
    p = np.arange(t_len - ctx_len)
    pos = np.where(lane[None, :] < half, (p // GRID_W)[:, None], (p % GRID_W)[:, None]).astype(np.float32)
    ang = jnp.asarray(pos) * jnp.asarray(inv[(lane % half) % (half // 2)])[None, :]
    cos, sin = jnp.cos(ang), jnp.sin(ang)
    lower = jnp.asarray((lane % half) < half // 2)[None, :]
    pad = lambda a, v: jnp.concatenate([jnp.full((ctx_len, GLA_DK), v, F32), a], axis=0)
    return pad(cos, 1.0), pad(jnp.where(lower, -sin, 0.0), 0.0), pad(jnp.where(lower, 0.0, sin), 0.0)


def kernel(x, c, ctx, c_ctx, w_ada, b_ada, norm1_w, norm2_w, w_in, w_gk_up, b_gk, gla_norm_w, rpb,
           w_bo_gla, w_bo_na, w_out, w_router, b_router, w1, w3, w2, final_norm_w):
    bsz, seq, d = x.shape
    ctx_len = ctx.shape[1]
    depth = w_ada.shape[0]
    assert ctx_len == TM and seq % TM == 0 and TM % GRID_W == 0 and seq // GRID_W >= 3 * (TM // GRID_W)
    t = ctx_len + seq
    nt = t // TM
    n_tok = bsz * t

    xs = jnp.concatenate([ctx, x], axis=1)
    cvec = jnp.concatenate([c, c_ctx[None], jnp.zeros((8 - bsz - 1, d), F32)], axis=0)
    mods = _modulation(cvec, w_ada, b_ada).reshape(depth, 8, 6, d)
    cos, sa, sb = _rope_tables(t, ctx_len)
    wrt = w_router.T.astype(F32)
    brt = jnp.broadcast_to(b_router.astype(F32)[:, None], (N_EXPERTS, TM))
    gq_end = 2 * GLA_QK + 2 * GLA_V
    wp = (w_in[:, :, :gq_end].astype(BF16), w_in[:, :, gq_end + 2 * GLA_LR:].astype(BF16),
          jnp.pad(w_in[:, :, gq_end:gq_end + 2 * GLA_LR], ((0, 0), (0, 0), (0, LR_PAD - 2 * GLA_LR))).astype(BF16))
    wup = jnp.stack([jnp.pad(w_gk_up[:, dirn], ((0, 0), (dirn * GLA_LR, LR_PAD - (dirn + 1) * GLA_LR), (0, 0)))
                     for dirn in range(2)], axis=1).astype(BF16)
    bg = b_gk.reshape(depth, 2, 1, GLA_QK)
    nw1, nw2 = norm1_w.reshape(depth, 1, d), norm2_w.reshape(depth, 1, d)
    gnw = gla_norm_w.reshape(depth, 1, GLA_DV)
    wbg, wbn, wo = w_bo_gla.astype(BF16), w_bo_na.astype(BF16), w_out.astype(BF16)
    bias = _na_bias_tables(rpb, seq // GRID_W)

    moe = None
    for l in range(depth):
        outs = _inproj(xs, mods, nw1, wp, cos, sa, sb, l, moe)
        if moe is not None:
            xs, outs = outs[0], outs[1:]
        gq, gk, gv, gg, nq, nk, nv, m1, m2, lr = outs
        ob = _gla_pass(True, gq, gk, gv, lr, wup, bg, l)
        gy = _gla_pass(False, gq, gk, gv, lr, wup, bg, l, extra=(ob, gg, gnw))
        ny = _na(nq, nk, nv, bias, l)
        xs, hp, ridx, rw = _merge(gy, ny, m1, m2, xs, mods, nw2, wbg, wbn, wo, wrt, brt, l)
        wcol = jnp.transpose(rw[:, :TOP_K, :], (0, 2, 1))
        blk_expert, n_used, slot_src, slot_dst = _slot_plan(ridx, n_tok, d // LANES)
        y = _moe(blk_expert, n_used, slot_src, slot_dst, hp, w1, w3, w2, l)
        moe = (y, wcol)
    return _final(moe[0], xs, moe[1], mods, depth - 1, final_norm_w[None])
```

```python
import functools

import jax
import jax.numpy as jnp
import numpy as np
from jax import lax
from jax.experimental import pallas as pl
from jax.experimental.pallas import tpu as pltpu

F32 = jnp.float32
BF16 = jnp.bfloat16

EPS = 1e-6
GRID_W = 64
GLA_HEADS = 4
GLA_DK = 128
GLA_DV = 256
GLA_LR = 16
GLA_TAU = 16.0
GLA_CHUNK = 64
ROPE_BASE = 10000.0
NA_HEADS = 8
NA_HD = 64
WIN_R = 8
WIN_C = 16
N_EXPERTS = 16
N_GROUPS = 4
EXPERTS_PER_GROUP = N_EXPERTS // N_GROUPS
TOP_K = 2

GLA_QK = GLA_HEADS * GLA_DK
GLA_V = GLA_HEADS * GLA_DV
NA_W = NA_HEADS * NA_HD

TM = 256
LANES = 128
LR_PAD = LANES
NEG = -1e30
VMEM_LIMIT = 56 * 1024 * 1024

_NT = (((1,), (1,)), ((), ()))
_TN = (((0,), (0,)), ((), ()))


def _cparams(sem):
    return pltpu.CompilerParams(dimension_semantics=sem, vmem_limit_bytes=VMEM_LIMIT)


def _dot(a, b):
    return jnp.dot(a, b, preferred_element_type=F32)


def _dg(a, b, dims):
    return lax.dot_general(a, b, dims, preferred_element_type=F32)


def _split(a):
    hi = a.astype(BF16)
    lo = (a - hi.astype(F32)).astype(BF16)
    return hi, lo


def _sigmoid(x):
    return 1.0 / (1.0 + jnp.exp(-x))


def _norm_mod(x, w, shift, scale):
    y = x * lax.rsqrt(jnp.mean(x * x, axis=-1, keepdims=True) + EPS)
    return (y * w) * (1.0 + scale) + shift


def _rows_to_tiles(ref, base, val):
    st = val.shape[1] // LANES
    for s in range(st):
        ref[pl.ds(base + s, val.shape[0], stride=st), :] = val[:, s * LANES:(s + 1) * LANES]


def _tiles_to_rows(ref, base, n, st):
    return jnp.concatenate([ref[pl.ds(base + s, n, stride=st), :] for s in range(st)], axis=1)


def _mod_kernel(c_ref, w_ref, b_ref, o_ref):
    c = c_ref[...]
    s = c * _sigmoid(c)
    sh, sl = _split(s)
    wh, wl = _split(w_ref[0])
    o_ref[0] = _dot(sh, wh) + _dot(sl, wh) + _dot(sh, wl) + b_ref[0]


def _modulation(cvec, w_ada, b_ada):
    depth, d, n = w_ada.shape
    tn = 1024
    return pl.pallas_call(
        _mod_kernel,
        grid=(depth, n // tn),
        in_specs=[pl.BlockSpec((8, d), lambda l, j: (0, 0)),
                  pl.BlockSpec((1, d, tn), lambda l, j: (l, 0, j)),
                  pl.BlockSpec((1, 1, tn), lambda l, j: (l, 0, j))],
        out_specs=pl.BlockSpec((1, 8, tn), lambda l, j: (l, 0, j)),
        out_shape=jax.ShapeDtypeStruct((depth, 8, n), F32),
        compiler_params=_cparams(("arbitrary", "arbitrary")),
        name="modulation",
    )(cvec, w_ada, b_ada.reshape(depth, 1, n))


_IN_OUT = (("gq", GLA_QK, BF16), ("gk", GLA_QK, BF16), ("gv", GLA_V, BF16), ("gg", GLA_V, BF16),
           ("nq", NA_W, BF16), ("nk", NA_W, BF16), ("nv", NA_W, BF16),
           ("m1", None, BF16), ("m2", None, BF16), ("lr", LR_PAD, F32))


def _moe_residual(x, y_refs, w, mod):
    y = None
    for kk, y_ref in enumerate(y_refs):
        yk = _tiles_to_rows(y_ref, 0, x.shape[0], x.shape[1] // LANES) * w[:, kk:kk + 1]
        y = yk if y is None else y + yk
    return x + mod[5:6] * y


_IN_SPLIT = (4, 9, 10)


def _stream_specs(stream, b, d):
    if isinstance(stream, tuple):
        return [pl.BlockSpec((b, TM, d), lambda j: (0, 0, 0)),
                pl.BlockSpec((b, TM, d), lambda j: (0, jnp.maximum(j - 1, 0), 0))], list(stream)
    return [pl.BlockSpec((b, TM, d), lambda j: (0, j, 0))], [stream]


def _stream_tile(x_refs, bi, is_ctx):
    if len(x_refs) == 2:
        return jnp.where(is_ctx, x_refs[0][bi], x_refs[1][bi])
    return x_refs[0][bi]


def _inproj_kernel(fused, nb, n_stream, *refs):
    if fused:
        n_planes = nb * TOP_K
        y_refs, (w_ref, pmod_ref), refs = refs[:n_planes], refs[n_planes:n_planes + 2], refs[n_planes + 2:]
    x_refs, refs = refs[:n_stream], refs[n_stream:]
    mod_ref, nw_ref, wa_ref, wb_ref, wc_ref, cos_ref, sa_ref, sb_ref = refs[:8]
    o_refs = refs[8:]
    d = x_refs[0].shape[2]
    is_ctx = pl.program_id(0) == 0
    if fused:
        xo_ref, o_refs = o_refs[0], o_refs[1:]
    hs = []
    for bi in range(nb):
        x = _stream_tile(x_refs, bi, is_ctx)
        if fused:
            pmod = jnp.where(is_ctx, pmod_ref[0, nb], pmod_ref[0, bi])
            x = _moe_residual(x, y_refs[bi * TOP_K:(bi + 1) * TOP_K], w_ref[bi], pmod)
            xo_ref[bi] = x
        mod = jnp.where(is_ctx, mod_ref[0, nb], mod_ref[0, bi])
        hs.append(_norm_mod(x, nw_ref[0], mod[0:1], mod[1:2]).astype(BF16))
    h = jnp.concatenate(hs, axis=0)
    cos, sa, sb = cos_ref[...], sa_ref[...], sb_ref[...]

    def rope(r, scale):
        parts = []
        for hh in range(GLA_HEADS):
            xs = r[:, hh * GLA_DK:(hh + 1) * GLA_DK]
            y = xs * cos + pltpu.roll(xs, GLA_DK - 32, 1) * sa + pltpu.roll(xs, 32, 1) * sb
            parts.append(y * scale if scale != 1.0 else y)
        return jnp.concatenate(parts, axis=1)

    off = 0
    for gi, ((name, width, dt), o_ref) in enumerate(zip(_IN_OUT, o_refs)):
        width = d if width is None else width
        if gi in _IN_SPLIT:
            off = 0
        w_ref = wa_ref if gi < _IN_SPLIT[0] else (wb_ref if gi < _IN_SPLIT[1] else wc_ref)
        rr = _dot(h, w_ref[0, :, off:off + width])
        for bi in range(nb):
            r = rr[bi * TM:(bi + 1) * TM]
            if name == "gq":
                r = rope(r, GLA_DK ** -0.5)
            elif name == "gk":
                r = rope(r, 1.0)
            elif name == "nq":
                r = r * (NA_HD ** -0.5)
            o_ref[bi] = r.astype(dt)
        off += width


def _layer_spec(arr, layer, *lead, single=False):
    tail = arr.shape[1 + len(lead):]
    index = (layer,) + lead + (0,) * len(tail)
    mode = dict(pipeline_mode=pl.Buffered(1)) if single else {}
    return pl.BlockSpec((1,) * (1 + len(lead)) + tail, lambda *_: index, **mode)


def _mod_spec(mods, layer, ctx_row, skip=0):
    return pl.BlockSpec((1, 1) + mods.shape[2:], lambda i, j: (layer, jnp.where(j + skip == 0, ctx_row, i), 0, 0))


def _plane_specs(b, nt, d, skip):
    def plane(kk):
        return pl.BlockSpec((TM * d // LANES, LANES), lambda i, j: ((kk * b + i) * nt + j + skip, 0))
    return [plane(kk) for kk in range(TOP_K)]


def _inproj(x, mods, nw, wp, cos, sa, sb, layer, moe=None):
    b, _, d = (x[1] if isinstance(x, tuple) else x).shape
    nt = cos.shape[0] // TM
    t = nt * TM
    widths = [d if w is None else w for _, w, _ in _IN_OUT]
    tile = lambda j: (0, j, 0)
    table = pl.BlockSpec((TM, GLA_DK), lambda j: (j, 0))
    x_specs, x_args = _stream_specs(x, b, d)
    in_specs = x_specs + [_layer_spec(mods, layer), _layer_spec(nw, layer)] + [
        _layer_spec(w, layer, single=True) for w in wp] + [table, table, table]
    args = x_args + [mods, nw, *wp, cos, sa, sb]
    out_specs = [pl.BlockSpec((b, TM, w), tile) for w in widths]
    out_shape = [jax.ShapeDtypeStruct((b, t, w), dt) for w, (_, _, dt) in zip(widths, _IN_OUT)]
    if moe is not None:
        y, wcol = moe
        planes = [pl.BlockSpec((TM * d // LANES, LANES), lambda j, r=(kk * b + bi) * nt: (r + j, 0))
                  for bi in range(b) for kk in range(TOP_K)]
        in_specs = planes + [pl.BlockSpec((b, TM, TOP_K), tile), _layer_spec(mods, layer - 1)] + in_specs
        args = [y] * len(planes) + [wcol, mods] + args
        out_specs = [pl.BlockSpec((b, TM, d), tile)] + out_specs
        out_shape = [jax.ShapeDtypeStruct((b, t, d), F32)] + out_shape
    return pl.pallas_call(
        functools.partial(_inproj_kernel, moe is not None, b, len(x_args)),
        grid=(nt,),
        in_specs=in_specs,
        out_specs=out_specs,
        out_shape=out_shape,
        compiler_params=_cparams(("arbitrary",)),
        name="inproj",
    )(*args)


def _gla_kernel(reverse, finish, q_ref, k_ref, v_ref, lr_ref, wup_ref, bg_ref, *rest):
    if finish:
        ob_ref, g_ref, nw_ref, o_ref, st_ref = rest
    else:
        o_ref, st_ref = rest
    nchunk = TM // GLA_CHUNK
    nb = q_ref.shape[0]

    @pl.when(pl.program_id(0) == 0)
    def _():
        st_ref[...] = jnp.zeros_like(st_ref)

    r = lax.broadcasted_iota(jnp.int32, (TM, TM), 0)
    s = lax.broadcasted_iota(jnp.int32, (TM, TM), 1)
    order = (s >= r) if reverse else (s <= r)
    tri = jnp.where(((r // GLA_CHUNK) == (s // GLA_CHUNK)) & order, 1.0, 0.0).astype(BF16)
    cmask = order[:GLA_CHUNK, :GLA_CHUNK]
    bcs = []
    for bi in range(nb):
        z = _dot(lr_ref[bi].astype(BF16), wup_ref[0, 0]) + bg_ref[0, 0]
        logg = (jnp.minimum(z, 0.0) - jnp.log(1.0 + jnp.exp(-jnp.abs(z)))) * (1.0 / GLA_TAU)
        hi, lo = _split(logg)
        bcs.append(_dot(tri, hi) + _dot(tri, lo))

    for c in (range(nchunk - 1, -1, -1) if reverse else range(nchunk)):
        rs = slice(c * GLA_CHUNK, (c + 1) * GLA_CHUNK)
        for bi in range(nb):
            for hh in range(GLA_HEADS):
                ls = slice(hh * GLA_DK, (hh + 1) * GLA_DK)
                vs = slice(hh * GLA_DV, (hh + 1) * GLA_DV)
                b = bcs[bi][rs, ls]
                bend = b[0:1] if reverse else b[GLA_CHUNK - 1:GLA_CHUNK]
                qc = q_ref[bi, rs, ls].astype(F32)
                kc = k_ref[bi, rs, ls].astype(F32)
                vc = v_ref[bi, rs, vs]
                kd = (kc * jnp.exp(bend - b)).astype(BF16)
                qi = (qc * jnp.exp(b)).astype(BF16)
                qa = (qc * jnp.exp(b - bend)).astype(BF16)
                att = jnp.where(cmask, _dg(qa, kd, _NT), 0.0).astype(BF16)
                st = st_ref[bi * GLA_HEADS + hh]
                o = _dg(qi, st.astype(BF16), _NT) + _dot(att, vc)
                st_ref[bi * GLA_HEADS + hh] = st * jnp.exp(bend) + _dg(vc, kd, _TN)
                if finish:
                    o = o + ob_ref[bi, rs, vs].astype(F32)
                    o = o * lax.rsqrt(jnp.mean(o * o, axis=-1, keepdims=True) + EPS) * nw_ref[0]
                    g = g_ref[bi, rs, vs].astype(F32)
                    o = o * (g * _sigmoid(g))
                o_ref[bi, rs, vs] = o.astype(o_ref.dtype)


def _gla_pass(reverse, q, k, v, lr, wup, bg, layer, extra=None):
    b, t, _ = q.shape
    nt = t // TM
    dirn = 1 if reverse else 0
    if reverse:
        tile = lambda j: (0, jnp.where(j == 0, 0, nt - j), 0)
    else:
        tile = lambda j: (0, j, 0)
    in_specs = [pl.BlockSpec((b, TM, GLA_QK), tile), pl.BlockSpec((b, TM, GLA_QK), tile),
                pl.BlockSpec((b, TM, GLA_V), tile), pl.BlockSpec((b, TM, LR_PAD), tile),
                _layer_spec(wup, layer, dirn), _layer_spec(bg, layer, dirn)]
    args = [q, k, v, lr, wup, bg]
    if extra is not None:
        ob, g, nw = extra
        in_specs += [pl.BlockSpec((b, TM, GLA_V), tile), pl.BlockSpec((b, TM, GLA_V), tile),
                     _layer_spec(nw, layer)]
        args += [ob, g, nw]
    return pl.pallas_call(
        functools.partial(_gla_kernel, reverse, extra is not None),
        grid=(nt,),
        in_specs=in_specs,
        out_specs=pl.BlockSpec((b, TM, GLA_V), tile),
        out_shape=jax.ShapeDtypeStruct((b, t, GLA_V), BF16),
        scratch_shapes=[pltpu.VMEM((b * GLA_HEADS, GLA_DV, GLA_DK), F32)],
        compiler_params=_cparams(("arbitrary",)),
        name="gla_bwd" if reverse else "gla_fwd",
    )(*args)


def _na_bias_tables(rpb, rows):
    rpt = TM // GRID_W
    kr = min(WIN_R, rows)
    col = np.arange(GRID_W)
    cidx = np.clip(col[None, :] - col[:, None], -(WIN_C - 1), WIN_C - 1) + (WIN_C - 1)
    c_sel = (cidx[None] == np.arange(2 * WIN_C - 1)[:, None, None]).astype(np.float32)
    c0 = np.clip(col - WIN_C // 2, 0, GRID_W - WIN_C)
    c_ok = (col[None, :] >= c0[:, None]) & (col[None, :] < c0[:, None] + WIN_C)
    toep = jnp.einsum("lhrd,dqk->lhrqk", rpb.astype(F32), c_sel, precision=lax.Precision.HIGHEST)
    toep = jnp.where(c_ok, toep, NEG)
    depth = rpb.shape[0]
    kinds = ((0, 0), (rpt, 0), (rows - rpt, rows - 3 * rpt))

    def assemble(toep_ref, o_ref):
        kind = pl.program_id(1)
        masked = jnp.full((NA_HEADS, GRID_W, GRID_W), NEG, F32)

        @pl.when(kind == 0)
        def _():
            o_ref[0, 0] = jnp.full(o_ref.shape[2:], NEG, F32)

        for k, (r_base, u_base) in enumerate(kinds, start=1):
            @pl.when(kind == k)
            def _(r_base=r_base, u_base=u_base):
                for i in range(rpt):
                    rq = r_base + i
                    r0 = min(max(rq - kr // 2, 0), rows - kr)
                    for m in range(3 * rpt):
                        rk = u_base + m
                        blk = toep_ref[0, :, rk - rq + (WIN_R - 1)] if r0 <= rk < r0 + kr else masked
                        o_ref[0, 0, :, i * GRID_W:(i + 1) * GRID_W, m * GRID_W:(m + 1) * GRID_W] = blk

    return pl.pallas_call(
        assemble,
        grid=(depth, 1 + len(kinds)),
        in_specs=[pl.BlockSpec((1,) + toep.shape[1:], lambda l, k: (l, 0, 0, 0, 0))],
        out_specs=pl.BlockSpec((1, 1, NA_HEADS, TM, 3 * TM), lambda l, k: (l, k, 0, 0, 0)),
        out_shape=jax.ShapeDtypeStruct((depth, 1 + len(kinds), NA_HEADS, TM, 3 * TM), F32),
        compiler_params=_cparams(("arbitrary", "arbitrary")),
        name="natten_bias",
    )(toep)


def _na_kernel(q_ref, kp_ref, kc_ref, kn_ref, kx_ref, vp_ref, vc_ref, vn_ref, vx_ref, bias_ref, o_ref):
    lane = lax.broadcasted_iota(jnp.int32, (TM, 2 * NA_HD), 1)
    first = lane < NA_HD
    k_refs = (kp_ref, kc_ref, kn_ref, kx_ref)
    v_refs = (vp_ref, vc_ref, vn_ref, vx_ref)
    for bi in range(q_ref.shape[0]):
        for hp in range(NA_HEADS // 2):
            ls = slice(hp * 2 * NA_HD, (hp + 1) * 2 * NA_HD)
            q2 = q_ref[bi, :, ls]
            ks = [kr[bi, :, ls] for kr in k_refs]
            vs = [vr[bi, :, ls] for vr in v_refs]
            outs = []
            for sub in range(2):
                hd = 2 * hp + sub
                own = first if sub == 0 else jnp.logical_not(first)
                qm = jnp.where(own, q2, jnp.zeros_like(q2))
                sc = []
                for i in range(4):
                    s = _dg(qm, ks[i], _NT)
                    if i < 3:
                        s = s + bias_ref[0, 0, hd, :, i * TM:(i + 1) * TM]
                    sc.append(s)
                m = jnp.max(sc[0], axis=-1, keepdims=True)
                for i in range(1, 4):
                    m = jnp.maximum(m, jnp.max(sc[i], axis=-1, keepdims=True))
                acc = jnp.zeros((TM, 2 * NA_HD), F32)
                for i in range(4):
                    p = jnp.exp((sc[i] - m).astype(BF16))
                    acc = acc + _dot(p, jnp.where(own, vs[i], jnp.ones_like(vs[i])))
                l = acc[:, NA_HD:NA_HD + 1] if sub == 0 else acc[:, 0:1]
                outs.append(acc / l)
            o_ref[bi, :, ls] = jnp.where(first, outs[0], outs[1]).astype(o_ref.dtype)


def _na(q, k, v, bias, layer):
    b, t, _ = q.shape
    nt = t // TM
    qt = lambda j: (0, j, 0)
    ctx = lambda j: (0, 0, 0)

    def near(o):
        return lambda j: (0, jnp.clip(j, 2, nt - 2) + o, 0)

    def kind(j):
        return (layer, jnp.where(j == 0, 0, jnp.where(j == 1, 1, jnp.where(j == nt - 1, 3, 2))), 0, 0, 0)

    blk = lambda f: pl.BlockSpec((b, TM, NA_W), f)
    return pl.pallas_call(
        _na_kernel,
        grid=(nt,),
        in_specs=[blk(qt), blk(near(-1)), blk(near(0)), blk(near(1)), blk(ctx),
                  blk(near(-1)), blk(near(0)), blk(near(1)), blk(ctx),
                  pl.BlockSpec((1, 1, NA_HEADS, TM, 3 * TM), kind)],
        out_specs=blk(qt),
        out_shape=jax.ShapeDtypeStruct((b, t, NA_W), BF16),
        compiler_params=_cparams(("arbitrary",)),
        name="natten",
    )(q, k, k, k, k, v, v, v, v, bias)


def _route(sel, aff):
    rows = lambda a, e: a[e:e + 1, :]
    gscore = []
    for g in range(N_GROUPS):
        a, b, c, d = (rows(sel, EXPERTS_PER_GROUP * g + i) for i in range(EXPERTS_PER_GROUP))
        hi1, lo1 = jnp.maximum(a, b), jnp.minimum(a, b)
        hi2, lo2 = jnp.maximum(c, d), jnp.minimum(c, d)
        gscore.append(jnp.maximum(hi1, hi2) + jnp.maximum(jnp.minimum(hi1, hi2), jnp.maximum(lo1, lo2)))
    gbest = jnp.zeros_like(gscore[0], dtype=jnp.int32)
    gval = gscore[0]
    for g in range(1, N_GROUPS):
        better = gscore[g] > gval
        gbest = jnp.where(better, g, gbest)
        gval = jnp.where(better, gscore[g], gval)
    cs, ca = [], []
    for i in range(EXPERTS_PER_GROUP):
        s_i, a_i = rows(sel, i), rows(aff, i)
        for g in range(1, N_GROUPS):
            pick = gbest == g
            s_i = jnp.where(pick, rows(sel, EXPERTS_PER_GROUP * g + i), s_i)
            a_i = jnp.where(pick, rows(aff, EXPERTS_PER_GROUP * g + i), a_i)
        cs.append(s_i)
        ca.append(a_i)
    i1 = jnp.zeros_like(gbest)
    v1, w1 = cs[0], ca[0]
    for i in range(1, EXPERTS_PER_GROUP):
        better = cs[i] > v1
        i1 = jnp.where(better, i, i1)
        v1 = jnp.where(better, cs[i], v1)
        w1 = jnp.where(better, ca[i], w1)
    i2 = jnp.full_like(gbest, -1)
    v2 = jnp.full_like(v1, -jnp.inf)
    w2 = jnp.zeros_like(w1)
    for i in range(EXPERTS_PER_GROUP):
        better = (i1 != i) & ((cs[i] > v2) | (i2 < 0))
        i2 = jnp.where(better, i, i2)
        v2 = jnp.where(better, cs[i], v2)
        w2 = jnp.where(better, ca[i], w2)
    tot = w1 + w2
    base = gbest * EXPERTS_PER_GROUP
    return base + i1, base + i2, w1 / tot, w2 / tot


def _merge_kernel(n_stream, gy_ref, ny_ref, m1_ref, m2_ref, *refs):
    x_refs, refs = refs[:n_stream], refs[n_stream:]
    mod_ref, nw_ref, wbg_ref, wbn_ref, wo_ref, wr_ref, br_ref, xo_ref, hp_ref, ri_ref, rw_ref = refs
    nb = gy_ref.shape[0]
    is_ctx = pl.program_id(0) == 0
    wh, wl = _split(wr_ref[...])
    zi = jnp.zeros((6, TM), jnp.int32)
    for bi in range(nb):
        a = _dot(gy_ref[bi], wbg_ref[0])
        b = _dot(ny_ref[bi], wbn_ref[0])
        m = _sigmoid(m1_ref[bi].astype(F32)) * a + _sigmoid(m2_ref[bi].astype(F32)) * b
        y = _dot(m.astype(BF16), wo_ref[0])
        mod = jnp.where(is_ctx, mod_ref[0, nb], mod_ref[0, bi])
        xn = _stream_tile(x_refs, bi, is_ctx) + mod[2:3] * y
        xo_ref[bi] = xn
        h2 = _norm_mod(xn, nw_ref[0], mod[3:4], mod[4:5])
        _rows_to_tiles(hp_ref.at[bi], 0, h2)
        hh, hl = _split(h2)
        logit = _dg(wh, hh, _NT) + _dg(wh, hl, _NT) + _dg(wl, hh, _NT)
        aff = _sigmoid(logit)
        i1, i2, w1, w2 = _route(aff + br_ref[...], aff)
        ri_ref[bi] = jnp.concatenate([i1, i2, zi], axis=0)
        rw_ref[bi] = jnp.concatenate([w1, w2, zi.astype(F32)], axis=0)


def _merge(gy, ny, m1, m2, x, mods, nw2, wbg, wbn, wo, wrt, brt, layer):
    b, t, d = m1.shape
    nt = t // TM
    rows = TM * d // LANES
    tile = lambda j: (0, j, 0)
    const = lambda j: (0, 0)
    x_specs, x_args = _stream_specs(x, b, d)
    xo, hp, ri, rw = pl.pallas_call(
        functools.partial(_merge_kernel, len(x_args)),
        grid=(nt,),
        in_specs=[pl.BlockSpec((b, TM, GLA_V), tile), pl.BlockSpec((b, TM, NA_W), tile),
                  pl.BlockSpec((b, TM, d), tile), pl.BlockSpec((b, TM, d), tile)] + x_specs + [
                  _layer_spec(mods, layer), _layer_spec(nw2, layer),
                  _layer_spec(wbg, layer), _layer_spec(wbn, layer), _layer_spec(wo, layer),
                  pl.BlockSpec((N_EXPERTS, d), const), pl.BlockSpec((N_EXPERTS, TM), const)],
        out_specs=[pl.BlockSpec((b, TM, d), tile), pl.BlockSpec((b, rows, LANES), tile),
                   pl.BlockSpec((b, 8, TM), lambda j: (0, 0, j)),
                   pl.BlockSpec((b, 8, TM), lambda j: (0, 0, j))],
        out_shape=[jax.ShapeDtypeStruct((b, t, d), F32), jax.ShapeDtypeStruct((b, nt * rows, LANES), F32),
                   jax.ShapeDtypeStruct((b, 8, t), jnp.int32), jax.ShapeDtypeStruct((b, 8, t), F32)],
        compiler_params=_cparams(("arbitrary",)),
        name="merge_router",
    )(gy, ny, m1, m2, *x_args, mods, nw2, wbg, wbn, wo, wrt, brt)
    return xo, hp.reshape(b * nt * rows, LANES), ri, rw


def _moe_kernel(be_ref, nu_ref, src_cur, src_nxt, src_nx2, dst_cur, dst_prv, h_ref, w1_ref, w3_ref, w2_ref,
                o_ref, xbuf, ybuf, w1b, w3b, w2b, gsem, ssem):
    i = pl.program_id(0)
    nu = nu_ref[0]
    xb = lax.rem(i, 3)
    yb = lax.rem(i, 3)
    used = i < nu
    st = w1_ref.shape[2] // LANES
    blk = TM * st
    n_real = o_ref.shape[0] - 2 * blk

    def part(ref, b):
        return ref.at[pl.ds(pl.multiple_of(b * blk, blk), blk)]

    def token(ref, row):
        return ref.at[pl.ds(pl.multiple_of(row, st), st)]

    def start_gather(idx_ref, b):
        for r in range(TM):
            pltpu.make_async_copy(token(h_ref, idx_ref[0, 0, r]), token(xbuf, b * blk + r * st),
                                  gsem.at[b]).start(priority=r % 2)

    def start_scatter(idx_ref, b):
        for r in range(TM):
            pltpu.make_async_copy(token(ybuf, b * blk + r * st), token(o_ref, idx_ref[0, 0, r]),
                                  ssem.at[b]).start(priority=r % 2)

    def wait_gather(b):
        pltpu.make_async_copy(h_ref.at[pl.ds(0, blk)], part(xbuf, b), gsem.at[b]).wait()

    def wait_scatter(b):
        pltpu.make_async_copy(part(ybuf, b), o_ref.at[pl.ds(0, blk)], ssem.at[b]).wait()

    @pl.when(i == 0)
    def _():
        xbuf[...] = jnp.zeros_like(xbuf)
        ybuf[...] = jnp.zeros_like(ybuf)
        for b in range(2):
            spare = pltpu.make_async_copy(part(xbuf, b), o_ref.at[pl.ds(n_real + b * blk, blk)], ssem.at[b])
            spare.start()
            spare.wait()
        start_gather(src_cur, 0)
        start_gather(src_nxt, 1)

    fresh = jnp.logical_or(i == 0, be_ref[i] != be_ref[jnp.maximum(i - 1, 0)])

    @pl.when(jnp.logical_and(used, fresh))
    def _():
        for src, dst in ((w1_ref, w1b), (w3_ref, w3b), (w2_ref, w2b)):
            def body(r, carry, src=src, dst=dst):
                rows = pl.ds(pl.multiple_of(r * LANES, LANES), LANES)
                dst[rows, :] = src[0, 0, rows, :].astype(BF16)
                return carry
            lax.fori_loop(0, src.shape[2] // LANES, body, 0)

    @pl.when(jnp.logical_and(used, i > 0))
    def _():
        wait_scatter(lax.rem(i + 1, 3))

    @pl.when(used)
    def _():
        wait_gather(xb)
        x = _tiles_to_rows(xbuf, xb * blk, TM, st).astype(BF16)
        start_gather(src_nx2, lax.rem(i + 2, 3))
        start_scatter(dst_prv, lax.rem(i + 2, 3))
        a = _dot(x, w1b[...])
        b = _dot(x, w3b[...])
        hmid = (a * _sigmoid(a)) * b
        _rows_to_tiles(ybuf, yb * blk, _dot(hmid.astype(BF16), w2b[...]))

    @pl.when(i == nu - 1)
    def _():
        start_scatter(dst_cur, yb)
        wait_scatter(yb)
        wait_scatter(lax.rem(i + 2, 3))
        wait_gather(lax.rem(i + 1, 3))
        wait_gather(lax.rem(i + 2, 3))


def _moe(blk_expert, n_used, slot_src, slot_dst, h, w1, w3, w2, layer):
    d, f = w1.shape[2], w1.shape[3]
    st = d // LANES
    blk = TM * st
    n_rows = h.shape[0]
    nb = blk_expert.shape[0]
    wsel = lambda i, be, nu: (layer, be[i], 0, 0)
    smem = lambda f_: pl.BlockSpec((1, 1, TM), f_, memory_space=pltpu.SMEM)
    cur = lambda i, be, nu: (i, 0, 0)
    nxt = lambda i, be, nu: (jnp.minimum(i + 1, nb - 1), 0, 0)
    nx2 = lambda i, be, nu: (jnp.minimum(i + 2, nb - 1), 0, 0)
    prv = lambda i, be, nu: (jnp.maximum(i - 1, 0), 0, 0)
    return pl.pallas_call(
        _moe_kernel,
        grid_spec=pltpu.PrefetchScalarGridSpec(
            num_scalar_prefetch=2,
            grid=(nb,),
            in_specs=[smem(cur), smem(nxt), smem(nx2), smem(cur), smem(prv),
                      pl.BlockSpec(memory_space=pl.ANY), pl.BlockSpec((1, 1, d, f), wsel),
                      pl.BlockSpec((1, 1, d, f), wsel), pl.BlockSpec((1, 1, f, d), wsel)],
            out_specs=pl.BlockSpec(memory_space=pl.ANY),
            scratch_shapes=[pltpu.VMEM((3 * blk, LANES), F32), pltpu.VMEM((3 * blk, LANES), F32),
                            pltpu.VMEM((d, f), BF16), pltpu.VMEM((d, f), BF16), pltpu.VMEM((f, d), BF16),
                            pltpu.SemaphoreType.DMA((3,)), pltpu.SemaphoreType.DMA((3,))]),
        out_shape=jax.ShapeDtypeStruct((TOP_K * n_rows + 2 * blk, LANES), F32),
        compiler_params=_cparams(("arbitrary",)),
        name="moe_experts",
    )(blk_expert, n_used, slot_src, slot_src, slot_src, slot_dst, slot_dst, h, w1, w3, w2)


def _final_kernel(*refs):
    y_refs, (x_ref, w_ref, mod_ref, fw_ref, o_ref) = refs[:TOP_K], refs[TOP_K:]
    xn = _moe_residual(x_ref[0], y_refs, w_ref[0], mod_ref[0, 0])
    o_ref[0] = xn * lax.rsqrt(jnp.mean(xn * xn, axis=-1, keepdims=True) + EPS) * fw_ref[...]


def _final(y, x, wcol, mods, layer, final_w):
    b, t, d = x.shape
    nt = t // TM
    tile = lambda i, j: (i, j + 1, 0)
    return pl.pallas_call(
        _final_kernel,
        grid=(b, nt - 1),
        in_specs=_plane_specs(b, nt, d, 1) + [
            pl.BlockSpec((1, TM, d), tile), pl.BlockSpec((1, TM, TOP_K), tile), _mod_spec(mods, layer, b, 1),
            pl.BlockSpec((1, d), lambda i, j: (0, 0))],
        out_specs=pl.BlockSpec((1, TM, d), lambda i, j: (i, j, 0)),
        out_shape=jax.ShapeDtypeStruct((b, t - TM, d), F32),
        compiler_params=_cparams(("arbitrary", "arbitrary")),
        name="moe_combine_final",
    )(*([y] * TOP_K), x, wcol, mods, final_w)


def _invert_kernel(dest_ref, spare_ref, out_ref):
    pltpu.sync_copy(spare_ref, out_ref)

    def place(p, carry):
        out_ref[dest_ref[p]] = p
        return carry

    lax.fori_loop(0, dest_ref.shape[0], place, 0, unroll=16)


def _invert_slots(dest, n_slots):
    spare = dest.shape[0] + np.arange(n_slots, dtype=np.int32) % (2 * TM)
    return pl.pallas_call(
        _invert_kernel,
        in_specs=[pl.BlockSpec(memory_space=pltpu.SMEM), pl.BlockSpec(memory_space=pl.ANY)],
        out_specs=pl.BlockSpec(memory_space=pltpu.SMEM),
        out_shape=jax.ShapeDtypeStruct((n_slots,), jnp.int32),
        name="slot_invert",
    )(dest, jnp.asarray(spare))


def _slot_plan(ridx, n_tok, st):
    e = jnp.transpose(ridx[:, :TOP_K, :], (1, 0, 2)).reshape(-1)
    onehot = (e[:, None] == jnp.arange(N_EXPERTS, dtype=jnp.int32)[None, :]).astype(jnp.int32)
    incl = jnp.cumsum(onehot, axis=0)
    rank = jnp.sum((incl - onehot) * onehot, axis=1)
    counts = incl[-1]
    padded = (counts + TM - 1) // TM * TM
    pends = jnp.cumsum(padded)
    pstarts = pends - padded
    dest = pstarts[e] + rank
    n_blocks = (n_tok * TOP_K + N_EXPERTS * (TM - 1) + TM - 1) // TM
    blk_start = jnp.arange(n_blocks, dtype=jnp.int32) * TM
    blk_expert = jnp.minimum(jnp.sum((pends[None, :] <= blk_start[:, None]).astype(jnp.int32), axis=1),
                             N_EXPERTS - 1)
    n_used = (pends[-1] // TM).astype(jnp.int32).reshape(1)
    slot_dst = _invert_slots(dest.astype(jnp.int32), n_blocks * TM)
    slot_src = jnp.where(slot_dst < n_tok * TOP_K, slot_dst % n_tok, 0)
    shape3 = (n_blocks, 1, TM)
    return blk_expert.astype(jnp.int32), n_used, (slot_src * st).reshape(shape3), (slot_dst * st).reshape(shape3)


def _rope_tables(t_len, ctx_len):
    half = GLA_DK // 2
    inv = ROPE_BASE ** (-np.arange(0, half, 2, dtype=np.float32) / half)
    lane = np.arange(GLA_DK)
    p = np.arange(t_len - ctx_len)
    pos = np.where(lane[None, :] < half, (p // GRID_W)[:, None], (p % GRID_W)[:, None]).astype(np.float32)
    ang = jnp.asarray(pos) * jnp.asarray(inv[(lane % half) % (half // 2)])[None, :]
    cos, sin = jnp.cos(ang), jnp.sin(ang)
    lower = jnp.asarray((lane % half) < half // 2)[None, :]
    pad = lambda a, v: jnp.concatenate([jnp.full((ctx_len, GLA_DK), v, F32), a], axis=0)
    return pad(cos, 1.0), pad(jnp.where(lower, -sin, 0.0), 0.0), pad(jnp.where(lower, 0.0, sin), 0.0)


def kernel(x, c, ctx, c_ctx, w_ada, b_ada, norm1_w, norm2_w, w_in, w_gk_up, b_gk, gla_norm_w, rpb,
           w_bo_gla, w_bo_na, w_out, w_router, b_router, w1, w3, w2, final_norm_w):
    bsz, seq, d = x.shape
    ctx_len = ctx.shape[1]
    depth = w_ada.shape[0]
    assert ctx_len == TM and seq % TM == 0 and TM % GRID_W == 0 and seq // GRID_W >= 3 * (TM // GRID_W)
    t = ctx_len + seq
    nt = t // TM
    n_tok = bsz * t

    xs = (ctx, x)
    cvec = jnp.concatenate([c, c_ctx[None], jnp.zeros((8 - bsz - 1, d), F32)], axis=0)
    mods = _modulation(cvec, w_ada, b_ada).reshape(depth, 8, 6, d)
    cos, sa, sb = _rope_tables(t, ctx_len)
    wrt = w_router.T.astype(F32)
    brt = jnp.broadcast_to(b_router.astype(F32)[:, None], (N_EXPERTS, TM))
    gq_end = 2 * GLA_QK + 2 * GLA_V
    wp = (w_in[:, :, :gq_end].astype(BF16), w_in[:, :, gq_end + 2 * GLA_LR:].astype(BF16),
          jnp.pad(w_in[:, :, gq_end:gq_end + 2 * GLA_LR], ((0, 0), (0, 0), (0, LR_PAD - 2 * GLA_LR))).astype(BF16))
    wup = jnp.stack([jnp.pad(w_gk_up[:, dirn], ((0, 0), (dirn * GLA_LR, LR_PAD - (dirn + 1) * GLA_LR), (0, 0)))
                     for dirn in range(2)], axis=1).astype(BF16)
    bg = b_gk.reshape(depth, 2, 1, GLA_QK)
    nw1, nw2 = norm1_w.reshape(depth, 1, d), norm2_w.reshape(depth, 1, d)
    gnw = gla_norm_w.reshape(depth, 1, GLA_DV)
    wbg, wbn, wo = w_bo_gla.astype(BF16), w_bo_na.astype(BF16), w_out.astype(BF16)
    bias = _na_bias_tables(rpb, seq // GRID_W)

    moe = None
    for l in range(depth):
        outs = _inproj(xs, mods, nw1, wp, cos, sa, sb, l, moe)
        if moe is not None:
            xs, outs = outs[0], outs[1:]
        gq, gk, gv, gg, nq, nk, nv, m1, m2, lr = outs
        ob = _gla_pass(True, gq, gk, gv, lr, wup, bg, l)
        gy = _gla_pass(False, gq, gk, gv, lr, wup, bg, l, extra=(ob, gg, gnw))
        ny = _na(nq, nk, nv, bias, l)
        xs, hp, ridx, rw = _merge(gy, ny, m1, m2, xs, mods, nw2, wbg, wbn, wo, wrt, brt, l)
        wcol = jnp.transpose(rw[:, :TOP_K, :], (0, 2, 1))
        blk_expert, n_used, slot_src, slot_dst = _slot_plan(ridx, n_tok, d // LANES)
        y = _moe(blk_expert, n_used, slot_src, slot_dst, hp, w1, w3, w2, l)
        moe = (y, wcol)
    return _final(moe[0], xs, moe[1], mods, depth - 1, final_norm_w[None])
```

```python
import functools

import jax
import jax.numpy as jnp
import numpy as np
from jax import lax
from jax.experimental import pallas as pl
from jax.experimental.pallas import tpu as pltpu

F32 = jnp.float32
BF16 = jnp.bfloat16

EPS = 1e-6
GRID_W = 64
GLA_HEADS = 4
GLA_DK = 128
GLA_DV = 256
GLA_LR = 16
GLA_TAU = 16.0
GLA_CHUNK = 64
ROPE_BASE = 10000.0
NA_HEADS = 8
NA_HD = 64
WIN_R = 8
WIN_C = 16
N_EXPERTS = 16
N_GROUPS = 4
EXPERTS_PER_GROUP = N_EXPERTS // N_GROUPS
TOP_K = 2

GLA_QK = GLA_HEADS * GLA_DK
GLA_V = GLA_HEADS * GLA_DV
NA_W = NA_HEADS * NA_HD

TM = 256
LANES = 128
LR_PAD = LANES
NEG = -1e30
VMEM_LIMIT = 56 * 1024 * 1024

_NT = (((1,), (1,)), ((), ()))
_TN = (((0,), (0,)), ((), ()))


def _cparams(sem):
    return pltpu.CompilerParams(dimension_semantics=sem, vmem_limit_bytes=VMEM_LIMIT)


def _dot(a, b):
    return jnp.dot(a, b, preferred_element_type=F32)


def _dg(a, b, dims):
    return lax.dot_general(a, b, dims, preferred_element_type=F32)


def _split(a):
    hi = a.astype(BF16)
    lo = (a - hi.astype(F32)).astype(BF16)
    return hi, lo


def _sigmoid(x):
    return 1.0 / (1.0 + jnp.exp(-x))


def _norm_mod(x, w, shift, scale):
    y = x * lax.rsqrt(jnp.mean(x * x, axis=-1, keepdims=True) + EPS)
    return (y * w) * (1.0 + scale) + shift


def _rows_to_tiles(ref, base, val):
    st = val.shape[1] // LANES
    for s in range(st):
        ref[pl.ds(base + s, val.shape[0], stride=st), :] = val[:, s * LANES:(s + 1) * LANES]


def _tiles_to_rows(ref, base, n, st):
    return jnp.concatenate([ref[pl.ds(base + s, n, stride=st), :] for s in range(st)], axis=1)


def _mod_kernel(c_ref, w_ref, b_ref, o_ref):
    c = c_ref[...]
    s = c * _sigmoid(c)
    sh, sl = _split(s)
    wh, wl = _split(w_ref[0])
    o_ref[0] = _dot(sh, wh) + _dot(sl, wh) + _dot(sh, wl) + b_ref[0]


def _modulation(cvec, w_ada, b_ada):
    depth, d, n = w_ada.shape
    tn = 1024
    return pl.pallas_call(
        _mod_kernel,
        grid=(depth, n // tn),
        in_specs=[pl.BlockSpec((8, d), lambda l, j: (0, 0)),
                  pl.BlockSpec((1, d, tn), lambda l, j: (l, 0, j)),
                  pl.BlockSpec((1, 1, tn), lambda l, j: (l, 0, j))],
        out_specs=pl.BlockSpec((1, 8, tn), lambda l, j: (l, 0, j)),
        out_shape=jax.ShapeDtypeStruct((depth, 8, n), F32),
        compiler_params=_cparams(("arbitrary", "arbitrary")),
        name="modulation",
    )(cvec, w_ada, b_ada.reshape(depth, 1, n))


_IN_OUT = (("gq", GLA_QK, BF16), ("gk", GLA_QK, BF16), ("gv", GLA_V, BF16), ("gg", GLA_V, BF16),
           ("nq", NA_W, BF16), ("nk", NA_W, BF16), ("nv", NA_W, BF16),
           ("m1", None, BF16), ("m2", None, BF16), ("lr", LR_PAD, F32))


def _moe_residual(x, y_refs, w, mod):
    y = None
    for kk, y_ref in enumerate(y_refs):
        yk = _tiles_to_rows(y_ref, 0, x.shape[0], x.shape[1] // LANES) * w[:, kk:kk + 1]
        y = yk if y is None else y + yk
    return x + mod[5:6] * y


_IN_SPLIT = (4, 9, 10)


def _stream_specs(stream, b, d):
    if isinstance(stream, tuple):
        return [pl.BlockSpec((b, TM, d), lambda j: (0, 0, 0)),
                pl.BlockSpec((b, TM, d), lambda j: (0, jnp.maximum(j - 1, 0), 0))], list(stream)
    return [pl.BlockSpec((b, TM, d), lambda j: (0, j, 0))], [stream]


def _stream_tile(x_refs, bi, is_ctx):
    if len(x_refs) == 2:
        return jnp.where(is_ctx, x_refs[0][bi], x_refs[1][bi])
    return x_refs[0][bi]


def _inproj_kernel(fused, nb, n_stream, *refs):
    if fused:
        n_planes = nb * TOP_K
        y_refs, (w_ref, pmod_ref), refs = refs[:n_planes], refs[n_planes:n_planes + 2], refs[n_planes + 2:]
    x_refs, refs = refs[:n_stream], refs[n_stream:]
    mod_ref, nw_ref, wa_ref, wb_ref, wc_ref, cos_ref, sa_ref, sb_ref = refs[:8]
    o_refs = refs[8:]
    d = x_refs[0].shape[2]
    is_ctx = pl.program_id(0) == 0
    if fused:
        xo_ref, o_refs = o_refs[0], o_refs[1:]
    hs = []
    for bi in range(nb):
        x = _stream_tile(x_refs, bi, is_ctx)
        if fused:
            pmod = jnp.where(is_ctx, pmod_ref[0, nb], pmod_ref[0, bi])
            x = _moe_residual(x, y_refs[bi * TOP_K:(bi + 1) * TOP_K], w_ref[bi], pmod)
            xo_ref[bi] = x
        mod = jnp.where(is_ctx, mod_ref[0, nb], mod_ref[0, bi])
        hs.append(_norm_mod(x, nw_ref[0], mod[0:1], mod[1:2]).astype(BF16))
    h = jnp.concatenate(hs, axis=0)
    cos, sa, sb = cos_ref[...], sa_ref[...], sb_ref[...]

    def rope(r, scale):
        parts = []
        for hh in range(GLA_HEADS):
            xs = r[:, hh * GLA_DK:(hh + 1) * GLA_DK]
            y = xs * cos + pltpu.roll(xs, GLA_DK - 32, 1) * sa + pltpu.roll(xs, 32, 1) * sb
            parts.append(y * scale if scale != 1.0 else y)
        return jnp.concatenate(parts, axis=1)

    off = 0
    for gi, ((name, width, dt), o_ref) in enumerate(zip(_IN_OUT, o_refs)):
        width = d if width is None else width
        if gi in _IN_SPLIT:
            off = 0
        w_ref = wa_ref if gi < _IN_SPLIT[0] else (wb_ref if gi < _IN_SPLIT[1] else wc_ref)
        rr = _dot(h, w_ref[0, :, off:off + width])
        for bi in range(nb):
            r = rr[bi * TM:(bi + 1) * TM]
            if name == "gq":
                r = rope(r, GLA_DK ** -0.5)
            elif name == "gk":
                r = rope(r, 1.0)
            elif name == "nq":
                r = r * (NA_HD ** -0.5)
            o_ref[bi] = r.astype(dt)
        off += width


def _layer_spec(arr, layer, *lead, single=False):
    tail = arr.shape[1 + len(lead):]
    index = (layer,) + lead + (0,) * len(tail)
    mode = dict(pipeline_mode=pl.Buffered(1)) if single else {}
    return pl.BlockSpec((1,) * (1 + len(lead)) + tail, lambda *_: index, **mode)


def _mod_spec(mods, layer, ctx_row, skip=0):
    return pl.BlockSpec((1, 1) + mods.shape[2:], lambda i, j: (layer, jnp.where(j + skip == 0, ctx_row, i), 0, 0))


def _plane_specs(b, nt, d, skip):
    def plane(kk):
        return pl.BlockSpec((TM * d // LANES, LANES), lambda i, j: ((kk * b + i) * nt + j + skip, 0))
    return [plane(kk) for kk in range(TOP_K)]


def _inproj(x, mods, nw, wp, cos, sa, sb, layer, moe=None):
    b, _, d = (x[1] if isinstance(x, tuple) else x).shape
    nt = cos.shape[0] // TM
    t = nt * TM
    widths = [d if w is None else w for _, w, _ in _IN_OUT]
    tile = lambda j: (0, j, 0)
    table = pl.BlockSpec((TM, GLA_DK), lambda j: (j, 0))
    x_specs, x_args = _stream_specs(x, b, d)
    in_specs = x_specs + [_layer_spec(mods, layer), _layer_spec(nw, layer)] + [
        _layer_spec(w, layer, single=True) for w in wp] + [table, table, table]
    args = x_args + [mods, nw, *wp, cos, sa, sb]
    out_specs = [pl.BlockSpec((b, TM, w), tile) for w in widths]
    out_shape = [jax.ShapeDtypeStruct((b, t, w), dt) for w, (_, _, dt) in zip(widths, _IN_OUT)]
    if moe is not None:
        y, wcol = moe
        planes = [pl.BlockSpec((TM * d // LANES, LANES), lambda j, r=(kk * b + bi) * nt: (r + j, 0))
                  for bi in range(b) for kk in range(TOP_K)]
        in_specs = planes + [pl.BlockSpec((b, TM, TOP_K), tile), _layer_spec(mods, layer - 1)] + in_specs
        args = [y] * len(planes) + [wcol, mods] + args
        out_specs = [pl.BlockSpec((b, TM, d), tile)] + out_specs
        out_shape = [jax.ShapeDtypeStruct((b, t, d), F32)] + out_shape
    return pl.pallas_call(
        functools.partial(_inproj_kernel, moe is not None, b, len(x_args)),
        grid=(nt,),
        in_specs=in_specs,
        out_specs=out_specs,
        out_shape=out_shape,
        compiler_params=_cparams(("arbitrary",)),
        name="inproj",
    )(*args)


def _gla_kernel(reverse, finish, q_ref, k_ref, v_ref, lr_ref, wup_ref, bg_ref, *rest):
    if finish:
        ob_ref, g_ref, nw_ref, o_ref, st_ref = rest
    else:
        o_ref, st_ref = rest
    nchunk = TM // GLA_CHUNK
    nb = q_ref.shape[0]

    @pl.when(pl.program_id(0) == 0)
    def _():
        st_ref[...] = jnp.zeros_like(st_ref)

    r = lax.broadcasted_iota(jnp.int32, (TM, TM), 0)
    s = lax.broadcasted_iota(jnp.int32, (TM, TM), 1)
    order = (s >= r) if reverse else (s <= r)
    tri = jnp.where(((r // GLA_CHUNK) == (s // GLA_CHUNK)) & order, 1.0, 0.0).astype(BF16)
    cmask = order[:GLA_CHUNK, :GLA_CHUNK]
    bcs = []
    for bi in range(nb):
        z = _dot(lr_ref[bi].astype(BF16), wup_ref[0, 0]) + bg_ref[0, 0]
        logg = (jnp.minimum(z, 0.0) - jnp.log(1.0 + jnp.exp(-jnp.abs(z)))) * (1.0 / GLA_TAU)
        hi, lo = _split(logg)
        bcs.append(_dot(tri, hi) + _dot(tri, lo))

    for c in (range(nchunk - 1, -1, -1) if reverse else range(nchunk)):
        rs = slice(c * GLA_CHUNK, (c + 1) * GLA_CHUNK)
        for bi in range(nb):
            for hh in range(GLA_HEADS):
                ls = slice(hh * GLA_DK, (hh + 1) * GLA_DK)
                vs = slice(hh * GLA_DV, (hh + 1) * GLA_DV)
                b = bcs[bi][rs, ls]
                bend = b[0:1] if reverse else b[GLA_CHUNK - 1:GLA_CHUNK]
                qc = q_ref[bi, rs, ls].astype(F32)
                kc = k_ref[bi, rs, ls].astype(F32)
                vc = v_ref[bi, rs, vs]
                kd = (kc * jnp.exp(bend - b)).astype(BF16)
                qi = (qc * jnp.exp(b)).astype(BF16)
                qa = (qc * jnp.exp(b - bend)).astype(BF16)
                att = jnp.where(cmask, _dg(qa, kd, _NT), 0.0).astype(BF16)
                st = st_ref[bi * GLA_HEADS + hh]
                o = _dg(qi, st.astype(BF16), _NT) + _dot(att, vc)
                st_ref[bi * GLA_HEADS + hh] = st * jnp.exp(bend) + _dg(vc, kd, _TN)
                if finish:
                    o = o + ob_ref[bi, rs, vs].astype(F32)
                    o = o * lax.rsqrt(jnp.mean(o * o, axis=-1, keepdims=True) + EPS) * nw_ref[0]
                    g = g_ref[bi, rs, vs].astype(F32)
                    o = o * (g * _sigmoid(g))
                o_ref[bi, rs, vs] = o.astype(o_ref.dtype)


def _gla_pass(reverse, q, k, v, lr, wup, bg, layer, extra=None):
    b, t, _ = q.shape
    nt = t // TM
    dirn = 1 if reverse else 0
    if reverse:
        tile = lambda j: (0, jnp.where(j == 0, 0, nt - j), 0)
    else:
        tile = lambda j: (0, j, 0)
    in_specs = [pl.BlockSpec((b, TM, GLA_QK), tile), pl.BlockSpec((b, TM, GLA_QK), tile),
                pl.BlockSpec((b, TM, GLA_V), tile), pl.BlockSpec((b, TM, LR_PAD), tile),
                _layer_spec(wup, layer, dirn), _layer_spec(bg, layer, dirn)]
    args = [q, k, v, lr, wup, bg]
    if extra is not None:
        ob, g, nw = extra
        in_specs += [pl.BlockSpec((b, TM, GLA_V), tile), pl.BlockSpec((b, TM, GLA_V), tile),
                     _layer_spec(nw, layer)]
        args += [ob, g, nw]
    return pl.pallas_call(
        functools.partial(_gla_kernel, reverse, extra is not None),
        grid=(nt,),
        in_specs=in_specs,
        out_specs=pl.BlockSpec((b, TM, GLA_V), tile),
        out_shape=jax.ShapeDtypeStruct((b, t, GLA_V), BF16),
        scratch_shapes=[pltpu.VMEM((b * GLA_HEADS, GLA_DV, GLA_DK), F32)],
        compiler_params=_cparams(("arbitrary",)),
        name="gla_bwd" if reverse else "gla_fwd",
    )(*args)


def _na_bias_tables(rpb, rows):
    rpt = TM // GRID_W
    kr = min(WIN_R, rows)
    col = np.arange(GRID_W)
    cidx = np.clip(col[None, :] - col[:, None], -(WIN_C - 1), WIN_C - 1) + (WIN_C - 1)
    c_sel = (cidx[None] == np.arange(2 * WIN_C - 1)[:, None, None]).astype(np.float32)
    c0 = np.clip(col - WIN_C // 2, 0, GRID_W - WIN_C)
    c_ok = (col[None, :] >= c0[:, None]) & (col[None, :] < c0[:, None] + WIN_C)
    toep = jnp.einsum("lhrd,dqk->lhrqk", rpb.astype(F32), c_sel, precision=lax.Precision.HIGHEST)
    toep = jnp.where(c_ok, toep, NEG)
    depth = rpb.shape[0]
    kinds = ((0, 0), (rpt, 0), (rows - rpt, rows - 3 * rpt))

    def assemble(toep_ref, o_ref):
        kind = pl.program_id(1)
        masked = jnp.full((NA_HEADS, GRID_W, GRID_W), NEG, F32)

        @pl.when(kind == 0)
        def _():
            o_ref[0, 0] = jnp.full(o_ref.shape[2:], NEG, F32)

        for k, (r_base, u_base) in enumerate(kinds, start=1):
            @pl.when(kind == k)
            def _(r_base=r_base, u_base=u_base):
                for i in range(rpt):
                    rq = r_base + i
                    r0 = min(max(rq - kr // 2, 0), rows - kr)
                    for m in range(3 * rpt):
                        rk = u_base + m
                        blk = toep_ref[0, :, rk - rq + (WIN_R - 1)] if r0 <= rk < r0 + kr else masked
                        o_ref[0, 0, :, i * GRID_W:(i + 1) * GRID_W, m * GRID_W:(m + 1) * GRID_W] = blk

    return pl.pallas_call(
        assemble,
        grid=(depth, 1 + len(kinds)),
        in_specs=[pl.BlockSpec((1,) + toep.shape[1:], lambda l, k: (l, 0, 0, 0, 0))],
        out_specs=pl.BlockSpec((1, 1, NA_HEADS, TM, 3 * TM), lambda l, k: (l, k, 0, 0, 0)),
        out_shape=jax.ShapeDtypeStruct((depth, 1 + len(kinds), NA_HEADS, TM, 3 * TM), F32),
        compiler_params=_cparams(("arbitrary", "arbitrary")),
        name="natten_bias",
    )(toep)


def _na_kernel(q_ref, kp_ref, kc_ref, kn_ref, kx_ref, vp_ref, vc_ref, vn_ref, vx_ref, bias_ref, o_ref):
    lane = lax.broadcasted_iota(jnp.int32, (TM, 2 * NA_HD), 1)
    first = lane < NA_HD
    k_refs = (kp_ref, kc_ref, kn_ref, kx_ref)
    v_refs = (vp_ref, vc_ref, vn_ref, vx_ref)
    for bi in range(q_ref.shape[0]):
        for hp in range(NA_HEADS // 2):
            ls = slice(hp * 2 * NA_HD, (hp + 1) * 2 * NA_HD)
            q2 = q_ref[bi, :, ls]
            ks = [kr[bi, :, ls] for kr in k_refs]
            vs = [vr[bi, :, ls] for vr in v_refs]
            outs = []
            for sub in range(2):
                hd = 2 * hp + sub
                own = first if sub == 0 else jnp.logical_not(first)
                qm = jnp.where(own, q2, jnp.zeros_like(q2))
                sc = []
                for i in range(4):
                    s = _dg(qm, ks[i], _NT)
                    if i < 3:
                        s = s + bias_ref[0, 0, hd, :, i * TM:(i + 1) * TM]
                    sc.append(s)
                m = jnp.max(sc[0], axis=-1, keepdims=True)
                for i in range(1, 4):
                    m = jnp.maximum(m, jnp.max(sc[i], axis=-1, keepdims=True))
                acc = jnp.zeros((TM, 2 * NA_HD), F32)
                for i in range(4):
                    p = jnp.exp((sc[i] - m).astype(BF16))
                    acc = acc + _dot(p, jnp.where(own, vs[i], jnp.ones_like(vs[i])))
                l = acc[:, NA_HD:NA_HD + 1] if sub == 0 else acc[:, 0:1]
                outs.append(acc / l)
            o_ref[bi, :, ls] = jnp.where(first, outs[0], outs[1]).astype(o_ref.dtype)


def _na(q, k, v, bias, layer):
    b, t, _ = q.shape
    nt = t // TM
    qt = lambda j: (0, j, 0)
    ctx = lambda j: (0, 0, 0)

    def near(o):
        return lambda j: (0, jnp.clip(j, 2, nt - 2) + o, 0)

    def kind(j):
        return (layer, jnp.where(j == 0, 0, jnp.where(j == 1, 1, jnp.where(j == nt - 1, 3, 2))), 0, 0, 0)

    blk = lambda f: pl.BlockSpec((b, TM, NA_W), f)
    return pl.pallas_call(
        _na_kernel,
        grid=(nt,),
        in_specs=[blk(qt), blk(near(-1)), blk(near(0)), blk(near(1)), blk(ctx),
                  blk(near(-1)), blk(near(0)), blk(near(1)), blk(ctx),
                  pl.BlockSpec((1, 1, NA_HEADS, TM, 3 * TM), kind)],
        out_specs=blk(qt),
        out_shape=jax.ShapeDtypeStruct((b, t, NA_W), BF16),
        compiler_params=_cparams(("arbitrary",)),
        name="natten",
    )(q, k, k, k, k, v, v, v, v, bias)


def _route(sel, aff):
    rows = lambda a, e: a[e:e + 1, :]
    gscore = []
    for g in range(N_GROUPS):
        a, b, c, d = (rows(sel, EXPERTS_PER_GROUP * g + i) for i in range(EXPERTS_PER_GROUP))
        hi1, lo1 = jnp.maximum(a, b), jnp.minimum(a, b)
        hi2, lo2 = jnp.maximum(c, d), jnp.minimum(c, d)
        gscore.append(jnp.maximum(hi1, hi2) + jnp.maximum(jnp.minimum(hi1, hi2), jnp.maximum(lo1, lo2)))
    gbest = jnp.zeros_like(gscore[0], dtype=jnp.int32)
    gval = gscore[0]
    for g in range(1, N_GROUPS):
        better = gscore[g] > gval
        gbest = jnp.where(better, g, gbest)
        gval = jnp.where(better, gscore[g], gval)
    cs, ca = [], []
    for i in range(EXPERTS_PER_GROUP):
        s_i, a_i = rows(sel, i), rows(aff, i)
        for g in range(1, N_GROUPS):
            pick = gbest == g
            s_i = jnp.where(pick, rows(sel, EXPERTS_PER_GROUP * g + i), s_i)
            a_i = jnp.where(pick, rows(aff, EXPERTS_PER_GROUP * g + i), a_i)
        cs.append(s_i)
        ca.append(a_i)
    i1 = jnp.zeros_like(gbest)
    v1, w1 = cs[0], ca[0]
    for i in range(1, EXPERTS_PER_GROUP):
        better = cs[i] > v1
        i1 = jnp.where(better, i, i1)
        v1 = jnp.where(better, cs[i], v1)
        w1 = jnp.where(better, ca[i], w1)
    i2 = jnp.full_like(gbest, -1)
    v2 = jnp.full_like(v1, -jnp.inf)
    w2 = jnp.zeros_like(w1)
    for i in range(EXPERTS_PER_GROUP):
        better = (i1 != i) & ((cs[i] > v2) | (i2 < 0))
        i2 = jnp.where(better, i, i2)
        v2 = jnp.where(better, cs[i], v2)
        w2 = jnp.where(better, ca[i], w2)
    tot = w1 + w2
    base = gbest * EXPERTS_PER_GROUP
    return base + i1, base + i2, w1 / tot, w2 / tot


def _merge_kernel(n_stream, gy_ref, ny_ref, m1_ref, m2_ref, *refs):
    x_refs, refs = refs[:n_stream], refs[n_stream:]
    mod_ref, nw_ref, wbg_ref, wbn_ref, wo_ref, wr_ref, br_ref, xo_ref, hp_ref, ri_ref, rw_ref = refs
    nb = gy_ref.shape[0]
    is_ctx = pl.program_id(0) == 0
    wh, wl = _split(wr_ref[...])
    zi = jnp.zeros((6, TM), jnp.int32)
    for bi in range(nb):
        a = _dot(gy_ref[bi], wbg_ref[0])
        b = _dot(ny_ref[bi], wbn_ref[0])
        m = _sigmoid(m1_ref[bi].astype(F32)) * a + _sigmoid(m2_ref[bi].astype(F32)) * b
        y = _dot(m.astype(BF16), wo_ref[0])
        mod = jnp.where(is_ctx, mod_ref[0, nb], mod_ref[0, bi])
        xn = _stream_tile(x_refs, bi, is_ctx) + mod[2:3] * y
        xo_ref[bi] = xn
        h2 = _norm_mod(xn, nw_ref[0], mod[3:4], mod[4:5])
        _rows_to_tiles(hp_ref.at[bi], 0, h2)
        hh, hl = _split(h2)
        logit = (_dot(hh, wh) + _dot(hl, wh) + _dot(hh, wl)).T[:N_EXPERTS]
        aff = _sigmoid(logit)
        i1, i2, w1, w2 = _route(aff + br_ref[...], aff)
        ri_ref[bi] = jnp.concatenate([i1, i2, zi], axis=0)
        rw_ref[bi] = jnp.concatenate([w1, w2, zi.astype(F32)], axis=0)


def _merge(gy, ny, m1, m2, x, mods, nw2, wbg, wbn, wo, wrt, brt, layer):
    b, t, d = m1.shape
    nt = t // TM
    rows = TM * d // LANES
    tile = lambda j: (0, j, 0)
    const = lambda j: (0, 0)
    x_specs, x_args = _stream_specs(x, b, d)
    xo, hp, ri, rw = pl.pallas_call(
        functools.partial(_merge_kernel, len(x_args)),
        grid=(nt,),
        in_specs=[pl.BlockSpec((b, TM, GLA_V), tile), pl.BlockSpec((b, TM, NA_W), tile),
                  pl.BlockSpec((b, TM, d), tile), pl.BlockSpec((b, TM, d), tile)] + x_specs + [
                  _layer_spec(mods, layer), _layer_spec(nw2, layer),
                  _layer_spec(wbg, layer), _layer_spec(wbn, layer), _layer_spec(wo, layer),
                  pl.BlockSpec((d, LANES), const), pl.BlockSpec((N_EXPERTS, TM), const)],
        out_specs=[pl.BlockSpec((b, TM, d), tile), pl.BlockSpec((b, rows, LANES), tile),
                   pl.BlockSpec((b, 8, TM), lambda j: (0, 0, j)),
                   pl.BlockSpec((b, 8, TM), lambda j: (0, 0, j))],
        out_shape=[jax.ShapeDtypeStruct((b, t, d), F32), jax.ShapeDtypeStruct((b, nt * rows, LANES), F32),
                   jax.ShapeDtypeStruct((b, 8, t), jnp.int32), jax.ShapeDtypeStruct((b, 8, t), F32)],
        compiler_params=_cparams(("arbitrary",)),
        name="merge_router",
    )(gy, ny, m1, m2, *x_args, mods, nw2, wbg, wbn, wo, wrt, brt)
    return xo, hp.reshape(b * nt * rows, LANES), ri, rw


def _moe_kernel(be_ref, nu_ref, src_cur, src_nxt, src_nx2, dst_cur, dst_prv, h_ref, w1_ref, w3_ref, w2_ref,
                o_ref, xbuf, ybuf, w1b, w3b, w2b, gsem, ssem):
    i = pl.program_id(0)
    nu = nu_ref[0]
    xb = lax.rem(i, 3)
    yb = lax.rem(i, 3)
    used = i < nu
    st = w1_ref.shape[2] // LANES
    blk = TM * st
    n_real = o_ref.shape[0] - 2 * blk

    def part(ref, b):
        return ref.at[pl.ds(pl.multiple_of(b * blk, blk), blk)]

    def token(ref, row):
        return ref.at[pl.ds(pl.multiple_of(row, st), st)]

    def start_gather(idx_ref, b):
        for r in range(TM):
            pltpu.make_async_copy(token(h_ref, idx_ref[0, 0, r]), token(xbuf, b * blk + r * st),
                                  gsem.at[b]).start(priority=r % 2)

    def start_scatter(idx_ref, b):
        for r in range(TM):
            pltpu.make_async_copy(token(ybuf, b * blk + r * st), token(o_ref, idx_ref[0, 0, r]),
                                  ssem.at[b]).start(priority=r % 2)

    def wait_gather(b):
        pltpu.make_async_copy(h_ref.at[pl.ds(0, blk)], part(xbuf, b), gsem.at[b]).wait()

    def wait_scatter(b):
        pltpu.make_async_copy(part(ybuf, b), o_ref.at[pl.ds(0, blk)], ssem.at[b]).wait()

    @pl.when(i == 0)
    def _():
        xbuf[...] = jnp.zeros_like(xbuf)
        ybuf[...] = jnp.zeros_like(ybuf)
        for b in range(2):
            spare = pltpu.make_async_copy(part(xbuf, b), o_ref.at[pl.ds(n_real + b * blk, blk)], ssem.at[b])
            spare.start()
            spare.wait()
        start_gather(src_cur, 0)
        start_gather(src_nxt, 1)

    fresh = jnp.logical_or(i == 0, be_ref[i] != be_ref[jnp.maximum(i - 1, 0)])

    @pl.when(jnp.logical_and(used, fresh))
    def _():
        for src, dst in ((w1_ref, w1b), (w3_ref, w3b), (w2_ref, w2b)):
            def body(r, carry, src=src, dst=dst):
                rows = pl.ds(pl.multiple_of(r * LANES, LANES), LANES)
                dst[rows, :] = src[0, 0, rows, :].astype(BF16)
                return carry
            lax.fori_loop(0, src.shape[2] // LANES, body, 0)

    @pl.when(jnp.logical_and(used, i > 0))
    def _():
        wait_scatter(lax.rem(i + 1, 3))

    @pl.when(used)
    def _():
        wait_gather(xb)
        x = _tiles_to_rows(xbuf, xb * blk, TM, st).astype(BF16)
        start_gather(src_nx2, lax.rem(i + 2, 3))
        start_scatter(dst_prv, lax.rem(i + 2, 3))
        a = _dot(x, w1b[...])
        b = _dot(x, w3b[...])
        hmid = (a * _sigmoid(a)) * b
        _rows_to_tiles(ybuf, yb * blk, _dot(hmid.astype(BF16), w2b[...]))

    @pl.when(i == nu - 1)
    def _():
        start_scatter(dst_cur, yb)
        wait_scatter(yb)
        wait_scatter(lax.rem(i + 2, 3))
        wait_gather(lax.rem(i + 1, 3))
        wait_gather(lax.rem(i + 2, 3))


def _moe(blk_expert, n_used, slot_src, slot_dst, h, w1, w3, w2, layer):
    d, f = w1.shape[2], w1.shape[3]
    st = d // LANES
    blk = TM * st
    n_rows = h.shape[0]
    nb = blk_expert.shape[0]
    wsel = lambda i, be, nu: (layer, be[i], 0, 0)
    smem = lambda f_: pl.BlockSpec((1, 1, TM), f_, memory_space=pltpu.SMEM)
    cur = lambda i, be, nu: (i, 0, 0)
    nxt = lambda i, be, nu: (jnp.minimum(i + 1, nb - 1), 0, 0)
    nx2 = lambda i, be, nu: (jnp.minimum(i + 2, nb - 1), 0, 0)
    prv = lambda i, be, nu: (jnp.maximum(i - 1, 0), 0, 0)
    return pl.pallas_call(
        _moe_kernel,
        grid_spec=pltpu.PrefetchScalarGridSpec(
            num_scalar_prefetch=2,
            grid=(nb,),
            in_specs=[smem(cur), smem(nxt), smem(nx2), smem(cur), smem(prv),
                      pl.BlockSpec(memory_space=pl.ANY), pl.BlockSpec((1, 1, d, f), wsel),
                      pl.BlockSpec((1, 1, d, f), wsel), pl.BlockSpec((1, 1, f, d), wsel)],
            out_specs=pl.BlockSpec(memory_space=pl.ANY),
            scratch_shapes=[pltpu.VMEM((3 * blk, LANES), F32), pltpu.VMEM((3 * blk, LANES), F32),
                            pltpu.VMEM((d, f), BF16), pltpu.VMEM((d, f), BF16), pltpu.VMEM((f, d), BF16),
                            pltpu.SemaphoreType.DMA((3,)), pltpu.SemaphoreType.DMA((3,))]),
        out_shape=jax.ShapeDtypeStruct((TOP_K * n_rows + 2 * blk, LANES), F32),
        compiler_params=_cparams(("arbitrary",)),
        name="moe_experts",
    )(blk_expert, n_used, slot_src, slot_src, slot_src, slot_dst, slot_dst, h, w1, w3, w2)


def _final_kernel(*refs):
    y_refs, (x_ref, w_ref, mod_ref, fw_ref, o_ref) = refs[:TOP_K], refs[TOP_K:]
    xn = _moe_residual(x_ref[0], y_refs, w_ref[0], mod_ref[0, 0])
    o_ref[0] = xn * lax.rsqrt(jnp.mean(xn * xn, axis=-1, keepdims=True) + EPS) * fw_ref[...]


def _final(y, x, wcol, mods, layer, final_w):
    b, t, d = x.shape
    nt = t // TM
    tile = lambda i, j: (i, j + 1, 0)
    return pl.pallas_call(
        _final_kernel,
        grid=(b, nt - 1),
        in_specs=_plane_specs(b, nt, d, 1) + [
            pl.BlockSpec((1, TM, d), tile), pl.BlockSpec((1, TM, TOP_K), tile), _mod_spec(mods, layer, b, 1),
            pl.BlockSpec((1, d), lambda i, j: (0, 0))],
        out_specs=pl.BlockSpec((1, TM, d), lambda i, j: (i, j, 0)),
        out_shape=jax.ShapeDtypeStruct((b, t - TM, d), F32),
        compiler_params=_cparams(("arbitrary", "arbitrary")),
        name="moe_combine_final",
    )(*([y] * TOP_K), x, wcol, mods, final_w)


def _invert_kernel(dest_ref, spare_ref, out_ref):
    pltpu.sync_copy(spare_ref, out_ref)

    def place(p, carry):
        out_ref[dest_ref[p]] = p
        return carry

    lax.fori_loop(0, dest_ref.shape[0], place, 0, unroll=32)


def _invert_slots(dest, n_slots):
    spare = dest.shape[0] + np.arange(n_slots, dtype=np.int32) % (2 * TM)
    return pl.pallas_call(
        _invert_kernel,
        in_specs=[pl.BlockSpec(memory_space=pltpu.SMEM), pl.BlockSpec(memory_space=pl.ANY)],
        out_specs=pl.BlockSpec(memory_space=pltpu.SMEM),
        out_shape=jax.ShapeDtypeStruct((n_slots,), jnp.int32),
        name="slot_invert",
    )(dest, jnp.asarray(spare))


def _slot_plan(ridx, n_tok, st):
    e = jnp.transpose(ridx[:, :TOP_K, :], (1, 0, 2)).reshape(-1)
    onehot = (e[:, None] == jnp.arange(N_EXPERTS, dtype=jnp.int32)[None, :]).astype(jnp.int32)
    incl = jnp.cumsum(onehot, axis=0)
    rank = jnp.sum((incl - onehot) * onehot, axis=1)
    counts = incl[-1]
    padded = (counts + TM - 1) // TM * TM
    pends = jnp.cumsum(padded)
    pstarts = pends - padded
    dest = pstarts[e] + rank
    n_blocks = (n_tok * TOP_K + N_EXPERTS * (TM - 1) + TM - 1) // TM
    blk_start = jnp.arange(n_blocks, dtype=jnp.int32) * TM
    blk_expert = jnp.minimum(jnp.sum((pends[None, :] <= blk_start[:, None]).astype(jnp.int32), axis=1),
                             N_EXPERTS - 1)
    n_used = (pends[-1] // TM).astype(jnp.int32).reshape(1)
    slot_dst = _invert_slots(dest.astype(jnp.int32), n_blocks * TM)
    slot_src = jnp.where(slot_dst < n_tok * TOP_K, slot_dst % n_tok, 0)
    shape3 = (n_blocks, 1, TM)
    return blk_expert.astype(jnp.int32), n_used, (slot_src * st).reshape(shape3), (slot_dst * st).reshape(shape3)


def _rope_tables(t_len, ctx_len):
    half = GLA_DK // 2
    inv = ROPE_BASE ** (-np.arange(0, half, 2, dtype=np.float32) / half)
    lane = np.arange(GLA_DK)
    p = np.arange(t_len - ctx_len)
    pos = np.where(lane[None, :] < half, (p // GRID_W)[:, None], (p % GRID_W)[:, None]).astype(np.float32)
    ang = jnp.asarray(pos) * jnp.asarray(inv[(lane % half) % (half // 2)])[None, :]
    cos, sin = jnp.cos(ang), jnp.sin(ang)
    lower = jnp.asarray((lane % half) < half // 2)[None, :]
    pad = lambda a, v: jnp.concatenate([jnp.full((ctx_len, GLA_DK), v, F32), a], axis=0)
    return pad(cos, 1.0), pad(jnp.where(lower, -sin, 0.0), 0.0), pad(jnp.where(lower, 0.0, sin), 0.0)


def kernel(x, c, ctx, c_ctx, w_ada, b_ada, norm1_w, norm2_w, w_in, w_gk_up, b_gk, gla_norm_w, rpb,
           w_bo_gla, w_bo_na, w_out, w_router, b_router, w1, w3, w2, final_norm_w):
    bsz, seq, d = x.shape
    ctx_len = ctx.shape[1]
    depth = w_ada.shape[0]
    assert ctx_len == TM and seq % TM == 0 and TM % GRID_W == 0 and seq // GRID_W >= 3 * (TM // GRID_W)
    t = ctx_len + seq
    nt = t // TM
    n_tok = bsz * t

    xs = (ctx, x)
    cvec = jnp.concatenate([c, c_ctx[None], jnp.zeros((8 - bsz - 1, d), F32)], axis=0)
    mods = _modulation(cvec, w_ada, b_ada).reshape(depth, 8, 6, d)
    cos, sa, sb = _rope_tables(t, ctx_len)
    wrt = jnp.pad(w_router.astype(F32), ((0, 0), (0, LANES - N_EXPERTS)))
    brt = jnp.broadcast_to(b_router.astype(F32)[:, None], (N_EXPERTS, TM))
    gq_end = 2 * GLA_QK + 2 * GLA_V
    wp = (w_in[:, :, :gq_end].astype(BF16), w_in[:, :, gq_end + 2 * GLA_LR:].astype(BF16),
          jnp.pad(w_in[:, :, gq_end:gq_end + 2 * GLA_LR], ((0, 0), (0, 0), (0, LR_PAD - 2 * GLA_LR))).astype(BF16))
    wup = jnp.stack([jnp.pad(w_gk_up[:, dirn], ((0, 0), (dirn * GLA_LR, LR_PAD - (dirn + 1) * GLA_LR), (0, 0)))
                     for dirn in range(2)], axis=1).astype(BF16)
    bg = b_gk.reshape(depth, 2, 1, GLA_QK)
    nw1, nw2 = norm1_w.reshape(depth, 1, d), norm2_w.reshape(depth, 1, d)
    gnw = gla_norm_w.reshape(depth, 1, GLA_DV)
    wbg, wbn, wo = w_bo_gla.astype(BF16), w_bo_na.astype(BF16), w_out.astype(BF16)
    bias = _na_bias_tables(rpb, seq // GRID_W)

    moe = None
    for l in range(depth):
        outs = _inproj(xs, mods, nw1, wp, cos, sa, sb, l, moe)
        if moe is not None:
            xs, outs = outs[0], outs[1:]
        gq, gk, gv, gg, nq, nk, nv, m1, m2, lr = outs
        ob = _gla_pass(True, gq, gk, gv, lr, wup, bg, l)
        gy = _gla_pass(False, gq, gk, gv, lr, wup, bg, l, extra=(ob, gg, gnw))
        ny = _na(nq, nk, nv, bias, l)
        xs, hp, ridx, rw = _merge(gy, ny, m1, m2, xs, mods, nw2, wbg, wbn, wo, wrt, brt, l)
        wcol = jnp.transpose(rw[:, :TOP_K, :], (0, 2, 1))
        blk_expert, n_used, slot_src, slot_dst = _slot_plan(ridx, n_tok, d // LANES)
        y = _moe(blk_expert, n_used, slot_src, slot_dst, hp, w1, w3, w2, l)
        moe = (y, wcol)
    return _final(moe[0], xs, moe[1], mods, depth - 1, final_norm_w[None])
```

```python
import functools

import jax
import jax.numpy as jnp
import numpy as np
from jax import lax
from jax.experimental import pallas as pl
from jax.experimental.pallas import tpu as pltpu

F32 = jnp.float32
BF16 = jnp.bfloat16

EPS = 1e-6
GRID_W = 64
GLA_HEADS = 4
GLA_DK = 128
GLA_DV = 256
GLA_LR = 16
GLA_TAU = 16.0
GLA_CHUNK = 64
ROPE_BASE = 10000.0
NA_HEADS = 8
NA_HD = 64
WIN_R = 8
WIN_C = 16
N_EXPERTS = 16
N_GROUPS = 4
EXPERTS_PER_GROUP = N_EXPERTS // N_GROUPS
TOP_K = 2

GLA_QK = GLA_HEADS * GLA_DK
GLA_V = GLA_HEADS * GLA_DV
NA_W = NA_HEADS * NA_HD

TM = 256
LANES = 128
LR_PAD = LANES
NEG = -1e30
VMEM_LIMIT = 56 * 1024 * 1024

_NT = (((1,), (1,)), ((), ()))
_TN = (((0,), (0,)), ((), ()))


def _cparams(sem):
    return pltpu.CompilerParams(dimension_semantics=sem, vmem_limit_bytes=VMEM_LIMIT)


def _dot(a, b):
    return jnp.dot(a, b, preferred_element_type=F32)


def _dg(a, b, dims):
    return lax.dot_general(a, b, dims, preferred_element_type=F32)


def _split(a):
    hi = a.astype(BF16)
    lo = (a - hi.astype(F32)).astype(BF16)
    return hi, lo


def _sigmoid(x):
    return 1.0 / (1.0 + jnp.exp(-x))


def _norm_mod(x, w, shift, scale):
    y = x * lax.rsqrt(jnp.mean(x * x, axis=-1, keepdims=True) + EPS)
    return (y * w) * (1.0 + scale) + shift


def _rows_to_tiles(ref, base, val):
    st = val.shape[1] // LANES
    for s in range(st):
        ref[pl.ds(base + s, val.shape[0], stride=st), :] = val[:, s * LANES:(s + 1) * LANES]


def _tiles_to_rows(ref, base, n, st):
    return jnp.concatenate([ref[pl.ds(base + s, n, stride=st), :] for s in range(st)], axis=1)


def _mod_kernel(c_ref, w_ref, b_ref, o_ref):
    c = c_ref[...]
    s = c * _sigmoid(c)
    sh, sl = _split(s)
    wh, wl = _split(w_ref[0])
    o_ref[0] = _dot(sh, wh) + _dot(sl, wh) + _dot(sh, wl) + b_ref[0]


def _modulation(cvec, w_ada, b_ada):
    depth, d, n = w_ada.shape
    tn = 1024
    return pl.pallas_call(
        _mod_kernel,
        grid=(depth, n // tn),
        in_specs=[pl.BlockSpec((8, d), lambda l, j: (0, 0)),
                  pl.BlockSpec((1, d, tn), lambda l, j: (l, 0, j)),
                  pl.BlockSpec((1, 1, tn), lambda l, j: (l, 0, j))],
        out_specs=pl.BlockSpec((1, 8, tn), lambda l, j: (l, 0, j)),
        out_shape=jax.ShapeDtypeStruct((depth, 8, n), F32),
        compiler_params=_cparams(("arbitrary", "arbitrary")),
        name="modulation",
    )(cvec, w_ada, b_ada.reshape(depth, 1, n))


_IN_OUT = (("gq", GLA_QK, BF16), ("gk", GLA_QK, BF16), ("gv", GLA_V, BF16), ("gg", GLA_V, BF16),
           ("nq", NA_W, BF16), ("nk", NA_W, BF16), ("nv", NA_W, BF16),
           ("m1", None, BF16), ("m2", None, BF16), ("lr", LR_PAD, F32))


def _moe_residual(x, y_refs, w, mod):
    y = None
    for kk, y_ref in enumerate(y_refs):
        yk = _tiles_to_rows(y_ref, 0, x.shape[0], x.shape[1] // LANES) * w[:, kk:kk + 1]
        y = yk if y is None else y + yk
    return x + mod[5:6] * y


_IN_SPLIT = (4, 9, 10)


def _stream_specs(stream, b, d):
    if isinstance(stream, tuple):
        return [pl.BlockSpec((b, TM, d), lambda j: (0, 0, 0)),
                pl.BlockSpec((b, TM, d), lambda j: (0, jnp.maximum(j - 1, 0), 0))], list(stream)
    return [pl.BlockSpec((b, TM, d), lambda j: (0, j, 0))], [stream]


def _stream_tile(x_refs, bi, is_ctx):
    if len(x_refs) == 2:
        return jnp.where(is_ctx, x_refs[0][bi], x_refs[1][bi])
    return x_refs[0][bi]


def _inproj_kernel(fused, nb, n_stream, *refs):
    if fused:
        n_planes = nb * TOP_K
        y_refs, (w_ref, pmod_ref), refs = refs[:n_planes], refs[n_planes:n_planes + 2], refs[n_planes + 2:]
    x_refs, refs = refs[:n_stream], refs[n_stream:]
    mod_ref, nw_ref, wa_ref, wb_ref, wc_ref, cos_ref, sa_ref, sb_ref = refs[:8]
    o_refs = refs[8:]
    d = x_refs[0].shape[2]
    is_ctx = pl.program_id(0) == 0
    if fused:
        xo_ref, o_refs = o_refs[0], o_refs[1:]
    cos, sa, sb = cos_ref[...], sa_ref[...], sb_ref[...]

    def rope(r, scale):
        parts = []
        for hh in range(GLA_HEADS):
            xs = r[:, hh * GLA_DK:(hh + 1) * GLA_DK]
            y = xs * cos + pltpu.roll(xs, GLA_DK - 32, 1) * sa + pltpu.roll(xs, 32, 1) * sb
            parts.append(y * scale if scale != 1.0 else y)
        return jnp.concatenate(parts, axis=1)

    for bi in range(nb):
        x = _stream_tile(x_refs, bi, is_ctx)
        if fused:
            pmod = jnp.where(is_ctx, pmod_ref[0, nb], pmod_ref[0, bi])
            x = _moe_residual(x, y_refs[bi * TOP_K:(bi + 1) * TOP_K], w_ref[bi], pmod)
            xo_ref[bi] = x
        mod = jnp.where(is_ctx, mod_ref[0, nb], mod_ref[0, bi])
        h = _norm_mod(x, nw_ref[0], mod[0:1], mod[1:2]).astype(BF16)
        off = 0
        for gi, ((name, width, dt), o_ref) in enumerate(zip(_IN_OUT, o_refs)):
            width = d if width is None else width
            if gi in _IN_SPLIT:
                off = 0
            slab_ref = wa_ref if gi < _IN_SPLIT[0] else (wb_ref if gi < _IN_SPLIT[1] else wc_ref)
            r = _dot(h, slab_ref[0, :, off:off + width])
            if name == "gq":
                r = rope(r, GLA_DK ** -0.5)
            elif name == "gk":
                r = rope(r, 1.0)
            elif name == "nq":
                r = r * (NA_HD ** -0.5)
            o_ref[bi] = r.astype(dt)
            off += width


def _layer_spec(arr, layer, *lead, single=False):
    tail = arr.shape[1 + len(lead):]
    index = (layer,) + lead + (0,) * len(tail)
    mode = dict(pipeline_mode=pl.Buffered(1)) if single else {}
    return pl.BlockSpec((1,) * (1 + len(lead)) + tail, lambda *_: index, **mode)


def _mod_spec(mods, layer, ctx_row, skip=0):
    return pl.BlockSpec((1, 1) + mods.shape[2:], lambda i, j: (layer, jnp.where(j + skip == 0, ctx_row, i), 0, 0))


def _plane_specs(b, nt, d, skip):
    def plane(kk):
        return pl.BlockSpec((TM * d // LANES, LANES), lambda i, j: ((kk * b + i) * nt + j + skip, 0))
    return [plane(kk) for kk in range(TOP_K)]


def _inproj(x, mods, nw, wp, cos, sa, sb, layer, moe=None):
    b, _, d = (x[1] if isinstance(x, tuple) else x).shape
    nt = cos.shape[0] // TM
    t = nt * TM
    widths = [d if w is None else w for _, w, _ in _IN_OUT]
    tile = lambda j: (0, j, 0)
    table = pl.BlockSpec((TM, GLA_DK), lambda j: (j, 0))
    x_specs, x_args = _stream_specs(x, b, d)
    in_specs = x_specs + [_layer_spec(mods, layer), _layer_spec(nw, layer)] + [
        _layer_spec(w, layer, single=True) for w in wp] + [table, table, table]
    args = x_args + [mods, nw, *wp, cos, sa, sb]
    out_specs = [pl.BlockSpec((b, TM, w), tile) for w in widths]
    out_shape = [jax.ShapeDtypeStruct((b, t, w), dt) for w, (_, _, dt) in zip(widths, _IN_OUT)]
    if moe is not None:
        y, wcol = moe
        planes = [pl.BlockSpec((TM * d // LANES, LANES), lambda j, r=(kk * b + bi) * nt: (r + j, 0))
                  for bi in range(b) for kk in range(TOP_K)]
        in_specs = planes + [pl.BlockSpec((b, TM, TOP_K), tile), _layer_spec(mods, layer - 1)] + in_specs
        args = [y] * len(planes) + [wcol, mods] + args
        out_specs = [pl.BlockSpec((b, TM, d), tile)] + out_specs
        out_shape = [jax.ShapeDtypeStruct((b, t, d), F32)] + out_shape
    return pl.pallas_call(
        functools.partial(_inproj_kernel, moe is not None, b, len(x_args)),
        grid=(nt,),
        in_specs=in_specs,
        out_specs=out_specs,
        out_shape=out_shape,
        compiler_params=_cparams(("arbitrary",)),
        name="inproj",
    )(*args)


def _gla_kernel(reverse, finish, q_ref, k_ref, v_ref, lr_ref, wup_ref, bg_ref, *rest):
    if finish:
        ob_ref, g_ref, nw_ref, o_ref, st_ref = rest
    else:
        o_ref, st_ref = rest
    nchunk = TM // GLA_CHUNK
    nb = q_ref.shape[0]

    @pl.when(pl.program_id(0) == 0)
    def _():
        st_ref[...] = jnp.zeros_like(st_ref)

    r = lax.broadcasted_iota(jnp.int32, (TM, TM), 0)
    s = lax.broadcasted_iota(jnp.int32, (TM, TM), 1)
    order = (s >= r) if reverse else (s <= r)
    tri = jnp.where(((r // GLA_CHUNK) == (s // GLA_CHUNK)) & order, 1.0, 0.0).astype(BF16)
    cmask = order[:GLA_CHUNK, :GLA_CHUNK]
    bcs = []
    for bi in range(nb):
        z = _dot(lr_ref[bi].astype(BF16), wup_ref[0, 0]) + bg_ref[0, 0]
        logg = (jnp.minimum(z, 0.0) - jnp.log(1.0 + jnp.exp(-jnp.abs(z)))) * (1.0 / GLA_TAU)
        hi, lo = _split(logg)
        bcs.append(_dot(tri, hi) + _dot(tri, lo))

    for c in (range(nchunk - 1, -1, -1) if reverse else range(nchunk)):
        rs = slice(c * GLA_CHUNK, (c + 1) * GLA_CHUNK)
        for bi in range(nb):
            for hh in range(GLA_HEADS):
                ls = slice(hh * GLA_DK, (hh + 1) * GLA_DK)
                vs = slice(hh * GLA_DV, (hh + 1) * GLA_DV)
                b = bcs[bi][rs, ls]
                bend = b[0:1] if reverse else b[GLA_CHUNK - 1:GLA_CHUNK]
                qc = q_ref[bi, rs, ls].astype(F32)
                kc = k_ref[bi, rs, ls].astype(F32)
                vc = v_ref[bi, rs, vs]
                kd = (kc * jnp.exp(bend - b)).astype(BF16)
                qi = (qc * jnp.exp(b)).astype(BF16)
                qa = (qc * jnp.exp(b - bend)).astype(BF16)
                att = jnp.where(cmask, _dg(qa, kd, _NT), 0.0).astype(BF16)
                st = st_ref[bi * GLA_HEADS + hh]
                o = _dg(qi, st.astype(BF16), _NT) + _dot(att, vc)
                st_ref[bi * GLA_HEADS + hh] = st * jnp.exp(bend) + _dg(vc, kd, _TN)
                if finish:
                    o = o + ob_ref[bi, rs, vs].astype(F32)
                    o = o * lax.rsqrt(jnp.mean(o * o, axis=-1, keepdims=True) + EPS) * nw_ref[0]
                    g = g_ref[bi, rs, vs].astype(F32)
                    o = o * (g * _sigmoid(g))
                o_ref[bi, rs, vs] = o.astype(o_ref.dtype)


def _gla_pass(reverse, q, k, v, lr, wup, bg, layer, extra=None):
    b, t, _ = q.shape
    nt = t // TM
    dirn = 1 if reverse else 0
    if reverse:
        tile = lambda j: (0, jnp.where(j == 0, 0, nt - j), 0)
    else:
        tile = lambda j: (0, j, 0)
    in_specs = [pl.BlockSpec((b, TM, GLA_QK), tile), pl.BlockSpec((b, TM, GLA_QK), tile),
                pl.BlockSpec((b, TM, GLA_V), tile), pl.BlockSpec((b, TM, LR_PAD), tile),
                _layer_spec(wup, layer, dirn), _layer_spec(bg, layer, dirn)]
    args = [q, k, v, lr, wup, bg]
    if extra is not None:
        ob, g, nw = extra
        in_specs += [pl.BlockSpec((b, TM, GLA_V), tile), pl.BlockSpec((b, TM, GLA_V), tile),
                     _layer_spec(nw, layer)]
        args += [ob, g, nw]
    return pl.pallas_call(
        functools.partial(_gla_kernel, reverse, extra is not None),
        grid=(nt,),
        in_specs=in_specs,
        out_specs=pl.BlockSpec((b, TM, GLA_V), tile),
        out_shape=jax.ShapeDtypeStruct((b, t, GLA_V), BF16),
        scratch_shapes=[pltpu.VMEM((b * GLA_HEADS, GLA_DV, GLA_DK), F32)],
        compiler_params=_cparams(("arbitrary",)),
        name="gla_bwd" if reverse else "gla_fwd",
    )(*args)


def _na_bias_tables(rpb, rows):
    rpt = TM // GRID_W
    kr = min(WIN_R, rows)
    col = np.arange(GRID_W)
    cidx = np.clip(col[None, :] - col[:, None], -(WIN_C - 1), WIN_C - 1) + (WIN_C - 1)
    c_sel = (cidx[None] == np.arange(2 * WIN_C - 1)[:, None, None]).astype(np.float32)
    c0 = np.clip(col - WIN_C // 2, 0, GRID_W - WIN_C)
    c_ok = (col[None, :] >= c0[:, None]) & (col[None, :] < c0[:, None] + WIN_C)
    toep = jnp.einsum("lhrd,dqk->lhrqk", rpb.astype(F32), c_sel, precision=lax.Precision.HIGHEST)
    toep = jnp.where(c_ok, toep, NEG)
    depth = rpb.shape[0]
    kinds = ((0, 0), (rpt, 0), (rows - rpt, rows - 3 * rpt))

    def assemble(toep_ref, o_ref):
        kind = pl.program_id(1)
        masked = jnp.full((NA_HEADS, GRID_W, GRID_W), NEG, F32)

        @pl.when(kind == 0)
        def _():
            o_ref[0, 0] = jnp.full(o_ref.shape[2:], NEG, F32)

        for k, (r_base, u_base) in enumerate(kinds, start=1):
            @pl.when(kind == k)
            def _(r_base=r_base, u_base=u_base):
                for i in range(rpt):
                    rq = r_base + i
                    r0 = min(max(rq - kr // 2, 0), rows - kr)
                    for m in range(3 * rpt):
                        rk = u_base + m
                        blk = toep_ref[0, :, rk - rq + (WIN_R - 1)] if r0 <= rk < r0 + kr else masked
                        o_ref[0, 0, :, i * GRID_W:(i + 1) * GRID_W, m * GRID_W:(m + 1) * GRID_W] = blk

    return pl.pallas_call(
        assemble,
        grid=(depth, 1 + len(kinds)),
        in_specs=[pl.BlockSpec((1,) + toep.shape[1:], lambda l, k: (l, 0, 0, 0, 0))],
        out_specs=pl.BlockSpec((1, 1, NA_HEADS, TM, 3 * TM), lambda l, k: (l, k, 0, 0, 0)),
        out_shape=jax.ShapeDtypeStruct((depth, 1 + len(kinds), NA_HEADS, TM, 3 * TM), F32),
        compiler_params=_cparams(("arbitrary", "arbitrary")),
        name="natten_bias",
    )(toep)


def _na_kernel(q_ref, kp_ref, kc_ref, kn_ref, kx_ref, vp_ref, vc_ref, vn_ref, vx_ref, bias_ref, o_ref):
    lane = lax.broadcasted_iota(jnp.int32, (TM, 2 * NA_HD), 1)
    first = lane < NA_HD
    k_refs = (kp_ref, kc_ref, kn_ref, kx_ref)
    v_refs = (vp_ref, vc_ref, vn_ref, vx_ref)
    for bi in range(q_ref.shape[0]):
        for hp in range(NA_HEADS // 2):
            ls = slice(hp * 2 * NA_HD, (hp + 1) * 2 * NA_HD)
            q2 = q_ref[bi, :, ls]
            ks = [kr[bi, :, ls] for kr in k_refs]
            vs = [vr[bi, :, ls] for vr in v_refs]
            outs = []
            for sub in range(2):
                hd = 2 * hp + sub
                own = first if sub == 0 else jnp.logical_not(first)
                qm = jnp.where(own, q2, jnp.zeros_like(q2))
                sc = []
                for i in range(4):
                    s = _dg(qm, ks[i], _NT)
                    if i < 3:
                        s = s + bias_ref[0, 0, hd, :, i * TM:(i + 1) * TM]
                    sc.append(s)
                m = jnp.max(sc[0], axis=-1, keepdims=True)
                for i in range(1, 4):
                    m = jnp.maximum(m, jnp.max(sc[i], axis=-1, keepdims=True))
                acc = jnp.zeros((TM, 2 * NA_HD), F32)
                for i in range(4):
                    p = jnp.exp((sc[i] - m).astype(BF16))
                    acc = acc + _dot(p, jnp.where(own, vs[i], jnp.ones_like(vs[i])))
                l = acc[:, NA_HD:NA_HD + 1] if sub == 0 else acc[:, 0:1]
                outs.append(acc / l)
            o_ref[bi, :, ls] = jnp.where(first, outs[0], outs[1]).astype(o_ref.dtype)


def _na(q, k, v, bias, layer):
    b, t, _ = q.shape
    nt = t // TM
    qt = lambda j: (0, j, 0)
    ctx = lambda j: (0, 0, 0)

    def near(o):
        return lambda j: (0, jnp.clip(j, 2, nt - 2) + o, 0)

    def kind(j):
        return (layer, jnp.where(j == 0, 0, jnp.where(j == 1, 1, jnp.where(j == nt - 1, 3, 2))), 0, 0, 0)

    blk = lambda f: pl.BlockSpec((b, TM, NA_W), f)
    return pl.pallas_call(
        _na_kernel,
        grid=(nt,),
        in_specs=[blk(qt), blk(near(-1)), blk(near(0)), blk(near(1)), blk(ctx),
                  blk(near(-1)), blk(near(0)), blk(near(1)), blk(ctx),
                  pl.BlockSpec((1, 1, NA_HEADS, TM, 3 * TM), kind)],
        out_specs=blk(qt),
        out_shape=jax.ShapeDtypeStruct((b, t, NA_W), BF16),
        compiler_params=_cparams(("arbitrary",)),
        name="natten",
    )(q, k, k, k, k, v, v, v, v, bias)


def _route(sel, aff):
    rows = lambda a, e: a[e:e + 1, :]
    gscore = []
    for g in range(N_GROUPS):
        a, b, c, d = (rows(sel, EXPERTS_PER_GROUP * g + i) for i in range(EXPERTS_PER_GROUP))
        hi1, lo1 = jnp.maximum(a, b), jnp.minimum(a, b)
        hi2, lo2 = jnp.maximum(c, d), jnp.minimum(c, d)
        gscore.append(jnp.maximum(hi1, hi2) + jnp.maximum(jnp.minimum(hi1, hi2), jnp.maximum(lo1, lo2)))
    gbest = jnp.zeros_like(gscore[0], dtype=jnp.int32)
    gval = gscore[0]
    for g in range(1, N_GROUPS):
        better = gscore[g] > gval
        gbest = jnp.where(better, g, gbest)
        gval = jnp.where(better, gscore[g], gval)
    cs, ca = [], []
    for i in range(EXPERTS_PER_GROUP):
        s_i, a_i = rows(sel, i), rows(aff, i)
        for g in range(1, N_GROUPS):
            pick = gbest == g
            s_i = jnp.where(pick, rows(sel, EXPERTS_PER_GROUP * g + i), s_i)
            a_i = jnp.where(pick, rows(aff, EXPERTS_PER_GROUP * g + i), a_i)
        cs.append(s_i)
        ca.append(a_i)
    i1 = jnp.zeros_like(gbest)
    v1, w1 = cs[0], ca[0]
    for i in range(1, EXPERTS_PER_GROUP):
        better = cs[i] > v1
        i1 = jnp.where(better, i, i1)
        v1 = jnp.where(better, cs[i], v1)
        w1 = jnp.where(better, ca[i], w1)
    i2 = jnp.full_like(gbest, -1)
    v2 = jnp.full_like(v1, -jnp.inf)
    w2 = jnp.zeros_like(w1)
    for i in range(EXPERTS_PER_GROUP):
        better = (i1 != i) & ((cs[i] > v2) | (i2 < 0))
        i2 = jnp.where(better, i, i2)
        v2 = jnp.where(better, cs[i], v2)
        w2 = jnp.where(better, ca[i], w2)
    tot = w1 + w2
    base = gbest * EXPERTS_PER_GROUP
    return base + i1, base + i2, w1 / tot, w2 / tot


def _merge_kernel(n_stream, gy_ref, ny_ref, m1_ref, m2_ref, *refs):
    x_refs, refs = refs[:n_stream], refs[n_stream:]
    mod_ref, nw_ref, wbg_ref, wbn_ref, wo_ref, wr_ref, br_ref, xo_ref, hp_ref, ri_ref, rw_ref = refs
    nb = gy_ref.shape[0]
    is_ctx = pl.program_id(0) == 0
    wh, wl = _split(wr_ref[...])
    zi = jnp.zeros((6, TM), jnp.int32)
    for bi in range(nb):
        a = _dot(gy_ref[bi], wbg_ref[0])
        b = _dot(ny_ref[bi], wbn_ref[0])
        m = _sigmoid(m1_ref[bi].astype(F32)) * a + _sigmoid(m2_ref[bi].astype(F32)) * b
        y = _dot(m.astype(BF16), wo_ref[0])
        mod = jnp.where(is_ctx, mod_ref[0, nb], mod_ref[0, bi])
        xn = _stream_tile(x_refs, bi, is_ctx) + mod[2:3] * y
        xo_ref[bi] = xn
        h2 = _norm_mod(xn, nw_ref[0], mod[3:4], mod[4:5])
        _rows_to_tiles(hp_ref.at[bi], 0, h2)
        hh, hl = _split(h2)
        logit = (_dot(hh, wh) + _dot(hl, wh) + _dot(hh, wl)).T[:N_EXPERTS]
        aff = _sigmoid(logit)
        i1, i2, w1, w2 = _route(aff + br_ref[...], aff)
        ri_ref[bi] = jnp.concatenate([i1, i2, zi], axis=0)
        rw_ref[bi] = jnp.concatenate([w1, w2, zi.astype(F32)], axis=0)


def _merge(gy, ny, m1, m2, x, mods, nw2, wbg, wbn, wo, wrt, brt, layer):
    b, t, d = m1.shape
    nt = t // TM
    rows = TM * d // LANES
    tile = lambda j: (0, j, 0)
    const = lambda j: (0, 0)
    x_specs, x_args = _stream_specs(x, b, d)
    xo, hp, ri, rw = pl.pallas_call(
        functools.partial(_merge_kernel, len(x_args)),
        grid=(nt,),
        in_specs=[pl.BlockSpec((b, TM, GLA_V), tile), pl.BlockSpec((b, TM, NA_W), tile),
                  pl.BlockSpec((b, TM, d), tile), pl.BlockSpec((b, TM, d), tile)] + x_specs + [
                  _layer_spec(mods, layer), _layer_spec(nw2, layer),
                  _layer_spec(wbg, layer), _layer_spec(wbn, layer), _layer_spec(wo, layer),
                  pl.BlockSpec((d, LANES), const), pl.BlockSpec((N_EXPERTS, TM), const)],
        out_specs=[pl.BlockSpec((b, TM, d), tile), pl.BlockSpec((b, rows, LANES), tile),
                   pl.BlockSpec((b, 8, TM), lambda j: (0, 0, j)),
                   pl.BlockSpec((b, 8, TM), lambda j: (0, 0, j))],
        out_shape=[jax.ShapeDtypeStruct((b, t, d), F32), jax.ShapeDtypeStruct((b, nt * rows, LANES), F32),
                   jax.ShapeDtypeStruct((b, 8, t), jnp.int32), jax.ShapeDtypeStruct((b, 8, t), F32)],
        compiler_params=_cparams(("arbitrary",)),
        name="merge_router",
    )(gy, ny, m1, m2, *x_args, mods, nw2, wbg, wbn, wo, wrt, brt)
    return xo, hp.reshape(b * nt * rows, LANES), ri, rw


def _moe_kernel(be_ref, nu_ref, src_cur, src_nxt, src_nx2, dst_cur, dst_prv, h_ref, w1_ref, w3_ref, w2_ref,
                o_ref, xbuf, ybuf, w1b, w3b, w2b, gsem, ssem):
    i = pl.program_id(0)
    nu = nu_ref[0]
    xb = lax.rem(i, 3)
    yb = lax.rem(i, 3)
    used = i < nu
    st = w1_ref.shape[2] // LANES
    blk = TM * st
    n_real = o_ref.shape[0] - 2 * blk

    def part(ref, b):
        return ref.at[pl.ds(pl.multiple_of(b * blk, blk), blk)]

    def token(ref, row):
        return ref.at[pl.ds(pl.multiple_of(row, st), st)]

    def start_gather(idx_ref, b):
        for r in range(TM):
            pltpu.make_async_copy(token(h_ref, idx_ref[0, 0, r]), token(xbuf, b * blk + r * st),
                                  gsem.at[b]).start(priority=r % 2)

    def start_scatter(idx_ref, b):
        for r in range(TM):
            pltpu.make_async_copy(token(ybuf, b * blk + r * st), token(o_ref, idx_ref[0, 0, r]),
                                  ssem.at[b]).start(priority=r % 2)

    def wait_gather(b):
        pltpu.make_async_copy(h_ref.at[pl.ds(0, blk)], part(xbuf, b), gsem.at[b]).wait()

    def wait_scatter(b):
        pltpu.make_async_copy(part(ybuf, b), o_ref.at[pl.ds(0, blk)], ssem.at[b]).wait()

    @pl.when(i == 0)
    def _():
        xbuf[...] = jnp.zeros_like(xbuf)
        ybuf[...] = jnp.zeros_like(ybuf)
        for b in range(2):
            spare = pltpu.make_async_copy(part(xbuf, b), o_ref.at[pl.ds(n_real + b * blk, blk)], ssem.at[b])
            spare.start()
            spare.wait()
        start_gather(src_cur, 0)
        start_gather(src_nxt, 1)

    fresh = jnp.logical_or(i == 0, be_ref[i] != be_ref[jnp.maximum(i - 1, 0)])

    @pl.when(jnp.logical_and(used, fresh))
    def _():
        for src, dst in ((w1_ref, w1b), (w3_ref, w3b), (w2_ref, w2b)):
            def body(r, carry, src=src, dst=dst):
                rows = pl.ds(pl.multiple_of(r * LANES, LANES), LANES)
                dst[rows, :] = src[0, 0, rows, :].astype(BF16)
                return carry
            lax.fori_loop(0, src.shape[2] // LANES, body, 0)

    @pl.when(jnp.logical_and(used, i > 0))
    def _():
        wait_scatter(lax.rem(i + 1, 3))

    @pl.when(used)
    def _():
        wait_gather(xb)
        x = _tiles_to_rows(xbuf, xb * blk, TM, st).astype(BF16)
        start_gather(src_nx2, lax.rem(i + 2, 3))
        start_scatter(dst_prv, lax.rem(i + 2, 3))
        a = _dot(x, w1b[...])
        b = _dot(x, w3b[...])
        hmid = (a * _sigmoid(a)) * b
        _rows_to_tiles(ybuf, yb * blk, _dot(hmid.astype(BF16), w2b[...]))

    @pl.when(i == nu - 1)
    def _():
        start_scatter(dst_cur, yb)
        wait_scatter(yb)
        wait_scatter(lax.rem(i + 2, 3))
        wait_gather(lax.rem(i + 1, 3))
        wait_gather(lax.rem(i + 2, 3))


def _moe(blk_expert, n_used, slot_src, slot_dst, h, w1, w3, w2, layer):
    d, f = w1.shape[2], w1.shape[3]
    st = d // LANES
    blk = TM * st
    n_rows = h.shape[0]
    nb = blk_expert.shape[0]
    wsel = lambda i, be, nu: (layer, be[i], 0, 0)
    smem = lambda f_: pl.BlockSpec((1, 1, TM), f_, memory_space=pltpu.SMEM)
    cur = lambda i, be, nu: (i, 0, 0)
    nxt = lambda i, be, nu: (jnp.minimum(i + 1, nb - 1), 0, 0)
    nx2 = lambda i, be, nu: (jnp.minimum(i + 2, nb - 1), 0, 0)
    prv = lambda i, be, nu: (jnp.maximum(i - 1, 0), 0, 0)
    return pl.pallas_call(
        _moe_kernel,
        grid_spec=pltpu.PrefetchScalarGridSpec(
            num_scalar_prefetch=2,
            grid=(nb,),
            in_specs=[smem(cur), smem(nxt), smem(nx2), smem(cur), smem(prv),
                      pl.BlockSpec(memory_space=pl.ANY), pl.BlockSpec((1, 1, d, f), wsel),
                      pl.BlockSpec((1, 1, d, f), wsel), pl.BlockSpec((1, 1, f, d), wsel)],
            out_specs=pl.BlockSpec(memory_space=pl.ANY),
            scratch_shapes=[pltpu.VMEM((3 * blk, LANES), F32), pltpu.VMEM((3 * blk, LANES), F32),
                            pltpu.VMEM((d, f), BF16), pltpu.VMEM((d, f), BF16), pltpu.VMEM((f, d), BF16),
                            pltpu.SemaphoreType.DMA((3,)), pltpu.SemaphoreType.DMA((3,))]),
        out_shape=jax.ShapeDtypeStruct((TOP_K * n_rows + 2 * blk, LANES), F32),
        compiler_params=_cparams(("arbitrary",)),
        name="moe_experts",
    )(blk_expert, n_used, slot_src, slot_src, slot_src, slot_dst, slot_dst, h, w1, w3, w2)


def _final_kernel(*refs):
    y_refs, (x_ref, w_ref, mod_ref, fw_ref, o_ref) = refs[:TOP_K], refs[TOP_K:]
    xn = _moe_residual(x_ref[0], y_refs, w_ref[0], mod_ref[0, 0])
    o_ref[0] = xn * lax.rsqrt(jnp.mean(xn * xn, axis=-1, keepdims=True) + EPS) * fw_ref[...]


def _final(y, x, wcol, mods, layer, final_w):
    b, t, d = x.shape
    nt = t // TM
    tile = lambda i, j: (i, j + 1, 0)
    return pl.pallas_call(
        _final_kernel,
        grid=(b, nt - 1),
        in_specs=_plane_specs(b, nt, d, 1) + [
            pl.BlockSpec((1, TM, d), tile), pl.BlockSpec((1, TM, TOP_K), tile), _mod_spec(mods, layer, b, 1),
            pl.BlockSpec((1, d), lambda i, j: (0, 0))],
        out_specs=pl.BlockSpec((1, TM, d), lambda i, j: (i, j, 0)),
        out_shape=jax.ShapeDtypeStruct((b, t - TM, d), F32),
        compiler_params=_cparams(("arbitrary", "arbitrary")),
        name="moe_combine_final",
    )(*([y] * TOP_K), x, wcol, mods, final_w)


def _invert_kernel(dest_ref, spare_ref, out_ref):
    pltpu.sync_copy(spare_ref, out_ref)

    def place(p, carry):
        out_ref[dest_ref[p]] = p
        return carry

    lax.fori_loop(0, dest_ref.shape[0], place, 0, unroll=32)


def _invert_slots(dest, n_slots):
    spare = dest.shape[0] + np.arange(n_slots, dtype=np.int32) % (2 * TM)
    return pl.pallas_call(
        _invert_kernel,
        in_specs=[pl.BlockSpec(memory_space=pltpu.SMEM), pl.BlockSpec(memory_space=pl.ANY)],
        out_specs=pl.BlockSpec(memory_space=pltpu.SMEM),
        out_shape=jax.ShapeDtypeStruct((n_slots,), jnp.int32),
        name="slot_invert",
    )(dest, jnp.asarray(spare))


def _slot_plan(ridx, n_tok, st):
    e = jnp.transpose(ridx[:, :TOP_K, :], (1, 0, 2)).reshape(-1)
    onehot = (e[:, None] == jnp.arange(N_EXPERTS, dtype=jnp.int32)[None, :]).astype(jnp.int32)
    incl = jnp.cumsum(onehot, axis=0)
    rank = jnp.sum((incl - onehot) * onehot, axis=1)
    counts = incl[-1]
    padded = (counts + TM - 1) // TM * TM
    pends = jnp.cumsum(padded)
    pstarts = pends - padded
    dest = pstarts[e] + rank
    n_blocks = (n_tok * TOP_K + N_EXPERTS * (TM - 1) + TM - 1) // TM
    blk_start = jnp.arange(n_blocks, dtype=jnp.int32) * TM
    blk_expert = jnp.minimum(jnp.sum((pends[None, :] <= blk_start[:, None]).astype(jnp.int32), axis=1),
                             N_EXPERTS - 1)
    n_used = (pends[-1] // TM).astype(jnp.int32).reshape(1)
    slot_dst = _invert_slots(dest.astype(jnp.int32), n_blocks * TM)
    slot_src = jnp.where(slot_dst < n_tok * TOP_K, slot_dst % n_tok, 0)
    shape3 = (n_blocks, 1, TM)
    return blk_expert.astype(jnp.int32), n_used, (slot_src * st).reshape(shape3), (slot_dst * st).reshape(shape3)


def _rope_tables(t_len, ctx_len):
    half = GLA_DK // 2
    inv = ROPE_BASE ** (-np.arange(0, half, 2, dtype=np.float32) / half)
    lane = np.arange(GLA_DK)
    p = np.arange(t_len - ctx_len)
    pos = np.where(lane[None, :] < half, (p // GRID_W)[:, None], (p % GRID_W)[:, None]).astype(np.float32)
    ang = jnp.asarray(pos) * jnp.asarray(inv[(lane % half) % (half // 2)])[None, :]
    cos, sin = jnp.cos(ang), jnp.sin(ang)
    lower = jnp.asarray((lane % half) < half // 2)[None, :]
    pad = lambda a, v: jnp.concatenate([jnp.full((ctx_len, GLA_DK), v, F32), a], axis=0)
    return pad(cos, 1.0), pad(jnp.where(lower, -sin, 0.0), 0.0), pad(jnp.where(lower, 0.0, sin), 0.0)


def kernel(x, c, ctx, c_ctx, w_ada, b_ada, norm1_w, norm2_w, w_in, w_gk_up, b_gk, gla_norm_w, rpb,
           w_bo_gla, w_bo_na, w_out, w_router, b_router, w1, w3, w2, final_norm_w):
    bsz, seq, d = x.shape
    ctx_len = ctx.shape[1]
    depth = w_ada.shape[0]
    assert ctx_len == TM and seq % TM == 0 and TM % GRID_W == 0 and seq // GRID_W >= 3 * (TM // GRID_W)
    t = ctx_len + seq
    nt = t // TM
    n_tok = bsz * t

    xs = (ctx, x)
    cvec = jnp.concatenate([c, c_ctx[None], jnp.zeros((8 - bsz - 1, d), F32)], axis=0)
    mods = _modulation(cvec, w_ada, b_ada).reshape(depth, 8, 6, d)
    cos, sa, sb = _rope_tables(t, ctx_len)
    wrt = jnp.pad(w_router.astype(F32), ((0, 0), (0, LANES - N_EXPERTS)))
    brt = jnp.broadcast_to(b_router.astype(F32)[:, None], (N_EXPERTS, TM))
    gq_end = 2 * GLA_QK + 2 * GLA_V
    wp = (w_in[:, :, :gq_end].astype(BF16), w_in[:, :, gq_end + 2 * GLA_LR:].astype(BF16),
          jnp.pad(w_in[:, :, gq_end:gq_end + 2 * GLA_LR], ((0, 0), (0, 0), (0, LR_PAD - 2 * GLA_LR))).astype(BF16))
    wup = jnp.stack([jnp.pad(w_gk_up[:, dirn], ((0, 0), (dirn * GLA_LR, LR_PAD - (dirn + 1) * GLA_LR), (0, 0)))
                     for dirn in range(2)], axis=1).astype(BF16)
    bg = b_gk.reshape(depth, 2, 1, GLA_QK)
    nw1, nw2 = norm1_w.reshape(depth, 1, d), norm2_w.reshape(depth, 1, d)
    gnw = gla_norm_w.reshape(depth, 1, GLA_DV)
    wbg, wbn, wo = w_bo_gla.astype(BF16), w_bo_na.astype(BF16), w_out.astype(BF16)
    bias = _na_bias_tables(rpb, seq // GRID_W)

    moe = None
    for l in range(depth):
        outs = _inproj(xs, mods, nw1, wp, cos, sa, sb, l, moe)
        if moe is not None:
            xs, outs = outs[0], outs[1:]
        gq, gk, gv, gg, nq, nk, nv, m1, m2, lr = outs
        ob = _gla_pass(True, gq, gk, gv, lr, wup, bg, l)
        gy = _gla_pass(False, gq, gk, gv, lr, wup, bg, l, extra=(ob, gg, gnw))
        ny = _na(nq, nk, nv, bias, l)
        xs, hp, ridx, rw = _merge(gy, ny, m1, m2, xs, mods, nw2, wbg, wbn, wo, wrt, brt, l)
        wcol = jnp.transpose(rw[:, :TOP_K, :], (0, 2, 1))
        blk_expert, n_used, slot_src, slot_dst = _slot_plan(ridx, n_tok, d // LANES)
        y = _moe(blk_expert, n_used, slot_src, slot_dst, hp, w1, w3, w2, l)
        moe = (y, wcol)
    return _final(moe[0], xs, moe[1], mods, depth - 1, final_norm_w[None])
```

```python
import functools

import jax
import jax.numpy as jnp
import numpy as np
from jax import lax
from jax.experimental import pallas as pl
from jax.experimental.pallas import tpu as pltpu

F32 = jnp.float32
BF16 = jnp.bfloat16

EPS = 1e-6
GRID_W = 64
GLA_HEADS = 4
GLA_DK = 128
GLA_DV = 256
GLA_LR = 16
GLA_TAU = 16.0
GLA_CHUNK = 64
ROPE_BASE = 10000.0
NA_HEADS = 8
NA_HD = 64
WIN_R = 8
WIN_C = 16
N_EXPERTS = 16
N_GROUPS = 4
EXPERTS_PER_GROUP = N_EXPERTS // N_GROUPS
TOP_K = 2

GLA_QK = GLA_HEADS * GLA_DK
GLA_V = GLA_HEADS * GLA_DV
NA_W = NA_HEADS * NA_HD

N_MOD = 6
TM = 256
LANES = 128
SUBLANES = 8
INVERT_UNROLL = 32
LR_PAD = LANES
NEG = -1e30
VMEM_LIMIT = 56 * 1024 * 1024

_NT = (((1,), (1,)), ((), ()))
_TN = (((0,), (0,)), ((), ()))


def _cparams(sem):
    return pltpu.CompilerParams(dimension_semantics=sem, vmem_limit_bytes=VMEM_LIMIT)


def _dot(a, b):
    return jnp.dot(a, b, preferred_element_type=F32)


def _dg(a, b, dims):
    return lax.dot_general(a, b, dims, preferred_element_type=F32)


def _split(a):
    hi = a.astype(BF16)
    lo = (a - hi.astype(F32)).astype(BF16)
    return hi, lo


def _sigmoid(x):
    return 1.0 / (1.0 + jnp.exp(-x))


def _norm_mod(x, w, shift, scale):
    y = x * lax.rsqrt(jnp.mean(x * x, axis=-1, keepdims=True) + EPS)
    return (y * w) * (1.0 + scale) + shift


def _rows_to_tiles(ref, base, val):
    st = val.shape[1] // LANES
    for s in range(st):
        ref[pl.ds(base + s, val.shape[0], stride=st), :] = val[:, s * LANES:(s + 1) * LANES]


def _tiles_to_rows(ref, base, n, st):
    return jnp.concatenate([ref[pl.ds(base + s, n, stride=st), :] for s in range(st)], axis=1)


def _mod_kernel(c_ref, w_ref, b_ref, o_ref):
    c = c_ref[...]
    s = c * _sigmoid(c)
    sh, sl = _split(s)
    wh, wl = _split(w_ref[0])
    o_ref[0] = _dot(sh, wh) + _dot(sl, wh) + _dot(sh, wl) + b_ref[0]


def _modulation(cvec, w_ada, b_ada):
    depth, d, n = w_ada.shape
    tn = n // N_MOD
    return pl.pallas_call(
        _mod_kernel,
        grid=(depth, N_MOD),
        in_specs=[pl.BlockSpec((SUBLANES, d), lambda l, j: (0, 0)),
                  pl.BlockSpec((1, d, tn), lambda l, j: (l, 0, j)),
                  pl.BlockSpec((1, 1, tn), lambda l, j: (l, 0, j))],
        out_specs=pl.BlockSpec((1, SUBLANES, tn), lambda l, j: (l, 0, j)),
        out_shape=jax.ShapeDtypeStruct((depth, SUBLANES, n), F32),
        compiler_params=_cparams(("arbitrary", "arbitrary")),
        name="modulation",
    )(cvec, w_ada, b_ada.reshape(depth, 1, n))


_IN_OUT = (("gq", GLA_QK, BF16), ("gk", GLA_QK, BF16), ("gv", GLA_V, BF16), ("gg", GLA_V, BF16),
           ("nq", NA_W, BF16), ("nk", NA_W, BF16), ("nv", NA_W, BF16),
           ("m1", None, BF16), ("m2", None, BF16), ("lr", LR_PAD, F32))


def _moe_residual(x, y_refs, w, mod):
    y = None
    for kk, y_ref in enumerate(y_refs):
        yk = _tiles_to_rows(y_ref, 0, x.shape[0], x.shape[1] // LANES) * w[:, kk:kk + 1]
        y = yk if y is None else y + yk
    return x + mod[5:6] * y


_IN_SPLIT = (4, 9, 10)


def _stream_specs(stream, b, d):
    if isinstance(stream, tuple):
        return [pl.BlockSpec((b, TM, d), lambda j: (0, 0, 0)),
                pl.BlockSpec((b, TM, d), lambda j: (0, jnp.maximum(j - 1, 0), 0))], list(stream)
    return [pl.BlockSpec((b, TM, d), lambda j: (0, j, 0))], [stream]


def _stream_tile(x_refs, bi, is_ctx):
    if len(x_refs) == 2:
        return jnp.where(is_ctx, x_refs[0][bi], x_refs[1][bi])
    return x_refs[0][bi]


def _inproj_kernel(fused, nb, n_stream, *refs):
    if fused:
        n_planes = nb * TOP_K
        y_refs, (w_ref, pmod_ref), refs = refs[:n_planes], refs[n_planes:n_planes + 2], refs[n_planes + 2:]
    x_refs, refs = refs[:n_stream], refs[n_stream:]
    mod_ref, nw_ref, wa_ref, wb_ref, wc_ref, cos_ref, sa_ref, sb_ref = refs[:8]
    o_refs = refs[8:]
    d = x_refs[0].shape[2]
    is_ctx = pl.program_id(0) == 0
    if fused:
        xo_ref, o_refs = o_refs[0], o_refs[1:]
    cos, sa, sb = cos_ref[...], sa_ref[...], sb_ref[...]

    def rope(r, scale):
        parts = []
        for hh in range(GLA_HEADS):
            xs = r[:, hh * GLA_DK:(hh + 1) * GLA_DK]
            y = xs * cos + pltpu.roll(xs, GLA_DK - GLA_DK // 4, 1) * sa + pltpu.roll(xs, GLA_DK // 4, 1) * sb
            parts.append(y * scale if scale != 1.0 else y)
        return jnp.concatenate(parts, axis=1)

    for bi in range(nb):
        x = _stream_tile(x_refs, bi, is_ctx)
        if fused:
            pmod = jnp.where(is_ctx, pmod_ref[0, nb], pmod_ref[0, bi])
            x = _moe_residual(x, y_refs[bi * TOP_K:(bi + 1) * TOP_K], w_ref[bi], pmod)
            xo_ref[bi] = x
        mod = jnp.where(is_ctx, mod_ref[0, nb], mod_ref[0, bi])
        h = _norm_mod(x, nw_ref[0], mod[0:1], mod[1:2]).astype(BF16)
        off = 0
        for gi, ((name, width, dt), o_ref) in enumerate(zip(_IN_OUT, o_refs)):
            width = d if width is None else width
            if gi in _IN_SPLIT:
                off = 0
            slab_ref = wa_ref if gi < _IN_SPLIT[0] else (wb_ref if gi < _IN_SPLIT[1] else wc_ref)
            r = _dot(h, slab_ref[0, :, off:off + width])
            if name == "gq":
                r = rope(r, GLA_DK ** -0.5)
            elif name == "gk":
                r = rope(r, 1.0)
            elif name == "nq":
                r = r * (NA_HD ** -0.5)
            o_ref[bi] = r.astype(dt)
            off += width


def _layer_spec(arr, layer, *lead, single=False):
    tail = arr.shape[1 + len(lead):]
    index = (layer,) + lead + (0,) * len(tail)
    mode = dict(pipeline_mode=pl.Buffered(1)) if single else {}
    return pl.BlockSpec((1,) * (1 + len(lead)) + tail, lambda *_: index, **mode)


def _mod_spec(mods, layer, ctx_row, skip=0):
    return pl.BlockSpec((1, 1) + mods.shape[2:], lambda i, j: (layer, jnp.where(j + skip == 0, ctx_row, i), 0, 0))


def _plane_specs(b, nt, d, skip):
    def plane(kk):
        return pl.BlockSpec((TM * d // LANES, LANES), lambda i, j: ((kk * b + i) * nt + j + skip, 0))
    return [plane(kk) for kk in range(TOP_K)]


def _inproj(x, mods, nw, wp, cos, sa, sb, layer, moe=None):
    b, _, d = (x[1] if isinstance(x, tuple) else x).shape
    nt = cos.shape[0] // TM
    t = nt * TM
    widths = [d if w is None else w for _, w, _ in _IN_OUT]
    tile = lambda j: (0, j, 0)
    table = pl.BlockSpec((TM, GLA_DK), lambda j: (j, 0))
    x_specs, x_args = _stream_specs(x, b, d)
    in_specs = x_specs + [_layer_spec(mods, layer), _layer_spec(nw, layer)] + [
        _layer_spec(w, layer, single=True) for w in wp] + [table, table, table]
    args = x_args + [mods, nw, *wp, cos, sa, sb]
    out_specs = [pl.BlockSpec((b, TM, w), tile) for w in widths]
    out_shape = [jax.ShapeDtypeStruct((b, t, w), dt) for w, (_, _, dt) in zip(widths, _IN_OUT)]
    if moe is not None:
        y, wcol = moe
        planes = [pl.BlockSpec((TM * d // LANES, LANES), lambda j, r=(kk * b + bi) * nt: (r + j, 0))
                  for bi in range(b) for kk in range(TOP_K)]
        in_specs = planes + [pl.BlockSpec((b, TM, TOP_K), tile), _layer_spec(mods, layer - 1)] + in_specs
        args = [y] * len(planes) + [wcol, mods] + args
        out_specs = [pl.BlockSpec((b, TM, d), tile)] + out_specs
        out_shape = [jax.ShapeDtypeStruct((b, t, d), F32)] + out_shape
    return pl.pallas_call(
        functools.partial(_inproj_kernel, moe is not None, b, len(x_args)),
        grid=(nt,),
        in_specs=in_specs,
        out_specs=out_specs,
        out_shape=out_shape,
        compiler_params=_cparams(("arbitrary",)),
        name="inproj",
    )(*args)


def _gla_kernel(reverse, finish, q_ref, k_ref, v_ref, lr_ref, wup_ref, bg_ref, *rest):
    if finish:
        ob_ref, g_ref, nw_ref, o_ref, st_ref = rest
    else:
        o_ref, st_ref = rest
    nchunk = TM // GLA_CHUNK
    nb = q_ref.shape[0]

    @pl.when(pl.program_id(0) == 0)
    def _():
        st_ref[...] = jnp.zeros_like(st_ref)

    r = lax.broadcasted_iota(jnp.int32, (TM, TM), 0)
    s = lax.broadcasted_iota(jnp.int32, (TM, TM), 1)
    order = (s >= r) if reverse else (s <= r)
    tri = jnp.where(((r // GLA_CHUNK) == (s // GLA_CHUNK)) & order, 1.0, 0.0).astype(BF16)
    cmask = order[:GLA_CHUNK, :GLA_CHUNK]
    bcs = []
    for bi in range(nb):
        z = _dot(lr_ref[bi].astype(BF16), wup_ref[0, 0]) + bg_ref[0, 0]
        logg = (jnp.minimum(z, 0.0) - jnp.log(1.0 + jnp.exp(-jnp.abs(z)))) * (1.0 / GLA_TAU)
        hi, lo = _split(logg)
        bcs.append(_dot(tri, hi) + _dot(tri, lo))

    for c in (range(nchunk - 1, -1, -1) if reverse else range(nchunk)):
        rs = slice(c * GLA_CHUNK, (c + 1) * GLA_CHUNK)
        for bi in range(nb):
            for hh in range(GLA_HEADS):
                ls = slice(hh * GLA_DK, (hh + 1) * GLA_DK)
                vs = slice(hh * GLA_DV, (hh + 1) * GLA_DV)
                b = bcs[bi][rs, ls]
                bend = b[0:1] if reverse else b[GLA_CHUNK - 1:GLA_CHUNK]
                qc = q_ref[bi, rs, ls].astype(F32)
                kc = k_ref[bi, rs, ls].astype(F32)
                vc = v_ref[bi, rs, vs]
                kd = (kc * jnp.exp(bend - b)).astype(BF16)
                qi = (qc * jnp.exp(b)).astype(BF16)
                qa = (qc * jnp.exp(b - bend)).astype(BF16)
                att = jnp.where(cmask, _dg(qa, kd, _NT), 0.0).astype(BF16)
                st = st_ref[bi * GLA_HEADS + hh]
                o = _dg(qi, st.astype(BF16), _NT) + _dot(att, vc)
                st_ref[bi * GLA_HEADS + hh] = st * jnp.exp(bend) + _dg(vc, kd, _TN)
                if finish:
                    o = o + ob_ref[bi, rs, vs].astype(F32)
                    o = o * lax.rsqrt(jnp.mean(o * o, axis=-1, keepdims=True) + EPS) * nw_ref[0]
                    g = g_ref[bi, rs, vs].astype(F32)
                    o = o * (g * _sigmoid(g))
                o_ref[bi, rs, vs] = o.astype(o_ref.dtype)


def _gla_pass(reverse, q, k, v, lr, wup, bg, layer, extra=None):
    b, t, _ = q.shape
    nt = t // TM
    dirn = 1 if reverse else 0
    if reverse:
        tile = lambda j: (0, jnp.where(j == 0, 0, nt - j), 0)
    else:
        tile = lambda j: (0, j, 0)
    in_specs = [pl.BlockSpec((b, TM, GLA_QK), tile), pl.BlockSpec((b, TM, GLA_QK), tile),
                pl.BlockSpec((b, TM, GLA_V), tile), pl.BlockSpec((b, TM, LR_PAD), tile),
                _layer_spec(wup, layer, dirn), _layer_spec(bg, layer, dirn)]
    args = [q, k, v, lr, wup, bg]
    if extra is not None:
        ob, g, nw = extra
        in_specs += [pl.BlockSpec((b, TM, GLA_V), tile), pl.BlockSpec((b, TM, GLA_V), tile),
                     _layer_spec(nw, layer)]
        args += [ob, g, nw]
    return pl.pallas_call(
        functools.partial(_gla_kernel, reverse, extra is not None),
        grid=(nt,),
        in_specs=in_specs,
        out_specs=pl.BlockSpec((b, TM, GLA_V), tile),
        out_shape=jax.ShapeDtypeStruct((b, t, GLA_V), BF16),
        scratch_shapes=[pltpu.VMEM((b * GLA_HEADS, GLA_DV, GLA_DK), F32)],
        compiler_params=_cparams(("arbitrary",)),
        name="gla_bwd" if reverse else "gla_fwd",
    )(*args)


def _na_bias_tables(rpb, rows):
    rpt = TM // GRID_W
    kr = min(WIN_R, rows)
    col = np.arange(GRID_W)
    cidx = np.clip(col[None, :] - col[:, None], -(WIN_C - 1), WIN_C - 1) + (WIN_C - 1)
    c_sel = (cidx[None] == np.arange(2 * WIN_C - 1)[:, None, None]).astype(np.float32)
    c0 = np.clip(col - WIN_C // 2, 0, GRID_W - WIN_C)
    c_ok = (col[None, :] >= c0[:, None]) & (col[None, :] < c0[:, None] + WIN_C)
    toep = jnp.einsum("lhrd,dqk->lhrqk", rpb.astype(F32), c_sel, precision=lax.Precision.HIGHEST)
    toep = jnp.where(c_ok, toep, NEG)
    depth = rpb.shape[0]
    kinds = ((0, 0), (rpt, 0), (rows - rpt, rows - 3 * rpt))

    def assemble(toep_ref, o_ref):
        kind = pl.program_id(1)
        masked = jnp.full((NA_HEADS, GRID_W, GRID_W), NEG, F32)

        @pl.when(kind == 0)
        def _():
            o_ref[0, 0] = jnp.full(o_ref.shape[2:], NEG, F32)

        for k, (r_base, u_base) in enumerate(kinds, start=1):
            @pl.when(kind == k)
            def _(r_base=r_base, u_base=u_base):
                for i in range(rpt):
                    rq = r_base + i
                    r0 = min(max(rq - kr // 2, 0), rows - kr)
                    for m in range(3 * rpt):
                        rk = u_base + m
                        blk = toep_ref[0, :, rk - rq + (WIN_R - 1)] if r0 <= rk < r0 + kr else masked
                        o_ref[0, 0, :, i * GRID_W:(i + 1) * GRID_W, m * GRID_W:(m + 1) * GRID_W] = blk

    return pl.pallas_call(
        assemble,
        grid=(depth, 1 + len(kinds)),
        in_specs=[pl.BlockSpec((1,) + toep.shape[1:], lambda l, k: (l, 0, 0, 0, 0))],
        out_specs=pl.BlockSpec((1, 1, NA_HEADS, TM, 3 * TM), lambda l, k: (l, k, 0, 0, 0)),
        out_shape=jax.ShapeDtypeStruct((depth, 1 + len(kinds), NA_HEADS, TM, 3 * TM), F32),
        compiler_params=_cparams(("arbitrary", "arbitrary")),
        name="natten_bias",
    )(toep)


def _na_kernel(q_ref, kp_ref, kc_ref, kn_ref, kx_ref, vp_ref, vc_ref, vn_ref, vx_ref, bias_ref, o_ref):
    lane = lax.broadcasted_iota(jnp.int32, (TM, 2 * NA_HD), 1)
    first = lane < NA_HD
    k_refs = (kp_ref, kc_ref, kn_ref, kx_ref)
    v_refs = (vp_ref, vc_ref, vn_ref, vx_ref)
    for bi in range(q_ref.shape[0]):
        for hp in range(NA_HEADS // 2):
            ls = slice(hp * 2 * NA_HD, (hp + 1) * 2 * NA_HD)
            q2 = q_ref[bi, :, ls]
            ks = [kr[bi, :, ls] for kr in k_refs]
            vs = [vr[bi, :, ls] for vr in v_refs]
            outs = []
            for sub in range(2):
                hd = 2 * hp + sub
                own = first if sub == 0 else jnp.logical_not(first)
                qm = jnp.where(own, q2, jnp.zeros_like(q2))
                sc = []
                for i in range(4):
                    s = _dg(qm, ks[i], _NT)
                    if i < 3:
                        s = s + bias_ref[0, 0, hd, :, i * TM:(i + 1) * TM]
                    sc.append(s)
                m = jnp.max(sc[0], axis=-1, keepdims=True)
                for i in range(1, 4):
                    m = jnp.maximum(m, jnp.max(sc[i], axis=-1, keepdims=True))
                acc = jnp.zeros((TM, 2 * NA_HD), F32)
                for i in range(4):
                    p = jnp.exp((sc[i] - m).astype(BF16))
                    acc = acc + _dot(p, jnp.where(own, vs[i], jnp.ones_like(vs[i])))
                l = acc[:, NA_HD:NA_HD + 1] if sub == 0 else acc[:, 0:1]
                outs.append(acc / l)
            o_ref[bi, :, ls] = jnp.where(first, outs[0], outs[1]).astype(o_ref.dtype)


def _na(q, k, v, bias, layer):
    b, t, _ = q.shape
    nt = t // TM
    qt = lambda j: (0, j, 0)
    ctx = lambda j: (0, 0, 0)

    def near(o):
        return lambda j: (0, jnp.clip(j, 2, nt - 2) + o, 0)

    def kind(j):
        return (layer, jnp.where(j == 0, 0, jnp.where(j == 1, 1, jnp.where(j == nt - 1, 3, 2))), 0, 0, 0)

    blk = lambda f: pl.BlockSpec((b, TM, NA_W), f)
    return pl.pallas_call(
        _na_kernel,
        grid=(nt,),
        in_specs=[blk(qt), blk(near(-1)), blk(near(0)), blk(near(1)), blk(ctx),
                  blk(near(-1)), blk(near(0)), blk(near(1)), blk(ctx),
                  pl.BlockSpec((1, 1, NA_HEADS, TM, 3 * TM), kind)],
        out_specs=blk(qt),
        out_shape=jax.ShapeDtypeStruct((b, t, NA_W), BF16),
        compiler_params=_cparams(("arbitrary",)),
        name="natten",
    )(q, k, k, k, k, v, v, v, v, bias)


def _route(sel, aff):
    rows = lambda a, e: a[e:e + 1, :]
    gscore = []
    for g in range(N_GROUPS):
        a, b, c, d = (rows(sel, EXPERTS_PER_GROUP * g + i) for i in range(EXPERTS_PER_GROUP))
        hi1, lo1 = jnp.maximum(a, b), jnp.minimum(a, b)
        hi2, lo2 = jnp.maximum(c, d), jnp.minimum(c, d)
        gscore.append(jnp.maximum(hi1, hi2) + jnp.maximum(jnp.minimum(hi1, hi2), jnp.maximum(lo1, lo2)))
    gbest = jnp.zeros_like(gscore[0], dtype=jnp.int32)
    gval = gscore[0]
    for g in range(1, N_GROUPS):
        better = gscore[g] > gval
        gbest = jnp.where(better, g, gbest)
        gval = jnp.where(better, gscore[g], gval)
    cs, ca = [], []
    for i in range(EXPERTS_PER_GROUP):
        s_i, a_i = rows(sel, i), rows(aff, i)
        for g in range(1, N_GROUPS):
            pick = gbest == g
            s_i = jnp.where(pick, rows(sel, EXPERTS_PER_GROUP * g + i), s_i)
            a_i = jnp.where(pick, rows(aff, EXPERTS_PER_GROUP * g + i), a_i)
        cs.append(s_i)
        ca.append(a_i)
    i1 = jnp.zeros_like(gbest)
    v1, w1 = cs[0], ca[0]
    for i in range(1, EXPERTS_PER_GROUP):
        better = cs[i] > v1
        i1 = jnp.where(better, i, i1)
        v1 = jnp.where(better, cs[i], v1)
        w1 = jnp.where(better, ca[i], w1)
    i2 = jnp.full_like(gbest, -1)
    v2 = jnp.full_like(v1, -jnp.inf)
    w2 = jnp.zeros_like(w1)
    for i in range(EXPERTS_PER_GROUP):
        better = (i1 != i) & ((cs[i] > v2) | (i2 < 0))
        i2 = jnp.where(better, i, i2)
        v2 = jnp.where(better, cs[i], v2)
        w2 = jnp.where(better, ca[i], w2)
    tot = w1 + w2
    base = gbest * EXPERTS_PER_GROUP
    return base + i1, base + i2, w1 / tot, w2 / tot


def _merge_kernel(n_stream, gy_ref, ny_ref, m1_ref, m2_ref, *refs):
    x_refs, refs = refs[:n_stream], refs[n_stream:]
    mod_ref, nw_ref, wbg_ref, wbn_ref, wo_ref, wr_ref, br_ref, xo_ref, hp_ref, ri_ref, rw_ref = refs
    nb = gy_ref.shape[0]
    is_ctx = pl.program_id(0) == 0
    wh, wl = _split(wr_ref[...])
    zi = jnp.zeros((SUBLANES - TOP_K, TM), jnp.int32)
    for bi in range(nb):
        a = _dot(gy_ref[bi], wbg_ref[0])
        b = _dot(ny_ref[bi], wbn_ref[0])
        m = _sigmoid(m1_ref[bi].astype(F32)) * a + _sigmoid(m2_ref[bi].astype(F32)) * b
        y = _dot(m.astype(BF16), wo_ref[0])
        mod = jnp.where(is_ctx, mod_ref[0, nb], mod_ref[0, bi])
        xn = _stream_tile(x_refs, bi, is_ctx) + mod[2:3] * y
        xo_ref[bi] = xn
        h2 = _norm_mod(xn, nw_ref[0], mod[3:4], mod[4:5])
        _rows_to_tiles(hp_ref.at[bi], 0, h2)
        hh, hl = _split(h2)
        logit = (_dot(hh, wh) + _dot(hl, wh) + _dot(hh, wl)).T[:N_EXPERTS]
        aff = _sigmoid(logit)
        i1, i2, w1, w2 = _route(aff + br_ref[...], aff)
        ri_ref[bi] = jnp.concatenate([i1, i2, zi], axis=0)
        rw_ref[bi] = jnp.concatenate([w1, w2, zi.astype(F32)], axis=0)


def _merge(gy, ny, m1, m2, x, mods, nw2, wbg, wbn, wo, wrt, brt, layer):
    b, t, d = m1.shape
    nt = t // TM
    rows = TM * d // LANES
    tile = lambda j: (0, j, 0)
    const = lambda j: (0, 0)
    x_specs, x_args = _stream_specs(x, b, d)
    xo, hp, ri, rw = pl.pallas_call(
        functools.partial(_merge_kernel, len(x_args)),
        grid=(nt,),
        in_specs=[pl.BlockSpec((b, TM, GLA_V), tile), pl.BlockSpec((b, TM, NA_W), tile),
                  pl.BlockSpec((b, TM, d), tile), pl.BlockSpec((b, TM, d), tile)] + x_specs + [
                  _layer_spec(mods, layer), _layer_spec(nw2, layer),
                  _layer_spec(wbg, layer), _layer_spec(wbn, layer), _layer_spec(wo, layer),
                  pl.BlockSpec((d, LANES), const), pl.BlockSpec((N_EXPERTS, TM), const)],
        out_specs=[pl.BlockSpec((b, TM, d), tile), pl.BlockSpec((b, rows, LANES), tile),
                   pl.BlockSpec((b, SUBLANES, TM), lambda j: (0, 0, j)),
                   pl.BlockSpec((b, SUBLANES, TM), lambda j: (0, 0, j))],
        out_shape=[jax.ShapeDtypeStruct((b, t, d), F32), jax.ShapeDtypeStruct((b, nt * rows, LANES), F32),
                   jax.ShapeDtypeStruct((b, SUBLANES, t), jnp.int32), jax.ShapeDtypeStruct((b, SUBLANES, t), F32)],
        compiler_params=_cparams(("arbitrary",)),
        name="merge_router",
    )(gy, ny, m1, m2, *x_args, mods, nw2, wbg, wbn, wo, wrt, brt)
    return xo, hp.reshape(b * nt * rows, LANES), ri, rw


def _moe_kernel(be_ref, nu_ref, src_cur, src_nxt, src_nx2, dst_cur, dst_prv, h_ref, w1_ref, w3_ref, w2_ref,
                o_ref, xbuf, ybuf, w1b, w3b, w2b, gsem, ssem):
    i = pl.program_id(0)
    nu = nu_ref[0]
    xb = lax.rem(i, 3)
    yb = lax.rem(i, 3)
    used = i < nu
    st = w1_ref.shape[2] // LANES
    blk = TM * st
    n_real = o_ref.shape[0] - 2 * blk

    def part(ref, b):
        return ref.at[pl.ds(pl.multiple_of(b * blk, blk), blk)]

    def token(ref, row):
        return ref.at[pl.ds(pl.multiple_of(row, st), st)]

    def start_gather(idx_ref, b):
        for r in range(TM):
            pltpu.make_async_copy(token(h_ref, idx_ref[0, 0, r]), token(xbuf, b * blk + r * st),
                                  gsem.at[b]).start(priority=r % 2)

    def start_scatter(idx_ref, b):
        for r in range(TM):
            pltpu.make_async_copy(token(ybuf, b * blk + r * st), token(o_ref, idx_ref[0, 0, r]),
                                  ssem.at[b]).start(priority=r % 2)

    def wait_gather(b):
        pltpu.make_async_copy(h_ref.at[pl.ds(0, blk)], part(xbuf, b), gsem.at[b]).wait()

    def wait_scatter(b):
        pltpu.make_async_copy(part(ybuf, b), o_ref.at[pl.ds(0, blk)], ssem.at[b]).wait()

    @pl.when(i == 0)
    def _():
        xbuf[...] = jnp.zeros_like(xbuf)
        ybuf[...] = jnp.zeros_like(ybuf)
        for b in range(2):
            spare = pltpu.make_async_copy(part(xbuf, b), o_ref.at[pl.ds(n_real + b * blk, blk)], ssem.at[b])
            spare.start()
            spare.wait()
        start_gather(src_cur, 0)
        start_gather(src_nxt, 1)

    fresh = jnp.logical_or(i == 0, be_ref[i] != be_ref[jnp.maximum(i - 1, 0)])

    @pl.when(jnp.logical_and(used, fresh))
    def _():
        for src, dst in ((w1_ref, w1b), (w3_ref, w3b), (w2_ref, w2b)):
            def body(r, carry, src=src, dst=dst):
                rows = pl.ds(pl.multiple_of(r * LANES, LANES), LANES)
                dst[rows, :] = src[0, 0, rows, :].astype(BF16)
                return carry
            lax.fori_loop(0, src.shape[2] // LANES, body, 0)

    @pl.when(jnp.logical_and(used, i > 0))
    def _():
        wait_scatter(lax.rem(i + 1, 3))

    @pl.when(used)
    def _():
        wait_gather(xb)
        x = _tiles_to_rows(xbuf, xb * blk, TM, st).astype(BF16)
        start_gather(src_nx2, lax.rem(i + 2, 3))
        start_scatter(dst_prv, lax.rem(i + 2, 3))
        a = _dot(x, w1b[...])
        b = _dot(x, w3b[...])
        hmid = (a * _sigmoid(a)) * b
        _rows_to_tiles(ybuf, yb * blk, _dot(hmid.astype(BF16), w2b[...]))

    @pl.when(i == nu - 1)
    def _():
        start_scatter(dst_cur, yb)
        wait_scatter(yb)
        wait_scatter(lax.rem(i + 2, 3))
        wait_gather(lax.rem(i + 1, 3))
        wait_gather(lax.rem(i + 2, 3))


def _moe(blk_expert, n_used, slot_src, slot_dst, h, w1, w3, w2, layer):
    d, f = w1.shape[2], w1.shape[3]
    st = d // LANES
    blk = TM * st
    n_rows = h.shape[0]
    nb = blk_expert.shape[0]
    wsel = lambda i, be, nu: (layer, be[i], 0, 0)
    smem = lambda f_: pl.BlockSpec((1, 1, TM), f_, memory_space=pltpu.SMEM)
    cur = lambda i, be, nu: (i, 0, 0)
    nxt = lambda i, be, nu: (jnp.minimum(i + 1, nb - 1), 0, 0)
    nx2 = lambda i, be, nu: (jnp.minimum(i + 2, nb - 1), 0, 0)
    prv = lambda i, be, nu: (jnp.maximum(i - 1, 0), 0, 0)
    return pl.pallas_call(
        _moe_kernel,
        grid_spec=pltpu.PrefetchScalarGridSpec(
            num_scalar_prefetch=2,
            grid=(nb,),
            in_specs=[smem(cur), smem(nxt), smem(nx2), smem(cur), smem(prv),
                      pl.BlockSpec(memory_space=pl.ANY), pl.BlockSpec((1, 1, d, f), wsel),
                      pl.BlockSpec((1, 1, d, f), wsel), pl.BlockSpec((1, 1, f, d), wsel)],
            out_specs=pl.BlockSpec(memory_space=pl.ANY),
            scratch_shapes=[pltpu.VMEM((3 * blk, LANES), F32), pltpu.VMEM((3 * blk, LANES), F32),
                            pltpu.VMEM((d, f), BF16), pltpu.VMEM((d, f), BF16), pltpu.VMEM((f, d), BF16),
                            pltpu.SemaphoreType.DMA((3,)), pltpu.SemaphoreType.DMA((3,))]),
        out_shape=jax.ShapeDtypeStruct((TOP_K * n_rows + 2 * blk, LANES), F32),
        compiler_params=_cparams(("arbitrary",)),
        name="moe_experts",
    )(blk_expert, n_used, slot_src, slot_src, slot_src, slot_dst, slot_dst, h, w1, w3, w2)


def _final_kernel(*refs):
    y_refs, (x_ref, w_ref, mod_ref, fw_ref, o_ref) = refs[:TOP_K], refs[TOP_K:]
    xn = _moe_residual(x_ref[0], y_refs, w_ref[0], mod_ref[0, 0])
    o_ref[0] = xn * lax.rsqrt(jnp.mean(xn * xn, axis=-1, keepdims=True) + EPS) * fw_ref[...]


def _final(y, x, wcol, mods, layer, final_w):
    b, t, d = x.shape
    nt = t // TM
    tile = lambda i, j: (i, j + 1, 0)
    return pl.pallas_call(
        _final_kernel,
        grid=(b, nt - 1),
        in_specs=_plane_specs(b, nt, d, 1) + [
            pl.BlockSpec((1, TM, d), tile), pl.BlockSpec((1, TM, TOP_K), tile), _mod_spec(mods, layer, b, 1),
            pl.BlockSpec((1, d), lambda i, j: (0, 0))],
        out_specs=pl.BlockSpec((1, TM, d), lambda i, j: (i, j, 0)),
        out_shape=jax.ShapeDtypeStruct((b, t - TM, d), F32),
        compiler_params=_cparams(("arbitrary", "arbitrary")),
        name="moe_combine_final",
    )(*([y] * TOP_K), x, wcol, mods, final_w)


def _invert_kernel(dest_ref, spare_ref, out_ref):
    pltpu.sync_copy(spare_ref, out_ref)

    def place(p, carry):
        out_ref[dest_ref[p]] = p
        return carry

    lax.fori_loop(0, dest_ref.shape[0], place, 0, unroll=INVERT_UNROLL)


def _invert_slots(dest, n_slots):
    spare = dest.shape[0] + np.arange(n_slots, dtype=np.int32) % (2 * TM)
    return pl.pallas_call(
        _invert_kernel,
        in_specs=[pl.BlockSpec(memory_space=pltpu.SMEM), pl.BlockSpec(memory_space=pl.ANY)],
        out_specs=pl.BlockSpec(memory_space=pltpu.SMEM),
        out_shape=jax.ShapeDtypeStruct((n_slots,), jnp.int32),
        name="slot_invert",
    )(dest, jnp.asarray(spare))


def _slot_plan(ridx, n_tok, st):
    e = jnp.transpose(ridx[:, :TOP_K, :], (1, 0, 2)).reshape(-1)
    onehot = (e[:, None] == jnp.arange(N_EXPERTS, dtype=jnp.int32)[None, :]).astype(jnp.int32)
    incl = jnp.cumsum(onehot, axis=0)
    rank = jnp.sum((incl - onehot) * onehot, axis=1)
    counts = incl[-1]
    padded = (counts + TM - 1) // TM * TM
    pends = jnp.cumsum(padded)
    pstarts = pends - padded
    dest = pstarts[e] + rank
    n_blocks = (n_tok * TOP_K + N_EXPERTS * (TM - 1) + TM - 1) // TM
    blk_start = jnp.arange(n_blocks, dtype=jnp.int32) * TM
    blk_expert = jnp.minimum(jnp.sum((pends[None, :] <= blk_start[:, None]).astype(jnp.int32), axis=1),
                             N_EXPERTS - 1)
    n_used = (pends[-1] // TM).astype(jnp.int32).reshape(1)
    slot_dst = _invert_slots(dest.astype(jnp.int32), n_blocks * TM)
    slot_src = jnp.where(slot_dst < n_tok * TOP_K, slot_dst % n_tok, 0)
    shape3 = (n_blocks, 1, TM)
    return blk_expert.astype(jnp.int32), n_used, (slot_src * st).reshape(shape3), (slot_dst * st).reshape(shape3)


def _rope_tables(t_len, ctx_len):
    half = GLA_DK // 2
    inv = ROPE_BASE ** (-np.arange(0, half, 2, dtype=np.float32) / half)
    lane = np.arange(GLA_DK)
    p = np.arange(t_len - ctx_len)
    pos = np.where(lane[None, :] < half, (p // GRID_W)[:, None], (p % GRID_W)[:, None]).astype(np.float32)
    ang = jnp.asarray(pos) * jnp.asarray(inv[(lane % half) % (half // 2)])[None, :]
    cos, sin = jnp.cos(ang), jnp.sin(ang)
    lower = jnp.asarray((lane % half) < half // 2)[None, :]
    pad = lambda a, v: jnp.concatenate([jnp.full((ctx_len, GLA_DK), v, F32), a], axis=0)
    return pad(cos, 1.0), pad(jnp.where(lower, -sin, 0.0), 0.0), pad(jnp.where(lower, 0.0, sin), 0.0)


def kernel(x, c, ctx, c_ctx, w_ada, b_ada, norm1_w, norm2_w, w_in, w_gk_up, b_gk, gla_norm_w, rpb,
           w_bo_gla, w_bo_na, w_out, w_router, b_router, w1, w3, w2, final_norm_w):
    bsz, seq, d = x.shape
    ctx_len = ctx.shape[1]
    depth = w_ada.shape[0]
    assert ctx_len == TM and seq % TM == 0 and TM % GRID_W == 0 and seq // GRID_W >= 3 * (TM // GRID_W)
    t = ctx_len + seq
    nt = t // TM
    n_tok = bsz * t

    xs = (ctx, x)
    assert bsz + 1 <= SUBLANES and w_ada.shape[2] == N_MOD * d
    cvec = jnp.concatenate([c, c_ctx[None], jnp.zeros((SUBLANES - bsz - 1, d), F32)], axis=0)
    mods = _modulation(cvec, w_ada, b_ada).reshape(depth, SUBLANES, N_MOD, d)
    cos, sa, sb = _rope_tables(t, ctx_len)
    wrt = jnp.pad(w_router.astype(F32), ((0, 0), (0, LANES - N_EXPERTS)))
    brt = jnp.broadcast_to(b_router.astype(F32)[:, None], (N_EXPERTS, TM))
    gq_end = 2 * GLA_QK + 2 * GLA_V
    wp = (w_in[:, :, :gq_end].astype(BF16), w_in[:, :, gq_end + 2 * GLA_LR:].astype(BF16),
          jnp.pad(w_in[:, :, gq_end:gq_end + 2 * GLA_LR], ((0, 0), (0, 0), (0, LR_PAD - 2 * GLA_LR))).astype(BF16))
    wup = jnp.stack([jnp.pad(w_gk_up[:, dirn], ((0, 0), (dirn * GLA_LR, LR_PAD - (dirn + 1) * GLA_LR), (0, 0)))
                     for dirn in range(2)], axis=1).astype(BF16)
    bg = b_gk.reshape(depth, 2, 1, GLA_QK)
    nw1, nw2 = norm1_w.reshape(depth, 1, d), norm2_w.reshape(depth, 1, d)
    gnw = gla_norm_w.reshape(depth, 1, GLA_DV)
    wbg, wbn, wo = w_bo_gla.astype(BF16), w_bo_na.astype(BF16), w_out.astype(BF16)
    bias = _na_bias_tables(rpb, seq // GRID_W)

    moe = None
    for l in range(depth):
        outs = _inproj(xs, mods, nw1, wp, cos, sa, sb, l, moe)
        if moe is not None:
            xs, outs = outs[0], outs[1:]
        gq, gk, gv, gg, nq, nk, nv, m1, m2, lr = outs
        ob = _gla_pass(True, gq, gk, gv, lr, wup, bg, l)
        gy = _gla_pass(False, gq, gk, gv, lr, wup, bg, l, extra=(ob, gg, gnw))
        ny = _na(nq, nk, nv, bias, l)
        xs, hp, ridx, rw = _merge(gy, ny, m1, m2, xs, mods, nw2, wbg, wbn, wo, wrt, brt, l)
        wcol = jnp.transpose(rw[:, :TOP_K, :], (0, 2, 1))
        blk_expert, n_used, slot_src, slot_dst = _slot_plan(ridx, n_tok, d // LANES)
        y = _moe(blk_expert, n_used, slot_src, slot_dst, hp, w1, w3, w2, l)
        moe = (y, wcol)
    return _final(moe[0], xs, moe[1], mods, depth - 1, final_norm_w[None])
```

```python
import functools

import jax
import jax.numpy as jnp
import numpy as np
from jax import lax
from jax.experimental import pallas as pl
from jax.experimental.pallas import tpu as pltpu

F32 = jnp.float32
BF16 = jnp.bfloat16

EPS = 1e-6
GRID_W = 64
GLA_HEADS = 4
GLA_DK = 128
GLA_DV = 256
GLA_LR = 16
GLA_TAU = 16.0
GLA_CHUNK = 64
ROPE_BASE = 10000.0
NA_HEADS = 8
NA_HD = 64
WIN_R = 8
WIN_C = 16
N_EXPERTS = 16
N_GROUPS = 4
EXPERTS_PER_GROUP = N_EXPERTS // N_GROUPS
TOP_K = 2

GLA_QK = GLA_HEADS * GLA_DK
GLA_V = GLA_HEADS * GLA_DV
NA_W = NA_HEADS * NA_HD

N_MOD = 6
TM = 256
LANES = 128
SUBLANES = 8
INVERT_UNROLL = 32
LR_PAD = LANES
NEG = -1e30
VMEM_LIMIT = 56 * 1024 * 1024

_NT = (((1,), (1,)), ((), ()))
_TN = (((0,), (0,)), ((), ()))


def _cparams(sem):
    return pltpu.CompilerParams(dimension_semantics=sem, vmem_limit_bytes=VMEM_LIMIT)


def _dot(a, b):
    return jnp.dot(a, b, preferred_element_type=F32)


def _dg(a, b, dims):
    return lax.dot_general(a, b, dims, preferred_element_type=F32)


def _split(a):
    hi = a.astype(BF16)
    lo = (a - hi.astype(F32)).astype(BF16)
    return hi, lo


def _sigmoid(x):
    return 1.0 / (1.0 + jnp.exp(-x))


def _norm_mod(x, w, shift, scale):
    y = x * lax.rsqrt(jnp.mean(x * x, axis=-1, keepdims=True) + EPS)
    return (y * w) * (1.0 + scale) + shift


def _rows_to_tiles(ref, base, val):
    st = val.shape[1] // LANES
    for s in range(st):
        ref[pl.ds(base + s, val.shape[0], stride=st), :] = val[:, s * LANES:(s + 1) * LANES]


def _tiles_to_rows(ref, base, n, st):
    return jnp.concatenate([ref[pl.ds(base + s, n, stride=st), :] for s in range(st)], axis=1)


def _mod_kernel(c_ref, w_ref, b_ref, o_ref):
    c = c_ref[...]
    s = c * _sigmoid(c)
    sh, sl = _split(s)
    wh, wl = _split(w_ref[0])
    o_ref[0] = _dot(sh, wh) + _dot(sl, wh) + _dot(sh, wl) + b_ref[0]


def _modulation(cvec, w_ada, b_ada):
    depth, d, n = w_ada.shape
    tn = n // N_MOD
    return pl.pallas_call(
        _mod_kernel,
        grid=(depth, N_MOD),
        in_specs=[pl.BlockSpec((SUBLANES, d), lambda l, j: (0, 0)),
                  pl.BlockSpec((1, d, tn), lambda l, j: (l, 0, j)),
                  pl.BlockSpec((1, 1, tn), lambda l, j: (l, 0, j))],
        out_specs=pl.BlockSpec((1, SUBLANES, tn), lambda l, j: (l, 0, j)),
        out_shape=jax.ShapeDtypeStruct((depth, SUBLANES, n), F32),
        compiler_params=_cparams(("arbitrary", "arbitrary")),
        name="modulation",
    )(cvec, w_ada, b_ada.reshape(depth, 1, n))


_IN_OUT = (("gq", GLA_QK, BF16), ("gk", GLA_QK, BF16), ("gv", GLA_V, BF16), ("gg", GLA_V, BF16),
           ("nq", NA_W, BF16), ("nk", NA_W, BF16), ("nv", NA_W, BF16),
           ("m1", None, BF16), ("m2", None, BF16), ("lr", LR_PAD, F32))


def _moe_residual(x, y_refs, w, mod):
    y = None
    for kk, y_ref in enumerate(y_refs):
        yk = _tiles_to_rows(y_ref, 0, x.shape[0], x.shape[1] // LANES) * w[:, kk:kk + 1]
        y = yk if y is None else y + yk
    return x + mod[5:6] * y


_IN_SPLIT = (4, 9, 10)


def _stream_specs(stream, b, d):
    if isinstance(stream, tuple):
        return [pl.BlockSpec((b, TM, d), lambda j: (0, 0, 0)),
                pl.BlockSpec((b, TM, d), lambda j: (0, jnp.maximum(j - 1, 0), 0))], list(stream)
    return [pl.BlockSpec((b, TM, d), lambda j: (0, j, 0))], [stream]


def _stream_tile(x_refs, bi, is_ctx):
    if len(x_refs) == 2:
        return jnp.where(is_ctx, x_refs[0][bi], x_refs[1][bi])
    return x_refs[0][bi]


def _inproj_kernel(fused, nb, n_stream, *refs):
    if fused:
        n_planes = nb * TOP_K
        y_refs, (w_ref, pmod_ref), refs = refs[:n_planes], refs[n_planes:n_planes + 2], refs[n_planes + 2:]
    x_refs, refs = refs[:n_stream], refs[n_stream:]
    mod_ref, nw_ref, wa_ref, wb_ref, wc_ref, cos_ref, sa_ref, sb_ref = refs[:8]
    o_refs = refs[8:]
    d = x_refs[0].shape[2]
    is_ctx = pl.program_id(0) == 0
    if fused:
        xo_ref, o_refs = o_refs[0], o_refs[1:]
    cos, sa, sb = cos_ref[...], sa_ref[...], sb_ref[...]

    def rope(r, scale):
        parts = []
        for hh in range(GLA_HEADS):
            xs = r[:, hh * GLA_DK:(hh + 1) * GLA_DK]
            y = xs * cos + pltpu.roll(xs, GLA_DK - GLA_DK // 4, 1) * sa + pltpu.roll(xs, GLA_DK // 4, 1) * sb
            parts.append(y * scale if scale != 1.0 else y)
        return jnp.concatenate(parts, axis=1)

    hs = []
    for bi in range(nb):
        x = _stream_tile(x_refs, bi, is_ctx)
        if fused:
            pmod = jnp.where(is_ctx, pmod_ref[0, nb], pmod_ref[0, bi])
            x = _moe_residual(x, y_refs[bi * TOP_K:(bi + 1) * TOP_K], w_ref[bi], pmod)
            xo_ref[bi] = x
        mod = jnp.where(is_ctx, mod_ref[0, nb], mod_ref[0, bi])
        hs.append(_norm_mod(x, nw_ref[0], mod[0:1], mod[1:2]).astype(BF16))
    h = jnp.concatenate(hs, axis=0)

    off = 0
    for gi, ((name, width, dt), o_ref) in enumerate(zip(_IN_OUT, o_refs)):
        width = d if width is None else width
        if gi in _IN_SPLIT:
            off = 0
        slab_ref = wa_ref if gi < _IN_SPLIT[0] else (wb_ref if gi < _IN_SPLIT[1] else wc_ref)
        rr = _dot(h, slab_ref[0, :, off:off + width])
        for bi in range(nb):
            r = rr[bi * TM:(bi + 1) * TM]
            if name == "gq":
                r = rope(r, GLA_DK ** -0.5)
            elif name == "gk":
                r = rope(r, 1.0)
            elif name == "nq":
                r = r * (NA_HD ** -0.5)
            o_ref[bi] = r.astype(dt)
        off += width


def _layer_spec(arr, layer, *lead, single=False):
    tail = arr.shape[1 + len(lead):]
    index = (layer,) + lead + (0,) * len(tail)
    mode = dict(pipeline_mode=pl.Buffered(1)) if single else {}
    return pl.BlockSpec((1,) * (1 + len(lead)) + tail, lambda *_: index, **mode)


def _mod_spec(mods, layer, ctx_row, skip=0):
    return pl.BlockSpec((1, 1) + mods.shape[2:], lambda i, j: (layer, jnp.where(j + skip == 0, ctx_row, i), 0, 0))


def _plane_specs(b, nt, d, skip):
    def plane(kk):
        return pl.BlockSpec((TM * d // LANES, LANES), lambda i, j: ((kk * b + i) * nt + j + skip, 0))
    return [plane(kk) for kk in range(TOP_K)]


def _inproj(x, mods, nw, wp, cos, sa, sb, layer, moe=None):
    b, _, d = (x[1] if isinstance(x, tuple) else x).shape
    nt = cos.shape[0] // TM
    t = nt * TM
    widths = [d if w is None else w for _, w, _ in _IN_OUT]
    tile = lambda j: (0, j, 0)
    table = pl.BlockSpec((TM, GLA_DK), lambda j: (j, 0))
    x_specs, x_args = _stream_specs(x, b, d)
    in_specs = x_specs + [_layer_spec(mods, layer), _layer_spec(nw, layer)] + [
        _layer_spec(w, layer, single=True) for w in wp] + [table, table, table]
    args = x_args + [mods, nw, *wp, cos, sa, sb]
    out_specs = [pl.BlockSpec((b, TM, w), tile) for w in widths]
    out_shape = [jax.ShapeDtypeStruct((b, t, w), dt) for w, (_, _, dt) in zip(widths, _IN_OUT)]
    if moe is not None:
        y, wcol = moe
        planes = [pl.BlockSpec((TM * d // LANES, LANES), lambda j, r=(kk * b + bi) * nt: (r + j, 0))
                  for bi in range(b) for kk in range(TOP_K)]
        in_specs = planes + [pl.BlockSpec((b, TM, TOP_K), tile), _layer_spec(mods, layer - 1)] + in_specs
        args = [y] * len(planes) + [wcol, mods] + args
        out_specs = [pl.BlockSpec((b, TM, d), tile)] + out_specs
        out_shape = [jax.ShapeDtypeStruct((b, t, d), F32)] + out_shape
    return pl.pallas_call(
        functools.partial(_inproj_kernel, moe is not None, b, len(x_args)),
        grid=(nt,),
        in_specs=in_specs,
        out_specs=out_specs,
        out_shape=out_shape,
        compiler_params=_cparams(("arbitrary",)),
        name="inproj",
    )(*args)


def _gla_kernel(reverse, finish, q_ref, k_ref, v_ref, lr_ref, wup_ref, bg_ref, *rest):
    if finish:
        ob_ref, g_ref, nw_ref, o_ref, st_ref = rest
    else:
        o_ref, st_ref = rest
    nchunk = TM // GLA_CHUNK
    nb = q_ref.shape[0]

    @pl.when(pl.program_id(0) == 0)
    def _():
        st_ref[...] = jnp.zeros_like(st_ref)

    r = lax.broadcasted_iota(jnp.int32, (TM, TM), 0)
    s = lax.broadcasted_iota(jnp.int32, (TM, TM), 1)
    order = (s >= r) if reverse else (s <= r)
    tri = jnp.where(((r // GLA_CHUNK) == (s // GLA_CHUNK)) & order, 1.0, 0.0).astype(BF16)
    cmask = order[:GLA_CHUNK, :GLA_CHUNK]
    bcs = []
    for bi in range(nb):
        z = _dot(lr_ref[bi].astype(BF16), wup_ref[0, 0]) + bg_ref[0, 0]
        logg = (jnp.minimum(z, 0.0) - jnp.log(1.0 + jnp.exp(-jnp.abs(z)))) * (1.0 / GLA_TAU)
        hi, lo = _split(logg)
        bcs.append(_dot(tri, hi) + _dot(tri, lo))

    for c in (range(nchunk - 1, -1, -1) if reverse else range(nchunk)):
        rs = slice(c * GLA_CHUNK, (c + 1) * GLA_CHUNK)
        for bi in range(nb):
            for hh in range(GLA_HEADS):
                ls = slice(hh * GLA_DK, (hh + 1) * GLA_DK)
                vs = slice(hh * GLA_DV, (hh + 1) * GLA_DV)
                b = bcs[bi][rs, ls]
                bend = b[0:1] if reverse else b[GLA_CHUNK - 1:GLA_CHUNK]
                qc = q_ref[bi, rs, ls].astype(F32)
                kc = k_ref[bi, rs, ls].astype(F32)
                vc = v_ref[bi, rs, vs]
                kd = (kc * jnp.exp(bend - b)).astype(BF16)
                qi = (qc * jnp.exp(b)).astype(BF16)
                qa = (qc * jnp.exp(b - bend)).astype(BF16)
                att = jnp.where(cmask, _dg(qa, kd, _NT), 0.0).astype(BF16)
                st = st_ref[bi * GLA_HEADS + hh]
                o = _dg(qi, st.astype(BF16), _NT) + _dot(att, vc)
                st_ref[bi * GLA_HEADS + hh] = st * jnp.exp(bend) + _dg(vc, kd, _TN)
                if finish:
                    o = o + ob_ref[bi, rs, vs].astype(F32)
                    o = o * lax.rsqrt(jnp.mean(o * o, axis=-1, keepdims=True) + EPS) * nw_ref[0]
                    g = g_ref[bi, rs, vs].astype(F32)
                    o = o * (g * _sigmoid(g))
                o_ref[bi, rs, vs] = o.astype(o_ref.dtype)


def _gla_pass(reverse, q, k, v, lr, wup, bg, layer, extra=None):
    b, t, _ = q.shape
    nt = t // TM
    dirn = 1 if reverse else 0
    if reverse:
        tile = lambda j: (0, jnp.where(j == 0, 0, nt - j), 0)
    else:
        tile = lambda j: (0, j, 0)
    in_specs = [pl.BlockSpec((b, TM, GLA_QK), tile), pl.BlockSpec((b, TM, GLA_QK), tile),
                pl.BlockSpec((b, TM, GLA_V), tile), pl.BlockSpec((b, TM, LR_PAD), tile),
                _layer_spec(wup, layer, dirn), _layer_spec(bg, layer, dirn)]
    args = [q, k, v, lr, wup, bg]
    if extra is not None:
        ob, g, nw = extra
        in_specs += [pl.BlockSpec((b, TM, GLA_V), tile), pl.BlockSpec((b, TM, GLA_V), tile),
                     _layer_spec(nw, layer)]
        args += [ob, g, nw]
    return pl.pallas_call(
        functools.partial(_gla_kernel, reverse, extra is not None),
        grid=(nt,),
        in_specs=in_specs,
        out_specs=pl.BlockSpec((b, TM, GLA_V), tile),
        out_shape=jax.ShapeDtypeStruct((b, t, GLA_V), BF16),
        scratch_shapes=[pltpu.VMEM((b * GLA_HEADS, GLA_DV, GLA_DK), F32)],
        compiler_params=_cparams(("arbitrary",)),
        name="gla_bwd" if reverse else "gla_fwd",
    )(*args)


def _na_bias_tables(rpb, rows):
    rpt = TM // GRID_W
    kr = min(WIN_R, rows)
    col = np.arange(GRID_W)
    cidx = np.clip(col[None, :] - col[:, None], -(WIN_C - 1), WIN_C - 1) + (WIN_C - 1)
    c_sel = (cidx[None] == np.arange(2 * WIN_C - 1)[:, None, None]).astype(np.float32)
    c0 = np.clip(col - WIN_C // 2, 0, GRID_W - WIN_C)
    c_ok = (col[None, :] >= c0[:, None]) & (col[None, :] < c0[:, None] + WIN_C)
    toep = jnp.einsum("lhrd,dqk->lhrqk", rpb.astype(F32), c_sel, precision=lax.Precision.HIGHEST)
    toep = jnp.where(c_ok, toep, NEG)
    depth = rpb.shape[0]
    kinds = ((0, 0), (rpt, 0), (rows - rpt, rows - 3 * rpt))

    def assemble(toep_ref, o_ref):
        kind = pl.program_id(1)
        masked = jnp.full((NA_HEADS, GRID_W, GRID_W), NEG, F32)

        @pl.when(kind == 0)
        def _():
            o_ref[0, 0] = jnp.full(o_ref.shape[2:], NEG, F32)

        for k, (r_base, u_base) in enumerate(kinds, start=1):
            @pl.when(kind == k)
            def _(r_base=r_base, u_base=u_base):
                for i in range(rpt):
                    rq = r_base + i
                    r0 = min(max(rq - kr // 2, 0), rows - kr)
                    for m in range(3 * rpt):
                        rk = u_base + m
                        blk = toep_ref[0, :, rk - rq + (WIN_R - 1)] if r0 <= rk < r0 + kr else masked
                        o_ref[0, 0, :, i * GRID_W:(i + 1) * GRID_W, m * GRID_W:(m + 1) * GRID_W] = blk

    return pl.pallas_call(
        assemble,
        grid=(depth, 1 + len(kinds)),
        in_specs=[pl.BlockSpec((1,) + toep.shape[1:], lambda l, k: (l, 0, 0, 0, 0))],
        out_specs=pl.BlockSpec((1, 1, NA_HEADS, TM, 3 * TM), lambda l, k: (l, k, 0, 0, 0)),
        out_shape=jax.ShapeDtypeStruct((depth, 1 + len(kinds), NA_HEADS, TM, 3 * TM), F32),
        compiler_params=_cparams(("arbitrary", "arbitrary")),
        name="natten_bias",
    )(toep)


def _na_kernel(q_ref, kp_ref, kc_ref, kn_ref, kx_ref, vp_ref, vc_ref, vn_ref, vx_ref, bias_ref, o_ref):
    lane = lax.broadcasted_iota(jnp.int32, (TM, 2 * NA_HD), 1)
    first = lane < NA_HD
    k_refs = (kp_ref, kc_ref, kn_ref, kx_ref)
    v_refs = (vp_ref, vc_ref, vn_ref, vx_ref)
    for bi in range(q_ref.shape[0]):
        for hp in range(NA_HEADS // 2):
            ls = slice(hp * 2 * NA_HD, (hp + 1) * 2 * NA_HD)
            q2 = q_ref[bi, :, ls]
            ks = [kr[bi, :, ls] for kr in k_refs]
            vs = [vr[bi, :, ls] for vr in v_refs]
            outs = []
            for sub in range(2):
                hd = 2 * hp + sub
                own = first if sub == 0 else jnp.logical_not(first)
                qm = jnp.where(own, q2, jnp.zeros_like(q2))
                sc = []
                for i in range(4):
                    s = _dg(qm, ks[i], _NT)
                    if i < 3:
                        s = s + bias_ref[0, 0, hd, :, i * TM:(i + 1) * TM]
                    sc.append(s)
                m = jnp.max(sc[0], axis=-1, keepdims=True)
                for i in range(1, 4):
                    m = jnp.maximum(m, jnp.max(sc[i], axis=-1, keepdims=True))
                acc = jnp.zeros((TM, 2 * NA_HD), F32)
                for i in range(4):
                    p = jnp.exp((sc[i] - m).astype(BF16))
                    acc = acc + _dot(p, jnp.where(own, vs[i], jnp.ones_like(vs[i])))
                l = acc[:, NA_HD:NA_HD + 1] if sub == 0 else acc[:, 0:1]
                outs.append(acc / l)
            o_ref[bi, :, ls] = jnp.where(first, outs[0], outs[1]).astype(o_ref.dtype)


def _na(q, k, v, bias, layer):
    b, t, _ = q.shape
    nt = t // TM
    qt = lambda j: (0, j, 0)
    ctx = lambda j: (0, 0, 0)

    def near(o):
        return lambda j: (0, jnp.clip(j, 2, nt - 2) + o, 0)

    def kind(j):
        return (layer, jnp.where(j == 0, 0, jnp.where(j == 1, 1, jnp.where(j == nt - 1, 3, 2))), 0, 0, 0)

    blk = lambda f: pl.BlockSpec((b, TM, NA_W), f)
    return pl.pallas_call(
        _na_kernel,
        grid=(nt,),
        in_specs=[blk(qt), blk(near(-1)), blk(near(0)), blk(near(1)), blk(ctx),
                  blk(near(-1)), blk(near(0)), blk(near(1)), blk(ctx),
                  pl.BlockSpec((1, 1, NA_HEADS, TM, 3 * TM), kind)],
        out_specs=blk(qt),
        out_shape=jax.ShapeDtypeStruct((b, t, NA_W), BF16),
        compiler_params=_cparams(("arbitrary",)),
        name="natten",
    )(q, k, k, k, k, v, v, v, v, bias)


def _route(sel, aff):
    rows = lambda a, e: a[e:e + 1, :]
    gscore = []
    for g in range(N_GROUPS):
        a, b, c, d = (rows(sel, EXPERTS_PER_GROUP * g + i) for i in range(EXPERTS_PER_GROUP))
        hi1, lo1 = jnp.maximum(a, b), jnp.minimum(a, b)
        hi2, lo2 = jnp.maximum(c, d), jnp.minimum(c, d)
        gscore.append(jnp.maximum(hi1, hi2) + jnp.maximum(jnp.minimum(hi1, hi2), jnp.maximum(lo1, lo2)))
    gbest = jnp.zeros_like(gscore[0], dtype=jnp.int32)
    gval = gscore[0]
    for g in range(1, N_GROUPS):
        better = gscore[g] > gval
        gbest = jnp.where(better, g, gbest)
        gval = jnp.where(better, gscore[g], gval)
    cs, ca = [], []
    for i in range(EXPERTS_PER_GROUP):
        s_i, a_i = rows(sel, i), rows(aff, i)
        for g in range(1, N_GROUPS):
            pick = gbest == g
            s_i = jnp.where(pick, rows(sel, EXPERTS_PER_GROUP * g + i), s_i)
            a_i = jnp.where(pick, rows(aff, EXPERTS_PER_GROUP * g + i), a_i)
        cs.append(s_i)
        ca.append(a_i)
    i1 = jnp.zeros_like(gbest)
    v1, w1 = cs[0], ca[0]
    for i in range(1, EXPERTS_PER_GROUP):
        better = cs[i] > v1
        i1 = jnp.where(better, i, i1)
        v1 = jnp.where(better, cs[i], v1)
        w1 = jnp.where(better, ca[i], w1)
    i2 = jnp.full_like(gbest, -1)
    v2 = jnp.full_like(v1, -jnp.inf)
    w2 = jnp.zeros_like(w1)
    for i in range(EXPERTS_PER_GROUP):
        better = (i1 != i) & ((cs[i] > v2) | (i2 < 0))
        i2 = jnp.where(better, i, i2)
        v2 = jnp.where(better, cs[i], v2)
        w2 = jnp.where(better, ca[i], w2)
    tot = w1 + w2
    base = gbest * EXPERTS_PER_GROUP
    return base + i1, base + i2, w1 / tot, w2 / tot


def _merge_kernel(n_stream, gy_ref, ny_ref, m1_ref, m2_ref, *refs):
    x_refs, refs = refs[:n_stream], refs[n_stream:]
    mod_ref, nw_ref, wbg_ref, wbn_ref, wo_ref, wr_ref, br_ref, xo_ref, hp_ref, ri_ref, rw_ref = refs
    nb = gy_ref.shape[0]
    is_ctx = pl.program_id(0) == 0
    wh, wl = _split(wr_ref[...])
    zi = jnp.zeros((SUBLANES - TOP_K, TM), jnp.int32)
    for bi in range(nb):
        a = _dot(gy_ref[bi], wbg_ref[0])
        b = _dot(ny_ref[bi], wbn_ref[0])
        m = _sigmoid(m1_ref[bi].astype(F32)) * a + _sigmoid(m2_ref[bi].astype(F32)) * b
        y = _dot(m.astype(BF16), wo_ref[0])
        mod = jnp.where(is_ctx, mod_ref[0, nb], mod_ref[0, bi])
        xn = _stream_tile(x_refs, bi, is_ctx) + mod[2:3] * y
        xo_ref[bi] = xn
        h2 = _norm_mod(xn, nw_ref[0], mod[3:4], mod[4:5])
        _rows_to_tiles(hp_ref.at[bi], 0, h2)
        hh, hl = _split(h2)
        logit = (_dot(hh, wh) + _dot(hl, wh) + _dot(hh, wl)).T[:N_EXPERTS]
        aff = _sigmoid(logit)
        i1, i2, w1, w2 = _route(aff + br_ref[...], aff)
        ri_ref[bi] = jnp.concatenate([i1, i2, zi], axis=0)
        rw_ref[bi] = jnp.concatenate([w1, w2, zi.astype(F32)], axis=0)


def _merge(gy, ny, m1, m2, x, mods, nw2, wbg, wbn, wo, wrt, brt, layer):
    b, t, d = m1.shape
    nt = t // TM
    rows = TM * d // LANES
    tile = lambda j: (0, j, 0)
    const = lambda j: (0, 0)
    x_specs, x_args = _stream_specs(x, b, d)
    xo, hp, ri, rw = pl.pallas_call(
        functools.partial(_merge_kernel, len(x_args)),
        grid=(nt,),
        in_specs=[pl.BlockSpec((b, TM, GLA_V), tile), pl.BlockSpec((b, TM, NA_W), tile),
                  pl.BlockSpec((b, TM, d), tile), pl.BlockSpec((b, TM, d), tile)] + x_specs + [
                  _layer_spec(mods, layer), _layer_spec(nw2, layer),
                  _layer_spec(wbg, layer), _layer_spec(wbn, layer), _layer_spec(wo, layer),
                  pl.BlockSpec((d, LANES), const), pl.BlockSpec((N_EXPERTS, TM), const)],
        out_specs=[pl.BlockSpec((b, TM, d), tile), pl.BlockSpec((b, rows, LANES), tile),
                   pl.BlockSpec((b, SUBLANES, TM), lambda j: (0, 0, j)),
                   pl.BlockSpec((b, SUBLANES, TM), lambda j: (0, 0, j))],
        out_shape=[jax.ShapeDtypeStruct((b, t, d), F32), jax.ShapeDtypeStruct((b, nt * rows, LANES), F32),
                   jax.ShapeDtypeStruct((b, SUBLANES, t), jnp.int32), jax.ShapeDtypeStruct((b, SUBLANES, t), F32)],
        compiler_params=_cparams(("arbitrary",)),
        name="merge_router",
    )(gy, ny, m1, m2, *x_args, mods, nw2, wbg, wbn, wo, wrt, brt)
    return xo, hp.reshape(b * nt * rows, LANES), ri, rw


def _moe_kernel(be_ref, nu_ref, src_cur, src_nxt, src_nx2, dst_cur, dst_prv, h_ref, w1_ref, w3_ref, w2_ref,
                o_ref, xbuf, ybuf, w1b, w3b, w2b, gsem, ssem):
    i = pl.program_id(0)
    nu = nu_ref[0]
    xb = lax.rem(i, 3)
    yb = lax.rem(i, 3)
    used = i < nu
    st = w1_ref.shape[2] // LANES
    blk = TM * st
    n_real = o_ref.shape[0] - 2 * blk

    def part(ref, b):
        return ref.at[pl.ds(pl.multiple_of(b * blk, blk), blk)]

    def token(ref, row):
        return ref.at[pl.ds(pl.multiple_of(row, st), st)]

    def start_gather(idx_ref, b):
        for r in range(TM):
            pltpu.make_async_copy(token(h_ref, idx_ref[0, 0, r]), token(xbuf, b * blk + r * st),
                                  gsem.at[b]).start(priority=r % 2)

    def start_scatter(idx_ref, b):
        for r in range(TM):
            pltpu.make_async_copy(token(ybuf, b * blk + r * st), token(o_ref, idx_ref[0, 0, r]),
                                  ssem.at[b]).start(priority=r % 2)

    def wait_gather(b):
        pltpu.make_async_copy(h_ref.at[pl.ds(0, blk)], part(xbuf, b), gsem.at[b]).wait()

    def wait_scatter(b):
        pltpu.make_async_copy(part(ybuf, b), o_ref.at[pl.ds(0, blk)], ssem.at[b]).wait()

    @pl.when(i == 0)
    def _():
        xbuf[...] = jnp.zeros_like(xbuf)
        ybuf[...] = jnp.zeros_like(ybuf)
        for b in range(2):
            spare = pltpu.make_async_copy(part(xbuf, b), o_ref.at[pl.ds(n_real + b * blk, blk)], ssem.at[b])
            spare.start()
            spare.wait()
        start_gather(src_cur, 0)
        start_gather(src_nxt, 1)

    fresh = jnp.logical_or(i == 0, be_ref[i] != be_ref[jnp.maximum(i - 1, 0)])

    @pl.when(jnp.logical_and(used, fresh))
    def _():
        for src, dst in ((w1_ref, w1b), (w3_ref, w3b), (w2_ref, w2b)):
            def body(r, carry, src=src, dst=dst):
                rows = pl.ds(pl.multiple_of(r * LANES, LANES), LANES)
                dst[rows, :] = src[0, 0, rows, :].astype(BF16)
                return carry
            lax.fori_loop(0, src.shape[2] // LANES, body, 0)

    @pl.when(jnp.logical_and(used, i > 0))
    def _():
        wait_scatter(lax.rem(i + 1, 3))

    @pl.when(used)
    def _():
        wait_gather(xb)
        x = _tiles_to_rows(xbuf, xb * blk, TM, st).astype(BF16)
        start_gather(src_nx2, lax.rem(i + 2, 3))
        start_scatter(dst_prv, lax.rem(i + 2, 3))
        a = _dot(x, w1b[...])
        b = _dot(x, w3b[...])
        hmid = (a * _sigmoid(a)) * b
        _rows_to_tiles(ybuf, yb * blk, _dot(hmid.astype(BF16), w2b[...]))

    @pl.when(i == nu - 1)
    def _():
        start_scatter(dst_cur, yb)
        wait_scatter(yb)
        wait_scatter(lax.rem(i + 2, 3))
        wait_gather(lax.rem(i + 1, 3))
        wait_gather(lax.rem(i + 2, 3))


def _moe(blk_expert, n_used, slot_src, slot_dst, h, w1, w3, w2, layer):
    d, f = w1.shape[2], w1.shape[3]
    st = d // LANES
    blk = TM * st
    n_rows = h.shape[0]
    nb = blk_expert.shape[0]
    wsel = lambda i, be, nu: (layer, be[i], 0, 0)
    smem = lambda f_: pl.BlockSpec((1, 1, TM), f_, memory_space=pltpu.SMEM)
    cur = lambda i, be, nu: (i, 0, 0)
    nxt = lambda i, be, nu: (jnp.minimum(i + 1, nb - 1), 0, 0)
    nx2 = lambda i, be, nu: (jnp.minimum(i + 2, nb - 1), 0, 0)
    prv = lambda i, be, nu: (jnp.maximum(i - 1, 0), 0, 0)
    return pl.pallas_call(
        _moe_kernel,
        grid_spec=pltpu.PrefetchScalarGridSpec(
            num_scalar_prefetch=2,
            grid=(nb,),
            in_specs=[smem(cur), smem(nxt), smem(nx2), smem(cur), smem(prv),
                      pl.BlockSpec(memory_space=pl.ANY), pl.BlockSpec((1, 1, d, f), wsel),
                      pl.BlockSpec((1, 1, d, f), wsel), pl.BlockSpec((1, 1, f, d), wsel)],
            out_specs=pl.BlockSpec(memory_space=pl.ANY),
            scratch_shapes=[pltpu.VMEM((3 * blk, LANES), F32), pltpu.VMEM((3 * blk, LANES), F32),
                            pltpu.VMEM((d, f), BF16), pltpu.VMEM((d, f), BF16), pltpu.VMEM((f, d), BF16),
                            pltpu.SemaphoreType.DMA((3,)), pltpu.SemaphoreType.DMA((3,))]),
        out_shape=jax.ShapeDtypeStruct((TOP_K * n_rows + 2 * blk, LANES), F32),
        compiler_params=_cparams(("arbitrary",)),
        name="moe_experts",
    )(blk_expert, n_used, slot_src, slot_src, slot_src, slot_dst, slot_dst, h, w1, w3, w2)


def _final_kernel(*refs):
    y_refs, (x_ref, w_ref, mod_ref, fw_ref, o_ref) = refs[:TOP_K], refs[TOP_K:]
    xn = _moe_residual(x_ref[0], y_refs, w_ref[0], mod_ref[0, 0])
    o_ref[0] = xn * lax.rsqrt(jnp.mean(xn * xn, axis=-1, keepdims=True) + EPS) * fw_ref[...]


def _final(y, x, wcol, mods, layer, final_w):
    b, t, d = x.shape
    nt = t // TM
    tile = lambda i, j: (i, j + 1, 0)
    return pl.pallas_call(
        _final_kernel,
        grid=(b, nt - 1),
        in_specs=_plane_specs(b, nt, d, 1) + [
            pl.BlockSpec((1, TM, d), tile), pl.BlockSpec((1, TM, TOP_K), tile), _mod_spec(mods, layer, b, 1),
            pl.BlockSpec((1, d), lambda i, j: (0, 0))],
        out_specs=pl.BlockSpec((1, TM, d), lambda i, j: (i, j, 0)),
        out_shape=jax.ShapeDtypeStruct((b, t - TM, d), F32),
        compiler_params=_cparams(("arbitrary", "arbitrary")),
        name="moe_combine_final",
    )(*([y] * TOP_K), x, wcol, mods, final_w)


def _invert_kernel(dest_ref, spare_ref, out_ref):
    pltpu.sync_copy(spare_ref, out_ref)

    def place(p, carry):
        out_ref[dest_ref[p]] = p
        return carry

    lax.fori_loop(0, dest_ref.shape[0], place, 0, unroll=INVERT_UNROLL)


def _invert_slots(dest, n_slots):
    spare = dest.shape[0] + np.arange(n_slots, dtype=np.int32) % (2 * TM)
    return pl.pallas_call(
        _invert_kernel,
        in_specs=[pl.BlockSpec(memory_space=pltpu.SMEM), pl.BlockSpec(memory_space=pl.ANY)],
        out_specs=pl.BlockSpec(memory_space=pltpu.SMEM),
        out_shape=jax.ShapeDtypeStruct((n_slots,), jnp.int32),
        name="slot_invert",
    )(dest, jnp.asarray(spare))


def _slot_plan(ridx, n_tok, st):
    e = jnp.transpose(ridx[:, :TOP_K, :], (1, 0, 2)).reshape(-1)
    onehot = (e[:, None] == jnp.arange(N_EXPERTS, dtype=jnp.int32)[None, :]).astype(jnp.int32)
    incl = jnp.cumsum(onehot, axis=0)
    rank = jnp.sum((incl - onehot) * onehot, axis=1)
    counts = incl[-1]
    padded = (counts + TM - 1) // TM * TM
    pends = jnp.cumsum(padded)
    pstarts = pends - padded
    dest = pstarts[e] + rank
    n_blocks = (n_tok * TOP_K + N_EXPERTS * (TM - 1) + TM - 1) // TM
    blk_start = jnp.arange(n_blocks, dtype=jnp.int32) * TM
    blk_expert = jnp.minimum(jnp.sum((pends[None, :] <= blk_start[:, None]).astype(jnp.int32), axis=1),
                             N_EXPERTS - 1)
    n_used = (pends[-1] // TM).astype(jnp.int32).reshape(1)
    slot_dst = _invert_slots(dest.astype(jnp.int32), n_blocks * TM)
    slot_src = jnp.where(slot_dst < n_tok * TOP_K, slot_dst % n_tok, 0)
    shape3 = (n_blocks, 1, TM)
    return blk_expert.astype(jnp.int32), n_used, (slot_src * st).reshape(shape3), (slot_dst * st).reshape(shape3)


def _rope_tables(t_len, ctx_len):
    half = GLA_DK // 2
    inv = ROPE_BASE ** (-np.arange(0, half, 2, dtype=np.float32) / half)
    lane = np.arange(GLA_DK)
    p = np.arange(t_len - ctx_len)
    pos = np.where(lane[None, :] < half, (p // GRID_W)[:, None], (p % GRID_W)[:, None]).astype(np.float32)
    ang = jnp.asarray(pos) * jnp.asarray(inv[(lane % half) % (half // 2)])[None, :]
    cos, sin = jnp.cos(ang), jnp.sin(ang)
    lower = jnp.asarray((lane % half) < half // 2)[None, :]
    pad = lambda a, v: jnp.concatenate([jnp.full((ctx_len, GLA_DK), v, F32), a], axis=0)
    return pad(cos, 1.0), pad(jnp.where(lower, -sin, 0.0), 0.0), pad(jnp.where(lower, 0.0, sin), 0.0)


def kernel(x, c, ctx, c_ctx, w_ada, b_ada, norm1_w, norm2_w, w_in, w_gk_up, b_gk, gla_norm_w, rpb,
           w_bo_gla, w_bo_na, w_out, w_router, b_router, w1, w3, w2, final_norm_w):
    bsz, seq, d = x.shape
    ctx_len = ctx.shape[1]
    depth = w_ada.shape[0]
    assert ctx_len == TM and seq % TM == 0 and TM % GRID_W == 0 and seq // GRID_W >= 3 * (TM // GRID_W)
    t = ctx_len + seq
    nt = t // TM
    n_tok = bsz * t

    xs = (ctx, x)
    assert bsz + 1 <= SUBLANES and w_ada.shape[2] == N_MOD * d
    cvec = jnp.concatenate([c, c_ctx[None], jnp.zeros((SUBLANES - bsz - 1, d), F32)], axis=0)
    mods = _modulation(cvec, w_ada, b_ada).reshape(depth, SUBLANES, N_MOD, d)
    cos, sa, sb = _rope_tables(t, ctx_len)
    wrt = jnp.pad(w_router.astype(F32), ((0, 0), (0, LANES - N_EXPERTS)))
    brt = jnp.broadcast_to(b_router.astype(F32)[:, None], (N_EXPERTS, TM))
    gq_end = 2 * GLA_QK + 2 * GLA_V
    wp = (w_in[:, :, :gq_end].astype(BF16), w_in[:, :, gq_end + 2 * GLA_LR:].astype(BF16),
          jnp.pad(w_in[:, :, gq_end:gq_end + 2 * GLA_LR], ((0, 0), (0, 0), (0, LR_PAD - 2 * GLA_LR))).astype(BF16))
    wup = jnp.stack([jnp.pad(w_gk_up[:, dirn], ((0, 0), (dirn * GLA_LR, LR_PAD - (dirn + 1) * GLA_LR), (0, 0)))
                     for dirn in range(2)], axis=1).astype(BF16)
    bg = b_gk.reshape(depth, 2, 1, GLA_QK)
    nw1, nw2 = norm1_w.reshape(depth, 1, d), norm2_w.reshape(depth, 1, d)
    gnw = gla_norm_w.reshape(depth, 1, GLA_DV)
    wbg, wbn, wo = w_bo_gla.astype(BF16), w_bo_na.astype(BF16), w_out.astype(BF16)
    bias = _na_bias_tables(rpb, seq // GRID_W)

    moe = None
    for l in range(depth):
        outs = _inproj(xs, mods, nw1, wp, cos, sa, sb, l, moe)
        if moe is not None:
            xs, outs = outs[0], outs[1:]
        gq, gk, gv, gg, nq, nk, nv, m1, m2, lr = outs
        ob = _gla_pass(True, gq, gk, gv, lr, wup, bg, l)
        gy = _gla_pass(False, gq, gk, gv, lr, wup, bg, l, extra=(ob, gg, gnw))
        ny = _na(nq, nk, nv, bias, l)
        xs, hp, ridx, rw = _merge(gy, ny, m1, m2, xs, mods, nw2, wbg, wbn, wo, wrt, brt, l)
        wcol = jnp.transpose(rw[:, :TOP_K, :], (0, 2, 1))
        blk_expert, n_used, slot_src, slot_dst = _slot_plan(ridx, n_tok, d // LANES)
        y = _moe(blk_expert, n_used, slot_src, slot_dst, hp, w1, w3, w2, l)
        moe = (y, wcol)
    return _final(moe[0], xs, moe[1], mods, depth - 1, final_norm_w[None])
```

```python
import functools

import jax
import jax.numpy as jnp
import numpy as np
from jax import lax
from jax.experimental import pallas as pl
from jax.experimental.pallas import tpu as pltpu

F32 = jnp.float32
BF16 = jnp.bfloat16

EPS = 1e-6
GRID_W = 64
GLA_HEADS = 4
GLA_DK = 128
GLA_DV = 256
GLA_LR = 16
GLA_TAU = 16.0
GLA_CHUNK = 64
ROPE_BASE = 10000.0
NA_HEADS = 8
NA_HD = 64
WIN_R = 8
WIN_C = 16
N_EXPERTS = 16
N_GROUPS = 4
EXPERTS_PER_GROUP = N_EXPERTS // N_GROUPS
TOP_K = 2

GLA_QK = GLA_HEADS * GLA_DK
GLA_V = GLA_HEADS * GLA_DV
NA_W = NA_HEADS * NA_HD

N_MOD = 6
TM = 256
LANES = 128
SUBLANES = 8
INVERT_UNROLL = 32
LR_PAD = LANES
NEG = -1e30
VMEM_LIMIT = 56 * 1024 * 1024

_NT = (((1,), (1,)), ((), ()))
_TN = (((0,), (0,)), ((), ()))


def _cparams(sem):
    return pltpu.CompilerParams(dimension_semantics=sem, vmem_limit_bytes=VMEM_LIMIT)


def _dot(a, b):
    return jnp.dot(a, b, preferred_element_type=F32)


def _dg(a, b, dims):
    return lax.dot_general(a, b, dims, preferred_element_type=F32)


def _split(a):
    hi = a.astype(BF16)
    lo = (a - hi.astype(F32)).astype(BF16)
    return hi, lo


def _sigmoid(x):
    return 1.0 / (1.0 + jnp.exp(-x))


def _norm_mod(x, w, shift, scale):
    y = x * lax.rsqrt(jnp.mean(x * x, axis=-1, keepdims=True) + EPS)
    return (y * w) * (1.0 + scale) + shift


def _rows_to_tiles(ref, base, val):
    st = val.shape[1] // LANES
    for s in range(st):
        ref[pl.ds(base + s, val.shape[0], stride=st), :] = val[:, s * LANES:(s + 1) * LANES]


def _tiles_to_rows(ref, base, n, st):
    return jnp.concatenate([ref[pl.ds(base + s, n, stride=st), :] for s in range(st)], axis=1)


def _mod_kernel(c_ref, w_ref, b_ref, o_ref):
    c = c_ref[...]
    s = c * _sigmoid(c)
    sh, sl = _split(s)
    wh, wl = _split(w_ref[0])
    o_ref[0] = _dot(sh, wh) + _dot(sl, wh) + _dot(sh, wl) + b_ref[0]


def _modulation(cvec, w_ada, b_ada):
    depth, d, n = w_ada.shape
    tn = n // N_MOD
    return pl.pallas_call(
        _mod_kernel,
        grid=(depth, N_MOD),
        in_specs=[pl.BlockSpec((SUBLANES, d), lambda l, j: (0, 0)),
                  pl.BlockSpec((1, d, tn), lambda l, j: (l, 0, j)),
                  pl.BlockSpec((1, 1, tn), lambda l, j: (l, 0, j))],
        out_specs=pl.BlockSpec((1, SUBLANES, tn), lambda l, j: (l, 0, j)),
        out_shape=jax.ShapeDtypeStruct((depth, SUBLANES, n), F32),
        compiler_params=_cparams(("arbitrary", "arbitrary")),
        name="modulation",
    )(cvec, w_ada, b_ada.reshape(depth, 1, n))


_IN_OUT = (("gq", GLA_QK, BF16), ("gk", GLA_QK, BF16), ("gv", GLA_V, BF16), ("gg", GLA_V, BF16),
           ("nq", NA_W, BF16), ("nk", NA_W, BF16), ("nv", NA_W, BF16),
           ("m1", None, BF16), ("m2", None, BF16), ("lr", LR_PAD, F32))


def _moe_residual(x, y_refs, w, mod):
    y = None
    for kk, y_ref in enumerate(y_refs):
        yk = _tiles_to_rows(y_ref, 0, x.shape[0], x.shape[1] // LANES) * w[:, kk:kk + 1]
        y = yk if y is None else y + yk
    return x + mod[5:6] * y


_IN_SPLIT = (4, 9, 10)


def _stream_specs(stream, b, d):
    if isinstance(stream, tuple):
        return [pl.BlockSpec((b, TM, d), lambda j: (0, 0, 0)),
                pl.BlockSpec((b, TM, d), lambda j: (0, jnp.maximum(j - 1, 0), 0))], list(stream)
    return [pl.BlockSpec((b, TM, d), lambda j: (0, j, 0))], [stream]


def _stream_tile(x_refs, bi, is_ctx):
    if len(x_refs) == 2:
        return jnp.where(is_ctx, x_refs[0][bi], x_refs[1][bi])
    return x_refs[0][bi]


def _inproj_kernel(fused, nb, n_stream, *refs):
    if fused:
        n_planes = nb * TOP_K
        y_refs, (w_ref, pmod_ref), refs = refs[:n_planes], refs[n_planes:n_planes + 2], refs[n_planes + 2:]
    x_refs, refs = refs[:n_stream], refs[n_stream:]
    mod_ref, nw_ref, wa_ref, wb_ref, wc_ref, cos_ref, sa_ref, sb_ref = refs[:8]
    o_refs = refs[8:]
    d = x_refs[0].shape[2]
    is_ctx = pl.program_id(0) == 0
    if fused:
        xo_ref, o_refs = o_refs[0], o_refs[1:]
    cos, sa, sb = cos_ref[...], sa_ref[...], sb_ref[...]

    def rope(r, scale):
        parts = []
        for hh in range(GLA_HEADS):
            xs = r[:, hh * GLA_DK:(hh + 1) * GLA_DK]
            y = xs * cos + pltpu.roll(xs, GLA_DK - GLA_DK // 4, 1) * sa + pltpu.roll(xs, GLA_DK // 4, 1) * sb
            parts.append(y * scale if scale != 1.0 else y)
        return jnp.concatenate(parts, axis=1)

    hs = []
    for bi in range(nb):
        x = _stream_tile(x_refs, bi, is_ctx)
        if fused:
            pmod = jnp.where(is_ctx, pmod_ref[0, nb], pmod_ref[0, bi])
            x = _moe_residual(x, y_refs[bi * TOP_K:(bi + 1) * TOP_K], w_ref[bi], pmod)
            xo_ref[bi] = x
        mod = jnp.where(is_ctx, mod_ref[0, nb], mod_ref[0, bi])
        hs.append(_norm_mod(x, nw_ref[0], mod[0:1], mod[1:2]).astype(BF16))
    h = jnp.concatenate(hs, axis=0)

    off = 0
    for gi, ((name, width, dt), o_ref) in enumerate(zip(_IN_OUT, o_refs)):
        width = d if width is None else width
        if gi in _IN_SPLIT:
            off = 0
        slab_ref = wa_ref if gi < _IN_SPLIT[0] else (wb_ref if gi < _IN_SPLIT[1] else wc_ref)
        rr = _dot(h, slab_ref[0, :, off:off + width])
        for bi in range(nb):
            r = rr[bi * TM:(bi + 1) * TM]
            if name == "gq":
                r = rope(r, GLA_DK ** -0.5)
            elif name == "gk":
                r = rope(r, 1.0)
            elif name == "nq":
                r = r * (NA_HD ** -0.5)
            o_ref[bi] = r.astype(dt)
        off += width


def _layer_spec(arr, layer, *lead, single=False):
    tail = arr.shape[1 + len(lead):]
    index = (layer,) + lead + (0,) * len(tail)
    mode = dict(pipeline_mode=pl.Buffered(1)) if single else {}
    return pl.BlockSpec((1,) * (1 + len(lead)) + tail, lambda *_: index, **mode)


def _mod_spec(mods, layer, ctx_row, skip=0):
    return pl.BlockSpec((1, 1) + mods.shape[2:], lambda i, j: (layer, jnp.where(j + skip == 0, ctx_row, i), 0, 0))


def _plane_specs(b, nt, d, skip):
    def plane(kk):
        return pl.BlockSpec((TM * d // LANES, LANES), lambda i, j: ((kk * b + i) * nt + j + skip, 0))
    return [plane(kk) for kk in range(TOP_K)]


def _inproj(x, mods, nw, wp, cos, sa, sb, layer, moe=None):
    b, _, d = (x[1] if isinstance(x, tuple) else x).shape
    nt = cos.shape[0] // TM
    t = nt * TM
    widths = [d if w is None else w for _, w, _ in _IN_OUT]
    tile = lambda j: (0, j, 0)
    table = pl.BlockSpec((TM, GLA_DK), lambda j: (j, 0))
    x_specs, x_args = _stream_specs(x, b, d)
    in_specs = x_specs + [_layer_spec(mods, layer), _layer_spec(nw, layer)] + [
        _layer_spec(w, layer, single=True) for w in wp] + [table, table, table]
    args = x_args + [mods, nw, *wp, cos, sa, sb]
    out_specs = [pl.BlockSpec((b, TM, w), tile) for w in widths]
    out_shape = [jax.ShapeDtypeStruct((b, t, w), dt) for w, (_, _, dt) in zip(widths, _IN_OUT)]
    if moe is not None:
        y, wcol = moe
        planes = [pl.BlockSpec((TM * d // LANES, LANES), lambda j, r=(kk * b + bi) * nt: (r + j, 0))
                  for bi in range(b) for kk in range(TOP_K)]
        in_specs = planes + [pl.BlockSpec((b, TM, TOP_K), tile), _layer_spec(mods, layer - 1)] + in_specs
        args = [y] * len(planes) + [wcol, mods] + args
        out_specs = [pl.BlockSpec((b, TM, d), tile)] + out_specs
        out_shape = [jax.ShapeDtypeStruct((b, t, d), F32)] + out_shape
    return pl.pallas_call(
        functools.partial(_inproj_kernel, moe is not None, b, len(x_args)),
        grid=(nt,),
        in_specs=in_specs,
        out_specs=out_specs,
        out_shape=out_shape,
        compiler_params=_cparams(("arbitrary",)),
        name="inproj",
    )(*args)


def _gla_kernel(reverse, finish, q_ref, k_ref, v_ref, lr_ref, wup_ref, bg_ref, *rest):
    if finish:
        ob_ref, g_ref, nw_ref, o_ref, st_ref = rest
    else:
        o_ref, st_ref = rest
    nchunk = TM // GLA_CHUNK
    nb = q_ref.shape[0]

    @pl.when(pl.program_id(0) == 0)
    def _():
        st_ref[...] = jnp.zeros_like(st_ref)

    r = lax.broadcasted_iota(jnp.int32, (TM, TM), 0)
    s = lax.broadcasted_iota(jnp.int32, (TM, TM), 1)
    order = (s >= r) if reverse else (s <= r)
    tri = jnp.where(((r // GLA_CHUNK) == (s // GLA_CHUNK)) & order, 1.0, 0.0).astype(BF16)
    cmask = order[:GLA_CHUNK, :GLA_CHUNK]
    bcs = []
    for bi in range(nb):
        z = _dot(lr_ref[bi].astype(BF16), wup_ref[0, 0]) + bg_ref[0, 0]
        logg = (jnp.minimum(z, 0.0) - jnp.log(1.0 + jnp.exp(-jnp.abs(z)))) * (1.0 / GLA_TAU)
        hi, lo = _split(logg)
        bcs.append(_dot(tri, hi) + _dot(tri, lo))

    for c in (range(nchunk - 1, -1, -1) if reverse else range(nchunk)):
        rs = slice(c * GLA_CHUNK, (c + 1) * GLA_CHUNK)
        for bi in range(nb):
            for hh in range(GLA_HEADS):
                ls = slice(hh * GLA_DK, (hh + 1) * GLA_DK)
                vs = slice(hh * GLA_DV, (hh + 1) * GLA_DV)
                b = bcs[bi][rs, ls]
                bend = b[0:1] if reverse else b[GLA_CHUNK - 1:GLA_CHUNK]
                qc = q_ref[bi, rs, ls].astype(F32)
                kc = k_ref[bi, rs, ls].astype(F32)
                vc = v_ref[bi, rs, vs]
                kd = (kc * jnp.exp(bend - b)).astype(BF16)
                qi = (qc * jnp.exp(b)).astype(BF16)
                qa = (qc * jnp.exp(b - bend)).astype(BF16)
                att = jnp.where(cmask, _dg(qa, kd, _NT), 0.0).astype(BF16)
                st = st_ref[bi * GLA_HEADS + hh]
                o = _dg(qi, st.astype(BF16), _NT) + _dot(att, vc)
                st_ref[bi * GLA_HEADS + hh] = st * jnp.exp(bend) + _dg(vc, kd, _TN)
                if finish:
                    o = o + ob_ref[bi, rs, vs].astype(F32)
                    o = o * lax.rsqrt(jnp.mean(o * o, axis=-1, keepdims=True) + EPS) * nw_ref[0]
                    g = g_ref[bi, rs, vs].astype(F32)
                    o = o * (g * _sigmoid(g))
                o_ref[bi, rs, vs] = o.astype(o_ref.dtype)


def _gla_pass(reverse, q, k, v, lr, wup, bg, layer, extra=None):
    b, t, _ = q.shape
    nt = t // TM
    dirn = 1 if reverse else 0
    if reverse:
        tile = lambda j: (0, jnp.where(j == 0, 0, nt - j), 0)
    else:
        tile = lambda j: (0, j, 0)
    in_specs = [pl.BlockSpec((b, TM, GLA_QK), tile), pl.BlockSpec((b, TM, GLA_QK), tile),
                pl.BlockSpec((b, TM, GLA_V), tile), pl.BlockSpec((b, TM, LR_PAD), tile),
                _layer_spec(wup, layer, dirn), _layer_spec(bg, layer, dirn)]
    args = [q, k, v, lr, wup, bg]
    if extra is not None:
        ob, g, nw = extra
        in_specs += [pl.BlockSpec((b, TM, GLA_V), tile), pl.BlockSpec((b, TM, GLA_V), tile),
                     _layer_spec(nw, layer)]
        args += [ob, g, nw]
    return pl.pallas_call(
        functools.partial(_gla_kernel, reverse, extra is not None),
        grid=(nt,),
        in_specs=in_specs,
        out_specs=pl.BlockSpec((b, TM, GLA_V), tile),
        out_shape=jax.ShapeDtypeStruct((b, t, GLA_V), BF16),
        scratch_shapes=[pltpu.VMEM((b * GLA_HEADS, GLA_DV, GLA_DK), F32)],
        compiler_params=_cparams(("arbitrary",)),
        name="gla_bwd" if reverse else "gla_fwd",
    )(*args)


def _na_bias_tables(rpb, rows):
    rpt = TM // GRID_W
    kr = min(WIN_R, rows)
    col = np.arange(GRID_W)
    cidx = np.clip(col[None, :] - col[:, None], -(WIN_C - 1), WIN_C - 1) + (WIN_C - 1)
    c_sel = (cidx[None] == np.arange(2 * WIN_C - 1)[:, None, None]).astype(np.float32)
    c0 = np.clip(col - WIN_C // 2, 0, GRID_W - WIN_C)
    c_ok = (col[None, :] >= c0[:, None]) & (col[None, :] < c0[:, None] + WIN_C)
    toep = jnp.einsum("lhrd,dqk->lhrqk", rpb.astype(F32), c_sel, precision=lax.Precision.HIGHEST)
    toep = jnp.where(c_ok, toep, NEG)
    depth = rpb.shape[0]
    kinds = ((0, 0), (rpt, 0), (rows - rpt, rows - 3 * rpt))

    def assemble(toep_ref, o_ref):
        kind = pl.program_id(1)
        masked = jnp.full((NA_HEADS, GRID_W, GRID_W), NEG, F32)

        @pl.when(kind == 0)
        def _():
            o_ref[0, 0] = jnp.full(o_ref.shape[2:], NEG, F32)

        for k, (r_base, u_base) in enumerate(kinds, start=1):
            @pl.when(kind == k)
            def _(r_base=r_base, u_base=u_base):
                for i in range(rpt):
                    rq = r_base + i
                    r0 = min(max(rq - kr // 2, 0), rows - kr)
                    for m in range(3 * rpt):
                        rk = u_base + m
                        blk = toep_ref[0, :, rk - rq + (WIN_R - 1)] if r0 <= rk < r0 + kr else masked
                        o_ref[0, 0, :, i * GRID_W:(i + 1) * GRID_W, m * GRID_W:(m + 1) * GRID_W] = blk

    return pl.pallas_call(
        assemble,
        grid=(depth, 1 + len(kinds)),
        in_specs=[pl.BlockSpec((1,) + toep.shape[1:], lambda l, k: (l, 0, 0, 0, 0))],
        out_specs=pl.BlockSpec((1, 1, NA_HEADS, TM, 3 * TM), lambda l, k: (l, k, 0, 0, 0)),
        out_shape=jax.ShapeDtypeStruct((depth, 1 + len(kinds), NA_HEADS, TM, 3 * TM), F32),
        compiler_params=_cparams(("arbitrary", "arbitrary")),
        name="natten_bias",
    )(toep)


def _na_kernel(q_ref, kp_ref, kc_ref, kn_ref, kx_ref, vp_ref, vc_ref, vn_ref, vx_ref, bias_ref, o_ref):
    lane = lax.broadcasted_iota(jnp.int32, (TM, 2 * NA_HD), 1)
    first = lane < NA_HD
    k_refs = (kp_ref, kc_ref, kn_ref, kx_ref)
    v_refs = (vp_ref, vc_ref, vn_ref, vx_ref)
    for bi in range(q_ref.shape[0]):
        for hp in range(NA_HEADS // 2):
            ls = slice(hp * 2 * NA_HD, (hp + 1) * 2 * NA_HD)
            q2 = q_ref[bi, :, ls]
            ks = [kr[bi, :, ls] for kr in k_refs]
            vs = [vr[bi, :, ls] for vr in v_refs]
            outs = []
            for sub in range(2):
                hd = 2 * hp + sub
                own = first if sub == 0 else jnp.logical_not(first)
                qm = jnp.where(own, q2, jnp.zeros_like(q2))
                sc = []
                for i in range(4):
                    s = _dg(qm, ks[i], _NT)
                    if i < 3:
                        s = s + bias_ref[0, 0, hd, :, i * TM:(i + 1) * TM]
                    sc.append(s)
                m = jnp.max(sc[0], axis=-1, keepdims=True)
                for i in range(1, 4):
                    m = jnp.maximum(m, jnp.max(sc[i], axis=-1, keepdims=True))
                acc = jnp.zeros((TM, 2 * NA_HD), F32)
                for i in range(4):
                    p = jnp.exp((sc[i] - m).astype(BF16))
                    acc = acc + _dot(p, jnp.where(own, vs[i], jnp.ones_like(vs[i])))
                l = acc[:, NA_HD:NA_HD + 1] if sub == 0 else acc[:, 0:1]
                outs.append(acc / l)
            o_ref[bi, :, ls] = jnp.where(first, outs[0], outs[1]).astype(o_ref.dtype)


def _na(q, k, v, bias, layer):
    b, t, _ = q.shape
    nt = t // TM
    qt = lambda j: (0, j, 0)
    ctx = lambda j: (0, 0, 0)

    def near(o):
        return lambda j: (0, jnp.clip(j, 2, nt - 2) + o, 0)

    def kind(j):
        return (layer, jnp.where(j == 0, 0, jnp.where(j == 1, 1, jnp.where(j == nt - 1, 3, 2))), 0, 0, 0)

    blk = lambda f: pl.BlockSpec((b, TM, NA_W), f)
    return pl.pallas_call(
        _na_kernel,
        grid=(nt,),
        in_specs=[blk(qt), blk(near(-1)), blk(near(0)), blk(near(1)), blk(ctx),
                  blk(near(-1)), blk(near(0)), blk(near(1)), blk(ctx),
                  pl.BlockSpec((1, 1, NA_HEADS, TM, 3 * TM), kind)],
        out_specs=blk(qt),
        out_shape=jax.ShapeDtypeStruct((b, t, NA_W), BF16),
        compiler_params=_cparams(("arbitrary",)),
        name="natten",
    )(q, k, k, k, k, v, v, v, v, bias)


def _route(sel, aff):
    rows = lambda a, e: a[e:e + 1, :]
    gscore = []
    for g in range(N_GROUPS):
        a, b, c, d = (rows(sel, EXPERTS_PER_GROUP * g + i) for i in range(EXPERTS_PER_GROUP))
        hi1, lo1 = jnp.maximum(a, b), jnp.minimum(a, b)
        hi2, lo2 = jnp.maximum(c, d), jnp.minimum(c, d)
        gscore.append(jnp.maximum(hi1, hi2) + jnp.maximum(jnp.minimum(hi1, hi2), jnp.maximum(lo1, lo2)))
    gbest = jnp.zeros_like(gscore[0], dtype=jnp.int32)
    gval = gscore[0]
    for g in range(1, N_GROUPS):
        better = gscore[g] > gval
        gbest = jnp.where(better, g, gbest)
        gval = jnp.where(better, gscore[g], gval)
    cs, ca = [], []
    for i in range(EXPERTS_PER_GROUP):
        s_i, a_i = rows(sel, i), rows(aff, i)
        for g in range(1, N_GROUPS):
            pick = gbest == g
            s_i = jnp.where(pick, rows(sel, EXPERTS_PER_GROUP * g + i), s_i)
            a_i = jnp.where(pick, rows(aff, EXPERTS_PER_GROUP * g + i), a_i)
        cs.append(s_i)
        ca.append(a_i)
    i1 = jnp.zeros_like(gbest)
    v1, w1 = cs[0], ca[0]
    for i in range(1, EXPERTS_PER_GROUP):
        better = cs[i] > v1
        i1 = jnp.where(better, i, i1)
        v1 = jnp.where(better, cs[i], v1)
        w1 = jnp.where(better, ca[i], w1)
    i2 = jnp.full_like(gbest, -1)
    v2 = jnp.full_like(v1, -jnp.inf)
    w2 = jnp.zeros_like(w1)
    for i in range(EXPERTS_PER_GROUP):
        better = (i1 != i) & ((cs[i] > v2) | (i2 < 0))
        i2 = jnp.where(better, i, i2)
        v2 = jnp.where(better, cs[i], v2)
        w2 = jnp.where(better, ca[i], w2)
    tot = w1 + w2
    base = gbest * EXPERTS_PER_GROUP
    return base + i1, base + i2, w1 / tot, w2 / tot


def _merge_kernel(n_stream, gy_ref, ny_ref, m1_ref, m2_ref, *refs):
    x_refs, refs = refs[:n_stream], refs[n_stream:]
    mod_ref, nw_ref, wbg_ref, wbn_ref, wo_ref, wr_ref, br_ref, xo_ref, hp_ref, ri_ref, rw_ref = refs
    nb = gy_ref.shape[0]
    is_ctx = pl.program_id(0) == 0
    wh, wl = _split(wr_ref[...])
    zi = jnp.zeros((SUBLANES - TOP_K, TM), jnp.int32)
    for bi in range(nb):
        a = _dot(gy_ref[bi], wbg_ref[0])
        b = _dot(ny_ref[bi], wbn_ref[0])
        m = _sigmoid(m1_ref[bi].astype(F32)) * a + _sigmoid(m2_ref[bi].astype(F32)) * b
        y = _dot(m.astype(BF16), wo_ref[0])
        mod = jnp.where(is_ctx, mod_ref[0, nb], mod_ref[0, bi])
        xn = _stream_tile(x_refs, bi, is_ctx) + mod[2:3] * y
        xo_ref[bi] = xn
        h2 = _norm_mod(xn, nw_ref[0], mod[3:4], mod[4:5])
        _rows_to_tiles(hp_ref.at[bi], 0, h2)
        hh, hl = _split(h2)
        logit = (_dot(hh, wh) + _dot(hl, wh) + _dot(hh, wl)).T[:N_EXPERTS]
        aff = _sigmoid(logit)
        i1, i2, w1, w2 = _route(aff + br_ref[...], aff)
        ri_ref[bi] = jnp.concatenate([i1, i2, zi], axis=0)
        rw_ref[bi] = jnp.concatenate([w1, w2, zi.astype(F32)], axis=0)


def _merge(gy, ny, m1, m2, x, mods, nw2, wbg, wbn, wo, wrt, brt, layer):
    b, t, d = m1.shape
    nt = t // TM
    rows = TM * d // LANES
    tile = lambda j: (0, j, 0)
    const = lambda j: (0, 0)
    x_specs, x_args = _stream_specs(x, b, d)
    xo, hp, ri, rw = pl.pallas_call(
        functools.partial(_merge_kernel, len(x_args)),
        grid=(nt,),
        in_specs=[pl.BlockSpec((b, TM, GLA_V), tile), pl.BlockSpec((b, TM, NA_W), tile),
                  pl.BlockSpec((b, TM, d), tile), pl.BlockSpec((b, TM, d), tile)] + x_specs + [
                  _layer_spec(mods, layer), _layer_spec(nw2, layer),
                  _layer_spec(wbg, layer), _layer_spec(wbn, layer), _layer_spec(wo, layer),
                  pl.BlockSpec((d, LANES), const), pl.BlockSpec((N_EXPERTS, TM), const)],
        out_specs=[pl.BlockSpec((b, TM, d), tile), pl.BlockSpec((b, rows, LANES), tile),
                   pl.BlockSpec((b, SUBLANES, TM), lambda j: (0, 0, j)),
                   pl.BlockSpec((b, SUBLANES, TM), lambda j: (0, 0, j))],
        out_shape=[jax.ShapeDtypeStruct((b, t, d), F32), jax.ShapeDtypeStruct((b, nt * rows, LANES), F32),
                   jax.ShapeDtypeStruct((b, SUBLANES, t), jnp.int32), jax.ShapeDtypeStruct((b, SUBLANES, t), F32)],
        compiler_params=_cparams(("arbitrary",)),
        name="merge_router",
    )(gy, ny, m1, m2, *x_args, mods, nw2, wbg, wbn, wo, wrt, brt)
    return xo, hp.reshape(b * nt * rows, LANES), ri, rw


def _moe_kernel(layer, be_ref, nu_ref, nx_ref, src_cur, src_nxt, src_nx2, dst_cur, dst_prv, h_ref,
                w1_ref, w3_ref, w2_ref, o_ref, xbuf, ybuf, w1s, w3s, w2s, w1b, w3b, w2b, gsem, ssem, wsem):
    i = pl.program_id(0)
    nu = nu_ref[0]
    xb = lax.rem(i, 3)
    yb = lax.rem(i, 3)
    used = i < nu
    st = w1b.shape[0] // LANES
    blk = TM * st
    n_real = o_ref.shape[0] - 2 * blk

    def part(ref, b):
        return ref.at[pl.ds(pl.multiple_of(b * blk, blk), blk)]

    def token(ref, row):
        return ref.at[pl.ds(pl.multiple_of(row, st), st)]

    def start_gather(idx_ref, b):
        for r in range(TM):
            pltpu.make_async_copy(token(h_ref, idx_ref[0, 0, r]), token(xbuf, b * blk + r * st),
                                  gsem.at[b]).start(priority=r % 2)

    def start_scatter(idx_ref, b):
        for r in range(TM):
            pltpu.make_async_copy(token(ybuf, b * blk + r * st), token(o_ref, idx_ref[0, 0, r]),
                                  ssem.at[b]).start(priority=r % 2)

    def wait_gather(b):
        pltpu.make_async_copy(h_ref.at[pl.ds(0, blk)], part(xbuf, b), gsem.at[b]).wait()

    def wait_scatter(b):
        pltpu.make_async_copy(part(ybuf, b), o_ref.at[pl.ds(0, blk)], ssem.at[b]).wait()

    @pl.when(i == 0)
    def _():
        xbuf[...] = jnp.zeros_like(xbuf)
        ybuf[...] = jnp.zeros_like(ybuf)
        for b in range(2):
            spare = pltpu.make_async_copy(part(xbuf, b), o_ref.at[pl.ds(n_real + b * blk, blk)], ssem.at[b])
            spare.start()
            spare.wait()
        start_gather(src_cur, 0)
        start_gather(src_nxt, 1)

    def weight_copies(e):
        return [pltpu.make_async_copy(src.at[layer, e], dst, wsem.at[k])
                for k, (src, dst) in enumerate(((w1_ref, w1s), (w3_ref, w3s), (w2_ref, w2s)))]

    @pl.when(i == 0)
    def _():
        for cp in weight_copies(be_ref[0]):
            cp.start()

    fresh = jnp.logical_or(i == 0, be_ref[i] != be_ref[jnp.maximum(i - 1, 0)])

    @pl.when(jnp.logical_and(used, fresh))
    def _():
        for cp in weight_copies(be_ref[i]):
            cp.wait()
        for src, dst in ((w1s, w1b), (w3s, w3b), (w2s, w2b)):
            def body(r, carry, src=src, dst=dst):
                rows = pl.ds(pl.multiple_of(r * LANES, LANES), LANES)
                dst[rows, :] = src[rows, :].astype(BF16)
                return carry
            lax.fori_loop(0, src.shape[0] // LANES, body, 0)

        @pl.when(nx_ref[i] >= 0)
        def _():
            for cp in weight_copies(nx_ref[i]):
                cp.start()

    @pl.when(jnp.logical_and(used, i > 0))
    def _():
        wait_scatter(lax.rem(i + 1, 3))

    @pl.when(used)
    def _():
        wait_gather(xb)
        x = _tiles_to_rows(xbuf, xb * blk, TM, st).astype(BF16)
        start_gather(src_nx2, lax.rem(i + 2, 3))
        start_scatter(dst_prv, lax.rem(i + 2, 3))
        a = _dot(x, w1b[...])
        b = _dot(x, w3b[...])
        hmid = (a * _sigmoid(a)) * b
        _rows_to_tiles(ybuf, yb * blk, _dot(hmid.astype(BF16), w2b[...]))

    @pl.when(i == nu - 1)
    def _():
        start_scatter(dst_cur, yb)
        wait_scatter(yb)
        wait_scatter(lax.rem(i + 2, 3))
        wait_gather(lax.rem(i + 1, 3))
        wait_gather(lax.rem(i + 2, 3))


def _moe(blk_expert, n_used, next_expert, slot_src, slot_dst, h, w1, w3, w2, layer):
    d, f = w1.shape[2], w1.shape[3]
    st = d // LANES
    blk = TM * st
    n_rows = h.shape[0]
    nb = blk_expert.shape[0]
    smem = lambda f_: pl.BlockSpec((1, 1, TM), f_, memory_space=pltpu.SMEM)
    cur = lambda i, *_: (i, 0, 0)
    nxt = lambda i, *_: (jnp.minimum(i + 1, nb - 1), 0, 0)
    nx2 = lambda i, *_: (jnp.minimum(i + 2, nb - 1), 0, 0)
    prv = lambda i, *_: (jnp.maximum(i - 1, 0), 0, 0)
    hbm = pl.BlockSpec(memory_space=pl.ANY)
    return pl.pallas_call(
        functools.partial(_moe_kernel, layer),
        grid_spec=pltpu.PrefetchScalarGridSpec(
            num_scalar_prefetch=3,
            grid=(nb,),
            in_specs=[smem(cur), smem(nxt), smem(nx2), smem(cur), smem(prv), hbm, hbm, hbm, hbm],
            out_specs=hbm,
            scratch_shapes=[pltpu.VMEM((3 * blk, LANES), F32), pltpu.VMEM((3 * blk, LANES), F32),
                            pltpu.VMEM((d, f), F32), pltpu.VMEM((d, f), F32), pltpu.VMEM((f, d), F32),
                            pltpu.VMEM((d, f), BF16), pltpu.VMEM((d, f), BF16), pltpu.VMEM((f, d), BF16),
                            pltpu.SemaphoreType.DMA((3,)), pltpu.SemaphoreType.DMA((3,)),
                            pltpu.SemaphoreType.DMA((3,))]),
        out_shape=jax.ShapeDtypeStruct((TOP_K * n_rows + 2 * blk, LANES), F32),
        compiler_params=_cparams(("arbitrary",)),
        name="moe_experts",
    )(blk_expert, n_used, next_expert, slot_src, slot_src, slot_src, slot_dst, slot_dst, h, w1, w3, w2)


def _final_kernel(*refs):
    y_refs, (x_ref, w_ref, mod_ref, fw_ref, o_ref) = refs[:TOP_K], refs[TOP_K:]
    xn = _moe_residual(x_ref[0], y_refs, w_ref[0], mod_ref[0, 0])
    o_ref[0] = xn * lax.rsqrt(jnp.mean(xn * xn, axis=-1, keepdims=True) + EPS) * fw_ref[...]


def _final(y, x, wcol, mods, layer, final_w):
    b, t, d = x.shape
    nt = t // TM
    tile = lambda i, j: (i, j + 1, 0)
    return pl.pallas_call(
        _final_kernel,
        grid=(b, nt - 1),
        in_specs=_plane_specs(b, nt, d, 1) + [
            pl.BlockSpec((1, TM, d), tile), pl.BlockSpec((1, TM, TOP_K), tile), _mod_spec(mods, layer, b, 1),
            pl.BlockSpec((1, d), lambda i, j: (0, 0))],
        out_specs=pl.BlockSpec((1, TM, d), lambda i, j: (i, j, 0)),
        out_shape=jax.ShapeDtypeStruct((b, t - TM, d), F32),
        compiler_params=_cparams(("arbitrary", "arbitrary")),
        name="moe_combine_final",
    )(*([y] * TOP_K), x, wcol, mods, final_w)


def _invert_kernel(dest_ref, spare_ref, out_ref):
    pltpu.sync_copy(spare_ref, out_ref)

    def place(p, carry):
        out_ref[dest_ref[p]] = p
        return carry

    lax.fori_loop(0, dest_ref.shape[0], place, 0, unroll=INVERT_UNROLL)


def _invert_slots(dest, n_slots):
    spare = dest.shape[0] + np.arange(n_slots, dtype=np.int32) % (2 * TM)
    return pl.pallas_call(
        _invert_kernel,
        in_specs=[pl.BlockSpec(memory_space=pltpu.SMEM), pl.BlockSpec(memory_space=pl.ANY)],
        out_specs=pl.BlockSpec(memory_space=pltpu.SMEM),
        out_shape=jax.ShapeDtypeStruct((n_slots,), jnp.int32),
        name="slot_invert",
    )(dest, jnp.asarray(spare))


def _slot_plan(ridx, n_tok, st):
    e = jnp.transpose(ridx[:, :TOP_K, :], (1, 0, 2)).reshape(-1)
    onehot = (e[:, None] == jnp.arange(N_EXPERTS, dtype=jnp.int32)[None, :]).astype(jnp.int32)
    incl = jnp.cumsum(onehot, axis=0)
    rank = jnp.sum((incl - onehot) * onehot, axis=1)
    counts = incl[-1]
    padded = (counts + TM - 1) // TM * TM
    pends = jnp.cumsum(padded)
    pstarts = pends - padded
    dest = pstarts[e] + rank
    n_blocks = (n_tok * TOP_K + N_EXPERTS * (TM - 1) + TM - 1) // TM
    blk_start = jnp.arange(n_blocks, dtype=jnp.int32) * TM
    blk_expert = jnp.minimum(jnp.sum((pends[None, :] <= blk_start[:, None]).astype(jnp.int32), axis=1),
                             N_EXPERTS - 1)
    n_used = (pends[-1] // TM).astype(jnp.int32).reshape(1)
    ids = jnp.arange(N_EXPERTS, dtype=jnp.int32)
    later = (counts[None, :] > 0) & (ids[None, :] > ids[:, None])
    nxt_e = jnp.min(jnp.where(later, ids[None, :], N_EXPERTS), axis=1)
    nxt_e = jnp.where(nxt_e == N_EXPERTS, -1, nxt_e)
    next_expert = jnp.sum(jnp.where(blk_expert[:, None] == ids[None, :], nxt_e[None, :], 0), axis=1)
    slot_dst = _invert_slots(dest.astype(jnp.int32), n_blocks * TM)
    slot_src = jnp.where(slot_dst < n_tok * TOP_K, slot_dst % n_tok, 0)
    shape3 = (n_blocks, 1, TM)
    return (blk_expert.astype(jnp.int32), n_used, next_expert.astype(jnp.int32),
            (slot_src * st).reshape(shape3), (slot_dst * st).reshape(shape3))


def _rope_tables(t_len, ctx_len):
    half = GLA_DK // 2
    inv = ROPE_BASE ** (-np.arange(0, half, 2, dtype=np.float32) / half)
    lane = np.arange(GLA_DK)
    p = np.arange(t_len - ctx_len)
    pos = np.where(lane[None, :] < half, (p // GRID_W)[:, None], (p % GRID_W)[:, None]).astype(np.float32)
    ang = jnp.asarray(pos) * jnp.asarray(inv[(lane % half) % (half // 2)])[None, :]
    cos, sin = jnp.cos(ang), jnp.sin(ang)
    lower = jnp.asarray((lane % half) < half // 2)[None, :]
    pad = lambda a, v: jnp.concatenate([jnp.full((ctx_len, GLA_DK), v, F32), a], axis=0)
    return pad(cos, 1.0), pad(jnp.where(lower, -sin, 0.0), 0.0), pad(jnp.where(lower, 0.0, sin), 0.0)


def kernel(x, c, ctx, c_ctx, w_ada, b_ada, norm1_w, norm2_w, w_in, w_gk_up, b_gk, gla_norm_w, rpb,
           w_bo_gla, w_bo_na, w_out, w_router, b_router, w1, w3, w2, final_norm_w):
    bsz, seq, d = x.shape
    ctx_len = ctx.shape[1]
    depth = w_ada.shape[0]
    assert ctx_len == TM and seq % TM == 0 and TM % GRID_W == 0 and seq // GRID_W >= 3 * (TM // GRID_W)
    t = ctx_len + seq
    nt = t // TM
    n_tok = bsz * t

    xs = (ctx, x)
    assert bsz + 1 <= SUBLANES and w_ada.shape[2] == N_MOD * d
    cvec = jnp.concatenate([c, c_ctx[None], jnp.zeros((SUBLANES - bsz - 1, d), F32)], axis=0)
    mods = _modulation(cvec, w_ada, b_ada).reshape(depth, SUBLANES, N_MOD, d)
    cos, sa, sb = _rope_tables(t, ctx_len)
    wrt = jnp.pad(w_router.astype(F32), ((0, 0), (0, LANES - N_EXPERTS)))
    brt = jnp.broadcast_to(b_router.astype(F32)[:, None], (N_EXPERTS, TM))
    gq_end = 2 * GLA_QK + 2 * GLA_V
    wp = (w_in[:, :, :gq_end].astype(BF16), w_in[:, :, gq_end + 2 * GLA_LR:].astype(BF16),
          jnp.pad(w_in[:, :, gq_end:gq_end + 2 * GLA_LR], ((0, 0), (0, 0), (0, LR_PAD - 2 * GLA_LR))).astype(BF16))
    wup = jnp.stack([jnp.pad(w_gk_up[:, dirn], ((0, 0), (dirn * GLA_LR, LR_PAD - (dirn + 1) * GLA_LR), (0, 0)))
                     for dirn in range(2)], axis=1).astype(BF16)
    bg = b_gk.reshape(depth, 2, 1, GLA_QK)
    nw1, nw2 = norm1_w.reshape(depth, 1, d), norm2_w.reshape(depth, 1, d)
    gnw = gla_norm_w.reshape(depth, 1, GLA_DV)
    wbg, wbn, wo = w_bo_gla.astype(BF16), w_bo_na.astype(BF16), w_out.astype(BF16)
    bias = _na_bias_tables(rpb, seq // GRID_W)

    moe = None
    for l in range(depth):
        outs = _inproj(xs, mods, nw1, wp, cos, sa, sb, l, moe)
        if moe is not None:
            xs, outs = outs[0], outs[1:]
        gq, gk, gv, gg, nq, nk, nv, m1, m2, lr = outs
        ob = _gla_pass(True, gq, gk, gv, lr, wup, bg, l)
        gy = _gla_pass(False, gq, gk, gv, lr, wup, bg, l, extra=(ob, gg, gnw))
        ny = _na(nq, nk, nv, bias, l)
        xs, hp, ridx, rw = _merge(gy, ny, m1, m2, xs, mods, nw2, wbg, wbn, wo, wrt, brt, l)
        wcol = jnp.transpose(rw[:, :TOP_K, :], (0, 2, 1))
        blk_expert, n_used, next_expert, slot_src, slot_dst = _slot_plan(ridx, n_tok, d // LANES)
        y = _moe(blk_expert, n_used, next_expert, slot_src, slot_dst, hp, w1, w3, w2, l)
        moe = (y, wcol)
    return _final(moe[0], xs, moe[1], mods, depth - 1, final_norm_w[None])
```

```python
import functools

import jax
import jax.numpy as jnp
import numpy as np
from jax import lax
from jax.experimental import pallas as pl
from jax.experimental.pallas import tpu as pltpu

F32 = jnp.float32
BF16 = jnp.bfloat16

EPS = 1e-6
GRID_W = 64
GLA_HEADS = 4
GLA_DK = 128
GLA_DV = 256
GLA_LR = 16
GLA_TAU = 16.0
GLA_CHUNK = 64
ROPE_BASE = 10000.0
NA_HEADS = 8
NA_HD = 64
WIN_R = 8
WIN_C = 16
N_EXPERTS = 16
N_GROUPS = 4
EXPERTS_PER_GROUP = N_EXPERTS // N_GROUPS
TOP_K = 2

GLA_QK = GLA_HEADS * GLA_DK
GLA_V = GLA_HEADS * GLA_DV
NA_W = NA_HEADS * NA_HD

N_MOD = 6
TM = 256
LANES = 128
SUBLANES = 8
INVERT_UNROLL = 32
LR_PAD = LANES
NEG = -1e30
VMEM_LIMIT = 56 * 1024 * 1024

_NT = (((1,), (1,)), ((), ()))
_TN = (((0,), (0,)), ((), ()))


def _cparams(sem):
    return pltpu.CompilerParams(dimension_semantics=sem, vmem_limit_bytes=VMEM_LIMIT)


def _dot(a, b):
    return jnp.dot(a, b, preferred_element_type=F32)


def _dg(a, b, dims):
    return lax.dot_general(a, b, dims, preferred_element_type=F32)


def _split(a):
    hi = a.astype(BF16)
    lo = (a - hi.astype(F32)).astype(BF16)
    return hi, lo


def _sigmoid(x):
    return 1.0 / (1.0 + jnp.exp(-x))


def _norm_mod(x, w, shift, scale):
    y = x * lax.rsqrt(jnp.mean(x * x, axis=-1, keepdims=True) + EPS)
    return (y * w) * (1.0 + scale) + shift


def _rows_to_tiles(ref, base, val):
    st = val.shape[1] // LANES
    for s in range(st):
        ref[pl.ds(base + s, val.shape[0], stride=st), :] = val[:, s * LANES:(s + 1) * LANES]


def _tiles_to_rows(ref, base, n, st):
    return jnp.concatenate([ref[pl.ds(base + s, n, stride=st), :] for s in range(st)], axis=1)


def _mod_kernel(c_ref, w_ref, b_ref, o_ref):
    c = c_ref[...]
    s = c * _sigmoid(c)
    sh, sl = _split(s)
    wh, wl = _split(w_ref[0])
    o_ref[0] = _dot(sh, wh) + _dot(sl, wh) + _dot(sh, wl) + b_ref[0]


def _modulation(cvec, w_ada, b_ada):
    depth, d, n = w_ada.shape
    tn = n // N_MOD
    return pl.pallas_call(
        _mod_kernel,
        grid=(depth, N_MOD),
        in_specs=[pl.BlockSpec((SUBLANES, d), lambda l, j: (0, 0)),
                  pl.BlockSpec((1, d, tn), lambda l, j: (l, 0, j)),
                  pl.BlockSpec((1, 1, tn), lambda l, j: (l, 0, j))],
        out_specs=pl.BlockSpec((1, SUBLANES, tn), lambda l, j: (l, 0, j)),
        out_shape=jax.ShapeDtypeStruct((depth, SUBLANES, n), F32),
        compiler_params=_cparams(("arbitrary", "arbitrary")),
        name="modulation",
    )(cvec, w_ada, b_ada.reshape(depth, 1, n))


_IN_OUT = (("gq", GLA_QK, BF16), ("gk", GLA_QK, BF16), ("gv", GLA_V, BF16), ("gg", GLA_V, BF16),
           ("nq", NA_W, BF16), ("nk", NA_W, BF16), ("nv", NA_W, BF16),
           ("m1", None, BF16), ("m2", None, BF16), ("lr", LR_PAD, F32))


def _moe_residual(x, y_refs, w, mod):
    y = None
    for kk, y_ref in enumerate(y_refs):
        yk = _tiles_to_rows(y_ref, 0, x.shape[0], x.shape[1] // LANES) * w[:, kk:kk + 1]
        y = yk if y is None else y + yk
    return x + mod[5:6] * y


_IN_SPLIT = (4, 9, 10)


def _stream_specs(stream, b, d):
    if isinstance(stream, tuple):
        return [pl.BlockSpec((b, TM, d), lambda j: (0, 0, 0)),
                pl.BlockSpec((b, TM, d), lambda j: (0, jnp.maximum(j - 1, 0), 0))], list(stream)
    return [pl.BlockSpec((b, TM, d), lambda j: (0, j, 0))], [stream]


def _stream_tile(x_refs, bi, is_ctx):
    if len(x_refs) == 2:
        return jnp.where(is_ctx, x_refs[0][bi], x_refs[1][bi])
    return x_refs[0][bi]


def _inproj_kernel(fused, nb, n_stream, *refs):
    if fused:
        n_planes = nb * TOP_K
        y_refs, (w_ref, pmod_ref), refs = refs[:n_planes], refs[n_planes:n_planes + 2], refs[n_planes + 2:]
    x_refs, refs = refs[:n_stream], refs[n_stream:]
    mod_ref, nw_ref, wa_ref, wb_ref, wc_ref, cos_ref, sa_ref, sb_ref = refs[:8]
    o_refs = refs[8:]
    d = x_refs[0].shape[2]
    is_ctx = pl.program_id(0) == 0
    if fused:
        xo_ref, o_refs = o_refs[0], o_refs[1:]
    cos, sa, sb = cos_ref[...], sa_ref[...], sb_ref[...]

    def rope(r, scale):
        parts = []
        for hh in range(GLA_HEADS):
            xs = r[:, hh * GLA_DK:(hh + 1) * GLA_DK]
            y = xs * cos + pltpu.roll(xs, GLA_DK - GLA_DK // 4, 1) * sa + pltpu.roll(xs, GLA_DK // 4, 1) * sb
            parts.append(y * scale if scale != 1.0 else y)
        return jnp.concatenate(parts, axis=1)

    hs = []
    for bi in range(nb):
        x = _stream_tile(x_refs, bi, is_ctx)
        if fused:
            pmod = jnp.where(is_ctx, pmod_ref[0, nb], pmod_ref[0, bi])
            x = _moe_residual(x, y_refs[bi * TOP_K:(bi + 1) * TOP_K], w_ref[bi], pmod)
            xo_ref[bi] = x
        mod = jnp.where(is_ctx, mod_ref[0, nb], mod_ref[0, bi])
        hs.append(_norm_mod(x, nw_ref[0], mod[0:1], mod[1:2]).astype(BF16))
    h = jnp.concatenate(hs, axis=0)

    off = 0
    for gi, ((name, width, dt), o_ref) in enumerate(zip(_IN_OUT, o_refs)):
        width = d if width is None else width
        if gi in _IN_SPLIT:
            off = 0
        slab_ref = wa_ref if gi < _IN_SPLIT[0] else (wb_ref if gi < _IN_SPLIT[1] else wc_ref)
        rr = _dot(h, slab_ref[0, :, off:off + width])
        for bi in range(nb):
            r = rr[bi * TM:(bi + 1) * TM]
            if name == "gq":
                r = rope(r, GLA_DK ** -0.5)
            elif name == "gk":
                r = rope(r, 1.0)
            elif name == "nq":
                r = r * (NA_HD ** -0.5)
            o_ref[bi] = r.astype(dt)
        off += width


def _layer_spec(arr, layer, *lead, single=False):
    tail = arr.shape[1 + len(lead):]
    index = (layer,) + lead + (0,) * len(tail)
    mode = dict(pipeline_mode=pl.Buffered(1)) if single else {}
    return pl.BlockSpec((1,) * (1 + len(lead)) + tail, lambda *_: index, **mode)


def _mod_spec(mods, layer, ctx_row, skip=0):
    return pl.BlockSpec((1, 1) + mods.shape[2:], lambda i, j: (layer, jnp.where(j + skip == 0, ctx_row, i), 0, 0))


def _plane_specs(b, nt, d, skip):
    def plane(kk):
        return pl.BlockSpec((TM * d // LANES, LANES), lambda i, j: ((kk * b + i) * nt + j + skip, 0))
    return [plane(kk) for kk in range(TOP_K)]


def _inproj(x, mods, nw, wp, cos, sa, sb, layer, moe=None):
    b, _, d = (x[1] if isinstance(x, tuple) else x).shape
    nt = cos.shape[0] // TM
    t = nt * TM
    widths = [d if w is None else w for _, w, _ in _IN_OUT]
    tile = lambda j: (0, j, 0)
    table = pl.BlockSpec((TM, GLA_DK), lambda j: (j, 0))
    x_specs, x_args = _stream_specs(x, b, d)
    in_specs = x_specs + [_layer_spec(mods, layer), _layer_spec(nw, layer)] + [
        _layer_spec(w, layer, single=True) for w in wp] + [table, table, table]
    args = x_args + [mods, nw, *wp, cos, sa, sb]
    out_specs = [pl.BlockSpec((b, TM, w), tile) for w in widths]
    out_shape = [jax.ShapeDtypeStruct((b, t, w), dt) for w, (_, _, dt) in zip(widths, _IN_OUT)]
    if moe is not None:
        y, wcol = moe
        planes = [pl.BlockSpec((TM * d // LANES, LANES), lambda j, r=(kk * b + bi) * nt: (r + j, 0))
                  for bi in range(b) for kk in range(TOP_K)]
        in_specs = planes + [pl.BlockSpec((b, TM, TOP_K), tile), _layer_spec(mods, layer - 1)] + in_specs
        args = [y] * len(planes) + [wcol, mods] + args
        out_specs = [pl.BlockSpec((b, TM, d), tile)] + out_specs
        out_shape = [jax.ShapeDtypeStruct((b, t, d), F32)] + out_shape
    return pl.pallas_call(
        functools.partial(_inproj_kernel, moe is not None, b, len(x_args)),
        grid=(nt,),
        in_specs=in_specs,
        out_specs=out_specs,
        out_shape=out_shape,
        compiler_params=_cparams(("arbitrary",)),
        name="inproj",
    )(*args)


def _gla_kernel(reverse, finish, q_ref, k_ref, v_ref, lr_ref, wup_ref, bg_ref, *rest):
    if finish:
        ob_ref, g_ref, nw_ref, o_ref, st_ref = rest
    else:
        o_ref, st_ref = rest
    nchunk = TM // GLA_CHUNK
    nb = q_ref.shape[0]

    @pl.when(pl.program_id(0) == 0)
    def _():
        st_ref[...] = jnp.zeros_like(st_ref)

    r = lax.broadcasted_iota(jnp.int32, (TM, TM), 0)
    s = lax.broadcasted_iota(jnp.int32, (TM, TM), 1)
    order = (s >= r) if reverse else (s <= r)
    tri = jnp.where(((r // GLA_CHUNK) == (s // GLA_CHUNK)) & order, 1.0, 0.0).astype(BF16)
    cmask = order[:GLA_CHUNK, :GLA_CHUNK]
    bcs = []
    for bi in range(nb):
        z = _dot(lr_ref[bi].astype(BF16), wup_ref[0, 0]) + bg_ref[0, 0]
        logg = (jnp.minimum(z, 0.0) - jnp.log(1.0 + jnp.exp(-jnp.abs(z)))) * (1.0 / GLA_TAU)
        hi, lo = _split(logg)
        bcs.append(_dot(tri, hi) + _dot(tri, lo))

    for c in (range(nchunk - 1, -1, -1) if reverse else range(nchunk)):
        rs = slice(c * GLA_CHUNK, (c + 1) * GLA_CHUNK)
        for bi in range(nb):
            for hh in range(GLA_HEADS):
                ls = slice(hh * GLA_DK, (hh + 1) * GLA_DK)
                vs = slice(hh * GLA_DV, (hh + 1) * GLA_DV)
                b = bcs[bi][rs, ls]
                bend = b[0:1] if reverse else b[GLA_CHUNK - 1:GLA_CHUNK]
                qc = q_ref[bi, rs, ls].astype(F32)
                kc = k_ref[bi, rs, ls].astype(F32)
                vc = v_ref[bi, rs, vs]
                kd = (kc * jnp.exp(bend - b)).astype(BF16)
                qi = (qc * jnp.exp(b)).astype(BF16)
                qa = (qc * jnp.exp(b - bend)).astype(BF16)
                att = jnp.where(cmask, _dg(qa, kd, _NT), 0.0).astype(BF16)
                st = st_ref[bi * GLA_HEADS + hh]
                o = _dg(qi, st.astype(BF16), _NT) + _dot(att, vc)
                st_ref[bi * GLA_HEADS + hh] = st * jnp.exp(bend) + _dg(vc, kd, _TN)
                if finish:
                    o = o + ob_ref[bi, rs, vs].astype(F32)
                    o = o * lax.rsqrt(jnp.mean(o * o, axis=-1, keepdims=True) + EPS) * nw_ref[0]
                    g = g_ref[bi, rs, vs].astype(F32)
                    o = o * (g * _sigmoid(g))
                o_ref[bi, rs, vs] = o.astype(o_ref.dtype)


def _gla_pass(reverse, q, k, v, lr, wup, bg, layer, extra=None):
    b, t, _ = q.shape
    nt = t // TM
    dirn = 1 if reverse else 0
    if reverse:
        tile = lambda j: (0, jnp.where(j == 0, 0, nt - j), 0)
    else:
        tile = lambda j: (0, j, 0)
    in_specs = [pl.BlockSpec((b, TM, GLA_QK), tile), pl.BlockSpec((b, TM, GLA_QK), tile),
                pl.BlockSpec((b, TM, GLA_V), tile), pl.BlockSpec((b, TM, LR_PAD), tile),
                _layer_spec(wup, layer, dirn), _layer_spec(bg, layer, dirn)]
    args = [q, k, v, lr, wup, bg]
    if extra is not None:
        ob, g, nw = extra
        in_specs += [pl.BlockSpec((b, TM, GLA_V), tile), pl.BlockSpec((b, TM, GLA_V), tile),
                     _layer_spec(nw, layer)]
        args += [ob, g, nw]
    return pl.pallas_call(
        functools.partial(_gla_kernel, reverse, extra is not None),
        grid=(nt,),
        in_specs=in_specs,
        out_specs=pl.BlockSpec((b, TM, GLA_V), tile),
        out_shape=jax.ShapeDtypeStruct((b, t, GLA_V), BF16),
        scratch_shapes=[pltpu.VMEM((b * GLA_HEADS, GLA_DV, GLA_DK), F32)],
        compiler_params=_cparams(("arbitrary",)),
        name="gla_bwd" if reverse else "gla_fwd",
    )(*args)


def _na_bias_tables(rpb, rows):
    rpt = TM // GRID_W
    kr = min(WIN_R, rows)
    col = np.arange(GRID_W)
    cidx = np.clip(col[None, :] - col[:, None], -(WIN_C - 1), WIN_C - 1) + (WIN_C - 1)
    c_sel = (cidx[None] == np.arange(2 * WIN_C - 1)[:, None, None]).astype(np.float32)
    c0 = np.clip(col - WIN_C // 2, 0, GRID_W - WIN_C)
    c_ok = (col[None, :] >= c0[:, None]) & (col[None, :] < c0[:, None] + WIN_C)
    toep = jnp.einsum("lhrd,dqk->lhrqk", rpb.astype(F32), c_sel, precision=lax.Precision.HIGHEST)
    toep = jnp.where(c_ok, toep, NEG)
    depth = rpb.shape[0]
    kinds = ((0, 0), (rpt, 0), (rows - rpt, rows - 3 * rpt))

    def assemble(toep_ref, o_ref):
        kind = pl.program_id(1)
        masked = jnp.full((NA_HEADS, GRID_W, GRID_W), NEG, F32)

        @pl.when(kind == 0)
        def _():
            o_ref[0, 0] = jnp.full(o_ref.shape[2:], NEG, F32)

        for k, (r_base, u_base) in enumerate(kinds, start=1):
            @pl.when(kind == k)
            def _(r_base=r_base, u_base=u_base):
                for i in range(rpt):
                    rq = r_base + i
                    r0 = min(max(rq - kr // 2, 0), rows - kr)
                    for m in range(3 * rpt):
                        rk = u_base + m
                        blk = toep_ref[0, :, rk - rq + (WIN_R - 1)] if r0 <= rk < r0 + kr else masked
                        o_ref[0, 0, :, i * GRID_W:(i + 1) * GRID_W, m * GRID_W:(m + 1) * GRID_W] = blk

    return pl.pallas_call(
        assemble,
        grid=(depth, 1 + len(kinds)),
        in_specs=[pl.BlockSpec((1,) + toep.shape[1:], lambda l, k: (l, 0, 0, 0, 0))],
        out_specs=pl.BlockSpec((1, 1, NA_HEADS, TM, 3 * TM), lambda l, k: (l, k, 0, 0, 0)),
        out_shape=jax.ShapeDtypeStruct((depth, 1 + len(kinds), NA_HEADS, TM, 3 * TM), F32),
        compiler_params=_cparams(("arbitrary", "arbitrary")),
        name="natten_bias",
    )(toep)


def _na_kernel(q_ref, kp_ref, kc_ref, kn_ref, kx_ref, vp_ref, vc_ref, vn_ref, vx_ref, bias_ref, o_ref):
    lane = lax.broadcasted_iota(jnp.int32, (TM, 2 * NA_HD), 1)
    first = lane < NA_HD
    k_refs = (kp_ref, kc_ref, kn_ref, kx_ref)
    v_refs = (vp_ref, vc_ref, vn_ref, vx_ref)
    for bi in range(q_ref.shape[0]):
        for hp in range(NA_HEADS // 2):
            ls = slice(hp * 2 * NA_HD, (hp + 1) * 2 * NA_HD)
            q2 = q_ref[bi, :, ls]
            ks = [kr[bi, :, ls] for kr in k_refs]
            vs = [vr[bi, :, ls] for vr in v_refs]
            outs = []
            for sub in range(2):
                hd = 2 * hp + sub
                own = first if sub == 0 else jnp.logical_not(first)
                qm = jnp.where(own, q2, jnp.zeros_like(q2))
                sc = []
                for i in range(4):
                    s = _dg(qm, ks[i], _NT)
                    if i < 3:
                        s = s + bias_ref[0, 0, hd, :, i * TM:(i + 1) * TM]
                    sc.append(s)
                m = jnp.max(sc[0], axis=-1, keepdims=True)
                for i in range(1, 4):
                    m = jnp.maximum(m, jnp.max(sc[i], axis=-1, keepdims=True))
                acc = jnp.zeros((TM, 2 * NA_HD), F32)
                for i in range(4):
                    p = jnp.exp((sc[i] - m).astype(BF16))
                    acc = acc + _dot(p, jnp.where(own, vs[i], jnp.ones_like(vs[i])))
                l = acc[:, NA_HD:NA_HD + 1] if sub == 0 else acc[:, 0:1]
                outs.append(acc / l)
            o_ref[bi, :, ls] = jnp.where(first, outs[0], outs[1]).astype(o_ref.dtype)


def _na(q, k, v, bias, layer):
    b, t, _ = q.shape
    nt = t // TM
    qt = lambda j: (0, j, 0)
    ctx = lambda j: (0, 0, 0)

    def near(o):
        return lambda j: (0, jnp.clip(j, 2, nt - 2) + o, 0)

    def kind(j):
        return (layer, jnp.where(j == 0, 0, jnp.where(j == 1, 1, jnp.where(j == nt - 1, 3, 2))), 0, 0, 0)

    blk = lambda f: pl.BlockSpec((b, TM, NA_W), f)
    return pl.pallas_call(
        _na_kernel,
        grid=(nt,),
        in_specs=[blk(qt), blk(near(-1)), blk(near(0)), blk(near(1)), blk(ctx),
                  blk(near(-1)), blk(near(0)), blk(near(1)), blk(ctx),
                  pl.BlockSpec((1, 1, NA_HEADS, TM, 3 * TM), kind)],
        out_specs=blk(qt),
        out_shape=jax.ShapeDtypeStruct((b, t, NA_W), BF16),
        compiler_params=_cparams(("arbitrary",)),
        name="natten",
    )(q, k, k, k, k, v, v, v, v, bias)


def _route(sel, aff):
    rows = lambda a, e: a[e:e + 1, :]
    gscore = []
    for g in range(N_GROUPS):
        a, b, c, d = (rows(sel, EXPERTS_PER_GROUP * g + i) for i in range(EXPERTS_PER_GROUP))
        hi1, lo1 = jnp.maximum(a, b), jnp.minimum(a, b)
        hi2, lo2 = jnp.maximum(c, d), jnp.minimum(c, d)
        gscore.append(jnp.maximum(hi1, hi2) + jnp.maximum(jnp.minimum(hi1, hi2), jnp.maximum(lo1, lo2)))
    gbest = jnp.zeros_like(gscore[0], dtype=jnp.int32)
    gval = gscore[0]
    for g in range(1, N_GROUPS):
        better = gscore[g] > gval
        gbest = jnp.where(better, g, gbest)
        gval = jnp.where(better, gscore[g], gval)
    cs, ca = [], []
    for i in range(EXPERTS_PER_GROUP):
        s_i, a_i = rows(sel, i), rows(aff, i)
        for g in range(1, N_GROUPS):
            pick = gbest == g
            s_i = jnp.where(pick, rows(sel, EXPERTS_PER_GROUP * g + i), s_i)
            a_i = jnp.where(pick, rows(aff, EXPERTS_PER_GROUP * g + i), a_i)
        cs.append(s_i)
        ca.append(a_i)
    i1 = jnp.zeros_like(gbest)
    v1, w1 = cs[0], ca[0]
    for i in range(1, EXPERTS_PER_GROUP):
        better = cs[i] > v1
        i1 = jnp.where(better, i, i1)
        v1 = jnp.where(better, cs[i], v1)
        w1 = jnp.where(better, ca[i], w1)
    i2 = jnp.full_like(gbest, -1)
    v2 = jnp.full_like(v1, -jnp.inf)
    w2 = jnp.zeros_like(w1)
    for i in range(EXPERTS_PER_GROUP):
        better = (i1 != i) & ((cs[i] > v2) | (i2 < 0))
        i2 = jnp.where(better, i, i2)
        v2 = jnp.where(better, cs[i], v2)
        w2 = jnp.where(better, ca[i], w2)
    tot = w1 + w2
    base = gbest * EXPERTS_PER_GROUP
    return base + i1, base + i2, w1 / tot, w2 / tot


def _merge_kernel(n_stream, gy_ref, ny_ref, m1_ref, m2_ref, *refs):
    x_refs, refs = refs[:n_stream], refs[n_stream:]
    mod_ref, nw_ref, wbg_ref, wbn_ref, wo_ref, wr_ref, br_ref, xo_ref, hp_ref, ri_ref, rw_ref = refs
    nb = gy_ref.shape[0]
    is_ctx = pl.program_id(0) == 0
    wh, wl = _split(wr_ref[...])
    zi = jnp.zeros((SUBLANES - TOP_K, TM), jnp.int32)
    for bi in range(nb):
        a = _dot(gy_ref[bi], wbg_ref[0])
        b = _dot(ny_ref[bi], wbn_ref[0])
        m = _sigmoid(m1_ref[bi].astype(F32)) * a + _sigmoid(m2_ref[bi].astype(F32)) * b
        y = _dot(m.astype(BF16), wo_ref[0])
        mod = jnp.where(is_ctx, mod_ref[0, nb], mod_ref[0, bi])
        xn = _stream_tile(x_refs, bi, is_ctx) + mod[2:3] * y
        xo_ref[bi] = xn
        h2 = _norm_mod(xn, nw_ref[0], mod[3:4], mod[4:5])
        _rows_to_tiles(hp_ref.at[bi], 0, h2)
        hh, hl = _split(h2)
        logit = (_dot(hh, wh) + _dot(hl, wh) + _dot(hh, wl)).T[:N_EXPERTS]
        aff = _sigmoid(logit)
        i1, i2, w1, w2 = _route(aff + br_ref[...], aff)
        ri_ref[bi] = jnp.concatenate([i1, i2, zi], axis=0)
        rw_ref[bi] = jnp.concatenate([w1, w2, zi.astype(F32)], axis=0)


def _merge(gy, ny, m1, m2, x, mods, nw2, wbg, wbn, wo, wrt, brt, layer):
    b, t, d = m1.shape
    nt = t // TM
    rows = TM * d // LANES
    tile = lambda j: (0, j, 0)
    const = lambda j: (0, 0)
    x_specs, x_args = _stream_specs(x, b, d)
    xo, hp, ri, rw = pl.pallas_call(
        functools.partial(_merge_kernel, len(x_args)),
        grid=(nt,),
        in_specs=[pl.BlockSpec((b, TM, GLA_V), tile), pl.BlockSpec((b, TM, NA_W), tile),
                  pl.BlockSpec((b, TM, d), tile), pl.BlockSpec((b, TM, d), tile)] + x_specs + [
                  _layer_spec(mods, layer), _layer_spec(nw2, layer),
                  _layer_spec(wbg, layer), _layer_spec(wbn, layer), _layer_spec(wo, layer),
                  pl.BlockSpec((d, LANES), const), pl.BlockSpec((N_EXPERTS, TM), const)],
        out_specs=[pl.BlockSpec((b, TM, d), tile), pl.BlockSpec((b, rows, LANES), tile),
                   pl.BlockSpec((b, SUBLANES, TM), lambda j: (0, 0, j)),
                   pl.BlockSpec((b, SUBLANES, TM), lambda j: (0, 0, j))],
        out_shape=[jax.ShapeDtypeStruct((b, t, d), F32), jax.ShapeDtypeStruct((b, nt * rows, LANES), F32),
                   jax.ShapeDtypeStruct((b, SUBLANES, t), jnp.int32), jax.ShapeDtypeStruct((b, SUBLANES, t), F32)],
        compiler_params=_cparams(("arbitrary",)),
        name="merge_router",
    )(gy, ny, m1, m2, *x_args, mods, nw2, wbg, wbn, wo, wrt, brt)
    return xo, hp.reshape(b * nt * rows, LANES), ri, rw


def _moe_kernel(layer, be_ref, nu_ref, nx_ref, src_cur, src_nxt, src_nx2, dst_cur, dst_prv, h_ref,
                w1_ref, w3_ref, w2_ref, o_ref, xbuf, ybuf, w1s, w3s, w2s, w1b, w3b, w2b, gsem, ssem, wsem):
    i = pl.program_id(0)
    nu = nu_ref[0]
    xb = lax.rem(i, 3)
    yb = lax.rem(i, 3)
    used = i < nu
    st = w1b.shape[0] // LANES
    blk = TM * st
    n_real = o_ref.shape[0] - 2 * blk

    def part(ref, b):
        return ref.at[pl.ds(pl.multiple_of(b * blk, blk), blk)]

    def token(ref, row):
        return ref.at[pl.ds(pl.multiple_of(row, st), st)]

    def start_gather(idx_ref, b):
        for r in range(TM):
            pltpu.make_async_copy(token(h_ref, idx_ref[0, 0, r]), token(xbuf, b * blk + r * st),
                                  gsem.at[b]).start(priority=0)

    def start_scatter(idx_ref, b):
        for r in range(TM):
            pltpu.make_async_copy(token(ybuf, b * blk + r * st), token(o_ref, idx_ref[0, 0, r]),
                                  ssem.at[b]).start(priority=1)

    def wait_gather(b):
        pltpu.make_async_copy(h_ref.at[pl.ds(0, blk)], part(xbuf, b), gsem.at[b]).wait()

    def wait_scatter(b):
        pltpu.make_async_copy(part(ybuf, b), o_ref.at[pl.ds(0, blk)], ssem.at[b]).wait()

    @pl.when(i == 0)
    def _():
        xbuf[...] = jnp.zeros_like(xbuf)
        ybuf[...] = jnp.zeros_like(ybuf)
        for b in range(2):
            spare = pltpu.make_async_copy(part(xbuf, b), o_ref.at[pl.ds(n_real + b * blk, blk)], ssem.at[b])
            spare.start()
            spare.wait()
        start_gather(src_cur, 0)
        start_gather(src_nxt, 1)

    def weight_copies(e):
        return [pltpu.make_async_copy(src.at[layer, e], dst, wsem.at[k])
                for k, (src, dst) in enumerate(((w1_ref, w1s), (w3_ref, w3s), (w2_ref, w2s)))]

    @pl.when(i == 0)
    def _():
        for cp in weight_copies(be_ref[0]):
            cp.start(priority=1)

    fresh = jnp.logical_or(i == 0, be_ref[i] != be_ref[jnp.maximum(i - 1, 0)])

    @pl.when(jnp.logical_and(used, fresh))
    def _():
        for cp in weight_copies(be_ref[i]):
            cp.wait()
        for src, dst in ((w1s, w1b), (w3s, w3b), (w2s, w2b)):
            def body(r, carry, src=src, dst=dst):
                rows = pl.ds(pl.multiple_of(r * LANES, LANES), LANES)
                dst[rows, :] = src[rows, :].astype(BF16)
                return carry
            lax.fori_loop(0, src.shape[0] // LANES, body, 0)

        @pl.when(nx_ref[i] >= 0)
        def _():
            for cp in weight_copies(nx_ref[i]):
                cp.start(priority=1)

    @pl.when(jnp.logical_and(used, i > 0))
    def _():
        wait_scatter(lax.rem(i + 1, 3))

    @pl.when(used)
    def _():
        wait_gather(xb)
        x = _tiles_to_rows(xbuf, xb * blk, TM, st).astype(BF16)
        start_gather(src_nx2, lax.rem(i + 2, 3))
        start_scatter(dst_prv, lax.rem(i + 2, 3))
        a = _dot(x, w1b[...])
        b = _dot(x, w3b[...])
        hmid = (a * _sigmoid(a)) * b
        _rows_to_tiles(ybuf, yb * blk, _dot(hmid.astype(BF16), w2b[...]))

    @pl.when(i == nu - 1)
    def _():
        start_scatter(dst_cur, yb)
        wait_scatter(yb)
        wait_scatter(lax.rem(i + 2, 3))
        wait_gather(lax.rem(i + 1, 3))
        wait_gather(lax.rem(i + 2, 3))


def _moe(blk_expert, n_used, next_expert, slot_src, slot_dst, h, w1, w3, w2, layer):
    d, f = w1.shape[2], w1.shape[3]
    st = d // LANES
    blk = TM * st
    n_rows = h.shape[0]
    nb = blk_expert.shape[0]
    smem = lambda f_: pl.BlockSpec((1, 1, TM), f_, memory_space=pltpu.SMEM)
    cur = lambda i, *_: (i, 0, 0)
    nxt = lambda i, *_: (jnp.minimum(i + 1, nb - 1), 0, 0)
    nx2 = lambda i, *_: (jnp.minimum(i + 2, nb - 1), 0, 0)
    prv = lambda i, *_: (jnp.maximum(i - 1, 0), 0, 0)
    hbm = pl.BlockSpec(memory_space=pl.ANY)
    return pl.pallas_call(
        functools.partial(_moe_kernel, layer),
        grid_spec=pltpu.PrefetchScalarGridSpec(
            num_scalar_prefetch=3,
            grid=(nb,),
            in_specs=[smem(cur), smem(nxt), smem(nx2), smem(cur), smem(prv), hbm, hbm, hbm, hbm],
            out_specs=hbm,
            scratch_shapes=[pltpu.VMEM((3 * blk, LANES), F32), pltpu.VMEM((3 * blk, LANES), F32),
                            pltpu.VMEM((d, f), F32), pltpu.VMEM((d, f), F32), pltpu.VMEM((f, d), F32),
                            pltpu.VMEM((d, f), BF16), pltpu.VMEM((d, f), BF16), pltpu.VMEM((f, d), BF16),
                            pltpu.SemaphoreType.DMA((3,)), pltpu.SemaphoreType.DMA((3,)),
                            pltpu.SemaphoreType.DMA((3,))]),
        out_shape=jax.ShapeDtypeStruct((TOP_K * n_rows + 2 * blk, LANES), F32),
        compiler_params=_cparams(("arbitrary",)),
        name="moe_experts",
    )(blk_expert, n_used, next_expert, slot_src, slot_src, slot_src, slot_dst, slot_dst, h, w1, w3, w2)


def _final_kernel(*refs):
    y_refs, (x_ref, w_ref, mod_ref, fw_ref, o_ref) = refs[:TOP_K], refs[TOP_K:]
    xn = _moe_residual(x_ref[0], y_refs, w_ref[0], mod_ref[0, 0])
    o_ref[0] = xn * lax.rsqrt(jnp.mean(xn * xn, axis=-1, keepdims=True) + EPS) * fw_ref[...]


def _final(y, x, wcol, mods, layer, final_w):
    b, t, d = x.shape
    nt = t // TM
    tile = lambda i, j: (i, j + 1, 0)
    return pl.pallas_call(
        _final_kernel,
        grid=(b, nt - 1),
        in_specs=_plane_specs(b, nt, d, 1) + [
            pl.BlockSpec((1, TM, d), tile), pl.BlockSpec((1, TM, TOP_K), tile), _mod_spec(mods, layer, b, 1),
            pl.BlockSpec((1, d), lambda i, j: (0, 0))],
        out_specs=pl.BlockSpec((1, TM, d), lambda i, j: (i, j, 0)),
        out_shape=jax.ShapeDtypeStruct((b, t - TM, d), F32),
        compiler_params=_cparams(("arbitrary", "arbitrary")),
        name="moe_combine_final",
    )(*([y] * TOP_K), x, wcol, mods, final_w)


def _invert_kernel(dest_ref, spare_ref, out_ref):
    pltpu.sync_copy(spare_ref, out_ref)

    def place(p, carry):
        out_ref[dest_ref[p]] = p
        return carry

    lax.fori_loop(0, dest_ref.shape[0], place, 0, unroll=INVERT_UNROLL)


def _invert_slots(dest, n_slots):
    spare = dest.shape[0] + np.arange(n_slots, dtype=np.int32) % (2 * TM)
    return pl.pallas_call(
        _invert_kernel,
        in_specs=[pl.BlockSpec(memory_space=pltpu.SMEM), pl.BlockSpec(memory_space=pl.ANY)],
        out_specs=pl.BlockSpec(memory_space=pltpu.SMEM),
        out_shape=jax.ShapeDtypeStruct((n_slots,), jnp.int32),
        name="slot_invert",
    )(dest, jnp.asarray(spare))


def _slot_plan(ridx, n_tok, st):
    e = jnp.transpose(ridx[:, :TOP_K, :], (1, 0, 2)).reshape(-1)
    onehot = (e[:, None] == jnp.arange(N_EXPERTS, dtype=jnp.int32)[None, :]).astype(jnp.int32)
    incl = jnp.cumsum(onehot, axis=0)
    rank = jnp.sum((incl - onehot) * onehot, axis=1)
    counts = incl[-1]
    padded = (counts + TM - 1) // TM * TM
    pends = jnp.cumsum(padded)
    pstarts = pends - padded
    dest = pstarts[e] + rank
    n_blocks = (n_tok * TOP_K + N_EXPERTS * (TM - 1) + TM - 1) // TM
    blk_start = jnp.arange(n_blocks, dtype=jnp.int32) * TM
    blk_expert = jnp.minimum(jnp.sum((pends[None, :] <= blk_start[:, None]).astype(jnp.int32), axis=1),
                             N_EXPERTS - 1)
    n_used = (pends[-1] // TM).astype(jnp.int32).reshape(1)
    ids = jnp.arange(N_EXPERTS, dtype=jnp.int32)
    later = (counts[None, :] > 0) & (ids[None, :] > ids[:, None])
    nxt_e = jnp.min(jnp.where(later, ids[None, :], N_EXPERTS), axis=1)
    nxt_e = jnp.where(nxt_e == N_EXPERTS, -1, nxt_e)
    next_expert = jnp.sum(jnp.where(blk_expert[:, None] == ids[None, :], nxt_e[None, :], 0), axis=1)
    slot_dst = _invert_slots(dest.astype(jnp.int32), n_blocks * TM)
    slot_src = jnp.where(slot_dst < n_tok * TOP_K, slot_dst % n_tok, 0)
    shape3 = (n_blocks, 1, TM)
    return (blk_expert.astype(jnp.int32), n_used, next_expert.astype(jnp.int32),
            (slot_src * st).reshape(shape3), (slot_dst * st).reshape(shape3))


def _rope_tables(t_len, ctx_len):
    half = GLA_DK // 2
    inv = ROPE_BASE ** (-np.arange(0, half, 2, dtype=np.float32) / half)
    lane = np.arange(GLA_DK)
    p = np.arange(t_len - ctx_len)
    pos = np.where(lane[None, :] < half, (p // GRID_W)[:, None], (p % GRID_W)[:, None]).astype(np.float32)
    ang = jnp.asarray(pos) * jnp.asarray(inv[(lane % half) % (half // 2)])[None, :]
    cos, sin = jnp.cos(ang), jnp.sin(ang)
    lower = jnp.asarray((lane % half) < half // 2)[None, :]
    pad = lambda a, v: jnp.concatenate([jnp.full((ctx_len, GLA_DK), v, F32), a], axis=0)
    return pad(cos, 1.0), pad(jnp.where(lower, -sin, 0.0), 0.0), pad(jnp.where(lower, 0.0, sin), 0.0)


def kernel(x, c, ctx, c_ctx, w_ada, b_ada, norm1_w, norm2_w, w_in, w_gk_up, b_gk, gla_norm_w, rpb,
           w_bo_gla, w_bo_na, w_out, w_router, b_router, w1, w3, w2, final_norm_w):
    bsz, seq, d = x.shape
    ctx_len = ctx.shape[1]
    depth = w_ada.shape[0]
    assert ctx_len == TM and seq % TM == 0 and TM % GRID_W == 0 and seq // GRID_W >= 3 * (TM // GRID_W)
    t = ctx_len + seq
    nt = t // TM
    n_tok = bsz * t

    xs = (ctx, x)
    assert bsz + 1 <= SUBLANES and w_ada.shape[2] == N_MOD * d
    cvec = jnp.concatenate([c, c_ctx[None], jnp.zeros((SUBLANES - bsz - 1, d), F32)], axis=0)
    mods = _modulation(cvec, w_ada, b_ada).reshape(depth, SUBLANES, N_MOD, d)
    cos, sa, sb = _rope_tables(t, ctx_len)
    wrt = jnp.pad(w_router.astype(F32), ((0, 0), (0, LANES - N_EXPERTS)))
    brt = jnp.broadcast_to(b_router.astype(F32)[:, None], (N_EXPERTS, TM))
    gq_end = 2 * GLA_QK + 2 * GLA_V
    wp = (w_in[:, :, :gq_end].astype(BF16), w_in[:, :, gq_end + 2 * GLA_LR:].astype(BF16),
          jnp.pad(w_in[:, :, gq_end:gq_end + 2 * GLA_LR], ((0, 0), (0, 0), (0, LR_PAD - 2 * GLA_LR))).astype(BF16))
    wup = jnp.stack([jnp.pad(w_gk_up[:, dirn], ((0, 0), (dirn * GLA_LR, LR_PAD - (dirn + 1) * GLA_LR), (0, 0)))
                     for dirn in range(2)], axis=1).astype(BF16)
    bg = b_gk.reshape(depth, 2, 1, GLA_QK)
    nw1, nw2 = norm1_w.reshape(depth, 1, d), norm2_w.reshape(depth, 1, d)
    gnw = gla_norm_w.reshape(depth, 1, GLA_DV)
    wbg, wbn, wo = w_bo_gla.astype(BF16), w_bo_na.astype(BF16), w_out.astype(BF16)
    bias = _na_bias_tables(rpb, seq // GRID_W)

    moe = None
    for l in range(depth):
        outs = _inproj(xs, mods, nw1, wp, cos, sa, sb, l, moe)
        if moe is not None:
            xs, outs = outs[0], outs[1:]
        gq, gk, gv, gg, nq, nk, nv, m1, m2, lr = outs
        ob = _gla_pass(True, gq, gk, gv, lr, wup, bg, l)
        gy = _gla_pass(False, gq, gk, gv, lr, wup, bg, l, extra=(ob, gg, gnw))
        ny = _na(nq, nk, nv, bias, l)
        xs, hp, ridx, rw = _merge(gy, ny, m1, m2, xs, mods, nw2, wbg, wbn, wo, wrt, brt, l)
        wcol = jnp.transpose(rw[:, :TOP_K, :], (0, 2, 1))
        blk_expert, n_used, next_expert, slot_src, slot_dst = _slot_plan(ridx, n_tok, d // LANES)
        y = _moe(blk_expert, n_used, next_expert, slot_src, slot_dst, hp, w1, w3, w2, l)
        moe = (y, wcol)
    return _final(moe[0], xs, moe[1], mods, depth - 1, final_norm_w[None])
```
